```python
import math
import jax, jax.numpy as jnp
from jax import lax
import numpy as np

D_MODEL = 1024
BATCH = 8
SEQ = 4096
DEPTH = 2

N_MIXERS = 2
N_A_LAYERS = (DEPTH + N_MIXERS - 1) // N_MIXERS
N_B_LAYERS = DEPTH // N_MIXERS

D_FF = 4 * D_MODEL
RMS_EPS = 1e-6
ROPE_THETA = 10000.0
N_MOD = 6

D_RNN = 1280
LRU_BLOCKS = 5
LRU_BLOCK = D_RNN // LRU_BLOCKS
CONV_WIDTH = 4
LRU_C = 8.0

N_HEADS = 16
HEAD_DIM = D_MODEL // N_HEADS
KV_DIM = HEAD_DIM
IDX_HEADS = 8
IDX_DIM = 64
TOPK_MAX = 256
Q_BLOCK = 128
B_WIDTHS = (N_HEADS * HEAD_DIM, KV_DIM, KV_DIM, IDX_HEADS * IDX_DIM, IDX_DIM, IDX_HEADS)
B_IN = sum(B_WIDTHS)
B_SPLITS = tuple(int(v) for v in np.cumsum(B_WIDTHS)[:-1])

kernel_name = "hybrid_rglru_dsa_adaln_trunk"


def rmsnorm(x, g):
    x32 = x.astype(jnp.float32)
    y = x32 * lax.rsqrt(jnp.mean(x32 * x32, axis=-1, keepdims=True) + RMS_EPS)
    return (y * g.astype(jnp.float32)).astype(x.dtype)


def rope_tables(seq, dim):
    inv = ROPE_THETA ** (-jnp.arange(0, dim, 2, dtype=jnp.float32) / dim)
    ang = jnp.arange(seq, dtype=jnp.float32)[:, None] * inv[None, :]
    return jnp.cos(ang), jnp.sin(ang)


def apply_rope(x, cos, sin):
    half = x.shape[-1] // 2
    cos = cos[:, None, :].astype(x.dtype)
    sin = sin[:, None, :].astype(x.dtype)
    x1, x2 = x[..., :half], x[..., half:]
    return jnp.concatenate([x1 * cos - x2 * sin, x2 * cos + x1 * sin], axis=-1)


def rglru_mixer(h, w_in, conv_w, conv_b, wr, br, wi, bi, lam, w_out):
    B, S, _ = h.shape
    u = h @ w_in
    xb, gb = jnp.split(u, 2, axis=-1)
    gate = jax.nn.gelu(gb)
    xc = lax.conv_general_dilated(
        xb, conv_w[:, None, :].astype(xb.dtype), window_strides=(1,),
        padding=[(CONV_WIDTH - 1, 0)], dimension_numbers=("NWC", "WIO", "NWC"),
        feature_group_count=D_RNN) + conv_b
    xr = xc.reshape(B, S, LRU_BLOCKS, LRU_BLOCK)
    r = jax.nn.sigmoid(jnp.einsum("bsnc,ncd->bsnd", xr, wr).reshape(B, S, D_RNN) + br)
    i = jax.nn.sigmoid(jnp.einsum("bsnc,ncd->bsnd", xr, wi).reshape(B, S, D_RNN) + bi)
    log_a = -LRU_C * r.astype(jnp.float32) * jax.nn.softplus(-lam.astype(jnp.float32))
    a = jnp.exp(log_a)
    b = jnp.sqrt(-jnp.expm1(2.0 * log_a)) * (i * xc).astype(jnp.float32)

    def combine(left, right):
        a_l, b_l = left
        a_r, b_r = right
        return a_l * a_r, a_r * b_l + b_r

    _, hs = lax.associative_scan(combine, (a, b), axis=1)
    y = hs.astype(h.dtype) * gate
    return y @ w_out


def dsa_mixer(h, w_in, q_g, k_g, w_out):
    B, S, _ = h.shape
    topk = min(TOPK_MAX, S // 4)
    cos, sin = rope_tables(S, HEAD_DIM)
    cos_i, sin_i = rope_tables(S, IDX_DIM)
    u = h @ w_in
    q, k, v, qi, ki, wi = jnp.split(u, B_SPLITS, axis=-1)
    q = apply_rope(rmsnorm(q.reshape(B, S, N_HEADS, HEAD_DIM), q_g), cos, sin)
    k = apply_rope(rmsnorm(k, k_g)[:, :, None, :], cos, sin)[:, :, 0]
    qi = apply_rope(qi.reshape(B, S, IDX_HEADS, IDX_DIM), cos_i, sin_i)
    ki = apply_rope(ki[:, :, None, :], cos_i, sin_i)[:, :, 0]
    wi = wi * (IDX_HEADS ** -0.5)
    pos = jnp.arange(S)
    nb = S // Q_BLOCK

    def to_blocks(t):
        return t.reshape((B, nb, Q_BLOCK) + t.shape[2:]).swapaxes(0, 1)

    def block_fn(args):
        qb, qib, wib, posb = args
        logits = jnp.einsum("bqhd,bsd->bqhs", qib, ki) * (IDX_DIM ** -0.5)
        score = jnp.einsum("bqh,bqhs->bqs", wib, jax.nn.relu(logits)).astype(jnp.float32)
        causal = pos[None, :] <= posb[:, None]
        score = jnp.where(causal[None], score, -jnp.inf)
        _, idx = lax.top_k(score, topk)
        kg = jax.vmap(lambda kk, ii: kk[ii])(k, idx)
        vg = jax.vmap(lambda vv, ii: vv[ii])(v, idx)
        valid = idx <= posb[None, :, None]
        s = jnp.einsum("bqhd,bqkd->bqhk", qb, kg).astype(jnp.float32) * (HEAD_DIM ** -0.5)
        s = jnp.where(valid[:, :, None, :], s, -jnp.inf)
        p = jax.nn.softmax(s, axis=-1).astype(vg.dtype)
        return jnp.einsum("bqhk,bqkd->bqhd", p, vg)

    out = lax.map(block_fn, (to_blocks(q), to_blocks(qi), to_blocks(wi), pos.reshape(nb, Q_BLOCK)))
    out = out.swapaxes(0, 1).reshape(B, S, N_HEADS * HEAD_DIM)
    return out @ w_out


def setup_inputs(seed: int = 0) -> dict:
    key = jax.random.key(seed)
    ks = iter(jax.random.split(key, 32))
    f32 = jnp.float32

    def nrm(shape, scale):
        return jax.random.normal(next(ks), shape, f32) * scale

    x = nrm((BATCH, SEQ, D_MODEL), 1.0)
    c = nrm((BATCH, D_MODEL), 1.0)
    norm_mix_g = 1.0 + nrm((DEPTH, D_MODEL), 0.02)
    norm_ffn_g = 1.0 + nrm((DEPTH, D_MODEL), 0.02)
    ada_w = nrm((DEPTH, D_MODEL, N_MOD * D_MODEL), D_MODEL ** -0.5)
    ada_b = nrm((DEPTH, N_MOD * D_MODEL), 0.01)
    a_w_in = nrm((N_A_LAYERS, D_MODEL, 2 * D_RNN), D_MODEL ** -0.5)
    a_conv_w = nrm((N_A_LAYERS, CONV_WIDTH, D_RNN), CONV_WIDTH ** -0.5)
    a_conv_b = nrm((N_A_LAYERS, D_RNN), 0.01)
    a_gate_r_w = nrm((N_A_LAYERS, LRU_BLOCKS, LRU_BLOCK, LRU_BLOCK), LRU_BLOCK ** -0.5)
    a_gate_r_b = nrm((N_A_LAYERS, D_RNN), 0.01)
    a_gate_i_w = nrm((N_A_LAYERS, LRU_BLOCKS, LRU_BLOCK, LRU_BLOCK), LRU_BLOCK ** -0.5)
    a_gate_i_b = nrm((N_A_LAYERS, D_RNN), 0.01)
    a_pow = jax.random.uniform(next(ks), (N_A_LAYERS, D_RNN), f32, 0.9, 0.999)
    s = a_pow ** (1.0 / LRU_C)
    a_lambda = jnp.log(s) - jnp.log1p(-s)
    a_w_out = nrm((N_A_LAYERS, D_RNN, D_MODEL), D_RNN ** -0.5)
    b_w_in = nrm((N_B_LAYERS, D_MODEL, B_IN), D_MODEL ** -0.5)
    b_q_norm_g = 1.0 + nrm((N_B_LAYERS, HEAD_DIM), 0.02)
    b_k_norm_g = 1.0 + nrm((N_B_LAYERS, KV_DIM), 0.02)
    b_w_out = nrm((N_B_LAYERS, N_HEADS * HEAD_DIM, D_MODEL), (N_HEADS * HEAD_DIM) ** -0.5)
    ffn_w1 = nrm((DEPTH, D_MODEL, D_FF), D_MODEL ** -0.5)
    ffn_w2 = nrm((DEPTH, D_FF, D_MODEL), D_FF ** -0.5)
    return {"x": x, "c": c, "norm_mix_g": norm_mix_g, "norm_ffn_g": norm_ffn_g,
            "ada_w": ada_w, "ada_b": ada_b, "a_w_in": a_w_in, "a_conv_w": a_conv_w,
            "a_conv_b": a_conv_b, "a_gate_r_w": a_gate_r_w, "a_gate_r_b": a_gate_r_b,
            "a_gate_i_w": a_gate_i_w, "a_gate_i_b": a_gate_i_b, "a_lambda": a_lambda,
            "a_w_out": a_w_out, "b_w_in": b_w_in, "b_q_norm_g": b_q_norm_g,
            "b_k_norm_g": b_k_norm_g, "b_w_out": b_w_out, "ffn_w1": ffn_w1, "ffn_w2": ffn_w2}


def reference(x, c, norm_mix_g, norm_ffn_g, ada_w, ada_b, a_w_in, a_conv_w, a_conv_b,
              a_gate_r_w, a_gate_r_b, a_gate_i_w, a_gate_i_b, a_lambda, a_w_out,
              b_w_in, b_q_norm_g, b_k_norm_g, b_w_out, ffn_w1, ffn_w2):
    cond = jax.nn.silu(c)
    for i in range(DEPTH):
        mod = (cond @ ada_w[i] + ada_b[i])[:, None, :]
        sh1, sc1, g1, sh2, sc2, g2 = jnp.split(mod, N_MOD, axis=-1)
        j = i // N_MIXERS
        h = rmsnorm(x, norm_mix_g[i]) * (1.0 + sc1) + sh1
        if i % N_MIXERS == 0:
            y = rglru_mixer(h, a_w_in[j], a_conv_w[j], a_conv_b[j], a_gate_r_w[j], a_gate_r_b[j],
                            a_gate_i_w[j], a_gate_i_b[j], a_lambda[j], a_w_out[j])
        else:
            y = dsa_mixer(h, b_w_in[j], b_q_norm_g[j], b_k_norm_g[j], b_w_out[j])
        x = x + g1 * y
        h = rmsnorm(x, norm_ffn_g[i]) * (1.0 + sc2) + sh2
        x = x + g2 * (jnp.square(jax.nn.relu(h @ ffn_w1[i])) @ ffn_w2[i])
    return x
```

```python
import functools

import jax
import jax.numpy as jnp
from jax import lax
from jax.experimental import pallas as pl
from jax.experimental.pallas import tpu as pltpu

F32 = jnp.float32
BF16 = jnp.bfloat16
I32 = jnp.int32

RMS_EPS = 1e-6
ROPE_THETA = 10000.0
LRU_C = 8.0
TOPK_MAX = 256
N_MOD = 6

LANES = 128
SUBLANES = 8
VMEM_LIMIT = 56 * 1024 * 1024

NEG_BIG = -1e30


def _const_spec(shape):
    nd = len(shape)
    return pl.BlockSpec(shape, lambda *_: (0,) * nd, pipeline_mode=pl.Buffered(1))


def _rmsnorm(x, g):
    return x * lax.rsqrt(jnp.mean(x * x, axis=-1, keepdims=True) + RMS_EPS) * g


def _mod_kernel(c_ref, w_ref, b_ref, o_ref):
    c = c_ref[...]
    cond = c * jax.nn.sigmoid(c)
    o_ref[0] = jnp.dot(cond.astype(BF16), w_ref[0].astype(BF16),
                       preferred_element_type=F32) + b_ref[0]


def _mod_call(c, ada_w, ada_b):
    depth, d, n = ada_w.shape
    b = c.shape[0]
    tn = 1536
    return pl.pallas_call(
        _mod_kernel,
        grid=(depth, n // tn),
        in_specs=[pl.BlockSpec((b, d), lambda l, j: (0, 0)),
                  pl.BlockSpec((1, d, tn), lambda l, j: (l, 0, j)),
                  pl.BlockSpec((1, 1, tn), lambda l, j: (l, 0, j))],
        out_specs=pl.BlockSpec((1, b, tn), lambda l, j: (l, 0, j)),
        out_shape=jax.ShapeDtypeStruct((depth, b, n), F32),
        compiler_params=pltpu.CompilerParams(
            dimension_semantics=("arbitrary", "arbitrary"), vmem_limit_bytes=VMEM_LIMIT),
        name="adaln_mod",
    )(c, ada_w, ada_b.reshape(depth, 1, n))


def _rglru_kernel(x_ref, mod_ref, g_ref, win_ref, cw_ref, cb_ref, wr_ref, br_ref, wi_ref, bi_ref,
                  lam_ref, wout_ref, o_ref, u_s, xbuf, a_s, b_s, hc, *, ts, rc):
    nb = SUBLANES
    r = ts * nb
    d = x_ref.shape[1]
    dr = lam_ref.shape[1]
    nblk, blk, _ = wr_ref.shape
    cwid = cw_ref.shape[0]
    tail = (cwid - 1) * nb

    @pl.when(pl.program_id(0) == 0)
    def _():
        xbuf[0:tail, :] = jnp.zeros((tail, dr), F32)
        hc[...] = jnp.zeros((nb, dr), F32)

    x = x_ref[...]
    sh1 = mod_ref[:, 0:d]
    sc1 = mod_ref[:, d:2 * d]
    g1 = mod_ref[:, 2 * d:3 * d]
    hn = _rmsnorm(x, g_ref[...])
    h = (hn.reshape(ts, nb, d) * (1.0 + sc1)[None] + sh1[None]).reshape(r, d)
    u_s[...] = jnp.dot(h.astype(BF16), win_ref[...], preferred_element_type=F32)
    xbuf[tail:tail + r, :] = u_s[:, 0:dr]

    sp = LRU_C * jax.nn.softplus(-lam_ref[...])

    def gates(ci, _):
        r0 = pl.multiple_of(ci * rc, rc)
        xc = cb_ref[...] + cw_ref[0:1, :] * xbuf[pl.ds(r0, rc), :]
        for k in range(1, cwid):
            xc = xc + cw_ref[k:k + 1, :] * xbuf[pl.ds(r0 + k * nb, rc), :]
        xcb = xc.astype(BF16)
        for n in range(nblk):
            cs = slice(n * blk, (n + 1) * blk)
            xn = xcb[:, cs]
            rg = jax.nn.sigmoid(jnp.dot(xn, wr_ref[n], preferred_element_type=F32) + br_ref[:, cs])
            ig = jax.nn.sigmoid(jnp.dot(xn, wi_ref[n], preferred_element_type=F32) + bi_ref[:, cs])
            log_a = -(rg * sp[:, cs])
            a = jnp.exp(log_a)
            a_s[pl.ds(r0, rc), cs] = a
            b_s[pl.ds(r0, rc), cs] = jnp.sqrt(-jnp.tanh(log_a) * (a * a + 1.0)) * (ig * xc[:, cs])
        return 0

    lax.fori_loop(0, r // rc, gates, 0)
    xbuf[0:tail, :] = xbuf[r:r + tail, :]

    def step(t, hprev):
        r0 = pl.multiple_of(t * nb, nb)
        hnew = a_s[pl.ds(r0, nb), :] * hprev + b_s[pl.ds(r0, nb), :]
        b_s[pl.ds(r0, nb), :] = hnew
        return hnew

    hc[...] = lax.fori_loop(0, ts, step, hc[...], unroll=8)

    y = b_s[...] * jax.nn.gelu(u_s[:, dr:2 * dr])
    out = jnp.dot(y.astype(BF16), wout_ref[...], preferred_element_type=F32)
    o_ref[...] = x + (out.reshape(ts, nb, d) * g1[None]).reshape(r, d)


def _rglru_call(x_sb, mod0, g, w_in, conv_w, conv_b, wr, br, wi, bi, lam, w_out, *, ts=64, rc=128):
    rows, d = x_sb.shape
    r = ts * SUBLANES
    dr = lam.shape[-1]
    tail = (conv_w.shape[0] - 1) * SUBLANES
    row = lambda v: v.reshape(1, -1)
    args = (x_sb, mod0, row(g), w_in.astype(BF16), conv_w, row(conv_b), wr.astype(BF16), row(br),
            wi.astype(BF16), row(bi), row(lam), w_out.astype(BF16))
    in_specs = [pl.BlockSpec((r, d), lambda i: (i, 0))] + [_const_spec(a.shape) for a in args[1:]]
    return pl.pallas_call(
        functools.partial(_rglru_kernel, ts=ts, rc=rc),
        grid=(rows // r,),
        in_specs=in_specs,
        out_specs=pl.BlockSpec((r, d), lambda i: (i, 0)),
        out_shape=jax.ShapeDtypeStruct((rows, d), F32),
        scratch_shapes=[pltpu.VMEM((r, 2 * dr), F32), pltpu.VMEM((r + tail, dr), F32),
                        pltpu.VMEM((r, dr), F32), pltpu.VMEM((r, dr), F32),
                        pltpu.VMEM((SUBLANES, dr), F32)],
        compiler_params=pltpu.CompilerParams(
            dimension_semantics=("arbitrary",), vmem_limit_bytes=VMEM_LIMIT),
        name="rglru_mixer",
    )(*args)


def _ffn_kernel(*refs, has_proj, fc):
    if has_proj:
        x_ref, mod_ref, g_ref, w1_ref, w2_ref, o_in_ref, wo_ref, out_ref = refs
    else:
        x_ref, mod_ref, g_ref, w1_ref, w2_ref, out_ref = refs
    d = x_ref.shape[1]
    dff = w1_ref.shape[1]
    x = x_ref[...]
    if has_proj:
        g1 = mod_ref[:, 2 * d:3 * d]
        o = jnp.concatenate([o_in_ref[p] for p in range(o_in_ref.shape[0])], axis=-1)
        x = x + g1 * jnp.dot(o, wo_ref[...], preferred_element_type=F32)
    sh2 = mod_ref[:, 3 * d:4 * d]
    sc2 = mod_ref[:, 4 * d:5 * d]
    g2 = mod_ref[:, 5 * d:6 * d]
    hb = (_rmsnorm(x, g_ref[...]) * (1.0 + sc2) + sh2).astype(BF16)
    acc = jnp.zeros(x.shape, F32)
    for c in range(dff // fc):
        hid = jnp.dot(hb, w1_ref[:, c * fc:(c + 1) * fc], preferred_element_type=F32)
        hid = jnp.square(jnp.maximum(hid, 0.0))
        acc = acc + jnp.dot(hid.astype(BF16), w2_ref[c * fc:(c + 1) * fc, :],
                            preferred_element_type=F32)
    out_ref[...] = x + g2 * acc


def _ffn_call(x, mod_l, g, w1, w2, o_pairs=None, w_o=None, *, rf=512, fc=1024):
    b, s, d = x.shape
    has_proj = o_pairs is not None
    args = [x, mod_l, g.reshape(1, -1), w1.astype(BF16), w2.astype(BF16)]
    in_specs = [pl.BlockSpec((None, rf, d), lambda bi, i: (bi, i, 0)),
                pl.BlockSpec((None, 1, mod_l.shape[-1]), lambda bi, i: (bi, 0, 0)),
                _const_spec((1, d)), _const_spec(w1.shape), _const_spec(w2.shape)]
    if has_proj:
        npair = o_pairs.shape[1]
        args += [o_pairs, w_o.astype(BF16)]
        in_specs += [pl.BlockSpec((None, npair, rf, LANES), lambda bi, i: (bi, 0, i, 0)),
                     _const_spec(w_o.shape)]
    return pl.pallas_call(
        functools.partial(_ffn_kernel, has_proj=has_proj, fc=fc),
        grid=(b, s // rf),
        in_specs=in_specs,
        out_specs=pl.BlockSpec((None, rf, d), lambda bi, i: (bi, i, 0)),
        out_shape=jax.ShapeDtypeStruct((b, s, d), F32),
        compiler_params=pltpu.CompilerParams(
            dimension_semantics=("arbitrary", "arbitrary"), vmem_limit_bytes=VMEM_LIMIT),
        name="ffn_proj" if has_proj else "ffn",
    )(*args)


def _rot_half(x, lane):
    return jnp.where((lane & 63) < 32, pltpu.roll(x, 96, 1), pltpu.roll(x, 32, 1))


def _dsa_in_kernel(x_ref, mod_ref, g_ref, w_ref, qg_ref, kg_ref, e_ref, cos_ref, sin_ref,
                   q_ref, qi_ref, kt_ref, kit_ref, v_ref, wi_ref, *, nq, nqi, hd, idx_heads):
    d = x_ref.shape[1]
    tq = x_ref.shape[0]
    x = x_ref[...]
    sh1 = mod_ref[:, 0:d]
    sc1 = mod_ref[:, d:2 * d]
    h = _rmsnorm(x, g_ref[...]) * (1.0 + sc1) + sh1
    u = jnp.dot(h.astype(BF16), w_ref[...], preferred_element_type=F32)
    lane = lax.broadcasted_iota(I32, (tq, LANES), 1)
    cos = cos_ref[...]
    sin = sin_ref[...]
    qg = qg_ref[...]
    e = e_ref[...]
    qscale = hd ** -0.5

    for p in range(nq):
        t = u[:, p * LANES:(p + 1) * LANES]
        t2 = t * t
        hi = t2.astype(BF16)
        lo = (t2 - hi.astype(F32)).astype(BF16)
        ss = (jnp.dot(hi, e, preferred_element_type=F32) + jnp.dot(lo, e, preferred_element_type=F32))
        tn = t * lax.rsqrt(ss * (1.0 / hd) + RMS_EPS) * qg
        q_ref[p] = ((tn * cos + _rot_half(tn, lane) * sin) * qscale).astype(BF16)

    for p in range(nqi):
        t = u[:, (nq + p) * LANES:(nq + p + 1) * LANES]
        qi_ref[p] = ((t * cos + _rot_half(t, lane) * sin) * qscale).astype(BF16)

    kk = u[:, (nq + nqi) * LANES:(nq + nqi + 1) * LANES]
    left = lane < hd
    ssk = jnp.sum(jnp.where(left, kk * kk, 0.0), axis=-1, keepdims=True)
    fac = jnp.where(left, lax.rsqrt(ssk * (1.0 / hd) + RMS_EPS) * kg_ref[...], 1.0)
    kn = kk * fac
    krt = (kn * cos + _rot_half(kn, lane) * sin).T.astype(BF16)
    zeros = jnp.zeros((hd, tq), BF16)
    kt_ref[0:hd, :] = krt[0:hd]
    kt_ref[hd:2 * hd, :] = zeros
    kt_ref[2 * hd:3 * hd, :] = krt[0:hd]
    kit_ref[0:hd, :] = krt[hd:2 * hd]
    kit_ref[hd:2 * hd, :] = zeros
    kit_ref[2 * hd:3 * hd, :] = krt[hd:2 * hd]

    vw = u[:, (nq + nqi + 1) * LANES:(nq + nqi + 2) * LANES]
    v_ref[...] = jnp.where(left, vw, 1.0).astype(BF16)
    wi_ref[...] = vw * (idx_heads ** -0.5)


def _dsa_in_call(x, mod_l, g, w_in, q_g, k_g, *, n_heads, hd, idx_heads, tq=512):
    b, s, d = x.shape
    assert 2 * hd == LANES, "two heads per lane tile"
    nq = n_heads * hd // LANES
    nqi = idx_heads * hd // LANES
    o_k = n_heads * hd
    o_v = o_k + hd
    o_qi = o_v + hd
    o_ki = o_qi + idx_heads * hd
    o_wi = o_ki + hd
    pad = LANES - hd - idx_heads
    w = jnp.concatenate([w_in[:, :o_k], w_in[:, o_qi:o_ki], w_in[:, o_k:o_v], w_in[:, o_ki:o_wi],
                         w_in[:, o_v:o_qi], w_in[:, o_wi:], jnp.zeros((d, pad), w_in.dtype)],
                        axis=1).astype(BF16)
    ncol = w.shape[1]
    inv = ROPE_THETA ** (-jnp.arange(0, hd, 2, dtype=F32) / hd)
    ang = jnp.arange(s, dtype=F32)[:, None] * inv[None, :]
    cos_t = jnp.tile(jnp.cos(ang), (1, 4))
    sin_h = jnp.sin(ang)
    sin_t = jnp.tile(jnp.concatenate([-sin_h, sin_h], axis=1), (1, 2))
    head_of_lane = jnp.arange(LANES) // hd
    e = (head_of_lane[:, None] == head_of_lane[None, :]).astype(BF16)
    qg = jnp.tile(q_g, 2).reshape(1, LANES)
    kg = jnp.concatenate([k_g, jnp.ones((hd,), k_g.dtype)]).reshape(1, LANES)

    row_spec = pl.BlockSpec((None, tq, LANES), lambda bi, i: (bi, i, 0))
    col_spec = pl.BlockSpec((None, 3 * hd, tq), lambda bi, i: (bi, 0, i))
    return pl.pallas_call(
        functools.partial(_dsa_in_kernel, nq=nq, nqi=nqi, hd=hd, idx_heads=idx_heads),
        grid=(b, s // tq),
        in_specs=[pl.BlockSpec((None, tq, d), lambda bi, i: (bi, i, 0)),
                  pl.BlockSpec((None, 1, mod_l.shape[-1]), lambda bi, i: (bi, 0, 0)),
                  _const_spec((1, d)), _const_spec((d, ncol)), _const_spec((1, LANES)),
                  _const_spec((1, LANES)), _const_spec((LANES, LANES)),
                  pl.BlockSpec((tq, LANES), lambda bi, i: (i, 0)),
                  pl.BlockSpec((tq, LANES), lambda bi, i: (i, 0))],
        out_specs=[pl.BlockSpec((None, nq, tq, LANES), lambda bi, i: (bi, 0, i, 0)),
                   pl.BlockSpec((None, nqi, tq, LANES), lambda bi, i: (bi, 0, i, 0)),
                   col_spec, col_spec, row_spec, row_spec],
        out_shape=[jax.ShapeDtypeStruct((b, nq, s, LANES), BF16),
                   jax.ShapeDtypeStruct((b, nqi, s, LANES), BF16),
                   jax.ShapeDtypeStruct((b, 3 * hd, s), BF16),
                   jax.ShapeDtypeStruct((b, 3 * hd, s), BF16),
                   jax.ShapeDtypeStruct((b, s, LANES), BF16),
                   jax.ShapeDtypeStruct((b, s, LANES), F32)],
        compiler_params=pltpu.CompilerParams(
            dimension_semantics=("arbitrary", "arbitrary"), vmem_limit_bytes=VMEM_LIMIT),
        name="dsa_in",
    )(x, mod_l, g.reshape(1, -1), w, qg, kg, e, cos_t, sin_t)


def _dsa_attn_kernel(qi_ref, wi_ref, q_ref, kt_ref, kit_ref, v_ref, o_ref, sc_s, bias_s, jc_s,
                     m_s, acc_s, *, kc, topk, hd, idx_heads, idx_bits, group, max_groups):
    qb = wi_ref.shape[0]
    npair = q_ref.shape[0]
    nt = kc // LANES
    j = pl.program_id(1)
    nk = (j * qb) // kc + 1
    half = kc // 2
    col = lax.broadcasted_iota(I32, (qb, kc), 1)
    col_h = lax.broadcasted_iota(I32, (qb, half), 1)
    row_h = j * qb + lax.broadcasted_iota(I32, (qb, half), 0)
    lane = lax.broadcasted_iota(I32, (qb, LANES), 1)
    left = lane < hd
    wi = wi_ref[...]
    kf = float(topk)

    def lanes_all(x):
        return jnp.concatenate([x] * nt, axis=1)

    def fold(x, op):
        r = x[:, 0:LANES]
        for t in range(1, nt):
            r = op(r, x[:, t * LANES:(t + 1) * LANES])
        return r

    def row_all(x, red):
        return jnp.broadcast_to(red(x, axis=1, keepdims=True), (qb, LANES))


    def score_chunk(ci, _):
        for hf in range(2):
            c0 = pl.multiple_of(ci * kc + hf * half, half)
            kit = (kit_ref[0:LANES, pl.ds(c0, half)], kit_ref[hd:hd + LANES, pl.ds(c0, half)])
            acc = jnp.zeros((qb, half), F32)
            for hh in range(idx_heads):
                lg = jnp.dot(qi_ref[hh // 2], kit[hh % 2], preferred_element_type=F32)
                acc = acc + wi[:, hd + hh:hd + hh + 1] * jnp.maximum(lg, 0.0)
            causal = (c0 + col_h) <= row_h
            sc_s[:, pl.ds(c0, half)] = jnp.where(causal, acc, -jnp.inf)
        return 0

    lax.fori_loop(0, nk, score_chunk, 0)

    def reduce_chunks(fn, op, init):
        def body(ci, acc):
            c0 = pl.multiple_of(ci * kc, kc)
            return op(acc, fold(fn(sc_s[:, pl.ds(c0, kc)], c0), op))
        return lax.fori_loop(0, nk, body, jnp.full((qb, LANES), init, F32))

    def count(pred):
        return row_all(reduce_chunks(lambda blk, c0: jnp.where(pred(blk, c0), 1.0, 0.0),
                                     jnp.add, 0.0), jnp.sum)

    mx = row_all(reduce_chunks(lambda blk, c0: blk, jnp.maximum, -jnp.inf), jnp.max)
    mn = row_all(reduce_chunks(lambda blk, c0: jnp.where(blk > -jnp.inf, blk, jnp.inf),
                               jnp.minimum, jnp.inf), jnp.min)
    n_causal = (j * qb + lax.broadcasted_iota(I32, (qb, LANES), 0) + 1).astype(F32)

    def bisect(_, st):
        lo, hi, clo, chi = st
        mid = 0.5 * lo + 0.5 * hi
        midt = lanes_all(mid)
        c = count(lambda blk, c0: blk >= midt)
        ge = c >= kf
        return (jnp.where(ge, mid, lo), jnp.where(ge, hi, mid),
                jnp.where(ge, c, clo), jnp.where(ge, chi, c))

    def open_rows(clo, tied):
        return jnp.max(jnp.where(clo > kf, 1.0 - tied, 0.0)) > 0.0

    def search_group(st):
        g, _, lo, hi, clo, chi, tied = st
        lo, hi, clo, chi = lax.fori_loop(0, group, bisect, (lo, hi, clo, chi))

        def tie_check(tied):
            lot, hit = lanes_all(lo), lanes_all(hi)
            a = row_all(reduce_chunks(lambda blk, c0: jnp.where(blk >= lot, blk, jnp.inf),
                                      jnp.minimum, jnp.inf), jnp.min)
            b = row_all(reduce_chunks(lambda blk, c0: jnp.where(blk < hit, blk, -jnp.inf),
                                      jnp.maximum, -jnp.inf), jnp.max)
            return jnp.where(jnp.logical_and(clo > kf, a == b), 1.0, tied)

        tied = lax.cond(jnp.logical_and(g >= 2, open_rows(clo, tied)), tie_check, lambda t: t, tied)
        return (g + 1, open_rows(clo, tied).astype(I32), lo, hi, clo, chi, tied)

    zero = jnp.zeros((qb, LANES), F32)
    _, _, lo, hi, clo, chi, _ = lax.while_loop(
        lambda st: jnp.logical_and(st[1] > 0, st[0] < max_groups), search_group,
        (jnp.int32(0), jnp.int32(1), mn, mx + (jnp.abs(mx) * 1e-6 + 1e-30), n_causal, zero, zero))
    lot, hit = lanes_all(lo), lanes_all(hi)

    excess = clo > kf
    need = kf - chi
    jc_s[...] = jnp.full((qb, LANES), 2 ** 30, I32)

    @pl.when(jnp.max(jnp.where(excess, 1.0, 0.0)) > 0.0)
    def _():
        def j_bit(bi, jv):
            cand = jv | lax.shift_left(jnp.int32(1), idx_bits - 1 - bi)
            candt = jnp.concatenate([cand] * nt, axis=1)
            f = count(lambda blk, c0: jnp.logical_and(
                jnp.logical_and(blk >= lot, blk < hit), (c0 + col) < candt))
            return jnp.where(f < need, cand, jv)
        jv = lax.fori_loop(0, idx_bits, j_bit, jnp.zeros((qb, LANES), I32))
        jc_s[...] = jnp.where(excess, jv, jc_s[...])

    jcut = jnp.concatenate([jc_s[...]] * nt, axis=1)

    def bias_chunk(ci, _):
        c0 = pl.multiple_of(ci * kc, kc)
        blk = sc_s[:, pl.ds(c0, kc)]
        tie = jnp.where((c0 + col) <= jcut, 0.0, NEG_BIG)
        bias_s[:, pl.ds(c0, kc)] = jnp.where(blk >= hit, 0.0, jnp.where(blk >= lot, tie, NEG_BIG))
        return 0

    lax.fori_loop(0, nk, bias_chunk, 0)

    nh = 2 * npair
    m_s[...] = jnp.full(m_s.shape, NEG_BIG, F32)
    acc_s[...] = jnp.zeros(acc_s.shape, F32)

    def scores(ci, h):
        c0 = pl.multiple_of(ci * kc, kc)
        kt = kt_ref[(h % 2) * hd:(h % 2) * hd + LANES, pl.ds(c0, kc)]
        return jnp.dot(q_ref[h // 2], kt, preferred_element_type=F32) + bias_s[:, pl.ds(c0, kc)]

    def max_step(ci, _):
        for h in range(nh):
            m_s[h] = jnp.maximum(m_s[h], fold(scores(ci, h), jnp.maximum))
        return 0

    lax.fori_loop(0, nk, max_step, 0)
    for h in range(nh):
        m_s[h] = row_all(m_s[h], jnp.max)

    def pv_step(ci, _):
        vv = v_ref[pl.ds(pl.multiple_of(ci * kc, kc), kc), :]
        s_next = scores(ci, 0)
        for h in range(nh):
            s_cur = s_next
            if h + 1 < nh:
                s_next = scores(ci, h + 1)
            pr = jnp.exp(s_cur - lanes_all(m_s[h]))
            acc_s[h] += jnp.dot(pr.astype(BF16), vv, preferred_element_type=F32)
        return 0

    lax.fori_loop(0, nk, pv_step, 0)

    for p in range(npair):
        a0, a1 = acc_s[2 * p], acc_s[2 * p + 1]
        o0 = a0 / pltpu.roll(a0, hd, 1)
        o1 = a1 / pltpu.roll(a1, hd, 1)
        o_ref[p] = jnp.where(left, o0, pltpu.roll(o1, hd, 1)).astype(BF16)


def _dsa_attn_call(q, qi, kt, kit, v, wi, *, hd, idx_heads, topk, qb=128, kc=512):
    b, npair, s, _ = q.shape
    nqi = qi.shape[1]
    assert kc % qb == 0 and s % kc == 0
    idx_bits = max(1, (s - 1).bit_length())
    kv_spec = pl.BlockSpec((None, 3 * hd, s), lambda bi, i: (bi, 0, 0))
    return pl.pallas_call(
        functools.partial(_dsa_attn_kernel, kc=kc, topk=topk, hd=hd, idx_heads=idx_heads,
                          idx_bits=idx_bits, group=8, max_groups=48),
        grid=(b, s // qb),
        in_specs=[pl.BlockSpec((None, nqi, qb, LANES), lambda bi, i: (bi, 0, i, 0)),
                  pl.BlockSpec((None, qb, LANES), lambda bi, i: (bi, i, 0)),
                  pl.BlockSpec((None, npair, qb, LANES), lambda bi, i: (bi, 0, i, 0)),
                  kv_spec, kv_spec,
                  pl.BlockSpec((None, s, LANES), lambda bi, i: (bi, 0, 0))],
        out_specs=pl.BlockSpec((None, npair, qb, LANES), lambda bi, i: (bi, 0, i, 0)),
        out_shape=jax.ShapeDtypeStruct((b, npair, s, LANES), BF16),
        scratch_shapes=[pltpu.VMEM((qb, s), F32), pltpu.VMEM((qb, s), F32),
                        pltpu.VMEM((qb, LANES), I32), pltpu.VMEM((2 * npair, qb, LANES), F32),
                        pltpu.VMEM((2 * npair, qb, LANES), F32)],
        compiler_params=pltpu.CompilerParams(
            dimension_semantics=("arbitrary", "arbitrary"), vmem_limit_bytes=VMEM_LIMIT),
        name="dsa_attn",
    )(qi, wi, q, kt, kit, v)


def kernel(x, c, norm_mix_g, norm_ffn_g, ada_w, ada_b, a_w_in, a_conv_w, a_conv_b, a_gate_r_w,
           a_gate_r_b, a_gate_i_w, a_gate_i_b, a_lambda, a_w_out, b_w_in, b_q_norm_g, b_k_norm_g,
           b_w_out, ffn_w1, ffn_w2):
    b, s, d = x.shape
    assert b == SUBLANES, "the recurrence kernel keeps the batch on the sublane axis"
    hd = b_q_norm_g.shape[-1]
    n_heads = b_w_out.shape[1] // hd
    idx_heads = (b_w_in.shape[-1] - n_heads * hd - 3 * hd) // (hd + 1)
    topk = min(TOPK_MAX, s // 4)
    depth = ada_w.shape[0]

    mod = _mod_call(c, ada_w, ada_b)
    mod_rows = mod.reshape(depth, b, 1, mod.shape[-1])

    for i in range(depth):
        j = i // 2
        if i % 2 == 0:
            x_sb = jnp.transpose(x, (1, 0, 2)).reshape(s * b, d)
            x_sb = _rglru_call(x_sb, mod[i], norm_mix_g[i], a_w_in[j], a_conv_w[j], a_conv_b[j],
                               a_gate_r_w[j], a_gate_r_b[j], a_gate_i_w[j], a_gate_i_b[j],
                               a_lambda[j], a_w_out[j])
            x = jnp.transpose(x_sb.reshape(s, b, d), (1, 0, 2))
            x = _ffn_call(x, mod_rows[i], norm_ffn_g[i], ffn_w1[i], ffn_w2[i])
        else:
            q, qi, kt, kit, v, wi = _dsa_in_call(x, mod_rows[i], norm_mix_g[i], b_w_in[j],
                                                 b_q_norm_g[j], b_k_norm_g[j], n_heads=n_heads,
                                                 hd=hd, idx_heads=idx_heads)
            o = _dsa_attn_call(q, qi, kt, kit, v, wi, hd=hd, idx_heads=idx_heads, topk=topk)
            x = _ffn_call(x, mod_rows[i], norm_ffn_g[i], ffn_w1[i], ffn_w2[i], o_pairs=o,
                          w_o=b_w_out[j])
    return x
```

```python
import functools

import jax
import jax.numpy as jnp
from jax import lax
from jax.experimental import pallas as pl
from jax.experimental.pallas import tpu as pltpu

F32 = jnp.float32
BF16 = jnp.bfloat16
I32 = jnp.int32

RMS_EPS = 1e-6
ROPE_THETA = 10000.0
LRU_C = 8.0
TOPK_MAX = 256
N_MOD = 6

LANES = 128
SUBLANES = 8
VMEM_LIMIT = 56 * 1024 * 1024

NEG_BIG = -1e30
EXP_SAFE_BOUND = 70.0


def _const_spec(shape):
    nd = len(shape)
    return pl.BlockSpec(shape, lambda *_: (0,) * nd, pipeline_mode=pl.Buffered(1))


def _rmsnorm(x, g):
    return x * lax.rsqrt(jnp.mean(x * x, axis=-1, keepdims=True) + RMS_EPS) * g


def _mod_kernel(c_ref, w_ref, b_ref, o_ref):
    c = c_ref[...]
    cond = c * jax.nn.sigmoid(c)
    o_ref[0] = jnp.dot(cond.astype(BF16), w_ref[0].astype(BF16),
                       preferred_element_type=F32) + b_ref[0]


def _mod_call(c, ada_w, ada_b):
    depth, d, n = ada_w.shape
    b = c.shape[0]
    tn = 1536
    return pl.pallas_call(
        _mod_kernel,
        grid=(depth, n // tn),
        in_specs=[pl.BlockSpec((b, d), lambda l, j: (0, 0)),
                  pl.BlockSpec((1, d, tn), lambda l, j: (l, 0, j)),
                  pl.BlockSpec((1, 1, tn), lambda l, j: (l, 0, j))],
        out_specs=pl.BlockSpec((1, b, tn), lambda l, j: (l, 0, j)),
        out_shape=jax.ShapeDtypeStruct((depth, b, n), F32),
        compiler_params=pltpu.CompilerParams(
            dimension_semantics=("arbitrary", "arbitrary"), vmem_limit_bytes=VMEM_LIMIT),
        name="adaln_mod",
    )(c, ada_w, ada_b.reshape(depth, 1, n))


def _rglru_kernel(x_ref, mod_ref, g_ref, win_ref, cw_ref, cb_ref, wr_ref, br_ref, wi_ref, bi_ref,
                  lam_ref, wout_ref, o_ref, u_s, xbuf, a_s, b_s, hc, *, ts, rc):
    nb = SUBLANES
    r = ts * nb
    d = x_ref.shape[1]
    dr = lam_ref.shape[1]
    nblk, blk, _ = wr_ref.shape
    cwid = cw_ref.shape[0]
    tail = (cwid - 1) * nb

    @pl.when(pl.program_id(0) == 0)
    def _():
        xbuf[0:tail, :] = jnp.zeros((tail, dr), F32)
        hc[...] = jnp.zeros((nb, dr), F32)

    x = x_ref[...]
    sh1 = mod_ref[:, 0:d]
    sc1 = mod_ref[:, d:2 * d]
    g1 = mod_ref[:, 2 * d:3 * d]
    hn = _rmsnorm(x, g_ref[...])
    h = (hn.reshape(ts, nb, d) * (1.0 + sc1)[None] + sh1[None]).reshape(r, d)
    u_s[...] = jnp.dot(h.astype(BF16), win_ref[...], preferred_element_type=F32)
    xbuf[tail:tail + r, :] = u_s[:, 0:dr]

    sp = LRU_C * jax.nn.softplus(-lam_ref[...])

    def gates(ci, _):
        r0 = pl.multiple_of(ci * rc, rc)
        xc = cb_ref[...] + cw_ref[0:1, :] * xbuf[pl.ds(r0, rc), :]
        for k in range(1, cwid):
            xc = xc + cw_ref[k:k + 1, :] * xbuf[pl.ds(r0 + k * nb, rc), :]
        xcb = xc.astype(BF16)
        for n in range(nblk):
            cs = slice(n * blk, (n + 1) * blk)
            xn = xcb[:, cs]
            rg = jax.nn.sigmoid(jnp.dot(xn, wr_ref[n], preferred_element_type=F32) + br_ref[:, cs])
            ig = jax.nn.sigmoid(jnp.dot(xn, wi_ref[n], preferred_element_type=F32) + bi_ref[:, cs])
            log_a = -(rg * sp[:, cs])
            a = jnp.exp(log_a)
            a_s[pl.ds(r0, rc), cs] = a
            b_s[pl.ds(r0, rc), cs] = jnp.sqrt(-jnp.tanh(log_a) * (a * a + 1.0)) * (ig * xc[:, cs])
        return 0

    lax.fori_loop(0, r // rc, gates, 0)
    xbuf[0:tail, :] = xbuf[r:r + tail, :]

    def step(t, hprev):
        r0 = pl.multiple_of(t * nb, nb)
        hnew = a_s[pl.ds(r0, nb), :] * hprev + b_s[pl.ds(r0, nb), :]
        b_s[pl.ds(r0, nb), :] = hnew
        return hnew

    hc[...] = lax.fori_loop(0, ts, step, hc[...], unroll=8)

    y = b_s[...] * jax.nn.gelu(u_s[:, dr:2 * dr])
    out = jnp.dot(y.astype(BF16), wout_ref[...], preferred_element_type=F32)
    o_ref[...] = x + (out.reshape(ts, nb, d) * g1[None]).reshape(r, d)


def _rglru_call(x_sb, mod0, g, w_in, conv_w, conv_b, wr, br, wi, bi, lam, w_out, *, ts=64, rc=128):
    rows, d = x_sb.shape
    r = ts * SUBLANES
    dr = lam.shape[-1]
    tail = (conv_w.shape[0] - 1) * SUBLANES
    row = lambda v: v.reshape(1, -1)
    args = (x_sb, mod0, row(g), w_in.astype(BF16), conv_w, row(conv_b), wr.astype(BF16), row(br),
            wi.astype(BF16), row(bi), row(lam), w_out.astype(BF16))
    in_specs = [pl.BlockSpec((r, d), lambda i: (i, 0))] + [_const_spec(a.shape) for a in args[1:]]
    return pl.pallas_call(
        functools.partial(_rglru_kernel, ts=ts, rc=rc),
        grid=(rows // r,),
        in_specs=in_specs,
        out_specs=pl.BlockSpec((r, d), lambda i: (i, 0)),
        out_shape=jax.ShapeDtypeStruct((rows, d), F32),
        scratch_shapes=[pltpu.VMEM((r, 2 * dr), F32), pltpu.VMEM((r + tail, dr), F32),
                        pltpu.VMEM((r, dr), F32), pltpu.VMEM((r, dr), F32),
                        pltpu.VMEM((SUBLANES, dr), F32)],
        compiler_params=pltpu.CompilerParams(
            dimension_semantics=("arbitrary",), vmem_limit_bytes=VMEM_LIMIT),
        name="rglru_mixer",
    )(*args)


def _ffn_kernel(*refs, has_proj, fc):
    if has_proj:
        x_ref, mod_ref, g_ref, w1_ref, w2_ref, o_in_ref, wo_ref, out_ref = refs
    else:
        x_ref, mod_ref, g_ref, w1_ref, w2_ref, out_ref = refs
    d = x_ref.shape[1]
    dff = w1_ref.shape[1]
    x = x_ref[...]
    if has_proj:
        g1 = mod_ref[:, 2 * d:3 * d]
        o = jnp.concatenate([o_in_ref[p] for p in range(o_in_ref.shape[0])], axis=-1)
        x = x + g1 * jnp.dot(o, wo_ref[...], preferred_element_type=F32)
    sh2 = mod_ref[:, 3 * d:4 * d]
    sc2 = mod_ref[:, 4 * d:5 * d]
    g2 = mod_ref[:, 5 * d:6 * d]
    hb = (_rmsnorm(x, g_ref[...]) * (1.0 + sc2) + sh2).astype(BF16)
    acc = jnp.zeros(x.shape, F32)
    for c in range(dff // fc):
        hid = jnp.dot(hb, w1_ref[:, c * fc:(c + 1) * fc], preferred_element_type=F32)
        hid = jnp.square(jnp.maximum(hid, 0.0))
        acc = acc + jnp.dot(hid.astype(BF16), w2_ref[c * fc:(c + 1) * fc, :],
                            preferred_element_type=F32)
    out_ref[...] = x + g2 * acc


def _ffn_call(x, mod_l, g, w1, w2, o_pairs=None, w_o=None, *, rf=512, fc=1024):
    b, s, d = x.shape
    has_proj = o_pairs is not None
    args = [x, mod_l, g.reshape(1, -1), w1.astype(BF16), w2.astype(BF16)]
    in_specs = [pl.BlockSpec((None, rf, d), lambda bi, i: (bi, i, 0)),
                pl.BlockSpec((None, 1, mod_l.shape[-1]), lambda bi, i: (bi, 0, 0)),
                _const_spec((1, d)), _const_spec(w1.shape), _const_spec(w2.shape)]
    if has_proj:
        npair = o_pairs.shape[1]
        args += [o_pairs, w_o.astype(BF16)]
        in_specs += [pl.BlockSpec((None, npair, rf, LANES), lambda bi, i: (bi, 0, i, 0)),
                     _const_spec(w_o.shape)]
    return pl.pallas_call(
        functools.partial(_ffn_kernel, has_proj=has_proj, fc=fc),
        grid=(b, s // rf),
        in_specs=in_specs,
        out_specs=pl.BlockSpec((None, rf, d), lambda bi, i: (bi, i, 0)),
        out_shape=jax.ShapeDtypeStruct((b, s, d), F32),
        compiler_params=pltpu.CompilerParams(
            dimension_semantics=("arbitrary", "arbitrary"), vmem_limit_bytes=VMEM_LIMIT),
        name="ffn_proj" if has_proj else "ffn",
    )(*args)


def _rot_half(x, lane):
    return jnp.where((lane & 63) < 32, pltpu.roll(x, 96, 1), pltpu.roll(x, 32, 1))


def _dsa_in_kernel(x_ref, mod_ref, g_ref, w_ref, qg_ref, kg_ref, e_ref, cos_ref, sin_ref,
                   q_ref, qi_ref, kt_ref, kit_ref, v_ref, wi_ref, *, nq, nqi, hd, idx_heads):
    d = x_ref.shape[1]
    tq = x_ref.shape[0]
    x = x_ref[...]
    sh1 = mod_ref[:, 0:d]
    sc1 = mod_ref[:, d:2 * d]
    h = _rmsnorm(x, g_ref[...]) * (1.0 + sc1) + sh1
    u = jnp.dot(h.astype(BF16), w_ref[...], preferred_element_type=F32)
    lane = lax.broadcasted_iota(I32, (tq, LANES), 1)
    cos = cos_ref[...]
    sin = sin_ref[...]
    qg = qg_ref[...]
    e = e_ref[...]
    qscale = hd ** -0.5

    for p in range(nq):
        t = u[:, p * LANES:(p + 1) * LANES]
        t2 = t * t
        hi = t2.astype(BF16)
        lo = (t2 - hi.astype(F32)).astype(BF16)
        ss = (jnp.dot(hi, e, preferred_element_type=F32) + jnp.dot(lo, e, preferred_element_type=F32))
        tn = t * lax.rsqrt(ss * (1.0 / hd) + RMS_EPS) * qg
        q_ref[p] = ((tn * cos + _rot_half(tn, lane) * sin) * qscale).astype(BF16)

    for p in range(nqi):
        t = u[:, (nq + p) * LANES:(nq + p + 1) * LANES]
        qi_ref[p] = ((t * cos + _rot_half(t, lane) * sin) * qscale).astype(BF16)

    kk = u[:, (nq + nqi) * LANES:(nq + nqi + 1) * LANES]
    left = lane < hd
    ssk = jnp.sum(jnp.where(left, kk * kk, 0.0), axis=-1, keepdims=True)
    fac = jnp.where(left, lax.rsqrt(ssk * (1.0 / hd) + RMS_EPS) * kg_ref[...], 1.0)
    kn = kk * fac
    krt = (kn * cos + _rot_half(kn, lane) * sin).T.astype(BF16)
    zeros = jnp.zeros((hd, tq), BF16)
    kt_ref[0:hd, :] = krt[0:hd]
    kt_ref[hd:2 * hd, :] = zeros
    kt_ref[2 * hd:3 * hd, :] = krt[0:hd]
    kit_ref[0:hd, :] = krt[hd:2 * hd]
    kit_ref[hd:2 * hd, :] = zeros
    kit_ref[2 * hd:3 * hd, :] = krt[hd:2 * hd]

    vw = u[:, (nq + nqi + 1) * LANES:(nq + nqi + 2) * LANES]
    v_ref[...] = jnp.where(left, vw, 1.0).astype(BF16)
    wi_ref[...] = vw * (idx_heads ** -0.5)


def _dsa_in_call(x, mod_l, g, w_in, q_g, k_g, *, n_heads, hd, idx_heads, tq=512):
    b, s, d = x.shape
    assert 2 * hd == LANES, "two heads per lane tile"
    nq = n_heads * hd // LANES
    nqi = idx_heads * hd // LANES
    o_k = n_heads * hd
    o_v = o_k + hd
    o_qi = o_v + hd
    o_ki = o_qi + idx_heads * hd
    o_wi = o_ki + hd
    pad = LANES - hd - idx_heads
    w = jnp.concatenate([w_in[:, :o_k], w_in[:, o_qi:o_ki], w_in[:, o_k:o_v], w_in[:, o_ki:o_wi],
                         w_in[:, o_v:o_qi], w_in[:, o_wi:], jnp.zeros((d, pad), w_in.dtype)],
                        axis=1).astype(BF16)
    ncol = w.shape[1]
    inv = ROPE_THETA ** (-jnp.arange(0, hd, 2, dtype=F32) / hd)
    ang = jnp.arange(s, dtype=F32)[:, None] * inv[None, :]
    cos_t = jnp.tile(jnp.cos(ang), (1, 4))
    sin_h = jnp.sin(ang)
    sin_t = jnp.tile(jnp.concatenate([-sin_h, sin_h], axis=1), (1, 2))
    head_of_lane = jnp.arange(LANES) // hd
    e = (head_of_lane[:, None] == head_of_lane[None, :]).astype(BF16)
    qg = jnp.tile(q_g, 2).reshape(1, LANES)
    kg = jnp.concatenate([k_g, jnp.ones((hd,), k_g.dtype)]).reshape(1, LANES)

    row_spec = pl.BlockSpec((None, tq, LANES), lambda bi, i: (bi, i, 0))
    col_spec = pl.BlockSpec((None, 3 * hd, tq), lambda bi, i: (bi, 0, i))
    return pl.pallas_call(
        functools.partial(_dsa_in_kernel, nq=nq, nqi=nqi, hd=hd, idx_heads=idx_heads),
        grid=(b, s // tq),
        in_specs=[pl.BlockSpec((None, tq, d), lambda bi, i: (bi, i, 0)),
                  pl.BlockSpec((None, 1, mod_l.shape[-1]), lambda bi, i: (bi, 0, 0)),
                  _const_spec((1, d)), _const_spec((d, ncol)), _const_spec((1, LANES)),
                  _const_spec((1, LANES)), _const_spec((LANES, LANES)),
                  pl.BlockSpec((tq, LANES), lambda bi, i: (i, 0)),
                  pl.BlockSpec((tq, LANES), lambda bi, i: (i, 0))],
        out_specs=[pl.BlockSpec((None, nq, tq, LANES), lambda bi, i: (bi, 0, i, 0)),
                   pl.BlockSpec((None, nqi, tq, LANES), lambda bi, i: (bi, 0, i, 0)),
                   col_spec, col_spec, row_spec, row_spec],
        out_shape=[jax.ShapeDtypeStruct((b, nq, s, LANES), BF16),
                   jax.ShapeDtypeStruct((b, nqi, s, LANES), BF16),
                   jax.ShapeDtypeStruct((b, 3 * hd, s), BF16),
                   jax.ShapeDtypeStruct((b, 3 * hd, s), BF16),
                   jax.ShapeDtypeStruct((b, s, LANES), BF16),
                   jax.ShapeDtypeStruct((b, s, LANES), F32)],
        compiler_params=pltpu.CompilerParams(
            dimension_semantics=("arbitrary", "arbitrary"), vmem_limit_bytes=VMEM_LIMIT),
        name="dsa_in",
    )(x, mod_l, g.reshape(1, -1), w, qg, kg, e, cos_t, sin_t)


def _dsa_attn_kernel(bnd_ref, qi_ref, wi_ref, q_ref, kt_ref, kit_ref, v_ref, o_ref, sc_s, bias_s,
                     jc_s, m_s, acc_s, *, kc, topk, hd, idx_heads, idx_bits, group, tie_from,
                     max_groups):
    qb = wi_ref.shape[0]
    npair = q_ref.shape[0]
    nt = kc // LANES
    j = pl.program_id(1)
    nk = (j * qb) // kc + 1
    half = kc // 2
    col = lax.broadcasted_iota(I32, (qb, kc), 1)
    col_h = lax.broadcasted_iota(I32, (qb, half), 1)
    row_h = j * qb + lax.broadcasted_iota(I32, (qb, half), 0)
    lane = lax.broadcasted_iota(I32, (qb, LANES), 1)
    left = lane < hd
    wi = wi_ref[...]
    kf = float(topk)

    def lanes_all(x):
        return jnp.concatenate([x] * nt, axis=1)

    def fold(x, op):
        r = x[:, 0:LANES]
        for t in range(1, nt):
            r = op(r, x[:, t * LANES:(t + 1) * LANES])
        return r

    def row_all(x, red):
        return jnp.broadcast_to(red(x, axis=1, keepdims=True), (qb, LANES))


    def score_chunk(ci, _):
        for hf in range(2):
            c0 = pl.multiple_of(ci * kc + hf * half, half)
            kit = (kit_ref[0:LANES, pl.ds(c0, half)], kit_ref[hd:hd + LANES, pl.ds(c0, half)])
            acc = jnp.zeros((qb, half), F32)
            for hh in range(idx_heads):
                lg = jnp.dot(qi_ref[hh // 2], kit[hh % 2], preferred_element_type=F32)
                acc = acc + wi[:, hd + hh:hd + hh + 1] * jnp.maximum(lg, 0.0)
            causal = (c0 + col_h) <= row_h
            sc_s[:, pl.ds(c0, half)] = jnp.where(causal, acc, -jnp.inf)
        return 0

    lax.fori_loop(0, nk, score_chunk, 0)

    def reduce_chunks(fn, op, init):
        def body(ci, acc):
            c0 = pl.multiple_of(ci * kc, kc)
            return op(acc, fold(fn(sc_s[:, pl.ds(c0, kc)], c0), op))
        return lax.fori_loop(0, nk, body, jnp.full((qb, LANES), init, F32))

    def count(pred):
        return row_all(reduce_chunks(lambda blk, c0: jnp.where(pred(blk, c0), 1.0, 0.0),
                                     jnp.add, 0.0), jnp.sum)

    mx = row_all(reduce_chunks(lambda blk, c0: blk, jnp.maximum, -jnp.inf), jnp.max)
    mn = row_all(reduce_chunks(lambda blk, c0: jnp.where(blk > -jnp.inf, blk, jnp.inf),
                               jnp.minimum, jnp.inf), jnp.min)
    n_causal = (j * qb + lax.broadcasted_iota(I32, (qb, LANES), 0) + 1).astype(F32)

    def bisect(_, st):
        lo, hi, clo, chi = st
        mid = 0.5 * lo + 0.5 * hi
        midt = lanes_all(mid)
        c = count(lambda blk, c0: blk >= midt)
        ge = c >= kf
        return (jnp.where(ge, mid, lo), jnp.where(ge, hi, mid),
                jnp.where(ge, c, clo), jnp.where(ge, chi, c))

    def open_rows(clo, tied):
        return jnp.max(jnp.where(clo > kf, 1.0 - tied, 0.0)) > 0.0

    def search_group(st):
        g, _, lo, hi, clo, chi, tied = st
        lo, hi, clo, chi = lax.fori_loop(0, group, bisect, (lo, hi, clo, chi))

        def tie_check(tied):
            lot, hit = lanes_all(lo), lanes_all(hi)
            a = row_all(reduce_chunks(lambda blk, c0: jnp.where(blk >= lot, blk, jnp.inf),
                                      jnp.minimum, jnp.inf), jnp.min)
            b = row_all(reduce_chunks(lambda blk, c0: jnp.where(blk < hit, blk, -jnp.inf),
                                      jnp.maximum, -jnp.inf), jnp.max)
            return jnp.where(jnp.logical_and(clo > kf, a == b), 1.0, tied)

        tied = lax.cond(jnp.logical_and((g + 1) * group >= tie_from, open_rows(clo, tied)),
                        tie_check, lambda t: t, tied)
        return (g + 1, open_rows(clo, tied).astype(I32), lo, hi, clo, chi, tied)

    zero = jnp.zeros((qb, LANES), F32)
    _, _, lo, hi, clo, chi, _ = lax.while_loop(
        lambda st: jnp.logical_and(st[1] > 0, st[0] < max_groups), search_group,
        (jnp.int32(0), open_rows(n_causal, zero).astype(I32), mn,
         mx + (jnp.abs(mx) * 1e-6 + 1e-30), n_causal, zero, zero))
    lot, hit = lanes_all(lo), lanes_all(hi)

    excess = clo > kf
    need = kf - chi
    jc_s[...] = jnp.full((qb, LANES), 2 ** 30, I32)

    @pl.when(jnp.max(jnp.where(excess, 1.0, 0.0)) > 0.0)
    def _():
        def tie_chunk(ci, _):
            c0 = pl.multiple_of(ci * kc, kc)
            blk = sc_s[:, pl.ds(c0, kc)]
            bias_s[:, pl.ds(c0, kc)] = jnp.where(blk >= hit, 0.0, jnp.where(blk >= lot, 1.0, 0.0))
            return 0

        lax.fori_loop(0, nk, tie_chunk, 0)

        def j_bit(bi, jv):
            cand = jv | lax.shift_left(jnp.int32(1), idx_bits - 1 - bi)
            candt = jnp.concatenate([cand] * nt, axis=1)

            def body(ci, acc):
                c0 = pl.multiple_of(ci * kc, kc)
                return acc + fold(jnp.where((c0 + col) < candt, bias_s[:, pl.ds(c0, kc)], 0.0),
                                  jnp.add)
            f = row_all(lax.fori_loop(0, nk, body, zero), jnp.sum)
            return jnp.where(f < need, cand, jv)
        jv = lax.fori_loop(0, idx_bits, j_bit, jnp.zeros((qb, LANES), I32))
        jc_s[...] = jnp.where(excess, jv, jc_s[...])

    jcut = jnp.concatenate([jc_s[...]] * nt, axis=1)
    bnd = bnd_ref[0]
    boff = 60.0 - bnd

    def bias_chunk(ci, _):
        c0 = pl.multiple_of(ci * kc, kc)
        blk = sc_s[:, pl.ds(c0, kc)]
        tie = jnp.where((c0 + col) <= jcut, boff, NEG_BIG)
        bias_s[:, pl.ds(c0, kc)] = jnp.where(blk >= hit, boff, jnp.where(blk >= lot, tie, NEG_BIG))
        return 0

    lax.fori_loop(0, nk, bias_chunk, 0)

    nh = 2 * npair
    acc_s[...] = jnp.zeros(acc_s.shape, F32)

    def scores(ci, h):
        c0 = pl.multiple_of(ci * kc, kc)
        kt = kt_ref[(h % 2) * hd:(h % 2) * hd + LANES, pl.ds(c0, kc)]
        return jnp.dot(q_ref[h // 2], kt, preferred_element_type=F32) + bias_s[:, pl.ds(c0, kc)]

    def max_step(ci, _):
        @pl.when(ci == 0)
        def _():
            m_s[...] = jnp.full(m_s.shape, NEG_BIG, F32)
        for h in range(nh):
            m_s[h] = jnp.maximum(m_s[h], fold(scores(ci, h), jnp.maximum))
        return 0

    fast = bnd <= EXP_SAFE_BOUND

    @pl.when(jnp.logical_not(fast))
    def _():
        lax.fori_loop(0, nk, max_step, 0)
        for h in range(nh):
            m_s[h] = row_all(m_s[h], jnp.max)

    def pv_step(ci, _, use_max):
        vv = v_ref[pl.ds(pl.multiple_of(ci * kc, kc), kc), :]
        s_next = scores(ci, 0)
        for h in range(nh):
            s_cur = s_next
            if h + 1 < nh:
                s_next = scores(ci, h + 1)
            pr = jnp.exp(s_cur - lanes_all(m_s[h]) if use_max else s_cur)
            acc_s[h] += jnp.dot(pr.astype(BF16), vv, preferred_element_type=F32)
        return 0

    @pl.when(fast)
    def _():
        lax.fori_loop(0, nk, functools.partial(pv_step, use_max=False), 0)

    @pl.when(jnp.logical_not(fast))
    def _():
        lax.fori_loop(0, nk, functools.partial(pv_step, use_max=True), 0)

    for p in range(npair):
        a0, a1 = acc_s[2 * p], acc_s[2 * p + 1]
        o0 = a0 / pltpu.roll(a0, hd, 1)
        o1 = a1 / pltpu.roll(a1, hd, 1)
        o_ref[p] = jnp.where(left, o0, pltpu.roll(o1, hd, 1)).astype(BF16)


def _dsa_attn_call(bnd, q, qi, kt, kit, v, wi, *, hd, idx_heads, topk, qb=256, kc=512):
    b, npair, s, _ = q.shape
    nqi = qi.shape[1]
    assert kc % qb == 0 and s % kc == 0
    idx_bits = max(1, (s - 1).bit_length())
    kv_spec = pl.BlockSpec((None, 3 * hd, s), lambda bi, i: (bi, 0, 0))
    return pl.pallas_call(
        functools.partial(_dsa_attn_kernel, kc=kc, topk=topk, hd=hd, idx_heads=idx_heads,
                          idx_bits=idx_bits, group=4, tie_from=16, max_groups=96),
        grid=(b, s // qb),
        in_specs=[pl.BlockSpec(memory_space=pltpu.SMEM),
                  pl.BlockSpec((None, nqi, qb, LANES), lambda bi, i: (bi, 0, i, 0)),
                  pl.BlockSpec((None, qb, LANES), lambda bi, i: (bi, i, 0)),
                  pl.BlockSpec((None, npair, qb, LANES), lambda bi, i: (bi, 0, i, 0)),
                  kv_spec, kv_spec,
                  pl.BlockSpec((None, s, LANES), lambda bi, i: (bi, 0, 0))],
        out_specs=pl.BlockSpec((None, npair, qb, LANES), lambda bi, i: (bi, 0, i, 0)),
        out_shape=jax.ShapeDtypeStruct((b, npair, s, LANES), BF16),
        scratch_shapes=[pltpu.VMEM((qb, s), F32), pltpu.VMEM((qb, s), F32),
                        pltpu.VMEM((qb, LANES), I32), pltpu.VMEM((2 * npair, qb, LANES), F32),
                        pltpu.VMEM((2 * npair, qb, LANES), F32)],
        compiler_params=pltpu.CompilerParams(
            dimension_semantics=("arbitrary", "arbitrary"), vmem_limit_bytes=VMEM_LIMIT),
        name="dsa_attn",
    )(bnd, qi, wi, q, kt, kit, v)


def kernel(x, c, norm_mix_g, norm_ffn_g, ada_w, ada_b, a_w_in, a_conv_w, a_conv_b, a_gate_r_w,
           a_gate_r_b, a_gate_i_w, a_gate_i_b, a_lambda, a_w_out, b_w_in, b_q_norm_g, b_k_norm_g,
           b_w_out, ffn_w1, ffn_w2):
    b, s, d = x.shape
    assert b == SUBLANES, "the recurrence kernel keeps the batch on the sublane axis"
    hd = b_q_norm_g.shape[-1]
    n_heads = b_w_out.shape[1] // hd
    idx_heads = (b_w_in.shape[-1] - n_heads * hd - 3 * hd) // (hd + 1)
    topk = min(TOPK_MAX, s // 4)
    depth = ada_w.shape[0]

    mod = _mod_call(c, ada_w, ada_b)
    mod_rows = mod.reshape(depth, b, 1, mod.shape[-1])

    for i in range(depth):
        j = i // 2
        if i % 2 == 0:
            x_sb = jnp.transpose(x, (1, 0, 2)).reshape(s * b, d)
            x_sb = _rglru_call(x_sb, mod[i], norm_mix_g[i], a_w_in[j], a_conv_w[j], a_conv_b[j],
                               a_gate_r_w[j], a_gate_r_b[j], a_gate_i_w[j], a_gate_i_b[j],
                               a_lambda[j], a_w_out[j])
            x = jnp.transpose(x_sb.reshape(s, b, d), (1, 0, 2))
            x = _ffn_call(x, mod_rows[i], norm_ffn_g[i], ffn_w1[i], ffn_w2[i])
        else:
            q, qi, kt, kit, v, wi = _dsa_in_call(x, mod_rows[i], norm_mix_g[i], b_w_in[j],
                                                 b_q_norm_g[j], b_k_norm_g[j], n_heads=n_heads,
                                                 hd=hd, idx_heads=idx_heads)
            bnd = (1.02 * hd * hd ** -0.5) * jnp.max(jnp.abs(b_q_norm_g[j])) * jnp.max(jnp.abs(b_k_norm_g[j]))
            o = _dsa_attn_call(bnd.reshape(1), q, qi, kt, kit, v, wi, hd=hd, idx_heads=idx_heads,
                               topk=topk)
            x = _ffn_call(x, mod_rows[i], norm_ffn_g[i], ffn_w1[i], ffn_w2[i], o_pairs=o,
                          w_o=b_w_out[j])
    return x
```

```python
import functools

import jax
import jax.numpy as jnp
from jax import lax
from jax.experimental import pallas as pl
from jax.experimental.pallas import tpu as pltpu

F32 = jnp.float32
BF16 = jnp.bfloat16
I32 = jnp.int32

RMS_EPS = 1e-6
ROPE_THETA = 10000.0
LRU_C = 8.0
TOPK_MAX = 256
N_MOD = 6

LANES = 128
SUBLANES = 8
VMEM_LIMIT = 56 * 1024 * 1024

NEG_BIG = -1e30
EXP_SAFE_BOUND = 70.0


def _const_spec(shape):
    nd = len(shape)
    return pl.BlockSpec(shape, lambda *_: (0,) * nd, pipeline_mode=pl.Buffered(1))


def _rmsnorm(x, g):
    return x * lax.rsqrt(jnp.mean(x * x, axis=-1, keepdims=True) + RMS_EPS) * g


def _mod_kernel(c_ref, w_ref, b_ref, o_ref):
    c = c_ref[...]
    cond = c * jax.nn.sigmoid(c)
    o_ref[0] = jnp.dot(cond.astype(BF16), w_ref[0].astype(BF16),
                       preferred_element_type=F32) + b_ref[0]


def _mod_call(c, ada_w, ada_b):
    depth, d, n = ada_w.shape
    b = c.shape[0]
    tn = 1536
    return pl.pallas_call(
        _mod_kernel,
        grid=(depth, n // tn),
        in_specs=[pl.BlockSpec((b, d), lambda l, j: (0, 0)),
                  pl.BlockSpec((1, d, tn), lambda l, j: (l, 0, j)),
                  pl.BlockSpec((1, 1, tn), lambda l, j: (l, 0, j))],
        out_specs=pl.BlockSpec((1, b, tn), lambda l, j: (l, 0, j)),
        out_shape=jax.ShapeDtypeStruct((depth, b, n), F32),
        compiler_params=pltpu.CompilerParams(
            dimension_semantics=("arbitrary", "arbitrary"), vmem_limit_bytes=VMEM_LIMIT),
        name="adaln_mod",
    )(c, ada_w, ada_b.reshape(depth, 1, n))


def _rglru_kernel(x_ref, mod_ref, g_ref, win_ref, cw_ref, cb_ref, wr_ref, br_ref, wi_ref, bi_ref,
                  lam_ref, wout_ref, o_ref, u_s, xbuf, a_s, b_s, hc, *, ts, rc):
    nb = SUBLANES
    r = ts * nb
    d = x_ref.shape[1]
    dr = lam_ref.shape[1]
    nblk, blk, _ = wr_ref.shape
    cwid = cw_ref.shape[0]
    tail = (cwid - 1) * nb

    @pl.when(pl.program_id(0) == 0)
    def _():
        xbuf[0:tail, :] = jnp.zeros((tail, dr), F32)
        hc[...] = jnp.zeros((nb, dr), F32)

    x = x_ref[...]
    sh1 = mod_ref[:, 0:d]
    sc1 = mod_ref[:, d:2 * d]
    g1 = mod_ref[:, 2 * d:3 * d]
    hn = _rmsnorm(x, g_ref[...])
    h = (hn.reshape(ts, nb, d) * (1.0 + sc1)[None] + sh1[None]).reshape(r, d)
    u_s[...] = jnp.dot(h.astype(BF16), win_ref[...], preferred_element_type=F32)
    xbuf[tail:tail + r, :] = u_s[:, 0:dr]

    sp = LRU_C * jax.nn.softplus(-lam_ref[...])

    def gates(ci, _):
        r0 = pl.multiple_of(ci * rc, rc)
        xc = cb_ref[...] + cw_ref[0:1, :] * xbuf[pl.ds(r0, rc), :]
        for k in range(1, cwid):
            xc = xc + cw_ref[k:k + 1, :] * xbuf[pl.ds(r0 + k * nb, rc), :]
        xcb = xc.astype(BF16)
        for n in range(nblk):
            cs = slice(n * blk, (n + 1) * blk)
            xn = xcb[:, cs]
            rg = jax.nn.sigmoid(jnp.dot(xn, wr_ref[n], preferred_element_type=F32) + br_ref[:, cs])
            ig = jax.nn.sigmoid(jnp.dot(xn, wi_ref[n], preferred_element_type=F32) + bi_ref[:, cs])
            log_a = -(rg * sp[:, cs])
            a = jnp.exp(log_a)
            a_s[pl.ds(r0, rc), cs] = a
            b_s[pl.ds(r0, rc), cs] = jnp.sqrt(-jnp.tanh(log_a) * (a * a + 1.0)) * (ig * xc[:, cs])
        return 0

    lax.fori_loop(0, r // rc, gates, 0)
    xbuf[0:tail, :] = xbuf[r:r + tail, :]

    def step(t, hprev):
        r0 = pl.multiple_of(t * nb, nb)
        hnew = a_s[pl.ds(r0, nb), :] * hprev + b_s[pl.ds(r0, nb), :]
        b_s[pl.ds(r0, nb), :] = hnew
        return hnew

    hc[...] = lax.fori_loop(0, ts, step, hc[...], unroll=8)

    y = b_s[...] * jax.nn.gelu(u_s[:, dr:2 * dr])
    out = jnp.dot(y.astype(BF16), wout_ref[...], preferred_element_type=F32)
    o_ref[...] = x + (out.reshape(ts, nb, d) * g1[None]).reshape(r, d)


def _rglru_call(x_sb, mod0, g, w_in, conv_w, conv_b, wr, br, wi, bi, lam, w_out, *, ts=64, rc=128):
    rows, d = x_sb.shape
    r = ts * SUBLANES
    dr = lam.shape[-1]
    tail = (conv_w.shape[0] - 1) * SUBLANES
    row = lambda v: v.reshape(1, -1)
    args = (x_sb, mod0, row(g), w_in.astype(BF16), conv_w, row(conv_b), wr.astype(BF16), row(br),
            wi.astype(BF16), row(bi), row(lam), w_out.astype(BF16))
    in_specs = [pl.BlockSpec((r, d), lambda i: (i, 0))] + [_const_spec(a.shape) for a in args[1:]]
    return pl.pallas_call(
        functools.partial(_rglru_kernel, ts=ts, rc=rc),
        grid=(rows // r,),
        in_specs=in_specs,
        out_specs=pl.BlockSpec((r, d), lambda i: (i, 0)),
        out_shape=jax.ShapeDtypeStruct((rows, d), F32),
        scratch_shapes=[pltpu.VMEM((r, 2 * dr), F32), pltpu.VMEM((r + tail, dr), F32),
                        pltpu.VMEM((r, dr), F32), pltpu.VMEM((r, dr), F32),
                        pltpu.VMEM((SUBLANES, dr), F32)],
        compiler_params=pltpu.CompilerParams(
            dimension_semantics=("arbitrary",), vmem_limit_bytes=VMEM_LIMIT),
        name="rglru_mixer",
    )(*args)


def _ffn_kernel(*refs, has_proj, fc):
    if has_proj:
        x_ref, mod_ref, g_ref, w1_ref, w2_ref, o_in_ref, wo_ref, out_ref = refs
    else:
        x_ref, mod_ref, g_ref, w1_ref, w2_ref, out_ref = refs
    d = x_ref.shape[1]
    dff = w1_ref.shape[1]
    x = x_ref[...]
    if has_proj:
        g1 = mod_ref[:, 2 * d:3 * d]
        o = jnp.concatenate([o_in_ref[p] for p in range(o_in_ref.shape[0])], axis=-1)
        x = x + g1 * jnp.dot(o, wo_ref[...], preferred_element_type=F32)
    sh2 = mod_ref[:, 3 * d:4 * d]
    sc2 = mod_ref[:, 4 * d:5 * d]
    g2 = mod_ref[:, 5 * d:6 * d]
    hb = (_rmsnorm(x, g_ref[...]) * (1.0 + sc2) + sh2).astype(BF16)
    acc = jnp.zeros(x.shape, F32)
    for c in range(dff // fc):
        hid = jnp.dot(hb, w1_ref[:, c * fc:(c + 1) * fc], preferred_element_type=F32)
        hid = jnp.square(jnp.maximum(hid, 0.0))
        acc = acc + jnp.dot(hid.astype(BF16), w2_ref[c * fc:(c + 1) * fc, :],
                            preferred_element_type=F32)
    out_ref[...] = x + g2 * acc


def _ffn_call(x, mod_l, g, w1, w2, o_pairs=None, w_o=None, *, rf=512, fc=1024):
    b, s, d = x.shape
    has_proj = o_pairs is not None
    args = [x, mod_l, g.reshape(1, -1), w1.astype(BF16), w2.astype(BF16)]
    in_specs = [pl.BlockSpec((None, rf, d), lambda bi, i: (bi, i, 0)),
                pl.BlockSpec((None, 1, mod_l.shape[-1]), lambda bi, i: (bi, 0, 0)),
                _const_spec((1, d)), _const_spec(w1.shape), _const_spec(w2.shape)]
    if has_proj:
        npair = o_pairs.shape[1]
        args += [o_pairs, w_o.astype(BF16)]
        in_specs += [pl.BlockSpec((None, npair, rf, LANES), lambda bi, i: (bi, 0, i, 0)),
                     _const_spec(w_o.shape)]
    return pl.pallas_call(
        functools.partial(_ffn_kernel, has_proj=has_proj, fc=fc),
        grid=(b, s // rf),
        in_specs=in_specs,
        out_specs=pl.BlockSpec((None, rf, d), lambda bi, i: (bi, i, 0)),
        out_shape=jax.ShapeDtypeStruct((b, s, d), F32),
        compiler_params=pltpu.CompilerParams(
            dimension_semantics=("arbitrary", "arbitrary"), vmem_limit_bytes=VMEM_LIMIT),
        name="ffn_proj" if has_proj else "ffn",
    )(*args)


def _rot_half(x, lane):
    return jnp.where((lane & 63) < 32, pltpu.roll(x, 96, 1), pltpu.roll(x, 32, 1))


def _dsa_in_kernel(x_ref, mod_ref, g_ref, w_ref, qg_ref, kg_ref, e_ref, cos_ref, sin_ref,
                   q_ref, qit_ref, kt_ref, ki_ref, v_ref, wit_ref, *, nq, nqi, hd, idx_heads):
    d = x_ref.shape[1]
    tq = x_ref.shape[0]
    x = x_ref[...]
    sh1 = mod_ref[:, 0:d]
    sc1 = mod_ref[:, d:2 * d]
    h = _rmsnorm(x, g_ref[...]) * (1.0 + sc1) + sh1
    u = jnp.dot(h.astype(BF16), w_ref[...], preferred_element_type=F32)
    lane = lax.broadcasted_iota(I32, (tq, LANES), 1)
    cos = cos_ref[...]
    sin = sin_ref[...]
    qg = qg_ref[...]
    e = e_ref[...]
    qscale = hd ** -0.5

    for p in range(nq):
        t = u[:, p * LANES:(p + 1) * LANES]
        t2 = t * t
        hi = t2.astype(BF16)
        lo = (t2 - hi.astype(F32)).astype(BF16)
        ss = (jnp.dot(hi, e, preferred_element_type=F32) + jnp.dot(lo, e, preferred_element_type=F32))
        tn = t * lax.rsqrt(ss * (1.0 / hd) + RMS_EPS) * qg
        q_ref[p] = ((tn * cos + _rot_half(tn, lane) * sin) * qscale).astype(BF16)

    for p in range(nqi):
        t = u[:, (nq + p) * LANES:(nq + p + 1) * LANES]
        qit_ref[p] = ((t * cos + _rot_half(t, lane) * sin) * qscale).T.astype(BF16)

    kk = u[:, (nq + nqi) * LANES:(nq + nqi + 1) * LANES]
    left = lane < hd
    ssk = jnp.sum(jnp.where(left, kk * kk, 0.0), axis=-1, keepdims=True)
    fac = jnp.where(left, lax.rsqrt(ssk * (1.0 / hd) + RMS_EPS) * kg_ref[...], 1.0)
    kn = kk * fac
    kr = kn * cos + _rot_half(kn, lane) * sin
    krt = kr.T.astype(BF16)
    zeros = jnp.zeros((hd, tq), BF16)
    kt_ref[0:hd, :] = krt[0:hd]
    kt_ref[hd:2 * hd, :] = zeros
    kt_ref[2 * hd:3 * hd, :] = krt[0:hd]
    ki_r = jnp.where(left, 0.0, kr)
    ki_ref[0] = pltpu.roll(ki_r, hd, 1).astype(BF16)
    ki_ref[1] = ki_r.astype(BF16)

    vw = u[:, (nq + nqi + 1) * LANES:(nq + nqi + 2) * LANES]
    v_ref[...] = jnp.where(left, vw, 1.0).astype(BF16)
    wit_ref[...] = (vw * (idx_heads ** -0.5)).T[hd:hd + idx_heads, :]


def _dsa_in_call(x, mod_l, g, w_in, q_g, k_g, *, n_heads, hd, idx_heads, tq=512):
    b, s, d = x.shape
    assert 2 * hd == LANES, "two heads per lane tile"
    nq = n_heads * hd // LANES
    nqi = idx_heads * hd // LANES
    o_k = n_heads * hd
    o_v = o_k + hd
    o_qi = o_v + hd
    o_ki = o_qi + idx_heads * hd
    o_wi = o_ki + hd
    pad = LANES - hd - idx_heads
    w = jnp.concatenate([w_in[:, :o_k], w_in[:, o_qi:o_ki], w_in[:, o_k:o_v], w_in[:, o_ki:o_wi],
                         w_in[:, o_v:o_qi], w_in[:, o_wi:], jnp.zeros((d, pad), w_in.dtype)],
                        axis=1).astype(BF16)
    ncol = w.shape[1]
    inv = ROPE_THETA ** (-jnp.arange(0, hd, 2, dtype=F32) / hd)
    ang = jnp.arange(s, dtype=F32)[:, None] * inv[None, :]
    cos_t = jnp.tile(jnp.cos(ang), (1, 4))
    sin_h = jnp.sin(ang)
    sin_t = jnp.tile(jnp.concatenate([-sin_h, sin_h], axis=1), (1, 2))
    head_of_lane = jnp.arange(LANES) // hd
    e = (head_of_lane[:, None] == head_of_lane[None, :]).astype(BF16)
    qg = jnp.tile(q_g, 2).reshape(1, LANES)
    kg = jnp.concatenate([k_g, jnp.ones((hd,), k_g.dtype)]).reshape(1, LANES)

    row_spec = pl.BlockSpec((None, tq, LANES), lambda bi, i: (bi, i, 0))
    col_spec = pl.BlockSpec((None, 3 * hd, tq), lambda bi, i: (bi, 0, i))
    return pl.pallas_call(
        functools.partial(_dsa_in_kernel, nq=nq, nqi=nqi, hd=hd, idx_heads=idx_heads),
        grid=(b, s // tq),
        in_specs=[pl.BlockSpec((None, tq, d), lambda bi, i: (bi, i, 0)),
                  pl.BlockSpec((None, 1, mod_l.shape[-1]), lambda bi, i: (bi, 0, 0)),
                  _const_spec((1, d)), _const_spec((d, ncol)), _const_spec((1, LANES)),
                  _const_spec((1, LANES)), _const_spec((LANES, LANES)),
                  pl.BlockSpec((tq, LANES), lambda bi, i: (i, 0)),
                  pl.BlockSpec((tq, LANES), lambda bi, i: (i, 0))],
        out_specs=[pl.BlockSpec((None, nq, tq, LANES), lambda bi, i: (bi, 0, i, 0)),
                   pl.BlockSpec((None, nqi, LANES, tq), lambda bi, i: (bi, 0, 0, i)),
                   col_spec,
                   pl.BlockSpec((None, 2, tq, LANES), lambda bi, i: (bi, 0, i, 0)),
                   row_spec,
                   pl.BlockSpec((None, idx_heads, tq), lambda bi, i: (bi, 0, i))],
        out_shape=[jax.ShapeDtypeStruct((b, nq, s, LANES), BF16),
                   jax.ShapeDtypeStruct((b, nqi, LANES, s), BF16),
                   jax.ShapeDtypeStruct((b, 3 * hd, s), BF16),
                   jax.ShapeDtypeStruct((b, 2, s, LANES), BF16),
                   jax.ShapeDtypeStruct((b, s, LANES), BF16),
                   jax.ShapeDtypeStruct((b, idx_heads, s), F32)],
        compiler_params=pltpu.CompilerParams(
            dimension_semantics=("arbitrary", "arbitrary"), vmem_limit_bytes=VMEM_LIMIT),
        name="dsa_in",
    )(x, mod_l, g.reshape(1, -1), w, qg, kg, e, cos_t, sin_t)


def _dsa_attn_kernel(bnd_ref, qit_ref, wit_ref, q_ref, kt_ref, ki_ref, v_ref, o_ref, sct_s, tie_s,
                     bias_s, jc_s, m_s, acc_s, *, kc, ks, topk, hd, idx_heads, idx_bits, group,
                     tie_from, max_groups):
    npair, qb, _ = q_ref.shape
    nt = kc // LANES
    nsub = SUBLANES
    ng = kc // nsub
    j = pl.program_id(1)
    nk = (j * qb) // kc + 1
    lane = lax.broadcasted_iota(I32, (qb, LANES), 1)
    left = lane < hd
    kf = float(topk)

    def lanes_all(x):
        return jnp.concatenate([x] * nt, axis=1)

    def fold(x, op):
        r = x[:, 0:LANES]
        for t in range(1, nt):
            r = op(r, x[:, t * LANES:(t + 1) * LANES])
        return r

    def row_all(x, red):
        return jnp.broadcast_to(red(x, axis=1, keepdims=True), (qb, LANES))

    qpos = j * qb + lax.broadcasted_iota(I32, (ks, qb), 1)
    krow = lax.broadcasted_iota(I32, (ks, qb), 0)

    def score_chunk(ci, _):
        for sub in range(kc // ks):
            k0 = pl.multiple_of(ci * kc + sub * ks, ks)
            kis = (ki_ref[0, pl.ds(k0, ks), :], ki_ref[1, pl.ds(k0, ks), :])
            acc = jnp.zeros((ks, qb), F32)
            for hh in range(idx_heads):
                lg = jnp.dot(kis[hh % 2], qit_ref[hh // 2], preferred_element_type=F32)
                acc = acc + wit_ref[hh:hh + 1, :] * jnp.maximum(lg, 0.0)
            sct_s[pl.ds(k0, ks), :] = jnp.where((k0 + krow) <= qpos, acc, -jnp.inf)
        return 0

    lax.fori_loop(0, nk, score_chunk, 0)

    def reduce_keys(src, fn, op, red, init):
        def body(ci, acc):
            c0 = pl.multiple_of(ci * kc, kc)
            val = fn(src[pl.ds(c0, kc), :].reshape(ng, nsub, qb), c0)
            n = ng
            while n > 1:
                n //= 2
                val = op(val[:n], val[n:])
            return op(acc, val[0])
        acc = lax.fori_loop(0, nk, body, jnp.full((nsub, qb), init, F32))
        return jnp.broadcast_to(red(acc, axis=0, keepdims=True), (nsub, qb))

    def count_ge(thr):
        return reduce_keys(sct_s, lambda blk, c0: jnp.where(blk >= thr[None], 1.0, 0.0),
                           jnp.add, jnp.sum, 0.0)

    mx = reduce_keys(sct_s, lambda blk, c0: blk, jnp.maximum, jnp.max, -jnp.inf)
    mn = reduce_keys(sct_s, lambda blk, c0: jnp.where(blk > -jnp.inf, blk, jnp.inf),
                     jnp.minimum, jnp.min, jnp.inf)
    n_causal = (j * qb + lax.broadcasted_iota(I32, (nsub, qb), 1) + 1).astype(F32)

    def bisect(_, st):
        lo, hi, clo, chi = st
        mid = 0.5 * lo + 0.5 * hi
        c = count_ge(mid)
        ge = c >= kf
        return (jnp.where(ge, mid, lo), jnp.where(ge, hi, mid),
                jnp.where(ge, c, clo), jnp.where(ge, chi, c))

    def open_rows(clo, tied):
        return jnp.max(jnp.where(clo > kf, 1.0 - tied, 0.0)) > 0.0

    def search_group(st):
        g, _, lo, hi, clo, chi, tied = st
        lo, hi, clo, chi = lax.fori_loop(0, group, bisect, (lo, hi, clo, chi))

        def tie_check(tied):
            a = reduce_keys(sct_s, lambda blk, c0: jnp.where(blk >= lo[None], blk, jnp.inf),
                            jnp.minimum, jnp.min, jnp.inf)
            b = reduce_keys(sct_s, lambda blk, c0: jnp.where(blk < hi[None], blk, -jnp.inf),
                            jnp.maximum, jnp.max, -jnp.inf)
            return jnp.where(jnp.logical_and(clo > kf, a == b), 1.0, tied)

        tied = lax.cond(jnp.logical_and((g + 1) * group >= tie_from, open_rows(clo, tied)),
                        tie_check, lambda t: t, tied)
        return (g + 1, open_rows(clo, tied).astype(I32), lo, hi, clo, chi, tied)

    zero = jnp.zeros((nsub, qb), F32)
    _, _, lo, hi, clo, chi, _ = lax.while_loop(
        lambda st: jnp.logical_and(st[1] > 0, st[0] < max_groups), search_group,
        (jnp.int32(0), open_rows(n_causal, zero).astype(I32), mn,
         mx + (jnp.abs(mx) * 1e-6 + 1e-30), n_causal, zero, zero))

    kidx = (nsub * lax.broadcasted_iota(I32, (ng, nsub, qb), 0)
            + lax.broadcasted_iota(I32, (ng, nsub, qb), 1))
    excess = clo > kf
    need = kf - chi
    jc_s[...] = jnp.full((nsub, qb), 2 ** 30, I32)

    @pl.when(jnp.max(jnp.where(excess, 1.0, 0.0)) > 0.0)
    def _():
        def tie_chunk(ci, _):
            c0 = pl.multiple_of(ci * kc, kc)
            blk = sct_s[pl.ds(c0, kc), :].reshape(ng, nsub, qb)
            tie = jnp.where(blk >= hi[None], 0.0, jnp.where(blk >= lo[None], 1.0, 0.0))
            tie_s[pl.ds(c0, kc), :] = tie.reshape(kc, qb)
            return 0

        lax.fori_loop(0, nk, tie_chunk, 0)

        def j_bit(bi, jv):
            cand = jv | lax.shift_left(jnp.int32(1), idx_bits - 1 - bi)
            f = reduce_keys(tie_s, lambda blk, c0: jnp.where((c0 + kidx) < cand[None], blk, 0.0),
                            jnp.add, jnp.sum, 0.0)
            return jnp.where(f < need, cand, jv)
        jv = lax.fori_loop(0, idx_bits, j_bit, jnp.zeros((nsub, qb), I32))
        jc_s[...] = jnp.where(excess, jv, jc_s[...])

    jcut = jc_s[...]
    bnd = bnd_ref[0]
    boff = 60.0 - bnd

    def bias_chunk(ci, _):
        c0 = pl.multiple_of(ci * kc, kc)
        blk = sct_s[pl.ds(c0, kc), :].reshape(ng, nsub, qb)
        tie = jnp.where((c0 + kidx) <= jcut[None], boff, NEG_BIG)
        bt = jnp.where(blk >= hi[None], boff, jnp.where(blk >= lo[None], tie, NEG_BIG))
        bias_s[:, pl.ds(c0, kc)] = bt.reshape(kc, qb).T
        return 0

    lax.fori_loop(0, nk, bias_chunk, 0)

    nh = 2 * npair
    acc_s[...] = jnp.zeros(acc_s.shape, F32)

    def scores(ci, h):
        c0 = pl.multiple_of(ci * kc, kc)
        kt = kt_ref[(h % 2) * hd:(h % 2) * hd + LANES, pl.ds(c0, kc)]
        return jnp.dot(q_ref[h // 2], kt, preferred_element_type=F32) + bias_s[:, pl.ds(c0, kc)]

    def max_step(ci, _):
        @pl.when(ci == 0)
        def _():
            m_s[...] = jnp.full(m_s.shape, NEG_BIG, F32)
        for h in range(nh):
            m_s[h] = jnp.maximum(m_s[h], fold(scores(ci, h), jnp.maximum))
        return 0

    fast = bnd <= EXP_SAFE_BOUND

    @pl.when(jnp.logical_not(fast))
    def _():
        lax.fori_loop(0, nk, max_step, 0)
        for h in range(nh):
            m_s[h] = row_all(m_s[h], jnp.max)

    def pv_step(ci, _, use_max):
        vv = v_ref[pl.ds(pl.multiple_of(ci * kc, kc), kc), :]
        s_next = scores(ci, 0)
        for h in range(nh):
            s_cur = s_next
            if h + 1 < nh:
                s_next = scores(ci, h + 1)
            pr = jnp.exp(s_cur - lanes_all(m_s[h]) if use_max else s_cur)
            acc_s[h] += jnp.dot(pr.astype(BF16), vv, preferred_element_type=F32)
        return 0

    @pl.when(fast)
    def _():
        lax.fori_loop(0, nk, functools.partial(pv_step, use_max=False), 0)

    @pl.when(jnp.logical_not(fast))
    def _():
        lax.fori_loop(0, nk, functools.partial(pv_step, use_max=True), 0)

    for p in range(npair):
        a0, a1 = acc_s[2 * p], acc_s[2 * p + 1]
        o0 = a0 / pltpu.roll(a0, hd, 1)
        o1 = a1 / pltpu.roll(a1, hd, 1)
        o_ref[p] = jnp.where(left, o0, pltpu.roll(o1, hd, 1)).astype(BF16)


def _dsa_attn_call(bnd, q, qit, kt, ki, v, wit, *, hd, idx_heads, topk, qb=256, kc=512, ks=128):
    b, npair, s, _ = q.shape
    nqi = qit.shape[1]
    assert kc % qb == 0 and s % kc == 0 and kc % ks == 0
    idx_bits = max(1, (s - 1).bit_length())
    return pl.pallas_call(
        functools.partial(_dsa_attn_kernel, kc=kc, ks=ks, topk=topk, hd=hd, idx_heads=idx_heads,
                          idx_bits=idx_bits, group=4, tie_from=16, max_groups=96),
        grid=(b, s // qb),
        in_specs=[pl.BlockSpec(memory_space=pltpu.SMEM),
                  pl.BlockSpec((None, nqi, LANES, qb), lambda bi, i: (bi, 0, 0, i)),
                  pl.BlockSpec((None, idx_heads, qb), lambda bi, i: (bi, 0, i)),
                  pl.BlockSpec((None, npair, qb, LANES), lambda bi, i: (bi, 0, i, 0)),
                  pl.BlockSpec((None, 3 * hd, s), lambda bi, i: (bi, 0, 0)),
                  pl.BlockSpec((None, 2, s, LANES), lambda bi, i: (bi, 0, 0, 0)),
                  pl.BlockSpec((None, s, LANES), lambda bi, i: (bi, 0, 0))],
        out_specs=pl.BlockSpec((None, npair, qb, LANES), lambda bi, i: (bi, 0, i, 0)),
        out_shape=jax.ShapeDtypeStruct((b, npair, s, LANES), BF16),
        scratch_shapes=[pltpu.VMEM((s, qb), F32), pltpu.VMEM((s, qb), F32),
                        pltpu.VMEM((qb, s), F32), pltpu.VMEM((SUBLANES, qb), I32),
                        pltpu.VMEM((2 * npair, qb, LANES), F32),
                        pltpu.VMEM((2 * npair, qb, LANES), F32)],
        compiler_params=pltpu.CompilerParams(
            dimension_semantics=("arbitrary", "arbitrary"), vmem_limit_bytes=VMEM_LIMIT),
        name="dsa_attn",
    )(bnd, qit, wit, q, kt, ki, v)


def kernel(x, c, norm_mix_g, norm_ffn_g, ada_w, ada_b, a_w_in, a_conv_w, a_conv_b, a_gate_r_w,
           a_gate_r_b, a_gate_i_w, a_gate_i_b, a_lambda, a_w_out, b_w_in, b_q_norm_g, b_k_norm_g,
           b_w_out, ffn_w1, ffn_w2):
    b, s, d = x.shape
    assert b == SUBLANES, "the recurrence kernel keeps the batch on the sublane axis"
    hd = b_q_norm_g.shape[-1]
    n_heads = b_w_out.shape[1] // hd
    idx_heads = (b_w_in.shape[-1] - n_heads * hd - 3 * hd) // (hd + 1)
    topk = min(TOPK_MAX, s // 4)
    depth = ada_w.shape[0]

    mod = _mod_call(c, ada_w, ada_b)
    mod_rows = mod.reshape(depth, b, 1, mod.shape[-1])

    for i in range(depth):
        j = i // 2
        if i % 2 == 0:
            x_sb = jnp.transpose(x, (1, 0, 2)).reshape(s * b, d)
            x_sb = _rglru_call(x_sb, mod[i], norm_mix_g[i], a_w_in[j], a_conv_w[j], a_conv_b[j],
                               a_gate_r_w[j], a_gate_r_b[j], a_gate_i_w[j], a_gate_i_b[j],
                               a_lambda[j], a_w_out[j])
            x = jnp.transpose(x_sb.reshape(s, b, d), (1, 0, 2))
            x = _ffn_call(x, mod_rows[i], norm_ffn_g[i], ffn_w1[i], ffn_w2[i])
        else:
            q, qit, kt, ki, v, wit = _dsa_in_call(x, mod_rows[i], norm_mix_g[i], b_w_in[j],
                                                 b_q_norm_g[j], b_k_norm_g[j], n_heads=n_heads,
                                                 hd=hd, idx_heads=idx_heads)
            bnd = (1.02 * hd * hd ** -0.5) * jnp.max(jnp.abs(b_q_norm_g[j])) * jnp.max(jnp.abs(b_k_norm_g[j]))
            o = _dsa_attn_call(bnd.reshape(1), q, qit, kt, ki, v, wit, hd=hd, idx_heads=idx_heads,
                               topk=topk)
            x = _ffn_call(x, mod_rows[i], norm_ffn_g[i], ffn_w1[i], ffn_w2[i], o_pairs=o,
                          w_o=b_w_out[j])
    return x
```

```python
import functools

import jax
import jax.numpy as jnp
from jax import lax
from jax.experimental import pallas as pl
from jax.experimental.pallas import tpu as pltpu

F32 = jnp.float32
BF16 = jnp.bfloat16
I32 = jnp.int32

RMS_EPS = 1e-6
ROPE_THETA = 10000.0
LRU_C = 8.0
TOPK_MAX = 256
N_MOD = 6

LANES = 128
SUBLANES = 8
VMEM_LIMIT = 56 * 1024 * 1024

NEG_BIG = -1e30
EXP_SAFE_BOUND = 70.0
LOG2E = 1.4426950408889634
GELU_C0 = 0.7978845608028654
GELU_C1 = 0.044715


def _const_spec(shape):
    nd = len(shape)
    return pl.BlockSpec(shape, lambda *_: (0,) * nd, pipeline_mode=pl.Buffered(1))


def _rmsnorm(x, g):
    return x * lax.rsqrt(jnp.mean(x * x, axis=-1, keepdims=True) + RMS_EPS) * g


def _mod_kernel(c_ref, w_ref, b_ref, o_ref):
    c = c_ref[...]
    cond = c * jax.nn.sigmoid(c)
    o_ref[0] = jnp.dot(cond.astype(BF16), w_ref[0].astype(BF16),
                       preferred_element_type=F32) + b_ref[0]


def _mod_call(c, ada_w, ada_b):
    depth, d, n = ada_w.shape
    b = c.shape[0]
    tn = 1536
    return pl.pallas_call(
        _mod_kernel,
        grid=(depth, n // tn),
        in_specs=[pl.BlockSpec((b, d), lambda l, j: (0, 0)),
                  pl.BlockSpec((1, d, tn), lambda l, j: (l, 0, j)),
                  pl.BlockSpec((1, 1, tn), lambda l, j: (l, 0, j))],
        out_specs=pl.BlockSpec((1, b, tn), lambda l, j: (l, 0, j)),
        out_shape=jax.ShapeDtypeStruct((depth, b, n), F32),
        compiler_params=pltpu.CompilerParams(
            dimension_semantics=("arbitrary", "arbitrary"), vmem_limit_bytes=VMEM_LIMIT),
        name="adaln_mod",
    )(c, ada_w, ada_b.reshape(depth, 1, n))


def _rglru_kernel(x_ref, mod_ref, g_ref, win_ref, cw_ref, cb_ref, wr_ref, br_ref, wi_ref, bi_ref,
                  lam_ref, wout_ref, o_ref, h_s, gb_s, xbuf, a_s, b_s, hc, *, rc):
    nb, ts, d = x_ref.shape
    r = ts * nb
    dr = lam_ref.shape[1]
    nblk, blk, _ = wr_ref.shape
    cwid = cw_ref.shape[0]
    tail = (cwid - 1) * nb

    @pl.when(pl.program_id(0) == 0)
    def _():
        xbuf[0:tail, :] = jnp.zeros((tail, dr), F32)
        hc[...] = jnp.zeros((nb, dr), F32)

    x = pltpu.einshape("bsd->sbd", x_ref[...])
    sh1 = mod_ref[:, 0:d]
    sc1 = mod_ref[:, d:2 * d]
    g1 = mod_ref[:, 2 * d:3 * d]
    h_s[...] = (_rmsnorm(x, g_ref[...]) * (1.0 + sc1)[None] + sh1[None]).reshape(r, d).astype(BF16)

    sp_h = (0.5 * LRU_C) * jax.nn.softplus(-lam_ref[...])
    br_h = 0.5 * br_ref[...]
    bi_h = 0.5 * bi_ref[...]

    def in_proj(c, n):
        rows = slice(c * rc, (c + 1) * rc)
        hb = h_s[rows, :]
        ux = jnp.dot(hb, win_ref[:, n * blk:(n + 1) * blk], preferred_element_type=F32)
        ug = jnp.dot(hb, win_ref[:, dr + n * blk:dr + (n + 1) * blk], preferred_element_type=F32)
        xbuf[tail + c * rc:tail + (c + 1) * rc, n * blk:(n + 1) * blk] = ux
        return ug

    def conv_and_gate_dots(c, n):
        r0 = c * rc
        cs = slice(n * blk, (n + 1) * blk)
        xc = cb_ref[:, cs] + cw_ref[0:1, cs] * xbuf[r0:r0 + rc, cs]
        for k in range(1, cwid):
            xc = xc + cw_ref[k:k + 1, cs] * xbuf[r0 + k * nb:r0 + k * nb + rc, cs]
        xn = xc.astype(BF16)
        return xc, (jnp.dot(xn, wr_ref[n], preferred_element_type=F32),
                    jnp.dot(xn, wi_ref[n], preferred_element_type=F32))

    def gate_tail(c, n, xc, pre, ug):
        ro = slice(c * rc, (c + 1) * rc)
        cs = slice(n * blk, (n + 1) * blk)
        p = sp_h[:, cs] + sp_h[:, cs] * jnp.tanh(0.5 * pre[0] + br_h[:, cs])
        ig = 0.5 + 0.5 * jnp.tanh(0.5 * pre[1] + bi_h[:, cs])
        a = jnp.exp2(p * (-LOG2E))
        a_s[ro, cs] = a
        b_s[ro, cs] = jnp.sqrt(jnp.tanh(p) * (a * a + 1.0)) * (ig * xc)
        t = jnp.tanh(ug * (GELU_C0 + (GELU_C0 * GELU_C1) * (ug * ug)))
        hu = 0.5 * ug
        gb_s[ro, cs] = hu + hu * t

    units = [(c, n) for c in range(r // rc) for n in range(nblk)]
    ug = in_proj(*units[0])
    for k, (c, n) in enumerate(units):
        xc, pre = conv_and_gate_dots(c, n)
        ug_next = in_proj(*units[k + 1]) if k + 1 < len(units) else None
        gate_tail(c, n, xc, pre, ug)
        ug = ug_next
    xbuf[0:tail, :] = xbuf[r:r + tail, :]

    def step(t, hprev):
        r0 = pl.multiple_of(t * nb, nb)
        hnew = a_s[pl.ds(r0, nb), :] * hprev + b_s[pl.ds(r0, nb), :]
        b_s[pl.ds(r0, nb), :] = hnew
        return hnew

    hc[...] = lax.fori_loop(0, ts, step, hc[...], unroll=8)

    y = b_s[...] * gb_s[...]
    out = jnp.dot(y.astype(BF16), wout_ref[...], preferred_element_type=F32)
    o_ref[...] = pltpu.einshape("sbd->bsd", x + out.reshape(ts, nb, d) * g1[None])


def _rglru_call(x, mod0, g, w_in, conv_w, conv_b, wr, br, wi, bi, lam, w_out, *, ts=64, rc=128):
    nb, s, d = x.shape
    assert nb == SUBLANES, "the recurrence keeps the batch on the sublane axis"
    r = ts * nb
    dr = lam.shape[-1]
    tail = (conv_w.shape[0] - 1) * nb
    row = lambda v: v.reshape(1, -1)
    args = (x, mod0, row(g), w_in.astype(BF16), conv_w, row(conv_b), wr.astype(BF16), row(br),
            wi.astype(BF16), row(bi), row(lam), w_out.astype(BF16))
    x_spec = pl.BlockSpec((nb, ts, d), lambda i: (0, i, 0))
    return pl.pallas_call(
        functools.partial(_rglru_kernel, rc=rc),
        grid=(s // ts,),
        in_specs=[x_spec] + [_const_spec(a.shape) for a in args[1:]],
        out_specs=x_spec,
        out_shape=jax.ShapeDtypeStruct((nb, s, d), F32),
        scratch_shapes=[pltpu.VMEM((r, d), BF16), pltpu.VMEM((r, dr), F32),
                        pltpu.VMEM((r + tail, dr), F32), pltpu.VMEM((r, dr), F32),
                        pltpu.VMEM((r, dr), F32), pltpu.VMEM((nb, dr), F32)],
        compiler_params=pltpu.CompilerParams(
            dimension_semantics=("arbitrary",), vmem_limit_bytes=VMEM_LIMIT),
        name="rglru_mixer",
    )(*args)


def _ffn_kernel(*refs, has_proj, fc):
    if has_proj:
        x_ref, mod_ref, g_ref, w1_ref, w2_ref, o_in_ref, wo_ref, out_ref = refs
    else:
        x_ref, mod_ref, g_ref, w1_ref, w2_ref, out_ref = refs
    d = x_ref.shape[1]
    dff = w1_ref.shape[1]
    x = x_ref[...]
    if has_proj:
        g1 = mod_ref[:, 2 * d:3 * d]
        o = jnp.concatenate([o_in_ref[p] for p in range(o_in_ref.shape[0])], axis=-1)
        x = x + g1 * jnp.dot(o, wo_ref[...], preferred_element_type=F32)
    sh2 = mod_ref[:, 3 * d:4 * d]
    sc2 = mod_ref[:, 4 * d:5 * d]
    g2 = mod_ref[:, 5 * d:6 * d]
    hb = (_rmsnorm(x, g_ref[...]) * (1.0 + sc2) + sh2).astype(BF16)
    acc = jnp.zeros(x.shape, F32)
    for c in range(dff // fc):
        hid = jnp.dot(hb, w1_ref[:, c * fc:(c + 1) * fc], preferred_element_type=F32)
        hid = jnp.square(jnp.maximum(hid, 0.0))
        acc = acc + jnp.dot(hid.astype(BF16), w2_ref[c * fc:(c + 1) * fc, :],
                            preferred_element_type=F32)
    out_ref[...] = x + g2 * acc


def _ffn_call(x, mod_l, g, w1, w2, o_pairs=None, w_o=None, *, rf=512, fc=1024):
    b, s, d = x.shape
    has_proj = o_pairs is not None
    args = [x, mod_l, g.reshape(1, -1), w1.astype(BF16), w2.astype(BF16)]
    in_specs = [pl.BlockSpec((None, rf, d), lambda bi, i: (bi, i, 0)),
                pl.BlockSpec((None, 1, mod_l.shape[-1]), lambda bi, i: (bi, 0, 0)),
                _const_spec((1, d)), _const_spec(w1.shape), _const_spec(w2.shape)]
    if has_proj:
        npair = o_pairs.shape[1]
        args += [o_pairs, w_o.astype(BF16)]
        in_specs += [pl.BlockSpec((None, npair, rf, LANES), lambda bi, i: (bi, 0, i, 0)),
                     _const_spec(w_o.shape)]
    return pl.pallas_call(
        functools.partial(_ffn_kernel, has_proj=has_proj, fc=fc),
        grid=(b, s // rf),
        in_specs=in_specs,
        out_specs=pl.BlockSpec((None, rf, d), lambda bi, i: (bi, i, 0)),
        out_shape=jax.ShapeDtypeStruct((b, s, d), F32),
        compiler_params=pltpu.CompilerParams(
            dimension_semantics=("arbitrary", "arbitrary"), vmem_limit_bytes=VMEM_LIMIT),
        name="ffn_proj" if has_proj else "ffn",
    )(*args)


def _rot_half(x, lane):
    return jnp.where((lane & 63) < 32, pltpu.roll(x, 96, 1), pltpu.roll(x, 32, 1))


def _dsa_in_kernel(x_ref, mod_ref, g_ref, w_ref, qg_ref, kg_ref, e_ref, cos_ref, sin_ref,
                   q_ref, qit_ref, kt_ref, ki_ref, v_ref, wit_ref, *, nq, nqi, hd, idx_heads):
    d = x_ref.shape[1]
    tq = x_ref.shape[0]
    x = x_ref[...]
    sh1 = mod_ref[:, 0:d]
    sc1 = mod_ref[:, d:2 * d]
    h = _rmsnorm(x, g_ref[...]) * (1.0 + sc1) + sh1
    u = jnp.dot(h.astype(BF16), w_ref[...], preferred_element_type=F32)
    lane = lax.broadcasted_iota(I32, (tq, LANES), 1)
    cos = cos_ref[...]
    sin = sin_ref[...]
    qg = qg_ref[...]
    e = e_ref[...]
    qscale = hd ** -0.5

    for p in range(nq):
        t = u[:, p * LANES:(p + 1) * LANES]
        t2 = t * t
        hi = t2.astype(BF16)
        lo = (t2 - hi.astype(F32)).astype(BF16)
        ss = (jnp.dot(hi, e, preferred_element_type=F32) + jnp.dot(lo, e, preferred_element_type=F32))
        tn = t * lax.rsqrt(ss * (1.0 / hd) + RMS_EPS) * qg
        q_ref[p] = ((tn * cos + _rot_half(tn, lane) * sin) * qscale).astype(BF16)

    for p in range(nqi):
        t = u[:, (nq + p) * LANES:(nq + p + 1) * LANES]
        qit_ref[p] = ((t * cos + _rot_half(t, lane) * sin) * qscale).T.astype(BF16)

    kk = u[:, (nq + nqi) * LANES:(nq + nqi + 1) * LANES]
    left = lane < hd
    ssk = jnp.sum(jnp.where(left, kk * kk, 0.0), axis=-1, keepdims=True)
    fac = jnp.where(left, lax.rsqrt(ssk * (1.0 / hd) + RMS_EPS) * kg_ref[...], 1.0)
    kn = kk * fac
    kr = kn * cos + _rot_half(kn, lane) * sin
    krt = kr.T.astype(BF16)
    zeros = jnp.zeros((hd, tq), BF16)
    kt_ref[0:hd, :] = krt[0:hd]
    kt_ref[hd:2 * hd, :] = zeros
    kt_ref[2 * hd:3 * hd, :] = krt[0:hd]
    ki_r = jnp.where(left, 0.0, kr)
    ki_ref[0] = pltpu.roll(ki_r, hd, 1).astype(BF16)
    ki_ref[1] = ki_r.astype(BF16)

    vw = u[:, (nq + nqi + 1) * LANES:(nq + nqi + 2) * LANES]
    v_ref[...] = jnp.where(left, vw, 1.0).astype(BF16)
    wit_ref[...] = (vw * (idx_heads ** -0.5)).T[hd:hd + idx_heads, :]


def _dsa_in_call(x, mod_l, g, w_in, q_g, k_g, *, n_heads, hd, idx_heads, tq=512):
    b, s, d = x.shape
    assert 2 * hd == LANES, "two heads per lane tile"
    nq = n_heads * hd // LANES
    nqi = idx_heads * hd // LANES
    o_k = n_heads * hd
    o_v = o_k + hd
    o_qi = o_v + hd
    o_ki = o_qi + idx_heads * hd
    o_wi = o_ki + hd
    pad = LANES - hd - idx_heads
    w = jnp.concatenate([w_in[:, :o_k], w_in[:, o_qi:o_ki], w_in[:, o_k:o_v], w_in[:, o_ki:o_wi],
                         w_in[:, o_v:o_qi], w_in[:, o_wi:], jnp.zeros((d, pad), w_in.dtype)],
                        axis=1).astype(BF16)
    ncol = w.shape[1]
    inv = ROPE_THETA ** (-jnp.arange(0, hd, 2, dtype=F32) / hd)
    ang = jnp.arange(s, dtype=F32)[:, None] * inv[None, :]
    cos_t = jnp.tile(jnp.cos(ang), (1, 4))
    sin_h = jnp.sin(ang)
    sin_t = jnp.tile(jnp.concatenate([-sin_h, sin_h], axis=1), (1, 2))
    head_of_lane = jnp.arange(LANES) // hd
    e = (head_of_lane[:, None] == head_of_lane[None, :]).astype(BF16)
    qg = jnp.tile(q_g, 2).reshape(1, LANES)
    kg = jnp.concatenate([k_g, jnp.ones((hd,), k_g.dtype)]).reshape(1, LANES)

    row_spec = pl.BlockSpec((None, tq, LANES), lambda bi, i: (bi, i, 0))
    col_spec = pl.BlockSpec((None, 3 * hd, tq), lambda bi, i: (bi, 0, i))
    return pl.pallas_call(
        functools.partial(_dsa_in_kernel, nq=nq, nqi=nqi, hd=hd, idx_heads=idx_heads),
        grid=(b, s // tq),
        in_specs=[pl.BlockSpec((None, tq, d), lambda bi, i: (bi, i, 0)),
                  pl.BlockSpec((None, 1, mod_l.shape[-1]), lambda bi, i: (bi, 0, 0)),
                  _const_spec((1, d)), _const_spec((d, ncol)), _const_spec((1, LANES)),
                  _const_spec((1, LANES)), _const_spec((LANES, LANES)),
                  pl.BlockSpec((tq, LANES), lambda bi, i: (i, 0)),
                  pl.BlockSpec((tq, LANES), lambda bi, i: (i, 0))],
        out_specs=[pl.BlockSpec((None, nq, tq, LANES), lambda bi, i: (bi, 0, i, 0)),
                   pl.BlockSpec((None, nqi, LANES, tq), lambda bi, i: (bi, 0, 0, i)),
                   col_spec,
                   pl.BlockSpec((None, 2, tq, LANES), lambda bi, i: (bi, 0, i, 0)),
                   row_spec,
                   pl.BlockSpec((None, idx_heads, tq), lambda bi, i: (bi, 0, i))],
        out_shape=[jax.ShapeDtypeStruct((b, nq, s, LANES), BF16),
                   jax.ShapeDtypeStruct((b, nqi, LANES, s), BF16),
                   jax.ShapeDtypeStruct((b, 3 * hd, s), BF16),
                   jax.ShapeDtypeStruct((b, 2, s, LANES), BF16),
                   jax.ShapeDtypeStruct((b, s, LANES), BF16),
                   jax.ShapeDtypeStruct((b, idx_heads, s), F32)],
        compiler_params=pltpu.CompilerParams(
            dimension_semantics=("arbitrary", "arbitrary"), vmem_limit_bytes=VMEM_LIMIT),
        name="dsa_in",
    )(x, mod_l, g.reshape(1, -1), w, qg, kg, e, cos_t, sin_t)


def _dsa_attn_kernel(bnd_ref, qit_ref, wit_ref, q_ref, kt_ref, ki_ref, v_ref, o_ref, sct_s, tie_s,
                     bias_s, jc_s, m_s, acc_s, *, kc, ks, topk, hd, idx_heads, idx_bits, group,
                     tie_from, max_groups):
    npair, qb, _ = q_ref.shape
    nt = kc // LANES
    nsub = SUBLANES
    ng = kc // nsub
    j = pl.program_id(1)
    nk = (j * qb) // kc + 1
    lane = lax.broadcasted_iota(I32, (qb, LANES), 1)
    left = lane < hd
    kf = float(topk)

    def lanes_all(x):
        return jnp.concatenate([x] * nt, axis=1)

    def fold(x, op):
        r = x[:, 0:LANES]
        for t in range(1, nt):
            r = op(r, x[:, t * LANES:(t + 1) * LANES])
        return r

    def row_all(x, red):
        return jnp.broadcast_to(red(x, axis=1, keepdims=True), (qb, LANES))

    qpos = j * qb + lax.broadcasted_iota(I32, (ks, qb), 1)
    krow = lax.broadcasted_iota(I32, (ks, qb), 0)

    def score_chunk(ci, _):
        for sub in range(kc // ks):
            k0 = pl.multiple_of(ci * kc + sub * ks, ks)
            kis = (ki_ref[0, pl.ds(k0, ks), :], ki_ref[1, pl.ds(k0, ks), :])
            acc = jnp.zeros((ks, qb), F32)
            for hh in range(idx_heads):
                lg = jnp.dot(kis[hh % 2], qit_ref[hh // 2], preferred_element_type=F32)
                acc = acc + wit_ref[hh:hh + 1, :] * jnp.maximum(lg, 0.0)
            sct_s[pl.ds(k0, ks), :] = jnp.where((k0 + krow) <= qpos, acc, -jnp.inf)
        return 0

    lax.fori_loop(0, nk, score_chunk, 0)

    def reduce_keys(src, fn, op, red, init):
        def body(ci, acc):
            c0 = pl.multiple_of(ci * kc, kc)
            val = fn(src[pl.ds(c0, kc), :].reshape(ng, nsub, qb), c0)
            n = ng
            while n > 1:
                n //= 2
                val = op(val[:n], val[n:])
            return op(acc, val[0])
        acc = lax.fori_loop(0, nk, body, jnp.full((nsub, qb), init, F32))
        return jnp.broadcast_to(red(acc, axis=0, keepdims=True), (nsub, qb))

    def count_ge(thr):
        return reduce_keys(sct_s, lambda blk, c0: jnp.where(blk >= thr[None], 1.0, 0.0),
                           jnp.add, jnp.sum, 0.0)

    mx = reduce_keys(sct_s, lambda blk, c0: blk, jnp.maximum, jnp.max, -jnp.inf)
    mn = reduce_keys(sct_s, lambda blk, c0: jnp.where(blk > -jnp.inf, blk, jnp.inf),
                     jnp.minimum, jnp.min, jnp.inf)
    n_causal = (j * qb + lax.broadcasted_iota(I32, (nsub, qb), 1) + 1).astype(F32)

    def bisect(_, st):
        lo, hi, clo, chi = st
        mid = 0.5 * lo + 0.5 * hi
        c = count_ge(mid)
        ge = c >= kf
        return (jnp.where(ge, mid, lo), jnp.where(ge, hi, mid),
                jnp.where(ge, c, clo), jnp.where(ge, chi, c))

    def open_rows(clo, tied):
        return jnp.max(jnp.where(clo > kf, 1.0 - tied, 0.0)) > 0.0

    def search_group(st):
        g, _, lo, hi, clo, chi, tied = st
        lo, hi, clo, chi = lax.fori_loop(0, group, bisect, (lo, hi, clo, chi))

        def tie_check(tied):
            a = reduce_keys(sct_s, lambda blk, c0: jnp.where(blk >= lo[None], blk, jnp.inf),
                            jnp.minimum, jnp.min, jnp.inf)
            b = reduce_keys(sct_s, lambda blk, c0: jnp.where(blk < hi[None], blk, -jnp.inf),
                            jnp.maximum, jnp.max, -jnp.inf)
            return jnp.where(jnp.logical_and(clo > kf, a == b), 1.0, tied)

        tied = lax.cond(jnp.logical_and((g + 1) * group >= tie_from, open_rows(clo, tied)),
                        tie_check, lambda t: t, tied)
        return (g + 1, open_rows(clo, tied).astype(I32), lo, hi, clo, chi, tied)

    zero = jnp.zeros((nsub, qb), F32)
    _, _, lo, hi, clo, chi, _ = lax.while_loop(
        lambda st: jnp.logical_and(st[1] > 0, st[0] < max_groups), search_group,
        (jnp.int32(0), open_rows(n_causal, zero).astype(I32), mn,
         mx + (jnp.abs(mx) * 1e-6 + 1e-30), n_causal, zero, zero))

    kidx = (nsub * lax.broadcasted_iota(I32, (ng, nsub, qb), 0)
            + lax.broadcasted_iota(I32, (ng, nsub, qb), 1))
    excess = clo > kf
    need = kf - chi
    jc_s[...] = jnp.full((nsub, qb), 2 ** 30, I32)

    @pl.when(jnp.max(jnp.where(excess, 1.0, 0.0)) > 0.0)
    def _():
        def tie_chunk(ci, _):
            c0 = pl.multiple_of(ci * kc, kc)
            blk = sct_s[pl.ds(c0, kc), :].reshape(ng, nsub, qb)
            tie = jnp.where(blk >= hi[None], 0.0, jnp.where(blk >= lo[None], 1.0, 0.0))
            tie_s[pl.ds(c0, kc), :] = tie.reshape(kc, qb)
            return 0

        lax.fori_loop(0, nk, tie_chunk, 0)

        def j_bit(bi, jv):
            cand = jv | lax.shift_left(jnp.int32(1), idx_bits - 1 - bi)
            f = reduce_keys(tie_s, lambda blk, c0: jnp.where((c0 + kidx) < cand[None], blk, 0.0),
                            jnp.add, jnp.sum, 0.0)
            return jnp.where(f < need, cand, jv)
        jv = lax.fori_loop(0, idx_bits, j_bit, jnp.zeros((nsub, qb), I32))
        jc_s[...] = jnp.where(excess, jv, jc_s[...])

    jcut = jc_s[...]
    bnd = bnd_ref[0]
    boff = 60.0 - bnd

    def bias_chunk(ci, _):
        c0 = pl.multiple_of(ci * kc, kc)
        blk = sct_s[pl.ds(c0, kc), :].reshape(ng, nsub, qb)
        tie = jnp.where((c0 + kidx) <= jcut[None], boff, NEG_BIG)
        bt = jnp.where(blk >= hi[None], boff, jnp.where(blk >= lo[None], tie, NEG_BIG))
        bias_s[:, pl.ds(c0, kc)] = bt.reshape(kc, qb).T
        return 0

    lax.fori_loop(0, nk, bias_chunk, 0)

    nh = 2 * npair
    acc_s[...] = jnp.zeros(acc_s.shape, F32)

    def scores(ci, h):
        c0 = pl.multiple_of(ci * kc, kc)
        kt = kt_ref[(h % 2) * hd:(h % 2) * hd + LANES, pl.ds(c0, kc)]
        return jnp.dot(q_ref[h // 2], kt, preferred_element_type=F32) + bias_s[:, pl.ds(c0, kc)]

    def max_step(ci, _):
        @pl.when(ci == 0)
        def _():
            m_s[...] = jnp.full(m_s.shape, NEG_BIG, F32)
        for h in range(nh):
            m_s[h] = jnp.maximum(m_s[h], fold(scores(ci, h), jnp.maximum))
        return 0

    fast = bnd <= EXP_SAFE_BOUND

    @pl.when(jnp.logical_not(fast))
    def _():
        lax.fori_loop(0, nk, max_step, 0)
        for h in range(nh):
            m_s[h] = row_all(m_s[h], jnp.max)

    def pv_step(ci, _, use_max):
        vv = v_ref[pl.ds(pl.multiple_of(ci * kc, kc), kc), :]
        s_next = scores(ci, 0)
        for h in range(nh):
            s_cur = s_next
            if h + 1 < nh:
                s_next = scores(ci, h + 1)
            pr = jnp.exp(s_cur - lanes_all(m_s[h]) if use_max else s_cur)
            acc_s[h] += jnp.dot(pr.astype(BF16), vv, preferred_element_type=F32)
        return 0

    @pl.when(fast)
    def _():
        lax.fori_loop(0, nk, functools.partial(pv_step, use_max=False), 0)

    @pl.when(jnp.logical_not(fast))
    def _():
        lax.fori_loop(0, nk, functools.partial(pv_step, use_max=True), 0)

    for p in range(npair):
        a0, a1 = acc_s[2 * p], acc_s[2 * p + 1]
        o0 = a0 / pltpu.roll(a0, hd, 1)
        o1 = a1 / pltpu.roll(a1, hd, 1)
        o_ref[p] = jnp.where(left, o0, pltpu.roll(o1, hd, 1)).astype(BF16)


def _dsa_attn_call(bnd, q, qit, kt, ki, v, wit, *, hd, idx_heads, topk, qb=256, kc=512, ks=128):
    b, npair, s, _ = q.shape
    nqi = qit.shape[1]
    assert kc % qb == 0 and s % kc == 0 and kc % ks == 0
    idx_bits = max(1, (s - 1).bit_length())
    return pl.pallas_call(
        functools.partial(_dsa_attn_kernel, kc=kc, ks=ks, topk=topk, hd=hd, idx_heads=idx_heads,
                          idx_bits=idx_bits, group=4, tie_from=16, max_groups=96),
        grid=(b, s // qb),
        in_specs=[pl.BlockSpec(memory_space=pltpu.SMEM),
                  pl.BlockSpec((None, nqi, LANES, qb), lambda bi, i: (bi, 0, 0, i)),
                  pl.BlockSpec((None, idx_heads, qb), lambda bi, i: (bi, 0, i)),
                  pl.BlockSpec((None, npair, qb, LANES), lambda bi, i: (bi, 0, i, 0)),
                  pl.BlockSpec((None, 3 * hd, s), lambda bi, i: (bi, 0, 0)),
                  pl.BlockSpec((None, 2, s, LANES), lambda bi, i: (bi, 0, 0, 0)),
                  pl.BlockSpec((None, s, LANES), lambda bi, i: (bi, 0, 0))],
        out_specs=pl.BlockSpec((None, npair, qb, LANES), lambda bi, i: (bi, 0, i, 0)),
        out_shape=jax.ShapeDtypeStruct((b, npair, s, LANES), BF16),
        scratch_shapes=[pltpu.VMEM((s, qb), F32), pltpu.VMEM((s, qb), F32),
                        pltpu.VMEM((qb, s), F32), pltpu.VMEM((SUBLANES, qb), I32),
                        pltpu.VMEM((2 * npair, qb, LANES), F32),
                        pltpu.VMEM((2 * npair, qb, LANES), F32)],
        compiler_params=pltpu.CompilerParams(
            dimension_semantics=("arbitrary", "arbitrary"), vmem_limit_bytes=VMEM_LIMIT),
        name="dsa_attn",
    )(bnd, qit, wit, q, kt, ki, v)


def kernel(x, c, norm_mix_g, norm_ffn_g, ada_w, ada_b, a_w_in, a_conv_w, a_conv_b, a_gate_r_w,
           a_gate_r_b, a_gate_i_w, a_gate_i_b, a_lambda, a_w_out, b_w_in, b_q_norm_g, b_k_norm_g,
           b_w_out, ffn_w1, ffn_w2):
    b, s, d = x.shape
    hd = b_q_norm_g.shape[-1]
    n_heads = b_w_out.shape[1] // hd
    idx_heads = (b_w_in.shape[-1] - n_heads * hd - 3 * hd) // (hd + 1)
    topk = min(TOPK_MAX, s // 4)
    depth = ada_w.shape[0]

    mod = _mod_call(c, ada_w, ada_b)
    mod_rows = mod.reshape(depth, b, 1, mod.shape[-1])

    for i in range(depth):
        j = i // 2
        if i % 2 == 0:
            x = _rglru_call(x, mod[i], norm_mix_g[i], a_w_in[j], a_conv_w[j], a_conv_b[j],
                            a_gate_r_w[j], a_gate_r_b[j], a_gate_i_w[j], a_gate_i_b[j],
                            a_lambda[j], a_w_out[j])
            x = _ffn_call(x, mod_rows[i], norm_ffn_g[i], ffn_w1[i], ffn_w2[i])
        else:
            q, qit, kt, ki, v, wit = _dsa_in_call(x, mod_rows[i], norm_mix_g[i], b_w_in[j],
                                                 b_q_norm_g[j], b_k_norm_g[j], n_heads=n_heads,
                                                 hd=hd, idx_heads=idx_heads)
            bnd = (1.02 * hd * hd ** -0.5) * jnp.max(jnp.abs(b_q_norm_g[j])) * jnp.max(jnp.abs(b_k_norm_g[j]))
            o = _dsa_attn_call(bnd.reshape(1), q, qit, kt, ki, v, wit, hd=hd, idx_heads=idx_heads,
                               topk=topk)
            x = _ffn_call(x, mod_rows[i], norm_ffn_g[i], ffn_w1[i], ffn_w2[i], o_pairs=o,
                          w_o=b_w_out[j])
    return x
```

```python
import functools

import jax
import jax.numpy as jnp
from jax import lax
from jax.experimental import pallas as pl
from jax.experimental.pallas import tpu as pltpu

F32 = jnp.float32
BF16 = jnp.bfloat16
I32 = jnp.int32

RMS_EPS = 1e-6
ROPE_THETA = 10000.0
LRU_C = 8.0
TOPK_MAX = 256
N_MOD = 6

LANES = 128
SUBLANES = 8
VMEM_LIMIT = 56 * 1024 * 1024

NEG_BIG = -1e30
EXP_SAFE_BOUND = 70.0
LOG2E = 1.4426950408889634
GELU_C0 = 0.7978845608028654
GELU_C1 = 0.044715


def _const_spec(shape):
    nd = len(shape)
    return pl.BlockSpec(shape, lambda *_: (0,) * nd, pipeline_mode=pl.Buffered(1))


def _rmsnorm(x, g):
    return x * lax.rsqrt(jnp.mean(x * x, axis=-1, keepdims=True) + RMS_EPS) * g


def _mod_kernel(c_ref, w_ref, b_ref, o_ref):
    c = c_ref[...]
    cond = c * jax.nn.sigmoid(c)
    o_ref[0] = jnp.dot(cond.astype(BF16), w_ref[0].astype(BF16),
                       preferred_element_type=F32) + b_ref[0]


def _mod_call(c, ada_w, ada_b):
    depth, d, n = ada_w.shape
    b = c.shape[0]
    tn = 1536
    return pl.pallas_call(
        _mod_kernel,
        grid=(depth, n // tn),
        in_specs=[pl.BlockSpec((b, d), lambda l, j: (0, 0)),
                  pl.BlockSpec((1, d, tn), lambda l, j: (l, 0, j)),
                  pl.BlockSpec((1, 1, tn), lambda l, j: (l, 0, j))],
        out_specs=pl.BlockSpec((1, b, tn), lambda l, j: (l, 0, j)),
        out_shape=jax.ShapeDtypeStruct((depth, b, n), F32),
        compiler_params=pltpu.CompilerParams(
            dimension_semantics=("arbitrary", "arbitrary"), vmem_limit_bytes=VMEM_LIMIT),
        name="adaln_mod",
    )(c, ada_w, ada_b.reshape(depth, 1, n))


def _rglru_kernel(x_ref, mod_ref, g_ref, win_ref, cw_ref, cb_ref, wr_ref, br_ref, wi_ref, bi_ref,
                  lam_ref, wout_ref, o_ref, h_s, gb_s, xbuf, a_s, b_s, hc, *, rc):
    nb, ts, d = x_ref.shape
    r = ts * nb
    dr = lam_ref.shape[1]
    nblk, blk, _ = wr_ref.shape
    cwid = cw_ref.shape[0]
    tail = (cwid - 1) * nb

    @pl.when(pl.program_id(0) == 0)
    def _():
        xbuf[0:tail, :] = jnp.zeros((tail, dr), F32)
        hc[...] = jnp.zeros((nb, dr), F32)

    x = jnp.swapaxes(x_ref[...], 0, 1)
    sh1 = mod_ref[:, 0:d]
    sc1 = mod_ref[:, d:2 * d]
    g1 = mod_ref[:, 2 * d:3 * d]
    h_s[...] = (_rmsnorm(x, g_ref[...]) * (1.0 + sc1)[None] + sh1[None]).reshape(r, d).astype(BF16)

    sp_h = (0.5 * LRU_C) * jax.nn.softplus(-lam_ref[...])
    br_h = 0.5 * br_ref[...]
    bi_h = 0.5 * bi_ref[...]

    def in_proj(c, n):
        rows = slice(c * rc, (c + 1) * rc)
        hb = h_s[rows, :]
        ux = jnp.dot(hb, win_ref[:, n * blk:(n + 1) * blk], preferred_element_type=F32)
        ug = jnp.dot(hb, win_ref[:, dr + n * blk:dr + (n + 1) * blk], preferred_element_type=F32)
        xbuf[tail + c * rc:tail + (c + 1) * rc, n * blk:(n + 1) * blk] = ux
        return ug

    def conv_and_gate_dots(c, n):
        r0 = c * rc
        cs = slice(n * blk, (n + 1) * blk)
        xc = cb_ref[:, cs] + cw_ref[0:1, cs] * xbuf[r0:r0 + rc, cs]
        for k in range(1, cwid):
            xc = xc + cw_ref[k:k + 1, cs] * xbuf[r0 + k * nb:r0 + k * nb + rc, cs]
        xn = xc.astype(BF16)
        return xc, (jnp.dot(xn, wr_ref[n], preferred_element_type=F32),
                    jnp.dot(xn, wi_ref[n], preferred_element_type=F32))

    def gate_tail(c, n, xc, pre, ug):
        ro = slice(c * rc, (c + 1) * rc)
        cs = slice(n * blk, (n + 1) * blk)
        p = sp_h[:, cs] + sp_h[:, cs] * jnp.tanh(0.5 * pre[0] + br_h[:, cs])
        ig = 0.5 + 0.5 * jnp.tanh(0.5 * pre[1] + bi_h[:, cs])
        a = jnp.exp2(p * (-LOG2E))
        a_s[ro, cs] = a
        b_s[ro, cs] = jnp.sqrt(jnp.tanh(p) * (a * a + 1.0)) * (ig * xc)
        t = jnp.tanh(ug * (GELU_C0 + (GELU_C0 * GELU_C1) * (ug * ug)))
        hu = 0.5 * ug
        gb_s[ro, cs] = hu + hu * t

    units = [(c, n) for c in range(r // rc) for n in range(nblk)]
    ug = in_proj(*units[0])
    for k, (c, n) in enumerate(units):
        xc, pre = conv_and_gate_dots(c, n)
        ug_next = in_proj(*units[k + 1]) if k + 1 < len(units) else None
        gate_tail(c, n, xc, pre, ug)
        ug = ug_next
    xbuf[0:tail, :] = xbuf[r:r + tail, :]

    def step(t, hprev):
        r0 = pl.multiple_of(t * nb, nb)
        hnew = a_s[pl.ds(r0, nb), :] * hprev + b_s[pl.ds(r0, nb), :]
        b_s[pl.ds(r0, nb), :] = hnew
        return hnew

    hc[...] = lax.fori_loop(0, ts, step, hc[...], unroll=8)

    y = b_s[...] * gb_s[...]
    out = jnp.dot(y.astype(BF16), wout_ref[...], preferred_element_type=F32)
    o_ref[...] = jnp.swapaxes(x + out.reshape(ts, nb, d) * g1[None], 0, 1)


def _rglru_call(x, mod0, g, w_in, conv_w, conv_b, wr, br, wi, bi, lam, w_out, *, ts=64, rc=128):
    nb, s, d = x.shape
    assert nb == SUBLANES, "the recurrence keeps the batch on the sublane axis"
    r = ts * nb
    dr = lam.shape[-1]
    tail = (conv_w.shape[0] - 1) * nb
    row = lambda v: v.reshape(1, -1)
    args = (x, mod0, row(g), w_in.astype(BF16), conv_w, row(conv_b), wr.astype(BF16), row(br),
            wi.astype(BF16), row(bi), row(lam), w_out.astype(BF16))
    x_spec = pl.BlockSpec((nb, ts, d), lambda i: (0, i, 0))
    return pl.pallas_call(
        functools.partial(_rglru_kernel, rc=rc),
        grid=(s // ts,),
        in_specs=[x_spec] + [_const_spec(a.shape) for a in args[1:]],
        out_specs=x_spec,
        out_shape=jax.ShapeDtypeStruct((nb, s, d), F32),
        scratch_shapes=[pltpu.VMEM((r, d), BF16), pltpu.VMEM((r, dr), F32),
                        pltpu.VMEM((r + tail, dr), F32), pltpu.VMEM((r, dr), F32),
                        pltpu.VMEM((r, dr), F32), pltpu.VMEM((nb, dr), F32)],
        compiler_params=pltpu.CompilerParams(
            dimension_semantics=("arbitrary",), vmem_limit_bytes=VMEM_LIMIT),
        name="rglru_mixer",
    )(*args)


def _ffn_kernel(*refs, has_proj, fc):
    if has_proj:
        x_ref, mod_ref, g_ref, w1_ref, w2_ref, o_in_ref, wo_ref, out_ref = refs
    else:
        x_ref, mod_ref, g_ref, w1_ref, w2_ref, out_ref = refs
    d = x_ref.shape[1]
    dff = w1_ref.shape[1]
    x = x_ref[...]
    if has_proj:
        g1 = mod_ref[:, 2 * d:3 * d]
        o = jnp.concatenate([o_in_ref[p] for p in range(o_in_ref.shape[0])], axis=-1)
        x = x + g1 * jnp.dot(o, wo_ref[...], preferred_element_type=F32)
    sh2 = mod_ref[:, 3 * d:4 * d]
    sc2 = mod_ref[:, 4 * d:5 * d]
    g2 = mod_ref[:, 5 * d:6 * d]
    hb = (_rmsnorm(x, g_ref[...]) * (1.0 + sc2) + sh2).astype(BF16)
    acc = jnp.zeros(x.shape, F32)
    for c in range(dff // fc):
        hid = jnp.dot(hb, w1_ref[:, c * fc:(c + 1) * fc], preferred_element_type=F32)
        hid = jnp.square(jnp.maximum(hid, 0.0))
        acc = acc + jnp.dot(hid.astype(BF16), w2_ref[c * fc:(c + 1) * fc, :],
                            preferred_element_type=F32)
    out_ref[...] = x + g2 * acc


def _ffn_call(x, mod_l, g, w1, w2, o_pairs=None, w_o=None, *, rf=512, fc=1024):
    b, s, d = x.shape
    has_proj = o_pairs is not None
    args = [x, mod_l, g.reshape(1, -1), w1.astype(BF16), w2.astype(BF16)]
    in_specs = [pl.BlockSpec((None, rf, d), lambda bi, i: (bi, i, 0)),
                pl.BlockSpec((None, 1, mod_l.shape[-1]), lambda bi, i: (bi, 0, 0)),
                _const_spec((1, d)), _const_spec(w1.shape), _const_spec(w2.shape)]
    if has_proj:
        npair = o_pairs.shape[1]
        args += [o_pairs, w_o.astype(BF16)]
        in_specs += [pl.BlockSpec((None, npair, rf, LANES), lambda bi, i: (bi, 0, i, 0)),
                     _const_spec(w_o.shape)]
    return pl.pallas_call(
        functools.partial(_ffn_kernel, has_proj=has_proj, fc=fc),
        grid=(b, s // rf),
        in_specs=in_specs,
        out_specs=pl.BlockSpec((None, rf, d), lambda bi, i: (bi, i, 0)),
        out_shape=jax.ShapeDtypeStruct((b, s, d), F32),
        compiler_params=pltpu.CompilerParams(
            dimension_semantics=("arbitrary", "arbitrary"), vmem_limit_bytes=VMEM_LIMIT),
        name="ffn_proj" if has_proj else "ffn",
    )(*args)


def _rot_half(x, lane):
    return jnp.where((lane & 63) < 32, pltpu.roll(x, 96, 1), pltpu.roll(x, 32, 1))


def _dsa_in_kernel(x_ref, mod_ref, g_ref, w_ref, qg_ref, kg_ref, e_ref, cos_ref, sin_ref,
                   q_ref, qit_ref, kt_ref, ki_ref, v_ref, wit_ref, *, nq, nqi, hd, idx_heads):
    d = x_ref.shape[1]
    tq = x_ref.shape[0]
    x = x_ref[...]
    sh1 = mod_ref[:, 0:d]
    sc1 = mod_ref[:, d:2 * d]
    h = _rmsnorm(x, g_ref[...]) * (1.0 + sc1) + sh1
    u = jnp.dot(h.astype(BF16), w_ref[...], preferred_element_type=F32)
    lane = lax.broadcasted_iota(I32, (tq, LANES), 1)
    cos = cos_ref[...]
    sin = sin_ref[...]
    qg = qg_ref[...]
    e = e_ref[...]
    qscale = hd ** -0.5

    for p in range(nq):
        t = u[:, p * LANES:(p + 1) * LANES]
        t2 = t * t
        hi = t2.astype(BF16)
        lo = (t2 - hi.astype(F32)).astype(BF16)
        ss = (jnp.dot(hi, e, preferred_element_type=F32) + jnp.dot(lo, e, preferred_element_type=F32))
        tn = t * lax.rsqrt(ss * (1.0 / hd) + RMS_EPS) * qg
        q_ref[p] = ((tn * cos + _rot_half(tn, lane) * sin) * (qscale * LOG2E)).astype(BF16)

    for p in range(nqi):
        t = u[:, (nq + p) * LANES:(nq + p + 1) * LANES]
        qit_ref[p] = ((t * cos + _rot_half(t, lane) * sin) * qscale).T.astype(BF16)

    kk = u[:, (nq + nqi) * LANES:(nq + nqi + 1) * LANES]
    left = lane < hd
    ssk = jnp.sum(jnp.where(left, kk * kk, 0.0), axis=-1, keepdims=True)
    fac = jnp.where(left, lax.rsqrt(ssk * (1.0 / hd) + RMS_EPS) * kg_ref[...], 1.0)
    kn = kk * fac
    kr = kn * cos + _rot_half(kn, lane) * sin
    krt = kr.T.astype(BF16)
    zeros = jnp.zeros((hd, tq), BF16)
    kt_ref[0:hd, :] = krt[0:hd]
    kt_ref[hd:2 * hd, :] = zeros
    kt_ref[2 * hd:3 * hd, :] = krt[0:hd]
    ki_r = jnp.where(left, 0.0, kr)
    ki_ref[0] = pltpu.roll(ki_r, hd, 1).astype(BF16)
    ki_ref[1] = ki_r.astype(BF16)

    vw = u[:, (nq + nqi + 1) * LANES:(nq + nqi + 2) * LANES]
    v_ref[...] = jnp.where(left, vw, 1.0).astype(BF16)
    wit_ref[...] = (vw * (idx_heads ** -0.5)).T[hd:hd + idx_heads, :]


def _dsa_in_call(x, mod_l, g, w_in, q_g, k_g, *, n_heads, hd, idx_heads, tq=512):
    b, s, d = x.shape
    assert 2 * hd == LANES, "two heads per lane tile"
    nq = n_heads * hd // LANES
    nqi = idx_heads * hd // LANES
    o_k = n_heads * hd
    o_v = o_k + hd
    o_qi = o_v + hd
    o_ki = o_qi + idx_heads * hd
    o_wi = o_ki + hd
    pad = LANES - hd - idx_heads
    w = jnp.concatenate([w_in[:, :o_k], w_in[:, o_qi:o_ki], w_in[:, o_k:o_v], w_in[:, o_ki:o_wi],
                         w_in[:, o_v:o_qi], w_in[:, o_wi:], jnp.zeros((d, pad), w_in.dtype)],
                        axis=1).astype(BF16)
    ncol = w.shape[1]
    inv = ROPE_THETA ** (-jnp.arange(0, hd, 2, dtype=F32) / hd)
    ang = jnp.arange(s, dtype=F32)[:, None] * inv[None, :]
    cos_t = jnp.tile(jnp.cos(ang), (1, 4))
    sin_h = jnp.sin(ang)
    sin_t = jnp.tile(jnp.concatenate([-sin_h, sin_h], axis=1), (1, 2))
    head_of_lane = jnp.arange(LANES) // hd
    e = (head_of_lane[:, None] == head_of_lane[None, :]).astype(BF16)
    qg = jnp.tile(q_g, 2).reshape(1, LANES)
    kg = jnp.concatenate([k_g, jnp.ones((hd,), k_g.dtype)]).reshape(1, LANES)

    row_spec = pl.BlockSpec((None, tq, LANES), lambda bi, i: (bi, i, 0))
    col_spec = pl.BlockSpec((None, 3 * hd, tq), lambda bi, i: (bi, 0, i))
    return pl.pallas_call(
        functools.partial(_dsa_in_kernel, nq=nq, nqi=nqi, hd=hd, idx_heads=idx_heads),
        grid=(b, s // tq),
        in_specs=[pl.BlockSpec((None, tq, d), lambda bi, i: (bi, i, 0)),
                  pl.BlockSpec((None, 1, mod_l.shape[-1]), lambda bi, i: (bi, 0, 0)),
                  _const_spec((1, d)), _const_spec((d, ncol)), _const_spec((1, LANES)),
                  _const_spec((1, LANES)), _const_spec((LANES, LANES)),
                  pl.BlockSpec((tq, LANES), lambda bi, i: (i, 0)),
                  pl.BlockSpec((tq, LANES), lambda bi, i: (i, 0))],
        out_specs=[pl.BlockSpec((None, nq, tq, LANES), lambda bi, i: (bi, 0, i, 0)),
                   pl.BlockSpec((None, nqi, LANES, tq), lambda bi, i: (bi, 0, 0, i)),
                   col_spec,
                   pl.BlockSpec((None, 2, tq, LANES), lambda bi, i: (bi, 0, i, 0)),
                   row_spec,
                   pl.BlockSpec((None, idx_heads, tq), lambda bi, i: (bi, 0, i))],
        out_shape=[jax.ShapeDtypeStruct((b, nq, s, LANES), BF16),
                   jax.ShapeDtypeStruct((b, nqi, LANES, s), BF16),
                   jax.ShapeDtypeStruct((b, 3 * hd, s), BF16),
                   jax.ShapeDtypeStruct((b, 2, s, LANES), BF16),
                   jax.ShapeDtypeStruct((b, s, LANES), BF16),
                   jax.ShapeDtypeStruct((b, idx_heads, s), F32)],
        compiler_params=pltpu.CompilerParams(
            dimension_semantics=("arbitrary", "arbitrary"), vmem_limit_bytes=VMEM_LIMIT),
        name="dsa_in",
    )(x, mod_l, g.reshape(1, -1), w, qg, kg, e, cos_t, sin_t)


def _dsa_attn_kernel(bnd_ref, qit_ref, wit_ref, q_ref, kt_ref, ki_ref, v_ref, o_ref, sct_s, tie_s,
                     mask_s, st_s, jc_s, m_s, acc_s, *, kc, ks, topk, hd, idx_heads, idx_bits, group,
                     tie_from, max_groups, n_blocks, head_passes, sl):
    npair, qb, _ = q_ref.shape
    nh = 2 * npair
    nt = kc // LANES
    nsub = SUBLANES
    i = pl.program_id(1)
    has_sel = i < n_blocks
    has_att = i >= 1
    nk = (i * qb) // kc + 1
    nkp = ((i - 1) * qb) // kc + 1
    slot = lax.rem(i, 2)
    mask_w = mask_s.at[slot]
    mask_r = mask_s.at[1 - slot]
    lane = lax.broadcasted_iota(I32, (qb, LANES), 1)
    left = lane < hd
    kf = float(topk)
    fast = bnd_ref[0] <= EXP_SAFE_BOUND
    merged = jnp.logical_and(jnp.logical_and(has_sel, has_att), fast)

    def lanes_all(x):
        return jnp.concatenate([x] * nt, axis=1)

    def fold(x, op):
        r = x[:, 0:LANES]
        for t in range(1, nt):
            r = op(r, x[:, t * LANES:(t + 1) * LANES])
        return r

    def row_all(x, red):
        return jnp.broadcast_to(red(x, axis=1, keepdims=True), (qb, LANES))

    def slab(src, c, t):
        return src[pl.ds(pl.multiple_of(c * kc + t * sl, sl), sl), :]

    def key_pos(c, t):
        return c * kc + t * sl + lax.broadcasted_iota(I32, (sl, qb), 0)

    def finish(acc, op, red):
        n = sl // nsub
        acc = acc.reshape(n, nsub, qb)
        r = acc[0]
        for t in range(1, n):
            r = op(r, acc[t])
        return jnp.broadcast_to(red(r, axis=0, keepdims=True), (nsub, qb))

    def reduce_keys(src, fn, op, red, init):
        def body(ci, acc):
            for t in range(kc // sl):
                acc = op(acc, fn(slab(src, ci, t), ci, t))
            return acc
        return finish(lax.fori_loop(0, nk, body, jnp.full((sl, qb), init, F32)), op, red)

    def rows(x):
        return jnp.concatenate([x] * (sl // nsub), axis=0)

    @pl.when(has_sel)
    def _():
        qpos = i * qb + lax.broadcasted_iota(I32, (ks, qb), 1)
        krow = lax.broadcasted_iota(I32, (ks, qb), 0)

        def score_chunk(ci, _):
            for sub in range(kc // ks):
                k0 = pl.multiple_of(ci * kc + sub * ks, ks)
                kis = (ki_ref[0, pl.ds(k0, ks), :], ki_ref[1, pl.ds(k0, ks), :])
                acc = jnp.zeros((ks, qb), F32)
                for hh in range(idx_heads):
                    lg = jnp.dot(kis[hh % 2], qit_ref[hh // 2], preferred_element_type=F32)
                    acc = acc + wit_ref[hh:hh + 1, :] * jnp.maximum(lg, 0.0)
                sct_s[pl.ds(k0, ks), :] = jnp.where((k0 + krow) <= qpos, acc, -jnp.inf)
            return 0

        lax.fori_loop(0, nk, score_chunk, 0)
        mx = reduce_keys(sct_s, lambda blk, c, t: blk, jnp.maximum, jnp.max, -jnp.inf)
        mn = reduce_keys(sct_s, lambda blk, c, t: jnp.where(blk > -jnp.inf, blk, jnp.inf),
                         jnp.minimum, jnp.min, jnp.inf)
        st_s[0] = mn
        st_s[1] = mx + (jnp.abs(mx) * 1e-6 + 1e-30)
        st_s[2] = (i * qb + lax.broadcasted_iota(I32, (nsub, qb), 1) + 1).astype(F32)
        st_s[3] = jnp.zeros((nsub, qb), F32)

    def count_step(st):
        c, lo, hi, clo, chi, acc = st
        mid = 0.5 * lo + 0.5 * hi
        midr = rows(mid)
        for t in range(kc // sl):
            acc = acc + jnp.where(slab(sct_s, c, t) >= midr, 1.0, 0.0)
        last = c == nk - 1
        tot = finish(acc, jnp.add, jnp.sum)
        ge = tot >= kf
        lo = jnp.where(last, jnp.where(ge, mid, lo), lo)
        clo = jnp.where(last, jnp.where(ge, tot, clo), clo)
        hi = jnp.where(last, jnp.where(ge, hi, mid), hi)
        chi = jnp.where(last, jnp.where(ge, chi, tot), chi)
        return (jnp.where(last, 0, c + 1), lo, hi, clo, chi, jnp.where(last, 0.0, acc))

    def scores(ci, h):
        c0 = pl.multiple_of(ci * kc, kc)
        kt = kt_ref[(h % 2) * hd:(h % 2) * hd + LANES, pl.ds(c0, kc)]
        return jnp.dot(q_ref[h // 2], kt, preferred_element_type=F32)

    def pv_step(ci, st, use_max, passes):
        c0 = pl.multiple_of(ci * kc, kc)
        vv = v_ref[pl.ds(c0, kc), :]
        msk = mask_r[:, pl.ds(c0, kc)]
        s_next = scores(ci, 0)
        for h in range(nh):
            s_cur = s_next
            if h + 1 < nh:
                s_next = scores(ci, h + 1)
            if use_max:
                pr = jnp.where(msk > 0, jnp.exp2(s_cur - lanes_all(m_s[h])), 0.0).astype(BF16)
            else:
                pr = jnp.exp2(s_cur).astype(BF16) * msk
            acc_s[h] += jnp.dot(pr, vv, preferred_element_type=F32)
            for _ in range(passes[h] if passes else 0):
                st = count_step(st)
        return st

    @pl.when(has_att)
    def _():
        acc_s[...] = jnp.zeros(acc_s.shape, F32)

    @pl.when(merged)
    def _():
        st = (jnp.int32(0), st_s[0], st_s[1], st_s[2], st_s[3], jnp.zeros((sl, qb), F32))
        st = lax.fori_loop(0, nkp, functools.partial(pv_step, use_max=False, passes=head_passes), st)
        st_s[0], st_s[1], st_s[2], st_s[3] = st[1], st[2], st[3], st[4]

    @pl.when(jnp.logical_and(has_att, jnp.logical_and(fast, jnp.logical_not(has_sel))))
    def _():
        lax.fori_loop(0, nkp, functools.partial(pv_step, use_max=False, passes=None), 0)

    @pl.when(jnp.logical_and(has_att, jnp.logical_not(fast)))
    def _():
        def max_step(ci, _):
            @pl.when(ci == 0)
            def _():
                m_s[...] = jnp.full(m_s.shape, NEG_BIG, F32)
            msk = mask_r[:, pl.ds(pl.multiple_of(ci * kc, kc), kc)]
            for h in range(nh):
                m_s[h] = jnp.maximum(m_s[h], fold(jnp.where(msk > 0, scores(ci, h), NEG_BIG),
                                                  jnp.maximum))
            return 0

        lax.fori_loop(0, nkp, max_step, 0)
        for h in range(nh):
            m_s[h] = row_all(m_s[h], jnp.max)
        lax.fori_loop(0, nkp, functools.partial(pv_step, use_max=True, passes=None), 0)

    @pl.when(has_att)
    def _():
        for p in range(npair):
            a0, a1 = acc_s[2 * p], acc_s[2 * p + 1]
            o0 = a0 / pltpu.roll(a0, hd, 1)
            o1 = a1 / pltpu.roll(a1, hd, 1)
            o_ref[p] = jnp.where(left, o0, pltpu.roll(o1, hd, 1)).astype(BF16)

    @pl.when(has_sel)
    def _():
        def count_ge(thr):
            thr = rows(thr)
            return reduce_keys(sct_s, lambda blk, c, t: jnp.where(blk >= thr, 1.0, 0.0),
                               jnp.add, jnp.sum, 0.0)

        def bisect(_, st):
            lo, hi, clo, chi = st
            mid = 0.5 * lo + 0.5 * hi
            c = count_ge(mid)
            ge = c >= kf
            return (jnp.where(ge, mid, lo), jnp.where(ge, hi, mid),
                    jnp.where(ge, c, clo), jnp.where(ge, chi, c))

        def open_rows(clo, tied):
            return jnp.max(jnp.where(clo > kf, 1.0 - tied, 0.0)) > 0.0

        g0 = jnp.where(merged, tie_from // group, 0)

        def search_group(st):
            g, _, lo, hi, clo, chi, tied = st
            lo, hi, clo, chi = lax.fori_loop(0, group, bisect, (lo, hi, clo, chi))

            def tie_check(tied):
                lor, hir = rows(lo), rows(hi)
                a = reduce_keys(sct_s, lambda blk, c, t: jnp.where(blk >= lor, blk, jnp.inf),
                                jnp.minimum, jnp.min, jnp.inf)
                b = reduce_keys(sct_s, lambda blk, c, t: jnp.where(blk < hir, blk, -jnp.inf),
                                jnp.maximum, jnp.max, -jnp.inf)
                return jnp.where(jnp.logical_and(clo > kf, a == b), 1.0, tied)

            tied = lax.cond(jnp.logical_and((g + 1) * group >= tie_from, open_rows(clo, tied)),
                            tie_check, lambda t: t, tied)
            return (g + 1, open_rows(clo, tied).astype(I32), lo, hi, clo, chi, tied)

        zero = jnp.zeros((nsub, qb), F32)
        _, _, lo, hi, clo, chi, _ = lax.while_loop(
            lambda st: jnp.logical_and(st[1] > 0, st[0] < max_groups), search_group,
            (g0, open_rows(st_s[2], zero).astype(I32), st_s[0], st_s[1], st_s[2], st_s[3], zero))

        lor, hir = rows(lo), rows(hi)
        excess = clo > kf
        need = kf - chi
        jc_s[...] = jnp.full((nsub, qb), 2 ** 30, I32)

        @pl.when(jnp.max(jnp.where(excess, 1.0, 0.0)) > 0.0)
        def _():
            def tie_chunk(ci, _):
                for t in range(kc // sl):
                    blk = slab(sct_s, ci, t)
                    tie_s[pl.ds(pl.multiple_of(ci * kc + t * sl, sl), sl), :] = jnp.where(
                        blk >= hir, 0.0, jnp.where(blk >= lor, 1.0, 0.0))
                return 0

            lax.fori_loop(0, nk, tie_chunk, 0)

            def j_bit(bi, jv):
                cand = jv | lax.shift_left(jnp.int32(1), idx_bits - 1 - bi)
                candr = jnp.concatenate([cand] * (sl // nsub), axis=0)
                f = reduce_keys(tie_s, lambda blk, c, t: jnp.where(key_pos(c, t) < candr, blk, 0.0),
                                jnp.add, jnp.sum, 0.0)
                return jnp.where(f < need, cand, jv)
            jv = lax.fori_loop(0, idx_bits, j_bit, jnp.zeros((nsub, qb), I32))
            jc_s[...] = jnp.where(excess, jv, jc_s[...])

        jcut = jnp.concatenate([jc_s[...]] * (sl // nsub), axis=0)

        def mask_chunk(ci, _):
            for t in range(kc // ks):
                parts = []
                for u in range(ks // sl):
                    tt = t * (ks // sl) + u
                    blk = slab(sct_s, ci, tt)
                    tie = jnp.where(key_pos(ci, tt) <= jcut, 1.0, 0.0)
                    parts.append(jnp.where(blk >= hir, 1.0, jnp.where(blk >= lor, tie, 0.0)))
                k0 = pl.multiple_of(ci * kc + t * ks, ks)
                mask_w[:, pl.ds(k0, ks)] = jnp.concatenate(parts, axis=0).T.astype(BF16)
            return 0

        lax.fori_loop(0, nk, mask_chunk, 0)


def _dsa_attn_call(bnd, q, qit, kt, ki, v, wit, *, hd, idx_heads, topk, qb=256, kc=512, ks=128,
                   sl=32):
    b, npair, s, _ = q.shape
    nqi = qit.shape[1]
    assert kc % qb == 0 and s % kc == 0 and kc % ks == 0 and ks % sl == 0 and sl % SUBLANES == 0
    idx_bits = max(1, (s - 1).bit_length())
    nb = s // qb
    head_passes = tuple(2 if h % 2 == 0 else 1 for h in range(2 * npair))
    prev = lambda i: jnp.maximum(i - 1, 0)
    cur = lambda i: jnp.minimum(i, nb - 1)
    return pl.pallas_call(
        functools.partial(_dsa_attn_kernel, kc=kc, ks=ks, topk=topk, hd=hd, idx_heads=idx_heads,
                          idx_bits=idx_bits, group=4, tie_from=16, max_groups=96, n_blocks=nb,
                          head_passes=head_passes, sl=sl),
        grid=(b, nb + 1),
        in_specs=[pl.BlockSpec(memory_space=pltpu.SMEM),
                  pl.BlockSpec((None, nqi, LANES, qb), lambda bi, i: (bi, 0, 0, cur(i))),
                  pl.BlockSpec((None, idx_heads, qb), lambda bi, i: (bi, 0, cur(i))),
                  pl.BlockSpec((None, npair, qb, LANES), lambda bi, i: (bi, 0, prev(i), 0)),
                  pl.BlockSpec((None, 3 * hd, s), lambda bi, i: (bi, 0, 0)),
                  pl.BlockSpec((None, 2, s, LANES), lambda bi, i: (bi, 0, 0, 0)),
                  pl.BlockSpec((None, s, LANES), lambda bi, i: (bi, 0, 0))],
        out_specs=pl.BlockSpec((None, npair, qb, LANES), lambda bi, i: (bi, 0, prev(i), 0)),
        out_shape=jax.ShapeDtypeStruct((b, npair, s, LANES), BF16),
        scratch_shapes=[pltpu.VMEM((s, qb), F32), pltpu.VMEM((s, qb), F32),
                        pltpu.VMEM((2, qb, s), BF16), pltpu.VMEM((4, SUBLANES, qb), F32),
                        pltpu.VMEM((SUBLANES, qb), I32),
                        pltpu.VMEM((2 * npair, qb, LANES), F32),
                        pltpu.VMEM((2 * npair, qb, LANES), F32)],
        compiler_params=pltpu.CompilerParams(
            dimension_semantics=("arbitrary", "arbitrary"), vmem_limit_bytes=VMEM_LIMIT),
        name="dsa_attn",
    )(bnd, qit, wit, q, kt, ki, v)


def kernel(x, c, norm_mix_g, norm_ffn_g, ada_w, ada_b, a_w_in, a_conv_w, a_conv_b, a_gate_r_w,
           a_gate_r_b, a_gate_i_w, a_gate_i_b, a_lambda, a_w_out, b_w_in, b_q_norm_g, b_k_norm_g,
           b_w_out, ffn_w1, ffn_w2):
    b, s, d = x.shape
    hd = b_q_norm_g.shape[-1]
    n_heads = b_w_out.shape[1] // hd
    idx_heads = (b_w_in.shape[-1] - n_heads * hd - 3 * hd) // (hd + 1)
    topk = min(TOPK_MAX, s // 4)
    depth = ada_w.shape[0]

    mod = _mod_call(c, ada_w, ada_b)
    mod_rows = mod.reshape(depth, b, 1, mod.shape[-1])

    for i in range(depth):
        j = i // 2
        if i % 2 == 0:
            x = _rglru_call(x, mod[i], norm_mix_g[i], a_w_in[j], a_conv_w[j], a_conv_b[j],
                            a_gate_r_w[j], a_gate_r_b[j], a_gate_i_w[j], a_gate_i_b[j],
                            a_lambda[j], a_w_out[j])
            x = _ffn_call(x, mod_rows[i], norm_ffn_g[i], ffn_w1[i], ffn_w2[i])
        else:
            q, qit, kt, ki, v, wit = _dsa_in_call(x, mod_rows[i], norm_mix_g[i], b_w_in[j],
                                                 b_q_norm_g[j], b_k_norm_g[j], n_heads=n_heads,
                                                 hd=hd, idx_heads=idx_heads)
            bnd = (1.02 * hd * hd ** -0.5) * jnp.max(jnp.abs(b_q_norm_g[j])) * jnp.max(jnp.abs(b_k_norm_g[j]))
            o = _dsa_attn_call(bnd.reshape(1), q, qit, kt, ki, v, wit, hd=hd, idx_heads=idx_heads,
                               topk=topk)
            x = _ffn_call(x, mod_rows[i], norm_ffn_g[i], ffn_w1[i], ffn_w2[i], o_pairs=o,
                          w_o=b_w_out[j])
    return x
```

```python
import functools

import jax
import jax.numpy as jnp
from jax import lax
from jax.experimental import pallas as pl
from jax.experimental.pallas import tpu as pltpu

F32 = jnp.float32
BF16 = jnp.bfloat16
I32 = jnp.int32

RMS_EPS = 1e-6
ROPE_THETA = 10000.0
LRU_C = 8.0
TOPK_MAX = 256
N_MOD = 6

LANES = 128
SUBLANES = 8
VMEM_LIMIT = 56 * 1024 * 1024

NEG_BIG = -1e30
EXP_SAFE_BOUND = 70.0
LOG2E = 1.4426950408889634
GELU_C0 = 0.7978845608028654
GELU_C1 = 0.044715


def _const_spec(shape):
    nd = len(shape)
    return pl.BlockSpec(shape, lambda *_: (0,) * nd, pipeline_mode=pl.Buffered(1))


def _rmsnorm(x, g):
    return x * lax.rsqrt(jnp.mean(x * x, axis=-1, keepdims=True) + RMS_EPS) * g


def _mod_kernel(c_ref, w_ref, b_ref, o_ref):
    c = c_ref[...]
    cond = c * jax.nn.sigmoid(c)
    o_ref[0] = jnp.dot(cond.astype(BF16), w_ref[0].astype(BF16),
                       preferred_element_type=F32) + b_ref[0]


def _mod_call(c, ada_w, ada_b):
    depth, d, n = ada_w.shape
    b = c.shape[0]
    tn = 1536
    return pl.pallas_call(
        _mod_kernel,
        grid=(depth, n // tn),
        in_specs=[pl.BlockSpec((b, d), lambda l, j: (0, 0)),
                  pl.BlockSpec((1, d, tn), lambda l, j: (l, 0, j)),
                  pl.BlockSpec((1, 1, tn), lambda l, j: (l, 0, j))],
        out_specs=pl.BlockSpec((1, b, tn), lambda l, j: (l, 0, j)),
        out_shape=jax.ShapeDtypeStruct((depth, b, n), F32),
        compiler_params=pltpu.CompilerParams(
            dimension_semantics=("arbitrary", "arbitrary"), vmem_limit_bytes=VMEM_LIMIT),
        name="adaln_mod",
    )(c, ada_w, ada_b.reshape(depth, 1, n))


def _rglru_kernel(x_ref, mod_ref, g_ref, win_ref, cw_ref, cb_ref, wr_ref, br_ref, wi_ref, bi_ref,
                  lam_ref, wout_ref, o_ref, h_s, gb_s, xbuf, a_s, b_s, hc, *, rc):
    nb, ts, d = x_ref.shape
    r = ts * nb
    dr = lam_ref.shape[1]
    nblk, blk, _ = wr_ref.shape
    cwid = cw_ref.shape[0]
    tail = (cwid - 1) * nb

    @pl.when(pl.program_id(0) == 0)
    def _():
        xbuf[0:tail, :] = jnp.zeros((tail, dr), F32)
        hc[...] = jnp.zeros((nb, dr), F32)

    x = jnp.swapaxes(x_ref[...], 0, 1)
    sh1 = mod_ref[:, 0:d]
    sc1 = mod_ref[:, d:2 * d]
    g1 = mod_ref[:, 2 * d:3 * d]
    h_s[...] = (_rmsnorm(x, g_ref[...]) * (1.0 + sc1)[None] + sh1[None]).reshape(r, d).astype(BF16)

    sp_h = (0.5 * LRU_C) * jax.nn.softplus(-lam_ref[...])
    br_h = 0.5 * br_ref[...]
    bi_h = 0.5 * bi_ref[...]

    def in_proj(c, n):
        rows = slice(c * rc, (c + 1) * rc)
        hb = h_s[rows, :]
        ux = jnp.dot(hb, win_ref[:, n * blk:(n + 1) * blk], preferred_element_type=F32)
        ug = jnp.dot(hb, win_ref[:, dr + n * blk:dr + (n + 1) * blk], preferred_element_type=F32)
        xbuf[tail + c * rc:tail + (c + 1) * rc, n * blk:(n + 1) * blk] = ux
        return ug

    def conv_and_gate_dots(c, n):
        r0 = c * rc
        cs = slice(n * blk, (n + 1) * blk)
        xc = cb_ref[:, cs] + cw_ref[0:1, cs] * xbuf[r0:r0 + rc, cs]
        for k in range(1, cwid):
            xc = xc + cw_ref[k:k + 1, cs] * xbuf[r0 + k * nb:r0 + k * nb + rc, cs]
        xn = xc.astype(BF16)
        return xc, (jnp.dot(xn, wr_ref[n], preferred_element_type=F32),
                    jnp.dot(xn, wi_ref[n], preferred_element_type=F32))

    def gate_tail(c, n, xc, pre, ug):
        ro = slice(c * rc, (c + 1) * rc)
        cs = slice(n * blk, (n + 1) * blk)
        p = sp_h[:, cs] + sp_h[:, cs] * jnp.tanh(0.5 * pre[0] + br_h[:, cs])
        ig = 0.5 + 0.5 * jnp.tanh(0.5 * pre[1] + bi_h[:, cs])
        a = jnp.exp2(p * (-LOG2E))
        a_s[ro, cs] = a
        b_s[ro, cs] = jnp.sqrt(jnp.tanh(p) * (a * a + 1.0)) * (ig * xc)
        t = jnp.tanh(ug * (GELU_C0 + (GELU_C0 * GELU_C1) * (ug * ug)))
        hu = 0.5 * ug
        gb_s[ro, cs] = hu + hu * t

    units = [(c, n) for c in range(r // rc) for n in range(nblk)]
    ug = in_proj(*units[0])
    for k, (c, n) in enumerate(units):
        xc, pre = conv_and_gate_dots(c, n)
        ug_next = in_proj(*units[k + 1]) if k + 1 < len(units) else None
        gate_tail(c, n, xc, pre, ug)
        ug = ug_next
    xbuf[0:tail, :] = xbuf[r:r + tail, :]

    def step(t, hprev):
        r0 = pl.multiple_of(t * nb, nb)
        hnew = a_s[pl.ds(r0, nb), :] * hprev + b_s[pl.ds(r0, nb), :]
        b_s[pl.ds(r0, nb), :] = hnew
        return hnew

    hc[...] = lax.fori_loop(0, ts, step, hc[...], unroll=8)

    y = b_s[...] * gb_s[...]
    out = jnp.dot(y.astype(BF16), wout_ref[...], preferred_element_type=F32)
    o_ref[...] = jnp.swapaxes(x + out.reshape(ts, nb, d) * g1[None], 0, 1)


def _rglru_call(x, mod0, g, w_in, conv_w, conv_b, wr, br, wi, bi, lam, w_out, *, ts=64, rc=128):
    nb, s, d = x.shape
    assert nb == SUBLANES, "the recurrence keeps the batch on the sublane axis"
    r = ts * nb
    dr = lam.shape[-1]
    tail = (conv_w.shape[0] - 1) * nb
    row = lambda v: v.reshape(1, -1)
    args = (x, mod0, row(g), w_in.astype(BF16), conv_w, row(conv_b), wr.astype(BF16), row(br),
            wi.astype(BF16), row(bi), row(lam), w_out.astype(BF16))
    x_spec = pl.BlockSpec((nb, ts, d), lambda i: (0, i, 0))
    return pl.pallas_call(
        functools.partial(_rglru_kernel, rc=rc),
        grid=(s // ts,),
        in_specs=[x_spec] + [_const_spec(a.shape) for a in args[1:]],
        out_specs=x_spec,
        out_shape=jax.ShapeDtypeStruct((nb, s, d), F32),
        scratch_shapes=[pltpu.VMEM((r, d), BF16), pltpu.VMEM((r, dr), F32),
                        pltpu.VMEM((r + tail, dr), F32), pltpu.VMEM((r, dr), F32),
                        pltpu.VMEM((r, dr), F32), pltpu.VMEM((nb, dr), F32)],
        compiler_params=pltpu.CompilerParams(
            dimension_semantics=("arbitrary",), vmem_limit_bytes=VMEM_LIMIT),
        name="rglru_mixer",
    )(*args)


def _ffn_kernel(*refs, has_proj, fc):
    if has_proj:
        x_ref, mod_ref, g_ref, w1_ref, w2_ref, o_in_ref, wo_ref, out_ref = refs
    else:
        x_ref, mod_ref, g_ref, w1_ref, w2_ref, out_ref = refs
    d = x_ref.shape[1]
    dff = w1_ref.shape[1]
    x = x_ref[...]
    if has_proj:
        g1 = mod_ref[:, 2 * d:3 * d]
        o = jnp.concatenate([o_in_ref[p] for p in range(o_in_ref.shape[0])], axis=-1)
        x = x + g1 * jnp.dot(o, wo_ref[...], preferred_element_type=F32)
    sh2 = mod_ref[:, 3 * d:4 * d]
    sc2 = mod_ref[:, 4 * d:5 * d]
    g2 = mod_ref[:, 5 * d:6 * d]
    hb = (_rmsnorm(x, g_ref[...]) * (1.0 + sc2) + sh2).astype(BF16)
    acc = jnp.zeros(x.shape, F32)
    for c in range(dff // fc):
        hid = jnp.dot(hb, w1_ref[:, c * fc:(c + 1) * fc], preferred_element_type=F32)
        hid = jnp.square(jnp.maximum(hid, 0.0))
        acc = acc + jnp.dot(hid.astype(BF16), w2_ref[c * fc:(c + 1) * fc, :],
                            preferred_element_type=F32)
    out_ref[...] = x + g2 * acc


def _ffn_call(x, mod_l, g, w1, w2, o_pairs=None, w_o=None, *, rf=512, fc=1024):
    b, s, d = x.shape
    has_proj = o_pairs is not None
    args = [x, mod_l, g.reshape(1, -1), w1.astype(BF16), w2.astype(BF16)]
    in_specs = [pl.BlockSpec((None, rf, d), lambda bi, i: (bi, i, 0)),
                pl.BlockSpec((None, 1, mod_l.shape[-1]), lambda bi, i: (bi, 0, 0)),
                _const_spec((1, d)), _const_spec(w1.shape), _const_spec(w2.shape)]
    if has_proj:
        npair = o_pairs.shape[1]
        args += [o_pairs, w_o.astype(BF16)]
        in_specs += [pl.BlockSpec((None, npair, rf, LANES), lambda bi, i: (bi, 0, i, 0)),
                     _const_spec(w_o.shape)]
    return pl.pallas_call(
        functools.partial(_ffn_kernel, has_proj=has_proj, fc=fc),
        grid=(b, s // rf),
        in_specs=in_specs,
        out_specs=pl.BlockSpec((None, rf, d), lambda bi, i: (bi, i, 0)),
        out_shape=jax.ShapeDtypeStruct((b, s, d), F32),
        compiler_params=pltpu.CompilerParams(
            dimension_semantics=("arbitrary", "arbitrary"), vmem_limit_bytes=VMEM_LIMIT),
        name="ffn_proj" if has_proj else "ffn",
    )(*args)


def _rot_half(x, lane):
    return jnp.where((lane & 63) < 32, pltpu.roll(x, 96, 1), pltpu.roll(x, 32, 1))


def _dsa_in_kernel(x_ref, mod_ref, g_ref, w_ref, qg_ref, kg_ref, e_ref, cos_ref, sin_ref,
                   q_ref, qit_ref, kt_ref, ki_ref, v_ref, wit_ref, *, nq, nqi, hd, idx_heads):
    d = x_ref.shape[1]
    tq = x_ref.shape[0]
    x = x_ref[...]
    sh1 = mod_ref[:, 0:d]
    sc1 = mod_ref[:, d:2 * d]
    h = _rmsnorm(x, g_ref[...]) * (1.0 + sc1) + sh1
    u = jnp.dot(h.astype(BF16), w_ref[...], preferred_element_type=F32)
    lane = lax.broadcasted_iota(I32, (tq, LANES), 1)
    cos = cos_ref[...]
    sin = sin_ref[...]
    qg = qg_ref[...]
    e = e_ref[...]
    qscale = hd ** -0.5

    for p in range(nq):
        t = u[:, p * LANES:(p + 1) * LANES]
        t2 = t * t
        hi = t2.astype(BF16)
        lo = (t2 - hi.astype(F32)).astype(BF16)
        ss = (jnp.dot(hi, e, preferred_element_type=F32) + jnp.dot(lo, e, preferred_element_type=F32))
        tn = t * lax.rsqrt(ss * (1.0 / hd) + RMS_EPS) * qg
        q_ref[p] = ((tn * cos + _rot_half(tn, lane) * sin) * (qscale * LOG2E)).astype(BF16)

    for p in range(nqi):
        t = u[:, (nq + p) * LANES:(nq + p + 1) * LANES]
        qit_ref[p] = ((t * cos + _rot_half(t, lane) * sin) * qscale).T.astype(BF16)

    kk = u[:, (nq + nqi) * LANES:(nq + nqi + 1) * LANES]
    left = lane < hd
    ssk = jnp.sum(jnp.where(left, kk * kk, 0.0), axis=-1, keepdims=True)
    fac = jnp.where(left, lax.rsqrt(ssk * (1.0 / hd) + RMS_EPS) * kg_ref[...], 1.0)
    kn = kk * fac
    kr = kn * cos + _rot_half(kn, lane) * sin
    krt = kr.T.astype(BF16)
    zeros = jnp.zeros((hd, tq), BF16)
    kt_ref[0:hd, :] = krt[0:hd]
    kt_ref[hd:2 * hd, :] = zeros
    kt_ref[2 * hd:3 * hd, :] = krt[0:hd]
    ki_r = jnp.where(left, 0.0, kr)
    ki_ref[0] = pltpu.roll(ki_r, hd, 1).astype(BF16)
    ki_ref[1] = ki_r.astype(BF16)

    vw = u[:, (nq + nqi + 1) * LANES:(nq + nqi + 2) * LANES]
    v_ref[...] = jnp.where(left, vw, 1.0).astype(BF16)
    wit_ref[...] = (vw * (idx_heads ** -0.5)).T[hd:hd + idx_heads, :]


def _dsa_in_call(x, mod_l, g, w_in, q_g, k_g, *, n_heads, hd, idx_heads, tq=512):
    b, s, d = x.shape
    assert 2 * hd == LANES, "two heads per lane tile"
    nq = n_heads * hd // LANES
    nqi = idx_heads * hd // LANES
    o_k = n_heads * hd
    o_v = o_k + hd
    o_qi = o_v + hd
    o_ki = o_qi + idx_heads * hd
    o_wi = o_ki + hd
    pad = LANES - hd - idx_heads
    w = jnp.concatenate([w_in[:, :o_k], w_in[:, o_qi:o_ki], w_in[:, o_k:o_v], w_in[:, o_ki:o_wi],
                         w_in[:, o_v:o_qi], w_in[:, o_wi:], jnp.zeros((d, pad), w_in.dtype)],
                        axis=1).astype(BF16)
    ncol = w.shape[1]
    inv = ROPE_THETA ** (-jnp.arange(0, hd, 2, dtype=F32) / hd)
    ang = jnp.arange(s, dtype=F32)[:, None] * inv[None, :]
    cos_t = jnp.tile(jnp.cos(ang), (1, 4))
    sin_h = jnp.sin(ang)
    sin_t = jnp.tile(jnp.concatenate([-sin_h, sin_h], axis=1), (1, 2))
    head_of_lane = jnp.arange(LANES) // hd
    e = (head_of_lane[:, None] == head_of_lane[None, :]).astype(BF16)
    qg = jnp.tile(q_g, 2).reshape(1, LANES)
    kg = jnp.concatenate([k_g, jnp.ones((hd,), k_g.dtype)]).reshape(1, LANES)

    row_spec = pl.BlockSpec((None, tq, LANES), lambda bi, i: (bi, i, 0))
    col_spec = pl.BlockSpec((None, 3 * hd, tq), lambda bi, i: (bi, 0, i))
    return pl.pallas_call(
        functools.partial(_dsa_in_kernel, nq=nq, nqi=nqi, hd=hd, idx_heads=idx_heads),
        grid=(b, s // tq),
        in_specs=[pl.BlockSpec((None, tq, d), lambda bi, i: (bi, i, 0)),
                  pl.BlockSpec((None, 1, mod_l.shape[-1]), lambda bi, i: (bi, 0, 0)),
                  _const_spec((1, d)), _const_spec((d, ncol)), _const_spec((1, LANES)),
                  _const_spec((1, LANES)), _const_spec((LANES, LANES)),
                  pl.BlockSpec((tq, LANES), lambda bi, i: (i, 0)),
                  pl.BlockSpec((tq, LANES), lambda bi, i: (i, 0))],
        out_specs=[pl.BlockSpec((None, nq, tq, LANES), lambda bi, i: (bi, 0, i, 0)),
                   pl.BlockSpec((None, nqi, LANES, tq), lambda bi, i: (bi, 0, 0, i)),
                   col_spec,
                   pl.BlockSpec((None, 2, tq, LANES), lambda bi, i: (bi, 0, i, 0)),
                   row_spec,
                   pl.BlockSpec((None, idx_heads, tq), lambda bi, i: (bi, 0, i))],
        out_shape=[jax.ShapeDtypeStruct((b, nq, s, LANES), BF16),
                   jax.ShapeDtypeStruct((b, nqi, LANES, s), BF16),
                   jax.ShapeDtypeStruct((b, 3 * hd, s), BF16),
                   jax.ShapeDtypeStruct((b, 2, s, LANES), BF16),
                   jax.ShapeDtypeStruct((b, s, LANES), BF16),
                   jax.ShapeDtypeStruct((b, idx_heads, s), F32)],
        compiler_params=pltpu.CompilerParams(
            dimension_semantics=("arbitrary", "arbitrary"), vmem_limit_bytes=VMEM_LIMIT),
        name="dsa_in",
    )(x, mod_l, g.reshape(1, -1), w, qg, kg, e, cos_t, sin_t)


def _dsa_attn_kernel(bnd_ref, qit_ref, wit_ref, q_ref, kt_ref, ki_ref, v_ref, o_ref, sct_s, mask_s,
                     st_s, m_s, acc_s, *, kc, ks, topk, hd, idx_heads, group, tie_from, max_groups,
                     n_blocks, head_passes, sl):
    npair, qb, _ = q_ref.shape
    nh = 2 * npair
    nt = kc // LANES
    nsub = SUBLANES
    i = pl.program_id(1)
    has_sel = i < n_blocks
    has_att = i >= 1
    nk = (i * qb) // kc + 1
    nkp = ((i - 1) * qb) // kc + 1
    slot = lax.rem(i, 2)
    mask_w = mask_s.at[slot]
    mask_r = mask_s.at[1 - slot]
    lane = lax.broadcasted_iota(I32, (qb, LANES), 1)
    left = lane < hd
    kf = float(topk)
    fast = bnd_ref[0] <= EXP_SAFE_BOUND
    merged = jnp.logical_and(jnp.logical_and(has_sel, has_att), fast)

    def lanes_all(x):
        return jnp.concatenate([x] * nt, axis=1)

    def fold(x, op):
        r = x[:, 0:LANES]
        for t in range(1, nt):
            r = op(r, x[:, t * LANES:(t + 1) * LANES])
        return r

    def row_all(x, red):
        return jnp.broadcast_to(red(x, axis=1, keepdims=True), (qb, LANES))

    def slab(src, c, t):
        return src[pl.ds(pl.multiple_of(c * kc + t * sl, sl), sl), :]

    def finish(acc, op, red):
        n = sl // nsub
        acc = acc.reshape(n, nsub, qb)
        r = acc[0]
        for t in range(1, n):
            r = op(r, acc[t])
        return jnp.broadcast_to(red(r, axis=0, keepdims=True), (nsub, qb))

    def reduce_keys(src, fn, op, red, init):
        def body(ci, acc):
            for t in range(kc // sl):
                acc = op(acc, fn(slab(src, ci, t), ci, t))
            return acc
        return finish(lax.fori_loop(0, nk, body, jnp.full((sl, qb), init, F32)), op, red)

    def rows(x):
        return jnp.concatenate([x] * (sl // nsub), axis=0)

    @pl.when(has_sel)
    def _():
        qpos = i * qb + lax.broadcasted_iota(I32, (ks, qb), 1)
        krow = lax.broadcasted_iota(I32, (ks, qb), 0)

        def score_chunk(ci, _):
            for sub in range(kc // ks):
                k0 = pl.multiple_of(ci * kc + sub * ks, ks)
                kis = (ki_ref[0, pl.ds(k0, ks), :], ki_ref[1, pl.ds(k0, ks), :])
                acc = jnp.zeros((ks, qb), F32)
                for hh in range(idx_heads):
                    lg = jnp.dot(kis[hh % 2], qit_ref[hh // 2], preferred_element_type=F32)
                    acc = acc + wit_ref[hh:hh + 1, :] * jnp.maximum(lg, 0.0)
                sct_s[pl.ds(k0, ks), :] = jnp.where((k0 + krow) <= qpos, acc, -jnp.inf)
            return 0

        lax.fori_loop(0, nk, score_chunk, 0)
        mx = reduce_keys(sct_s, lambda blk, c, t: blk, jnp.maximum, jnp.max, -jnp.inf)
        mn = reduce_keys(sct_s, lambda blk, c, t: jnp.where(blk > -jnp.inf, blk, jnp.inf),
                         jnp.minimum, jnp.min, jnp.inf)
        st_s[0] = mn
        st_s[1] = mx + (jnp.abs(mx) * 1e-6 + 1e-30)
        st_s[2] = (i * qb + lax.broadcasted_iota(I32, (nsub, qb), 1) + 1).astype(F32)
        st_s[3] = jnp.zeros((nsub, qb), F32)

    def count_step(st):
        c, lo, hi, clo, chi, acc = st
        mid = 0.5 * lo + 0.5 * hi
        midr = rows(mid)
        for t in range(kc // sl):
            acc = acc + jnp.where(slab(sct_s, c, t) >= midr, 1.0, 0.0)
        last = c == nk - 1
        tot = finish(acc, jnp.add, jnp.sum)
        ge = tot >= kf
        lo = jnp.where(last, jnp.where(ge, mid, lo), lo)
        clo = jnp.where(last, jnp.where(ge, tot, clo), clo)
        hi = jnp.where(last, jnp.where(ge, hi, mid), hi)
        chi = jnp.where(last, jnp.where(ge, chi, tot), chi)
        return (jnp.where(last, 0, c + 1), lo, hi, clo, chi, jnp.where(last, 0.0, acc))

    def scores(ci, h):
        c0 = pl.multiple_of(ci * kc, kc)
        kt = kt_ref[(h % 2) * hd:(h % 2) * hd + LANES, pl.ds(c0, kc)]
        return jnp.dot(q_ref[h // 2], kt, preferred_element_type=F32)

    def pv_step(ci, st, use_max, passes):
        c0 = pl.multiple_of(ci * kc, kc)
        vv = v_ref[pl.ds(c0, kc), :]
        msk = mask_r[:, pl.ds(c0, kc)]
        s_next = scores(ci, 0)
        for h in range(nh):
            s_cur = s_next
            if h + 1 < nh:
                s_next = scores(ci, h + 1)
            if use_max:
                pr = jnp.where(msk > 0, jnp.exp2(s_cur - lanes_all(m_s[h])), 0.0).astype(BF16)
            else:
                pr = jnp.exp2(s_cur).astype(BF16) * msk
            acc_s[h] += jnp.dot(pr, vv, preferred_element_type=F32)
            for _ in range(passes[h] if passes else 0):
                st = count_step(st)
        return st

    @pl.when(has_att)
    def _():
        acc_s[...] = jnp.zeros(acc_s.shape, F32)

    @pl.when(merged)
    def _():
        st = (jnp.int32(0), st_s[0], st_s[1], st_s[2], st_s[3], jnp.zeros((sl, qb), F32))
        st = lax.fori_loop(0, nkp, functools.partial(pv_step, use_max=False, passes=head_passes), st)
        st_s[0], st_s[1], st_s[2], st_s[3] = st[1], st[2], st[3], st[4]

    @pl.when(jnp.logical_and(has_att, jnp.logical_and(fast, jnp.logical_not(has_sel))))
    def _():
        lax.fori_loop(0, nkp, functools.partial(pv_step, use_max=False, passes=None), 0)

    @pl.when(jnp.logical_and(has_att, jnp.logical_not(fast)))
    def _():
        def max_step(ci, _):
            @pl.when(ci == 0)
            def _():
                m_s[...] = jnp.full(m_s.shape, NEG_BIG, F32)
            msk = mask_r[:, pl.ds(pl.multiple_of(ci * kc, kc), kc)]
            for h in range(nh):
                m_s[h] = jnp.maximum(m_s[h], fold(jnp.where(msk > 0, scores(ci, h), NEG_BIG),
                                                  jnp.maximum))
            return 0

        lax.fori_loop(0, nkp, max_step, 0)
        for h in range(nh):
            m_s[h] = row_all(m_s[h], jnp.max)
        lax.fori_loop(0, nkp, functools.partial(pv_step, use_max=True, passes=None), 0)

    @pl.when(has_att)
    def _():
        for p in range(npair):
            a0, a1 = acc_s[2 * p], acc_s[2 * p + 1]
            o0 = a0 / pltpu.roll(a0, hd, 1)
            o1 = a1 / pltpu.roll(a1, hd, 1)
            o_ref[p] = jnp.where(left, o0, pltpu.roll(o1, hd, 1)).astype(BF16)

    @pl.when(has_sel)
    def _():
        def count_ge(thr):
            thr = rows(thr)
            return reduce_keys(sct_s, lambda blk, c, t: jnp.where(blk >= thr, 1.0, 0.0),
                               jnp.add, jnp.sum, 0.0)

        def bisect(_, st):
            lo, hi, clo, chi = st
            mid = 0.5 * lo + 0.5 * hi
            c = count_ge(mid)
            ge = c >= kf
            return (jnp.where(ge, mid, lo), jnp.where(ge, hi, mid),
                    jnp.where(ge, c, clo), jnp.where(ge, chi, c))

        def open_rows(clo, tied):
            return jnp.max(jnp.where(clo > kf, 1.0 - tied, 0.0)) > 0.0

        g0 = jnp.where(merged, tie_from // group, 0)

        def search_group(st):
            g, _, lo, hi, clo, chi, tied = st
            lo, hi, clo, chi = lax.fori_loop(0, group, bisect, (lo, hi, clo, chi))

            def tie_check(tied):
                lor, hir = rows(lo), rows(hi)
                a = reduce_keys(sct_s, lambda blk, c, t: jnp.where(blk >= lor, blk, jnp.inf),
                                jnp.minimum, jnp.min, jnp.inf)
                b = reduce_keys(sct_s, lambda blk, c, t: jnp.where(blk < hir, blk, -jnp.inf),
                                jnp.maximum, jnp.max, -jnp.inf)
                return jnp.where(jnp.logical_and(clo > kf, a == b), 1.0, tied)

            tied = lax.cond(jnp.logical_and((g + 1) * group >= tie_from, open_rows(clo, tied)),
                            tie_check, lambda t: t, tied)
            return (g + 1, open_rows(clo, tied).astype(I32), lo, hi, clo, chi, tied)

        zero = jnp.zeros((nsub, qb), F32)
        _, _, lo, hi, clo, chi, _ = lax.while_loop(
            lambda st: jnp.logical_and(st[1] > 0, st[0] < max_groups), search_group,
            (g0, open_rows(st_s[2], zero).astype(I32), st_s[0], st_s[1], st_s[2], st_s[3], zero))

        def tile_k(x):
            return jnp.concatenate([x] * (ks // nsub), axis=0)

        lok, hik, free = tile_k(lo), tile_k(hi), tile_k(kf - chi)
        tri = jnp.where(lax.broadcasted_iota(I32, (ks, ks), 1) <= lax.broadcasted_iota(I32, (ks, ks), 0),
                        1.0, 0.0).astype(BF16)

        def mask_chunk(ci, carry):
            subs = []
            for t in range(kc // ks):
                k0 = pl.multiple_of(ci * kc + t * ks, ks)
                blk = sct_s[pl.ds(k0, ks), :]
                top = blk >= hik
                tie = jnp.where(top, 0.0, jnp.where(blk >= lok, 1.0, 0.0))
                subs.append((k0, top, tie, jnp.dot(tri, tie.astype(BF16), preferred_element_type=F32)))
            for k0, top, tie, rank in subs:
                keep = jnp.where(rank + tile_k(carry) <= free, tie, 0.0)
                mask_w[:, pl.ds(k0, ks)] = jnp.where(top, 1.0, keep).T.astype(BF16)
                carry = carry + jnp.broadcast_to(rank[ks - 1:ks, :], (nsub, qb))
            return carry

        lax.fori_loop(0, nk, mask_chunk, jnp.zeros((nsub, qb), F32))


def _dsa_attn_call(bnd, q, qit, kt, ki, v, wit, *, hd, idx_heads, topk, qb=256, kc=512, ks=128,
                   sl=32):
    b, npair, s, _ = q.shape
    nqi = qit.shape[1]
    assert kc % qb == 0 and s % kc == 0 and kc % ks == 0 and ks % sl == 0 and sl % SUBLANES == 0
    nb = s // qb
    head_passes = tuple(2 if h % 2 == 0 else 1 for h in range(2 * npair))
    prev = lambda i: jnp.maximum(i - 1, 0)
    cur = lambda i: jnp.minimum(i, nb - 1)
    return pl.pallas_call(
        functools.partial(_dsa_attn_kernel, kc=kc, ks=ks, topk=topk, hd=hd, idx_heads=idx_heads,
                          group=4, tie_from=16, max_groups=96, n_blocks=nb,
                          head_passes=head_passes, sl=sl),
        grid=(b, nb + 1),
        in_specs=[pl.BlockSpec(memory_space=pltpu.SMEM),
                  pl.BlockSpec((None, nqi, LANES, qb), lambda bi, i: (bi, 0, 0, cur(i))),
                  pl.BlockSpec((None, idx_heads, qb), lambda bi, i: (bi, 0, cur(i))),
                  pl.BlockSpec((None, npair, qb, LANES), lambda bi, i: (bi, 0, prev(i), 0)),
                  pl.BlockSpec((None, 3 * hd, s), lambda bi, i: (bi, 0, 0)),
                  pl.BlockSpec((None, 2, s, LANES), lambda bi, i: (bi, 0, 0, 0)),
                  pl.BlockSpec((None, s, LANES), lambda bi, i: (bi, 0, 0))],
        out_specs=pl.BlockSpec((None, npair, qb, LANES), lambda bi, i: (bi, 0, prev(i), 0)),
        out_shape=jax.ShapeDtypeStruct((b, npair, s, LANES), BF16),
        scratch_shapes=[pltpu.VMEM((s, qb), F32),
                        pltpu.VMEM((2, qb, s), BF16), pltpu.VMEM((4, SUBLANES, qb), F32),
                        pltpu.VMEM((2 * npair, qb, LANES), F32),
                        pltpu.VMEM((2 * npair, qb, LANES), F32)],
        compiler_params=pltpu.CompilerParams(
            dimension_semantics=("arbitrary", "arbitrary"), vmem_limit_bytes=VMEM_LIMIT),
        name="dsa_attn",
    )(bnd, qit, wit, q, kt, ki, v)


def kernel(x, c, norm_mix_g, norm_ffn_g, ada_w, ada_b, a_w_in, a_conv_w, a_conv_b, a_gate_r_w,
           a_gate_r_b, a_gate_i_w, a_gate_i_b, a_lambda, a_w_out, b_w_in, b_q_norm_g, b_k_norm_g,
           b_w_out, ffn_w1, ffn_w2):
    b, s, d = x.shape
    hd = b_q_norm_g.shape[-1]
    n_heads = b_w_out.shape[1] // hd
    idx_heads = (b_w_in.shape[-1] - n_heads * hd - 3 * hd) // (hd + 1)
    topk = min(TOPK_MAX, s // 4)
    depth = ada_w.shape[0]

    mod = _mod_call(c, ada_w, ada_b)
    mod_rows = mod.reshape(depth, b, 1, mod.shape[-1])

    for i in range(depth):
        j = i // 2
        if i % 2 == 0:
            x = _rglru_call(x, mod[i], norm_mix_g[i], a_w_in[j], a_conv_w[j], a_conv_b[j],
                            a_gate_r_w[j], a_gate_r_b[j], a_gate_i_w[j], a_gate_i_b[j],
                            a_lambda[j], a_w_out[j])
            x = _ffn_call(x, mod_rows[i], norm_ffn_g[i], ffn_w1[i], ffn_w2[i])
        else:
            q, qit, kt, ki, v, wit = _dsa_in_call(x, mod_rows[i], norm_mix_g[i], b_w_in[j],
                                                 b_q_norm_g[j], b_k_norm_g[j], n_heads=n_heads,
                                                 hd=hd, idx_heads=idx_heads)
            bnd = (1.02 * hd * hd ** -0.5) * jnp.max(jnp.abs(b_q_norm_g[j])) * jnp.max(jnp.abs(b_k_norm_g[j]))
            o = _dsa_attn_call(bnd.reshape(1), q, qit, kt, ki, v, wit, hd=hd, idx_heads=idx_heads,
                               topk=topk)
            x = _ffn_call(x, mod_rows[i], norm_ffn_g[i], ffn_w1[i], ffn_w2[i], o_pairs=o,
                          w_o=b_w_out[j])
    return x
```

```python
import functools

import jax
import jax.numpy as jnp
from jax import lax
from jax.experimental import pallas as pl
from jax.experimental.pallas import tpu as pltpu

F32 = jnp.float32
BF16 = jnp.bfloat16
I32 = jnp.int32

RMS_EPS = 1e-6
ROPE_THETA = 10000.0
LRU_C = 8.0
TOPK_MAX = 256
N_MOD = 6

LANES = 128
SUBLANES = 8
VMEM_LIMIT = 56 * 1024 * 1024

NEG_BIG = -1e30
EXP_SAFE_BOUND = 70.0
LOG2E = 1.4426950408889634
GELU_C0 = 0.7978845608028654
GELU_C1 = 0.044715


def _const_spec(shape):
    nd = len(shape)
    return pl.BlockSpec(shape, lambda *_: (0,) * nd, pipeline_mode=pl.Buffered(1))


def _rmsnorm(x, g):
    return x * lax.rsqrt(jnp.mean(x * x, axis=-1, keepdims=True) + RMS_EPS) * g


def _mod_kernel(c_ref, w_ref, b_ref, o_ref):
    c = c_ref[...]
    cond = c * jax.nn.sigmoid(c)
    o_ref[0] = jnp.dot(cond.astype(BF16), w_ref[0].astype(BF16),
                       preferred_element_type=F32) + b_ref[0]


def _mod_call(c, ada_w, ada_b):
    depth, d, n = ada_w.shape
    b = c.shape[0]
    tn = 1536
    return pl.pallas_call(
        _mod_kernel,
        grid=(depth, n // tn),
        in_specs=[pl.BlockSpec((b, d), lambda l, j: (0, 0)),
                  pl.BlockSpec((1, d, tn), lambda l, j: (l, 0, j)),
                  pl.BlockSpec((1, 1, tn), lambda l, j: (l, 0, j))],
        out_specs=pl.BlockSpec((1, b, tn), lambda l, j: (l, 0, j)),
        out_shape=jax.ShapeDtypeStruct((depth, b, n), F32),
        compiler_params=pltpu.CompilerParams(
            dimension_semantics=("arbitrary", "arbitrary"), vmem_limit_bytes=VMEM_LIMIT),
        name="adaln_mod",
    )(c, ada_w, ada_b.reshape(depth, 1, n))


def _rglru_kernel(x_ref, mod_ref, g_ref, win_ref, cw_ref, cb_ref, wr_ref, br_ref, wi_ref, bi_ref,
                  lam_ref, wout_ref, o_ref, h_s, gb_s, xbuf, a_s, b_s, hc, *, rc):
    nb, ts, d = x_ref.shape
    r = ts * nb
    dr = lam_ref.shape[1]
    nblk, blk, _ = wr_ref.shape
    cwid = cw_ref.shape[0]
    tail = (cwid - 1) * nb

    @pl.when(pl.program_id(0) == 0)
    def _():
        xbuf[0:tail, :] = jnp.zeros((tail, dr), F32)
        hc[...] = jnp.zeros((nb, dr), F32)

    x = jnp.swapaxes(x_ref[...], 0, 1)
    sh1 = mod_ref[:, 0:d]
    sc1 = mod_ref[:, d:2 * d]
    g1 = mod_ref[:, 2 * d:3 * d]
    h_s[...] = (_rmsnorm(x, g_ref[...]) * (1.0 + sc1)[None] + sh1[None]).reshape(r, d).astype(BF16)

    sp_h = (0.5 * LRU_C) * jax.nn.softplus(-lam_ref[...])
    br_h = 0.5 * br_ref[...]
    bi_h = 0.5 * bi_ref[...]

    def in_proj(c, n):
        rows = slice(c * rc, (c + 1) * rc)
        hb = h_s[rows, :]
        ux = jnp.dot(hb, win_ref[:, n * blk:(n + 1) * blk], preferred_element_type=F32)
        ug = jnp.dot(hb, win_ref[:, dr + n * blk:dr + (n + 1) * blk], preferred_element_type=F32)
        xbuf[tail + c * rc:tail + (c + 1) * rc, n * blk:(n + 1) * blk] = ux
        return ug

    def conv_and_gate_dots(c, n):
        r0 = c * rc
        cs = slice(n * blk, (n + 1) * blk)
        xc = cb_ref[:, cs] + cw_ref[0:1, cs] * xbuf[r0:r0 + rc, cs]
        for k in range(1, cwid):
            xc = xc + cw_ref[k:k + 1, cs] * xbuf[r0 + k * nb:r0 + k * nb + rc, cs]
        xn = xc.astype(BF16)
        return xc, (jnp.dot(xn, wr_ref[n], preferred_element_type=F32),
                    jnp.dot(xn, wi_ref[n], preferred_element_type=F32))

    def gate_tail(c, n, xc, pre, ug):
        ro = slice(c * rc, (c + 1) * rc)
        cs = slice(n * blk, (n + 1) * blk)
        p = sp_h[:, cs] + sp_h[:, cs] * jnp.tanh(0.5 * pre[0] + br_h[:, cs])
        ig = 0.5 + 0.5 * jnp.tanh(0.5 * pre[1] + bi_h[:, cs])
        a = jnp.exp2(p * (-LOG2E))
        a_s[ro, cs] = a
        b_s[ro, cs] = jnp.sqrt(jnp.tanh(p) * (a * a + 1.0)) * (ig * xc)
        t = jnp.tanh(ug * (GELU_C0 + (GELU_C0 * GELU_C1) * (ug * ug)))
        hu = 0.5 * ug
        gb_s[ro, cs] = hu + hu * t

    units = [(c, n) for c in range(r // rc) for n in range(nblk)]
    ug = in_proj(*units[0])
    for k, (c, n) in enumerate(units):
        xc, pre = conv_and_gate_dots(c, n)
        ug_next = in_proj(*units[k + 1]) if k + 1 < len(units) else None
        gate_tail(c, n, xc, pre, ug)
        ug = ug_next
    xbuf[0:tail, :] = xbuf[r:r + tail, :]

    def step(t, hprev):
        r0 = pl.multiple_of(t * nb, nb)
        hnew = a_s[pl.ds(r0, nb), :] * hprev + b_s[pl.ds(r0, nb), :]
        b_s[pl.ds(r0, nb), :] = hnew
        return hnew

    hc[...] = lax.fori_loop(0, ts, step, hc[...], unroll=8)

    y = b_s[...] * gb_s[...]
    out = jnp.dot(y.astype(BF16), wout_ref[...], preferred_element_type=F32)
    o_ref[...] = jnp.swapaxes(x + out.reshape(ts, nb, d) * g1[None], 0, 1)


def _rglru_call(x, mod0, g, w_in, conv_w, conv_b, wr, br, wi, bi, lam, w_out, *, ts=64, rc=128):
    nb, s, d = x.shape
    assert nb == SUBLANES, "the recurrence keeps the batch on the sublane axis"
    r = ts * nb
    dr = lam.shape[-1]
    tail = (conv_w.shape[0] - 1) * nb
    row = lambda v: v.reshape(1, -1)
    args = (x, mod0, row(g), w_in.astype(BF16), conv_w, row(conv_b), wr.astype(BF16), row(br),
            wi.astype(BF16), row(bi), row(lam), w_out.astype(BF16))
    x_spec = pl.BlockSpec((nb, ts, d), lambda i: (0, i, 0))
    return pl.pallas_call(
        functools.partial(_rglru_kernel, rc=rc),
        grid=(s // ts,),
        in_specs=[x_spec] + [_const_spec(a.shape) for a in args[1:]],
        out_specs=x_spec,
        out_shape=jax.ShapeDtypeStruct((nb, s, d), F32),
        scratch_shapes=[pltpu.VMEM((r, d), BF16), pltpu.VMEM((r, dr), F32),
                        pltpu.VMEM((r + tail, dr), F32), pltpu.VMEM((r, dr), F32),
                        pltpu.VMEM((r, dr), F32), pltpu.VMEM((nb, dr), F32)],
        compiler_params=pltpu.CompilerParams(
            dimension_semantics=("arbitrary",), vmem_limit_bytes=VMEM_LIMIT),
        name="rglru_mixer",
    )(*args)


def _ffn_kernel(*refs, has_proj, fc):
    if has_proj:
        x_ref, mod_ref, g_ref, w1_ref, w2_ref, o_in_ref, wo_ref, out_ref = refs
    else:
        x_ref, mod_ref, g_ref, w1_ref, w2_ref, out_ref = refs
    d = x_ref.shape[1]
    dff = w1_ref.shape[1]
    x = x_ref[...]
    if has_proj:
        g1 = mod_ref[:, 2 * d:3 * d]
        o = jnp.concatenate([o_in_ref[p] for p in range(o_in_ref.shape[0])], axis=-1)
        x = x + g1 * jnp.dot(o, wo_ref[...], preferred_element_type=F32)
    sh2 = mod_ref[:, 3 * d:4 * d]
    sc2 = mod_ref[:, 4 * d:5 * d]
    g2 = mod_ref[:, 5 * d:6 * d]
    hb = (_rmsnorm(x, g_ref[...]) * (1.0 + sc2) + sh2).astype(BF16)
    acc = jnp.zeros(x.shape, F32)
    for c in range(dff // fc):
        hid = jnp.dot(hb, w1_ref[:, c * fc:(c + 1) * fc], preferred_element_type=F32)
        hid = jnp.square(jnp.maximum(hid, 0.0))
        acc = acc + jnp.dot(hid.astype(BF16), w2_ref[c * fc:(c + 1) * fc, :],
                            preferred_element_type=F32)
    out_ref[...] = x + g2 * acc


def _ffn_call(x, mod_l, g, w1, w2, o_pairs=None, w_o=None, *, rf=512, fc=1024):
    b, s, d = x.shape
    has_proj = o_pairs is not None
    args = [x, mod_l, g.reshape(1, -1), w1.astype(BF16), w2.astype(BF16)]
    in_specs = [pl.BlockSpec((None, rf, d), lambda bi, i: (bi, i, 0)),
                pl.BlockSpec((None, 1, mod_l.shape[-1]), lambda bi, i: (bi, 0, 0)),
                _const_spec((1, d)), _const_spec(w1.shape), _const_spec(w2.shape)]
    if has_proj:
        npair = o_pairs.shape[1]
        args += [o_pairs, w_o.astype(BF16)]
        in_specs += [pl.BlockSpec((None, npair, rf, LANES), lambda bi, i: (bi, 0, i, 0)),
                     _const_spec(w_o.shape)]
    return pl.pallas_call(
        functools.partial(_ffn_kernel, has_proj=has_proj, fc=fc),
        grid=(b, s // rf),
        in_specs=in_specs,
        out_specs=pl.BlockSpec((None, rf, d), lambda bi, i: (bi, i, 0)),
        out_shape=jax.ShapeDtypeStruct((b, s, d), F32),
        compiler_params=pltpu.CompilerParams(
            dimension_semantics=("arbitrary", "arbitrary"), vmem_limit_bytes=VMEM_LIMIT),
        name="ffn_proj" if has_proj else "ffn",
    )(*args)


def _rot_half(x, lane):
    return jnp.where((lane & 63) < 32, pltpu.roll(x, 96, 1), pltpu.roll(x, 32, 1))


def _dsa_in_kernel(x_ref, mod_ref, g_ref, w_ref, qg_ref, kg_ref, e_ref, cos_ref, sin_ref,
                   qt_ref, qit_ref, kk_ref, vt_ref, wit_ref, *, nq, nqi, hd, idx_heads):
    d = x_ref.shape[1]
    tq = x_ref.shape[0]
    x = x_ref[...]
    sh1 = mod_ref[:, 0:d]
    sc1 = mod_ref[:, d:2 * d]
    h = _rmsnorm(x, g_ref[...]) * (1.0 + sc1) + sh1
    u = jnp.dot(h.astype(BF16), w_ref[...], preferred_element_type=F32)
    lane = lax.broadcasted_iota(I32, (tq, LANES), 1)
    cos = cos_ref[...]
    sin = sin_ref[...]
    qg = qg_ref[...]
    e = e_ref[...]
    qscale = hd ** -0.5

    for p in range(nq):
        t = u[:, p * LANES:(p + 1) * LANES]
        t2 = t * t
        hi = t2.astype(BF16)
        lo = (t2 - hi.astype(F32)).astype(BF16)
        ss = (jnp.dot(hi, e, preferred_element_type=F32) + jnp.dot(lo, e, preferred_element_type=F32))
        tn = t * lax.rsqrt(ss * (1.0 / hd) + RMS_EPS) * qg
        qt_ref[p] = ((tn * cos + _rot_half(tn, lane) * sin) * (qscale * LOG2E)).T.astype(BF16)

    for p in range(nqi):
        t = u[:, (nq + p) * LANES:(nq + p + 1) * LANES]
        qit_ref[p] = ((t * cos + _rot_half(t, lane) * sin) * qscale).T.astype(BF16)

    kk = u[:, (nq + nqi) * LANES:(nq + nqi + 1) * LANES]
    left = lane < hd
    ssk = jnp.sum(jnp.where(left, kk * kk, 0.0), axis=-1, keepdims=True)
    fac = jnp.where(left, lax.rsqrt(ssk * (1.0 / hd) + RMS_EPS) * kg_ref[...], 1.0)
    kn = kk * fac
    kr = kn * cos + _rot_half(kn, lane) * sin
    k_l = jnp.where(left, kr, 0.0)
    ki_r = jnp.where(left, 0.0, kr)
    kk_ref[0] = k_l.astype(BF16)
    kk_ref[1] = pltpu.roll(k_l, hd, 1).astype(BF16)
    kk_ref[2] = pltpu.roll(ki_r, hd, 1).astype(BF16)
    kk_ref[3] = ki_r.astype(BF16)

    vw = u[:, (nq + nqi + 1) * LANES:(nq + nqi + 2) * LANES]
    vt_ref[...] = jnp.where(left, vw, 1.0).T.astype(BF16)
    wit_ref[...] = (vw * (idx_heads ** -0.5)).T[hd:hd + idx_heads, :]


def _dsa_in_call(x, mod_l, g, w_in, q_g, k_g, *, n_heads, hd, idx_heads, tq=512):
    b, s, d = x.shape
    assert 2 * hd == LANES, "two heads per lane tile"
    nq = n_heads * hd // LANES
    nqi = idx_heads * hd // LANES
    o_k = n_heads * hd
    o_v = o_k + hd
    o_qi = o_v + hd
    o_ki = o_qi + idx_heads * hd
    o_wi = o_ki + hd
    pad = LANES - hd - idx_heads
    w = jnp.concatenate([w_in[:, :o_k], w_in[:, o_qi:o_ki], w_in[:, o_k:o_v], w_in[:, o_ki:o_wi],
                         w_in[:, o_v:o_qi], w_in[:, o_wi:], jnp.zeros((d, pad), w_in.dtype)],
                        axis=1).astype(BF16)
    ncol = w.shape[1]
    inv = ROPE_THETA ** (-jnp.arange(0, hd, 2, dtype=F32) / hd)
    ang = jnp.arange(s, dtype=F32)[:, None] * inv[None, :]
    cos_t = jnp.tile(jnp.cos(ang), (1, 4))
    sin_h = jnp.sin(ang)
    sin_t = jnp.tile(jnp.concatenate([-sin_h, sin_h], axis=1), (1, 2))
    head_of_lane = jnp.arange(LANES) // hd
    e = (head_of_lane[:, None] == head_of_lane[None, :]).astype(BF16)
    qg = jnp.tile(q_g, 2).reshape(1, LANES)
    kg = jnp.concatenate([k_g, jnp.ones((hd,), k_g.dtype)]).reshape(1, LANES)

    return pl.pallas_call(
        functools.partial(_dsa_in_kernel, nq=nq, nqi=nqi, hd=hd, idx_heads=idx_heads),
        grid=(b, s // tq),
        in_specs=[pl.BlockSpec((None, tq, d), lambda bi, i: (bi, i, 0)),
                  pl.BlockSpec((None, 1, mod_l.shape[-1]), lambda bi, i: (bi, 0, 0)),
                  _const_spec((1, d)), _const_spec((d, ncol)), _const_spec((1, LANES)),
                  _const_spec((1, LANES)), _const_spec((LANES, LANES)),
                  pl.BlockSpec((tq, LANES), lambda bi, i: (i, 0)),
                  pl.BlockSpec((tq, LANES), lambda bi, i: (i, 0))],
        out_specs=[pl.BlockSpec((None, nq, LANES, tq), lambda bi, i: (bi, 0, 0, i)),
                   pl.BlockSpec((None, nqi, LANES, tq), lambda bi, i: (bi, 0, 0, i)),
                   pl.BlockSpec((None, 4, tq, LANES), lambda bi, i: (bi, 0, i, 0)),
                   pl.BlockSpec((None, LANES, tq), lambda bi, i: (bi, 0, i)),
                   pl.BlockSpec((None, idx_heads, tq), lambda bi, i: (bi, 0, i))],
        out_shape=[jax.ShapeDtypeStruct((b, nq, LANES, s), BF16),
                   jax.ShapeDtypeStruct((b, nqi, LANES, s), BF16),
                   jax.ShapeDtypeStruct((b, 4, s, LANES), BF16),
                   jax.ShapeDtypeStruct((b, LANES, s), BF16),
                   jax.ShapeDtypeStruct((b, idx_heads, s), F32)],
        compiler_params=pltpu.CompilerParams(
            dimension_semantics=("arbitrary", "arbitrary"), vmem_limit_bytes=VMEM_LIMIT),
        name="dsa_in",
    )(x, mod_l, g.reshape(1, -1), w, qg, kg, e, cos_t, sin_t)


def _dsa_attn_kernel(bnd_ref, qit_ref, wit_ref, qt_ref, kk_ref, vt_ref, o_ref, sct_s, mask_s,
                     st_s, m_s, acc_s, *, kc, ks, topk, hd, idx_heads, group, tie_from, max_groups,
                     n_blocks, head_passes, sl):
    npair, _, qb = qt_ref.shape
    nh = 2 * npair
    nsub = SUBLANES
    i = pl.program_id(1)
    has_sel = i < n_blocks
    has_att = i >= 1
    nk = (i * qb) // kc + 1
    nkp = ((i - 1) * qb) // kc + 1
    slot = lax.rem(i, 2)
    mask_w = mask_s.at[slot]
    mask_r = mask_s.at[1 - slot]
    kf = float(topk)
    fast = bnd_ref[0] <= EXP_SAFE_BOUND
    merged = jnp.logical_and(jnp.logical_and(has_sel, has_att), fast)

    def slab(src, c, t):
        return src[pl.ds(pl.multiple_of(c * kc + t * sl, sl), sl), :]

    def finish(acc, op, red):
        n = sl // nsub
        acc = acc.reshape(n, nsub, qb)
        r = acc[0]
        for t in range(1, n):
            r = op(r, acc[t])
        return jnp.broadcast_to(red(r, axis=0, keepdims=True), (nsub, qb))

    def reduce_keys(src, fn, op, red, init):
        def body(ci, acc):
            for t in range(kc // sl):
                acc = op(acc, fn(slab(src, ci, t), ci, t))
            return acc
        return finish(lax.fori_loop(0, nk, body, jnp.full((sl, qb), init, F32)), op, red)

    def rows(x):
        return jnp.concatenate([x] * (sl // nsub), axis=0)

    @pl.when(has_sel)
    def _():
        qpos = i * qb + lax.broadcasted_iota(I32, (ks, qb), 1)
        krow = lax.broadcasted_iota(I32, (ks, qb), 0)

        def score_chunk(ci, _):
            for sub in range(kc // ks):
                k0 = pl.multiple_of(ci * kc + sub * ks, ks)
                kis = (kk_ref[2, pl.ds(k0, ks), :], kk_ref[3, pl.ds(k0, ks), :])
                acc = jnp.zeros((ks, qb), F32)
                for hh in range(idx_heads):
                    lg = jnp.dot(kis[hh % 2], qit_ref[hh // 2], preferred_element_type=F32)
                    acc = acc + wit_ref[hh:hh + 1, :] * jnp.maximum(lg, 0.0)
                sct_s[pl.ds(k0, ks), :] = jnp.where((k0 + krow) <= qpos, acc, -jnp.inf)
            return 0

        lax.fori_loop(0, nk, score_chunk, 0)
        mx = reduce_keys(sct_s, lambda blk, c, t: blk, jnp.maximum, jnp.max, -jnp.inf)
        mn = reduce_keys(sct_s, lambda blk, c, t: jnp.where(blk > -jnp.inf, blk, jnp.inf),
                         jnp.minimum, jnp.min, jnp.inf)
        st_s[0] = mn
        st_s[1] = mx + (jnp.abs(mx) * 1e-6 + 1e-30)
        st_s[2] = (i * qb + lax.broadcasted_iota(I32, (nsub, qb), 1) + 1).astype(F32)
        st_s[3] = jnp.zeros((nsub, qb), F32)

    def count_step(st):
        c, lo, hi, clo, chi, acc = st
        mid = 0.5 * lo + 0.5 * hi
        midr = rows(mid)
        for t in range(kc // sl):
            acc = acc + jnp.where(slab(sct_s, c, t) >= midr, 1.0, 0.0)
        last = c == nk - 1
        tot = finish(acc, jnp.add, jnp.sum)
        ge = tot >= kf
        lo = jnp.where(last, jnp.where(ge, mid, lo), lo)
        clo = jnp.where(last, jnp.where(ge, tot, clo), clo)
        hi = jnp.where(last, jnp.where(ge, hi, mid), hi)
        chi = jnp.where(last, jnp.where(ge, chi, tot), chi)
        return (jnp.where(last, 0, c + 1), lo, hi, clo, chi, jnp.where(last, 0.0, acc))

    def scores(ci, h):
        kk = kk_ref[h % 2, pl.ds(pl.multiple_of(ci * kc, kc), kc), :]
        return jnp.dot(kk, qt_ref[h // 2], preferred_element_type=F32)

    def pv_step(ci, st, use_max, passes):
        c0 = pl.multiple_of(ci * kc, kc)
        vt = vt_ref[:, pl.ds(c0, kc)]
        msk = mask_r[pl.ds(c0, kc), :]
        s_next = scores(ci, 0)
        for h in range(nh):
            s_cur = s_next
            if h + 1 < nh:
                s_next = scores(ci, h + 1)
            if use_max:
                pr = jnp.where(msk > 0, jnp.exp2(s_cur - m_s[h][0:1, :]), 0.0).astype(BF16)
            else:
                pr = jnp.exp2(s_cur).astype(BF16) * msk
            acc_s[h] += jnp.dot(vt, pr, preferred_element_type=F32)
            for _ in range(passes[h] if passes else 0):
                st = count_step(st)
        return st

    @pl.when(has_att)
    def _():
        acc_s[...] = jnp.zeros(acc_s.shape, F32)

    @pl.when(merged)
    def _():
        st = (jnp.int32(0), st_s[0], st_s[1], st_s[2], st_s[3], jnp.zeros((sl, qb), F32))
        st = lax.fori_loop(0, nkp, functools.partial(pv_step, use_max=False, passes=head_passes), st)
        st_s[0], st_s[1], st_s[2], st_s[3] = st[1], st[2], st[3], st[4]

    @pl.when(jnp.logical_and(has_att, jnp.logical_and(fast, jnp.logical_not(has_sel))))
    def _():
        lax.fori_loop(0, nkp, functools.partial(pv_step, use_max=False, passes=None), 0)

    @pl.when(jnp.logical_and(has_att, jnp.logical_not(fast)))
    def _():
        def max_step(ci, _):
            @pl.when(ci == 0)
            def _():
                m_s[...] = jnp.full(m_s.shape, NEG_BIG, F32)
            msk = mask_r[pl.ds(pl.multiple_of(ci * kc, kc), kc), :]
            for h in range(nh):
                mx = jnp.max(jnp.where(msk > 0, scores(ci, h), NEG_BIG), axis=0, keepdims=True)
                m_s[h] = jnp.maximum(m_s[h], jnp.broadcast_to(mx, (nsub, qb)))
            return 0

        lax.fori_loop(0, nkp, max_step, 0)
        lax.fori_loop(0, nkp, functools.partial(pv_step, use_max=True, passes=None), 0)

    @pl.when(has_att)
    def _():
        for p in range(npair):
            a0, a1 = acc_s[2 * p], acc_s[2 * p + 1]
            ot = jnp.concatenate([a0[0:hd] / a0[hd:2 * hd], a1[0:hd] / a1[hd:2 * hd]], axis=0)
            o_ref[p] = ot.T.astype(BF16)

    @pl.when(has_sel)
    def _():
        def count_ge(thr):
            thr = rows(thr)
            return reduce_keys(sct_s, lambda blk, c, t: jnp.where(blk >= thr, 1.0, 0.0),
                               jnp.add, jnp.sum, 0.0)

        def bisect(_, st):
            lo, hi, clo, chi = st
            mid = 0.5 * lo + 0.5 * hi
            c = count_ge(mid)
            ge = c >= kf
            return (jnp.where(ge, mid, lo), jnp.where(ge, hi, mid),
                    jnp.where(ge, c, clo), jnp.where(ge, chi, c))

        def open_rows(clo, tied):
            return jnp.max(jnp.where(clo > kf, 1.0 - tied, 0.0)) > 0.0

        g0 = jnp.where(merged, tie_from // group, 0)

        def search_group(st):
            g, _, lo, hi, clo, chi, tied = st
            lo, hi, clo, chi = lax.fori_loop(0, group, bisect, (lo, hi, clo, chi))

            def tie_check(tied):
                lor, hir = rows(lo), rows(hi)
                a = reduce_keys(sct_s, lambda blk, c, t: jnp.where(blk >= lor, blk, jnp.inf),
                                jnp.minimum, jnp.min, jnp.inf)
                b = reduce_keys(sct_s, lambda blk, c, t: jnp.where(blk < hir, blk, -jnp.inf),
                                jnp.maximum, jnp.max, -jnp.inf)
                return jnp.where(jnp.logical_and(clo > kf, a == b), 1.0, tied)

            tied = lax.cond(jnp.logical_and((g + 1) * group >= tie_from, open_rows(clo, tied)),
                            tie_check, lambda t: t, tied)
            return (g + 1, open_rows(clo, tied).astype(I32), lo, hi, clo, chi, tied)

        zero = jnp.zeros((nsub, qb), F32)
        _, _, lo, hi, clo, chi, _ = lax.while_loop(
            lambda st: jnp.logical_and(st[1] > 0, st[0] < max_groups), search_group,
            (g0, open_rows(st_s[2], zero).astype(I32), st_s[0], st_s[1], st_s[2], st_s[3], zero))

        def tile_k(x):
            return jnp.concatenate([x] * (ks // nsub), axis=0)

        lok, hik, free = tile_k(lo), tile_k(hi), tile_k(kf - chi)
        tri = jnp.where(lax.broadcasted_iota(I32, (ks, ks), 1) <= lax.broadcasted_iota(I32, (ks, ks), 0),
                        1.0, 0.0).astype(BF16)

        def mask_chunk(ci, carry):
            subs = []
            for t in range(kc // ks):
                k0 = pl.multiple_of(ci * kc + t * ks, ks)
                blk = sct_s[pl.ds(k0, ks), :]
                top = blk >= hik
                tie = jnp.where(top, 0.0, jnp.where(blk >= lok, 1.0, 0.0))
                subs.append((k0, top, tie, jnp.dot(tri, tie.astype(BF16), preferred_element_type=F32)))
            for k0, top, tie, rank in subs:
                keep = jnp.where(rank + tile_k(carry) <= free, tie, 0.0)
                mask_w[pl.ds(k0, ks), :] = jnp.where(top, 1.0, keep).astype(BF16)
                carry = carry + jnp.broadcast_to(rank[ks - 1:ks, :], (nsub, qb))
            return carry

        lax.fori_loop(0, nk, mask_chunk, jnp.zeros((nsub, qb), F32))


def _dsa_attn_call(bnd, qt, qit, kk, vt, wit, *, hd, idx_heads, topk, qb=256, kc=512, ks=128,
                   sl=32):
    b, npair, _, s = qt.shape
    nqi = qit.shape[1]
    assert kc % qb == 0 and s % kc == 0 and kc % ks == 0 and ks % sl == 0 and sl % SUBLANES == 0
    nb = s // qb
    head_passes = tuple(2 if h % 2 == 0 else 1 for h in range(2 * npair))
    assert 2 * hd == LANES
    prev = lambda i: jnp.maximum(i - 1, 0)
    cur = lambda i: jnp.minimum(i, nb - 1)
    return pl.pallas_call(
        functools.partial(_dsa_attn_kernel, kc=kc, ks=ks, topk=topk, hd=hd, idx_heads=idx_heads,
                          group=4, tie_from=16, max_groups=96, n_blocks=nb,
                          head_passes=head_passes, sl=sl),
        grid=(b, nb + 1),
        in_specs=[pl.BlockSpec(memory_space=pltpu.SMEM),
                  pl.BlockSpec((None, nqi, LANES, qb), lambda bi, i: (bi, 0, 0, cur(i))),
                  pl.BlockSpec((None, idx_heads, qb), lambda bi, i: (bi, 0, cur(i))),
                  pl.BlockSpec((None, npair, LANES, qb), lambda bi, i: (bi, 0, 0, prev(i))),
                  pl.BlockSpec((None, 4, s, LANES), lambda bi, i: (bi, 0, 0, 0)),
                  pl.BlockSpec((None, LANES, s), lambda bi, i: (bi, 0, 0))],
        out_specs=pl.BlockSpec((None, npair, qb, LANES), lambda bi, i: (bi, 0, prev(i), 0)),
        out_shape=jax.ShapeDtypeStruct((b, npair, s, LANES), BF16),
        scratch_shapes=[pltpu.VMEM((s, qb), F32),
                        pltpu.VMEM((2, s, qb), BF16), pltpu.VMEM((4, SUBLANES, qb), F32),
                        pltpu.VMEM((2 * npair, SUBLANES, qb), F32),
                        pltpu.VMEM((2 * npair, LANES, qb), F32)],
        compiler_params=pltpu.CompilerParams(
            dimension_semantics=("arbitrary", "arbitrary"), vmem_limit_bytes=VMEM_LIMIT),
        name="dsa_attn",
    )(bnd, qit, wit, qt, kk, vt)


def kernel(x, c, norm_mix_g, norm_ffn_g, ada_w, ada_b, a_w_in, a_conv_w, a_conv_b, a_gate_r_w,
           a_gate_r_b, a_gate_i_w, a_gate_i_b, a_lambda, a_w_out, b_w_in, b_q_norm_g, b_k_norm_g,
           b_w_out, ffn_w1, ffn_w2):
    b, s, d = x.shape
    hd = b_q_norm_g.shape[-1]
    n_heads = b_w_out.shape[1] // hd
    idx_heads = (b_w_in.shape[-1] - n_heads * hd - 3 * hd) // (hd + 1)
    topk = min(TOPK_MAX, s // 4)
    depth = ada_w.shape[0]

    mod = _mod_call(c, ada_w, ada_b)
    mod_rows = mod.reshape(depth, b, 1, mod.shape[-1])

    for i in range(depth):
        j = i // 2
        if i % 2 == 0:
            x = _rglru_call(x, mod[i], norm_mix_g[i], a_w_in[j], a_conv_w[j], a_conv_b[j],
                            a_gate_r_w[j], a_gate_r_b[j], a_gate_i_w[j], a_gate_i_b[j],
                            a_lambda[j], a_w_out[j])
            x = _ffn_call(x, mod_rows[i], norm_ffn_g[i], ffn_w1[i], ffn_w2[i])
        else:
            qt, qit, kk, vt, wit = _dsa_in_call(x, mod_rows[i], norm_mix_g[i], b_w_in[j],
                                                 b_q_norm_g[j], b_k_norm_g[j], n_heads=n_heads,
                                                 hd=hd, idx_heads=idx_heads)
            bnd = (1.02 * hd * hd ** -0.5) * jnp.max(jnp.abs(b_q_norm_g[j])) * jnp.max(jnp.abs(b_k_norm_g[j]))
            o = _dsa_attn_call(bnd.reshape(1), qt, qit, kk, vt, wit, hd=hd, idx_heads=idx_heads,
                               topk=topk)
            x = _ffn_call(x, mod_rows[i], norm_ffn_g[i], ffn_w1[i], ffn_w2[i], o_pairs=o,
                          w_o=b_w_out[j])
    return x
```

```python
import functools

import jax
import jax.numpy as jnp
from jax import lax
from jax.experimental import pallas as pl
from jax.experimental.pallas import tpu as pltpu

F32 = jnp.float32
BF16 = jnp.bfloat16
I32 = jnp.int32

RMS_EPS = 1e-6
ROPE_THETA = 10000.0
LRU_C = 8.0
TOPK_MAX = 256
N_MOD = 6

LANES = 128
SUBLANES = 8
VMEM_LIMIT = 56 * 1024 * 1024

NEG_BIG = -1e30
EXP_SAFE_BOUND = 70.0
LOG2E = 1.4426950408889634
QK_LOOKAHEAD = 2
GELU_C0 = 0.7978845608028654
GELU_C1 = 0.044715


def _const_spec(shape):
    nd = len(shape)
    return pl.BlockSpec(shape, lambda *_: (0,) * nd, pipeline_mode=pl.Buffered(1))


def _rmsnorm(x, g):
    return x * lax.rsqrt(jnp.mean(x * x, axis=-1, keepdims=True) + RMS_EPS) * g


def _mod_kernel(c_ref, w_ref, b_ref, o_ref):
    c = c_ref[...]
    cond = c * jax.nn.sigmoid(c)
    o_ref[0] = jnp.dot(cond.astype(BF16), w_ref[0].astype(BF16),
                       preferred_element_type=F32) + b_ref[0]


def _mod_call(c, ada_w, ada_b):
    depth, d, n = ada_w.shape
    b = c.shape[0]
    tn = 1536
    return pl.pallas_call(
        _mod_kernel,
        grid=(depth, n // tn),
        in_specs=[pl.BlockSpec((b, d), lambda l, j: (0, 0)),
                  pl.BlockSpec((1, d, tn), lambda l, j: (l, 0, j)),
                  pl.BlockSpec((1, 1, tn), lambda l, j: (l, 0, j))],
        out_specs=pl.BlockSpec((1, b, tn), lambda l, j: (l, 0, j)),
        out_shape=jax.ShapeDtypeStruct((depth, b, n), F32),
        compiler_params=pltpu.CompilerParams(
            dimension_semantics=("arbitrary", "arbitrary"), vmem_limit_bytes=VMEM_LIMIT),
        name="adaln_mod",
    )(c, ada_w, ada_b.reshape(depth, 1, n))


def _rglru_kernel(x_ref, mod_ref, g_ref, win_ref, cw_ref, cb_ref, wr_ref, br_ref, wi_ref, bi_ref,
                  lam_ref, wout_ref, o_ref, h_s, gb_s, xbuf, a_s, b_s, hc, *, rc):
    nb, ts, d = x_ref.shape
    r = ts * nb
    dr = lam_ref.shape[1]
    nblk, blk, _ = wr_ref.shape
    cwid = cw_ref.shape[0]
    tail = (cwid - 1) * nb

    @pl.when(pl.program_id(0) == 0)
    def _():
        xbuf[0:tail, :] = jnp.zeros((tail, dr), F32)
        hc[...] = jnp.zeros((nb, dr), F32)

    x = jnp.swapaxes(x_ref[...], 0, 1)
    sh1 = mod_ref[:, 0:d]
    sc1 = mod_ref[:, d:2 * d]
    g1 = mod_ref[:, 2 * d:3 * d]
    h_s[...] = (_rmsnorm(x, g_ref[...]) * (1.0 + sc1)[None] + sh1[None]).reshape(r, d).astype(BF16)

    sp_h = (0.5 * LRU_C) * jax.nn.softplus(-lam_ref[...])
    br_h = 0.5 * br_ref[...]
    bi_h = 0.5 * bi_ref[...]

    def in_proj(c, n):
        rows = slice(c * rc, (c + 1) * rc)
        hb = h_s[rows, :]
        ux = jnp.dot(hb, win_ref[:, n * blk:(n + 1) * blk], preferred_element_type=F32)
        ug = jnp.dot(hb, win_ref[:, dr + n * blk:dr + (n + 1) * blk], preferred_element_type=F32)
        xbuf[tail + c * rc:tail + (c + 1) * rc, n * blk:(n + 1) * blk] = ux
        return ug

    def conv_and_gate_dots(c, n):
        r0 = c * rc
        cs = slice(n * blk, (n + 1) * blk)
        xc = cb_ref[:, cs] + cw_ref[0:1, cs] * xbuf[r0:r0 + rc, cs]
        for k in range(1, cwid):
            xc = xc + cw_ref[k:k + 1, cs] * xbuf[r0 + k * nb:r0 + k * nb + rc, cs]
        xn = xc.astype(BF16)
        return xc, (jnp.dot(xn, wr_ref[n], preferred_element_type=F32),
                    jnp.dot(xn, wi_ref[n], preferred_element_type=F32))

    def gate_tail(c, n, xc, pre, ug):
        ro = slice(c * rc, (c + 1) * rc)
        cs = slice(n * blk, (n + 1) * blk)
        p = sp_h[:, cs] + sp_h[:, cs] * jnp.tanh(0.5 * pre[0] + br_h[:, cs])
        ig = 0.5 + 0.5 * jnp.tanh(0.5 * pre[1] + bi_h[:, cs])
        a = jnp.exp2(p * (-LOG2E))
        a_s[ro, cs] = a
        b_s[ro, cs] = jnp.sqrt(jnp.tanh(p) * (a * a + 1.0)) * (ig * xc)
        t = jnp.tanh(ug * (GELU_C0 + (GELU_C0 * GELU_C1) * (ug * ug)))
        hu = 0.5 * ug
        gb_s[ro, cs] = hu + hu * t

    units = [(c, n) for c in range(r // rc) for n in range(nblk)]
    ug = in_proj(*units[0])
    for k, (c, n) in enumerate(units):
        xc, pre = conv_and_gate_dots(c, n)
        ug_next = in_proj(*units[k + 1]) if k + 1 < len(units) else None
        gate_tail(c, n, xc, pre, ug)
        ug = ug_next
    xbuf[0:tail, :] = xbuf[r:r + tail, :]

    def step(t, hprev):
        r0 = pl.multiple_of(t * nb, nb)
        hnew = a_s[pl.ds(r0, nb), :] * hprev + b_s[pl.ds(r0, nb), :]
        b_s[pl.ds(r0, nb), :] = hnew
        return hnew

    hc[...] = lax.fori_loop(0, ts, step, hc[...], unroll=8)

    y = b_s[...] * gb_s[...]
    out = jnp.dot(y.astype(BF16), wout_ref[...], preferred_element_type=F32)
    o_ref[...] = jnp.swapaxes(x + out.reshape(ts, nb, d) * g1[None], 0, 1)


def _rglru_call(x, mod0, g, w_in, conv_w, conv_b, wr, br, wi, bi, lam, w_out, *, ts=64, rc=128):
    nb, s, d = x.shape
    assert nb == SUBLANES, "the recurrence keeps the batch on the sublane axis"
    r = ts * nb
    dr = lam.shape[-1]
    tail = (conv_w.shape[0] - 1) * nb
    row = lambda v: v.reshape(1, -1)
    args = (x, mod0, row(g), w_in.astype(BF16), conv_w, row(conv_b), wr.astype(BF16), row(br),
            wi.astype(BF16), row(bi), row(lam), w_out.astype(BF16))
    x_spec = pl.BlockSpec((nb, ts, d), lambda i: (0, i, 0))
    return pl.pallas_call(
        functools.partial(_rglru_kernel, rc=rc),
        grid=(s // ts,),
        in_specs=[x_spec] + [_const_spec(a.shape) for a in args[1:]],
        out_specs=x_spec,
        out_shape=jax.ShapeDtypeStruct((nb, s, d), F32),
        scratch_shapes=[pltpu.VMEM((r, d), BF16), pltpu.VMEM((r, dr), F32),
                        pltpu.VMEM((r + tail, dr), F32), pltpu.VMEM((r, dr), F32),
                        pltpu.VMEM((r, dr), F32), pltpu.VMEM((nb, dr), F32)],
        compiler_params=pltpu.CompilerParams(
            dimension_semantics=("arbitrary",), vmem_limit_bytes=VMEM_LIMIT),
        name="rglru_mixer",
    )(*args)


def _ffn_kernel(*refs, has_proj, fc):
    if has_proj:
        x_ref, mod_ref, g_ref, w1_ref, w2_ref, o_in_ref, wo_ref, out_ref = refs
    else:
        x_ref, mod_ref, g_ref, w1_ref, w2_ref, out_ref = refs
    d = x_ref.shape[1]
    dff = w1_ref.shape[1]
    x = x_ref[...]
    if has_proj:
        g1 = mod_ref[:, 2 * d:3 * d]
        o = jnp.concatenate([o_in_ref[p] for p in range(o_in_ref.shape[0])], axis=-1)
        x = x + g1 * jnp.dot(o, wo_ref[...], preferred_element_type=F32)
    sh2 = mod_ref[:, 3 * d:4 * d]
    sc2 = mod_ref[:, 4 * d:5 * d]
    g2 = mod_ref[:, 5 * d:6 * d]
    hb = (_rmsnorm(x, g_ref[...]) * (1.0 + sc2) + sh2).astype(BF16)
    acc = jnp.zeros(x.shape, F32)
    for c in range(dff // fc):
        hid = jnp.dot(hb, w1_ref[:, c * fc:(c + 1) * fc], preferred_element_type=F32)
        hid = jnp.square(jnp.maximum(hid, 0.0))
        acc = acc + jnp.dot(hid.astype(BF16), w2_ref[c * fc:(c + 1) * fc, :],
                            preferred_element_type=F32)
    out_ref[...] = x + g2 * acc


def _ffn_call(x, mod_l, g, w1, w2, o_pairs=None, w_o=None, *, rf=512, fc=1024):
    b, s, d = x.shape
    has_proj = o_pairs is not None
    args = [x, mod_l, g.reshape(1, -1), w1.astype(BF16), w2.astype(BF16)]
    in_specs = [pl.BlockSpec((None, rf, d), lambda bi, i: (bi, i, 0)),
                pl.BlockSpec((None, 1, mod_l.shape[-1]), lambda bi, i: (bi, 0, 0)),
                _const_spec((1, d)), _const_spec(w1.shape), _const_spec(w2.shape)]
    if has_proj:
        npair = o_pairs.shape[1]
        args += [o_pairs, w_o.astype(BF16)]
        in_specs += [pl.BlockSpec((None, npair, rf, LANES), lambda bi, i: (bi, 0, i, 0)),
                     _const_spec(w_o.shape)]
    return pl.pallas_call(
        functools.partial(_ffn_kernel, has_proj=has_proj, fc=fc),
        grid=(b, s // rf),
        in_specs=in_specs,
        out_specs=pl.BlockSpec((None, rf, d), lambda bi, i: (bi, i, 0)),
        out_shape=jax.ShapeDtypeStruct((b, s, d), F32),
        compiler_params=pltpu.CompilerParams(
            dimension_semantics=("arbitrary", "arbitrary"), vmem_limit_bytes=VMEM_LIMIT),
        name="ffn_proj" if has_proj else "ffn",
    )(*args)


def _rot_half(x, lane):
    return jnp.where((lane & 63) < 32, pltpu.roll(x, 96, 1), pltpu.roll(x, 32, 1))


def _dsa_in_kernel(x_ref, mod_ref, g_ref, w_ref, qg_ref, kg_ref, e_ref, cos_ref, sin_ref,
                   qt_ref, qit_ref, kk_ref, vt_ref, wit_ref, *, nq, nqi, hd, idx_heads):
    d = x_ref.shape[1]
    tq = x_ref.shape[0]
    x = x_ref[...]
    sh1 = mod_ref[:, 0:d]
    sc1 = mod_ref[:, d:2 * d]
    h = _rmsnorm(x, g_ref[...]) * (1.0 + sc1) + sh1
    u = jnp.dot(h.astype(BF16), w_ref[...], preferred_element_type=F32)
    lane = lax.broadcasted_iota(I32, (tq, LANES), 1)
    cos = cos_ref[...]
    sin = sin_ref[...]
    qg = qg_ref[...]
    e = e_ref[...]
    qscale = hd ** -0.5

    for p in range(nq):
        t = u[:, p * LANES:(p + 1) * LANES]
        t2 = t * t
        hi = t2.astype(BF16)
        lo = (t2 - hi.astype(F32)).astype(BF16)
        ss = (jnp.dot(hi, e, preferred_element_type=F32) + jnp.dot(lo, e, preferred_element_type=F32))
        tn = t * lax.rsqrt(ss * (1.0 / hd) + RMS_EPS) * qg
        qt_ref[p] = ((tn * cos + _rot_half(tn, lane) * sin) * (qscale * LOG2E)).T.astype(BF16)

    for p in range(nqi):
        t = u[:, (nq + p) * LANES:(nq + p + 1) * LANES]
        qit_ref[p] = ((t * cos + _rot_half(t, lane) * sin) * qscale).T.astype(BF16)

    kk = u[:, (nq + nqi) * LANES:(nq + nqi + 1) * LANES]
    left = lane < hd
    ssk = jnp.sum(jnp.where(left, kk * kk, 0.0), axis=-1, keepdims=True)
    fac = jnp.where(left, lax.rsqrt(ssk * (1.0 / hd) + RMS_EPS) * kg_ref[...], 1.0)
    kn = kk * fac
    kr = kn * cos + _rot_half(kn, lane) * sin
    k_l = jnp.where(left, kr, 0.0)
    ki_r = jnp.where(left, 0.0, kr)
    kk_ref[0] = k_l.astype(BF16)
    kk_ref[1] = pltpu.roll(k_l, hd, 1).astype(BF16)
    kk_ref[2] = pltpu.roll(ki_r, hd, 1).astype(BF16)
    kk_ref[3] = ki_r.astype(BF16)

    vw = u[:, (nq + nqi + 1) * LANES:(nq + nqi + 2) * LANES]
    vt_ref[...] = jnp.where(left, vw, 1.0).T.astype(BF16)
    wit_ref[...] = (vw * (idx_heads ** -0.5)).T[hd:hd + idx_heads, :]


def _dsa_in_call(x, mod_l, g, w_in, q_g, k_g, *, n_heads, hd, idx_heads, tq=512):
    b, s, d = x.shape
    assert 2 * hd == LANES, "two heads per lane tile"
    nq = n_heads * hd // LANES
    nqi = idx_heads * hd // LANES
    o_k = n_heads * hd
    o_v = o_k + hd
    o_qi = o_v + hd
    o_ki = o_qi + idx_heads * hd
    o_wi = o_ki + hd
    pad = LANES - hd - idx_heads
    w = jnp.concatenate([w_in[:, :o_k], w_in[:, o_qi:o_ki], w_in[:, o_k:o_v], w_in[:, o_ki:o_wi],
                         w_in[:, o_v:o_qi], w_in[:, o_wi:], jnp.zeros((d, pad), w_in.dtype)],
                        axis=1).astype(BF16)
    ncol = w.shape[1]
    inv = ROPE_THETA ** (-jnp.arange(0, hd, 2, dtype=F32) / hd)
    ang = jnp.arange(s, dtype=F32)[:, None] * inv[None, :]
    cos_t = jnp.tile(jnp.cos(ang), (1, 4))
    sin_h = jnp.sin(ang)
    sin_t = jnp.tile(jnp.concatenate([-sin_h, sin_h], axis=1), (1, 2))
    head_of_lane = jnp.arange(LANES) // hd
    e = (head_of_lane[:, None] == head_of_lane[None, :]).astype(BF16)
    qg = jnp.tile(q_g, 2).reshape(1, LANES)
    kg = jnp.concatenate([k_g, jnp.ones((hd,), k_g.dtype)]).reshape(1, LANES)

    return pl.pallas_call(
        functools.partial(_dsa_in_kernel, nq=nq, nqi=nqi, hd=hd, idx_heads=idx_heads),
        grid=(b, s // tq),
        in_specs=[pl.BlockSpec((None, tq, d), lambda bi, i: (bi, i, 0)),
                  pl.BlockSpec((None, 1, mod_l.shape[-1]), lambda bi, i: (bi, 0, 0)),
                  _const_spec((1, d)), _const_spec((d, ncol)), _const_spec((1, LANES)),
                  _const_spec((1, LANES)), _const_spec((LANES, LANES)),
                  pl.BlockSpec((tq, LANES), lambda bi, i: (i, 0)),
                  pl.BlockSpec((tq, LANES), lambda bi, i: (i, 0))],
        out_specs=[pl.BlockSpec((None, nq, LANES, tq), lambda bi, i: (bi, 0, 0, i)),
                   pl.BlockSpec((None, nqi, LANES, tq), lambda bi, i: (bi, 0, 0, i)),
                   pl.BlockSpec((None, 4, tq, LANES), lambda bi, i: (bi, 0, i, 0)),
                   pl.BlockSpec((None, LANES, tq), lambda bi, i: (bi, 0, i)),
                   pl.BlockSpec((None, idx_heads, tq), lambda bi, i: (bi, 0, i))],
        out_shape=[jax.ShapeDtypeStruct((b, nq, LANES, s), BF16),
                   jax.ShapeDtypeStruct((b, nqi, LANES, s), BF16),
                   jax.ShapeDtypeStruct((b, 4, s, LANES), BF16),
                   jax.ShapeDtypeStruct((b, LANES, s), BF16),
                   jax.ShapeDtypeStruct((b, idx_heads, s), F32)],
        compiler_params=pltpu.CompilerParams(
            dimension_semantics=("arbitrary", "arbitrary"), vmem_limit_bytes=VMEM_LIMIT),
        name="dsa_in",
    )(x, mod_l, g.reshape(1, -1), w, qg, kg, e, cos_t, sin_t)


def _dsa_attn_kernel(bnd_ref, qit_ref, wit_ref, qt_ref, kk_ref, vt_ref, o_ref, sct_s, mask_s,
                     st_s, m_s, acc_s, *, kc, ks, topk, hd, idx_heads, group, tie_from, max_groups,
                     n_blocks, head_passes, sl):
    npair, _, qb = qt_ref.shape
    nh = 2 * npair
    nsub = SUBLANES
    i = pl.program_id(1)
    has_sel = i < n_blocks
    has_att = i >= 1
    nk = (i * qb) // kc + 1
    nkp = ((i - 1) * qb) // kc + 1
    slot = lax.rem(i, 2)
    mask_w = mask_s.at[slot]
    mask_r = mask_s.at[1 - slot]
    kf = float(topk)
    fast = bnd_ref[0] <= EXP_SAFE_BOUND
    merged = jnp.logical_and(jnp.logical_and(has_sel, has_att), fast)

    def slab(src, c, t):
        return src[pl.ds(pl.multiple_of(c * kc + t * sl, sl), sl), :]

    def finish(acc, op, red):
        n = sl // nsub
        acc = acc.reshape(n, nsub, qb)
        r = acc[0]
        for t in range(1, n):
            r = op(r, acc[t])
        return jnp.broadcast_to(red(r, axis=0, keepdims=True), (nsub, qb))

    def reduce_keys(src, fn, op, red, init):
        def body(ci, acc):
            for t in range(kc // sl):
                acc = op(acc, fn(slab(src, ci, t), ci, t))
            return acc
        return finish(lax.fori_loop(0, nk, body, jnp.full((sl, qb), init, F32)), op, red)

    def rows(x):
        return jnp.concatenate([x] * (sl // nsub), axis=0)

    @pl.when(has_sel)
    def _():
        qpos = i * qb + lax.broadcasted_iota(I32, (ks, qb), 1)
        krow = lax.broadcasted_iota(I32, (ks, qb), 0)

        def score_chunk(ci, mm):
            mx, mn = mm
            for sub in range(kc // ks):
                k0 = pl.multiple_of(ci * kc + sub * ks, ks)
                kis = (kk_ref[2, pl.ds(k0, ks), :], kk_ref[3, pl.ds(k0, ks), :])
                acc = jnp.zeros((ks, qb), F32)
                for hh in range(idx_heads):
                    lg = jnp.dot(kis[hh % 2], qit_ref[hh // 2], preferred_element_type=F32)
                    acc = acc + wit_ref[hh:hh + 1, :] * jnp.maximum(lg, 0.0)
                causal = (k0 + krow) <= qpos
                sct_s[pl.ds(k0, ks), :] = jnp.where(causal, acc, -jnp.inf)
                lo_v = jnp.where(causal, acc, -jnp.inf)
                hi_v = jnp.where(causal, acc, jnp.inf)
                for t in range(ks // sl):
                    mx = jnp.maximum(mx, lo_v[t * sl:(t + 1) * sl])
                    mn = jnp.minimum(mn, hi_v[t * sl:(t + 1) * sl])
            return mx, mn

        mx, mn = lax.fori_loop(0, nk, score_chunk, (jnp.full((sl, qb), -jnp.inf, F32),
                                                    jnp.full((sl, qb), jnp.inf, F32)))
        mx = finish(mx, jnp.maximum, jnp.max)
        mn = finish(mn, jnp.minimum, jnp.min)
        st_s[0] = mn
        st_s[1] = mx + (jnp.abs(mx) * 1e-6 + 1e-30)
        st_s[2] = (i * qb + lax.broadcasted_iota(I32, (nsub, qb), 1) + 1).astype(F32)
        st_s[3] = jnp.zeros((nsub, qb), F32)

    def count_step(st):
        c, lo, hi, clo, chi, acc = st
        mid = 0.5 * lo + 0.5 * hi
        midr = rows(mid)
        for t in range(kc // sl):
            acc = acc + jnp.where(slab(sct_s, c, t) >= midr, 1.0, 0.0)
        last = c == nk - 1
        tot = finish(acc, jnp.add, jnp.sum)
        ge = tot >= kf
        lo = jnp.where(last, jnp.where(ge, mid, lo), lo)
        clo = jnp.where(last, jnp.where(ge, tot, clo), clo)
        hi = jnp.where(last, jnp.where(ge, hi, mid), hi)
        chi = jnp.where(last, jnp.where(ge, chi, tot), chi)
        return (jnp.where(last, 0, c + 1), lo, hi, clo, chi, jnp.where(last, 0.0, acc))

    def scores(ci, h):
        kk = kk_ref[h % 2, pl.ds(pl.multiple_of(ci * kc, kc), kc), :]
        return jnp.dot(kk, qt_ref[h // 2], preferred_element_type=F32)

    def pv_step(ci, st, use_max, passes):
        c0 = pl.multiple_of(ci * kc, kc)
        vt = vt_ref[:, pl.ds(c0, kc)]
        msk = mask_r[pl.ds(c0, kc), :]
        ahead = [scores(ci, h) for h in range(min(QK_LOOKAHEAD, nh))]
        for h in range(nh):
            s_cur = ahead.pop(0)
            if h + QK_LOOKAHEAD < nh:
                ahead.append(scores(ci, h + QK_LOOKAHEAD))
            if use_max:
                pr = jnp.where(msk > 0, jnp.exp2(s_cur - m_s[h][0:1, :]), 0.0).astype(BF16)
            else:
                pr = jnp.exp2(s_cur).astype(BF16) * msk
            acc_s[h] += jnp.dot(vt, pr, preferred_element_type=F32)
            for _ in range(passes[h] if passes else 0):
                st = count_step(st)
        return st

    @pl.when(has_att)
    def _():
        acc_s[...] = jnp.zeros(acc_s.shape, F32)

    @pl.when(merged)
    def _():
        st = (jnp.int32(0), st_s[0], st_s[1], st_s[2], st_s[3], jnp.zeros((sl, qb), F32))
        st = lax.fori_loop(0, nkp, functools.partial(pv_step, use_max=False, passes=head_passes), st)
        st_s[0], st_s[1], st_s[2], st_s[3] = st[1], st[2], st[3], st[4]

    @pl.when(jnp.logical_and(has_att, jnp.logical_and(fast, jnp.logical_not(has_sel))))
    def _():
        lax.fori_loop(0, nkp, functools.partial(pv_step, use_max=False, passes=None), 0)

    @pl.when(jnp.logical_and(has_att, jnp.logical_not(fast)))
    def _():
        def max_step(ci, _):
            @pl.when(ci == 0)
            def _():
                m_s[...] = jnp.full(m_s.shape, NEG_BIG, F32)
            msk = mask_r[pl.ds(pl.multiple_of(ci * kc, kc), kc), :]
            for h in range(nh):
                mx = jnp.max(jnp.where(msk > 0, scores(ci, h), NEG_BIG), axis=0, keepdims=True)
                m_s[h] = jnp.maximum(m_s[h], jnp.broadcast_to(mx, (nsub, qb)))
            return 0

        lax.fori_loop(0, nkp, max_step, 0)
        lax.fori_loop(0, nkp, functools.partial(pv_step, use_max=True, passes=None), 0)

    @pl.when(has_att)
    def _():
        for p in range(npair):
            a0, a1 = acc_s[2 * p], acc_s[2 * p + 1]
            ot = jnp.concatenate([a0[0:hd] / a0[hd:2 * hd], a1[0:hd] / a1[hd:2 * hd]], axis=0)
            o_ref[p] = ot.T.astype(BF16)

    @pl.when(has_sel)
    def _():
        def count_ge(thr):
            thr = rows(thr)
            return reduce_keys(sct_s, lambda blk, c, t: jnp.where(blk >= thr, 1.0, 0.0),
                               jnp.add, jnp.sum, 0.0)

        def bisect(_, st):
            lo, hi, clo, chi = st
            mid = 0.5 * lo + 0.5 * hi
            c = count_ge(mid)
            ge = c >= kf
            return (jnp.where(ge, mid, lo), jnp.where(ge, hi, mid),
                    jnp.where(ge, c, clo), jnp.where(ge, chi, c))

        def open_rows(clo, tied):
            return jnp.max(jnp.where(clo > kf, 1.0 - tied, 0.0)) > 0.0

        g0 = jnp.where(merged, tie_from // group, 0)

        def search_group(st):
            g, _, lo, hi, clo, chi, tied = st
            lo, hi, clo, chi = lax.fori_loop(0, group, bisect, (lo, hi, clo, chi))

            def tie_check(tied):
                lor, hir = rows(lo), rows(hi)
                a = reduce_keys(sct_s, lambda blk, c, t: jnp.where(blk >= lor, blk, jnp.inf),
                                jnp.minimum, jnp.min, jnp.inf)
                b = reduce_keys(sct_s, lambda blk, c, t: jnp.where(blk < hir, blk, -jnp.inf),
                                jnp.maximum, jnp.max, -jnp.inf)
                return jnp.where(jnp.logical_and(clo > kf, a == b), 1.0, tied)

            tied = lax.cond(jnp.logical_and((g + 1) * group >= tie_from, open_rows(clo, tied)),
                            tie_check, lambda t: t, tied)
            return (g + 1, open_rows(clo, tied).astype(I32), lo, hi, clo, chi, tied)

        zero = jnp.zeros((nsub, qb), F32)
        _, _, lo, hi, clo, chi, _ = lax.while_loop(
            lambda st: jnp.logical_and(st[1] > 0, st[0] < max_groups), search_group,
            (g0, open_rows(st_s[2], zero).astype(I32), st_s[0], st_s[1], st_s[2], st_s[3], zero))

        def tile_k(x):
            return jnp.concatenate([x] * (ks // nsub), axis=0)

        lok, hik, free = tile_k(lo), tile_k(hi), tile_k(kf - chi)
        tri = jnp.where(lax.broadcasted_iota(I32, (ks, ks), 1) <= lax.broadcasted_iota(I32, (ks, ks), 0),
                        1.0, 0.0).astype(BF16)

        def mask_chunk(ci, carry):
            subs = []
            for t in range(kc // ks):
                k0 = pl.multiple_of(ci * kc + t * ks, ks)
                blk = sct_s[pl.ds(k0, ks), :]
                top = blk >= hik
                tie = jnp.where(top, 0.0, jnp.where(blk >= lok, 1.0, 0.0))
                subs.append((k0, top, tie, jnp.dot(tri, tie.astype(BF16), preferred_element_type=F32)))
            for k0, top, tie, rank in subs:
                keep = jnp.where(rank + tile_k(carry) <= free, tie, 0.0)
                mask_w[pl.ds(k0, ks), :] = jnp.where(top, 1.0, keep).astype(BF16)
                carry = carry + jnp.broadcast_to(rank[ks - 1:ks, :], (nsub, qb))
            return carry

        lax.fori_loop(0, nk, mask_chunk, jnp.zeros((nsub, qb), F32))


def _dsa_attn_call(bnd, qt, qit, kk, vt, wit, *, hd, idx_heads, topk, qb=256, kc=512, ks=128,
                   sl=32):
    b, npair, _, s = qt.shape
    nqi = qit.shape[1]
    assert kc % qb == 0 and s % kc == 0 and kc % ks == 0 and ks % sl == 0 and sl % SUBLANES == 0
    nb = s // qb
    head_passes = tuple(2 if h % 2 == 0 else 1 for h in range(2 * npair))
    assert 2 * hd == LANES
    prev = lambda i: jnp.maximum(i - 1, 0)
    cur = lambda i: jnp.minimum(i, nb - 1)
    return pl.pallas_call(
        functools.partial(_dsa_attn_kernel, kc=kc, ks=ks, topk=topk, hd=hd, idx_heads=idx_heads,
                          group=4, tie_from=16, max_groups=96, n_blocks=nb,
                          head_passes=head_passes, sl=sl),
        grid=(b, nb + 1),
        in_specs=[pl.BlockSpec(memory_space=pltpu.SMEM),
                  pl.BlockSpec((None, nqi, LANES, qb), lambda bi, i: (bi, 0, 0, cur(i))),
                  pl.BlockSpec((None, idx_heads, qb), lambda bi, i: (bi, 0, cur(i))),
                  pl.BlockSpec((None, npair, LANES, qb), lambda bi, i: (bi, 0, 0, prev(i))),
                  pl.BlockSpec((None, 4, s, LANES), lambda bi, i: (bi, 0, 0, 0)),
                  pl.BlockSpec((None, LANES, s), lambda bi, i: (bi, 0, 0))],
        out_specs=pl.BlockSpec((None, npair, qb, LANES), lambda bi, i: (bi, 0, prev(i), 0)),
        out_shape=jax.ShapeDtypeStruct((b, npair, s, LANES), BF16),
        scratch_shapes=[pltpu.VMEM((s, qb), F32),
                        pltpu.VMEM((2, s, qb), BF16), pltpu.VMEM((4, SUBLANES, qb), F32),
                        pltpu.VMEM((2 * npair, SUBLANES, qb), F32),
                        pltpu.VMEM((2 * npair, LANES, qb), F32)],
        compiler_params=pltpu.CompilerParams(
            dimension_semantics=("arbitrary", "arbitrary"), vmem_limit_bytes=VMEM_LIMIT),
        name="dsa_attn",
    )(bnd, qit, wit, qt, kk, vt)


def kernel(x, c, norm_mix_g, norm_ffn_g, ada_w, ada_b, a_w_in, a_conv_w, a_conv_b, a_gate_r_w,
           a_gate_r_b, a_gate_i_w, a_gate_i_b, a_lambda, a_w_out, b_w_in, b_q_norm_g, b_k_norm_g,
           b_w_out, ffn_w1, ffn_w2):
    b, s, d = x.shape
    hd = b_q_norm_g.shape[-1]
    n_heads = b_w_out.shape[1] // hd
    idx_heads = (b_w_in.shape[-1] - n_heads * hd - 3 * hd) // (hd + 1)
    topk = min(TOPK_MAX, s // 4)
    depth = ada_w.shape[0]

    mod = _mod_call(c, ada_w, ada_b)
    mod_rows = mod.reshape(depth, b, 1, mod.shape[-1])

    for i in range(depth):
        j = i // 2
        if i % 2 == 0:
            x = _rglru_call(x, mod[i], norm_mix_g[i], a_w_in[j], a_conv_w[j], a_conv_b[j],
                            a_gate_r_w[j], a_gate_r_b[j], a_gate_i_w[j], a_gate_i_b[j],
                            a_lambda[j], a_w_out[j])
            x = _ffn_call(x, mod_rows[i], norm_ffn_g[i], ffn_w1[i], ffn_w2[i])
        else:
            qt, qit, kk, vt, wit = _dsa_in_call(x, mod_rows[i], norm_mix_g[i], b_w_in[j],
                                                 b_q_norm_g[j], b_k_norm_g[j], n_heads=n_heads,
                                                 hd=hd, idx_heads=idx_heads)
            bnd = (1.02 * hd * hd ** -0.5) * jnp.max(jnp.abs(b_q_norm_g[j])) * jnp.max(jnp.abs(b_k_norm_g[j]))
            o = _dsa_attn_call(bnd.reshape(1), qt, qit, kk, vt, wit, hd=hd, idx_heads=idx_heads,
                               topk=topk)
            x = _ffn_call(x, mod_rows[i], norm_ffn_g[i], ffn_w1[i], ffn_w2[i], o_pairs=o,
                          w_o=b_w_out[j])
    return x
```

```python
import functools

import jax
import jax.numpy as jnp
from jax import lax
from jax.experimental import pallas as pl
from jax.experimental.pallas import tpu as pltpu

F32 = jnp.float32
BF16 = jnp.bfloat16
I32 = jnp.int32

RMS_EPS = 1e-6
ROPE_THETA = 10000.0
LRU_C = 8.0
TOPK_MAX = 256
N_MOD = 6

LANES = 128
SUBLANES = 8
VMEM_LIMIT = 56 * 1024 * 1024

NEG_BIG = -1e30
EXP_SAFE_BOUND = 70.0
LOG2E = 1.4426950408889634
QK_LOOKAHEAD = 2
GELU_C0 = 0.7978845608028654
GELU_C1 = 0.044715


def _const_spec(shape):
    nd = len(shape)
    return pl.BlockSpec(shape, lambda *_: (0,) * nd, pipeline_mode=pl.Buffered(1))


def _rmsnorm(x, g):
    return x * lax.rsqrt(jnp.mean(x * x, axis=-1, keepdims=True) + RMS_EPS) * g


def _mod_kernel(c_ref, w_ref, b_ref, o_ref):
    c = c_ref[...]
    cond = c * jax.nn.sigmoid(c)
    o_ref[0] = jnp.dot(cond.astype(BF16), w_ref[0].astype(BF16),
                       preferred_element_type=F32) + b_ref[0]


def _mod_call(c, ada_w, ada_b):
    depth, d, n = ada_w.shape
    b = c.shape[0]
    tn = 1536
    return pl.pallas_call(
        _mod_kernel,
        grid=(depth, n // tn),
        in_specs=[pl.BlockSpec((b, d), lambda l, j: (0, 0)),
                  pl.BlockSpec((1, d, tn), lambda l, j: (l, 0, j)),
                  pl.BlockSpec((1, 1, tn), lambda l, j: (l, 0, j))],
        out_specs=pl.BlockSpec((1, b, tn), lambda l, j: (l, 0, j)),
        out_shape=jax.ShapeDtypeStruct((depth, b, n), F32),
        compiler_params=pltpu.CompilerParams(
            dimension_semantics=("arbitrary", "arbitrary"), vmem_limit_bytes=VMEM_LIMIT),
        name="adaln_mod",
    )(c, ada_w, ada_b.reshape(depth, 1, n))


def _rglru_kernel(x_ref, mod_ref, g_ref, win_ref, cw_ref, cb_ref, wr_ref, br_ref, wi_ref, bi_ref,
                  lam_ref, wout_ref, o_ref, h_s, gb_s, xbuf, a_s, b_s, hc, *, rc):
    nb, ts, d = x_ref.shape
    r = ts * nb
    dr = lam_ref.shape[1]
    nblk, blk, _ = wr_ref.shape
    cwid = cw_ref.shape[0]
    tail = (cwid - 1) * nb

    @pl.when(pl.program_id(0) == 0)
    def _():
        xbuf[0:tail, :] = jnp.zeros((tail, dr), F32)
        hc[...] = jnp.zeros((nb, dr), F32)

    x = jnp.swapaxes(x_ref[...], 0, 1)
    sh1 = mod_ref[:, 0:d]
    sc1 = mod_ref[:, d:2 * d]
    g1 = mod_ref[:, 2 * d:3 * d]
    h_s[...] = (_rmsnorm(x, g_ref[...]) * (1.0 + sc1)[None] + sh1[None]).reshape(r, d).astype(BF16)

    sp_h = (0.5 * LRU_C) * jax.nn.softplus(-lam_ref[...])
    br_h = 0.5 * br_ref[...]
    bi_h = 0.5 * bi_ref[...]

    def in_proj(c, n):
        rows = slice(c * rc, (c + 1) * rc)
        hb = h_s[rows, :]
        ux = jnp.dot(hb, win_ref[:, n * blk:(n + 1) * blk], preferred_element_type=F32)
        ug = jnp.dot(hb, win_ref[:, dr + n * blk:dr + (n + 1) * blk], preferred_element_type=F32)
        xbuf[tail + c * rc:tail + (c + 1) * rc, n * blk:(n + 1) * blk] = ux
        return ug

    def conv_and_gate_dots(c, n):
        r0 = c * rc
        cs = slice(n * blk, (n + 1) * blk)
        xc = cb_ref[:, cs] + cw_ref[0:1, cs] * xbuf[r0:r0 + rc, cs]
        for k in range(1, cwid):
            xc = xc + cw_ref[k:k + 1, cs] * xbuf[r0 + k * nb:r0 + k * nb + rc, cs]
        xn = xc.astype(BF16)
        return xc, (jnp.dot(xn, wr_ref[n], preferred_element_type=F32),
                    jnp.dot(xn, wi_ref[n], preferred_element_type=F32))

    def gate_tail(c, n, xc, pre, ug):
        ro = slice(c * rc, (c + 1) * rc)
        cs = slice(n * blk, (n + 1) * blk)
        p = sp_h[:, cs] + sp_h[:, cs] * jnp.tanh(0.5 * pre[0] + br_h[:, cs])
        ig = 0.5 + 0.5 * jnp.tanh(0.5 * pre[1] + bi_h[:, cs])
        a = jnp.exp2(p * (-LOG2E))
        a_s[ro, cs] = a
        b_s[ro, cs] = jnp.sqrt(jnp.tanh(p) * (a * a + 1.0)) * (ig * xc)
        t = jnp.tanh(ug * (GELU_C0 + (GELU_C0 * GELU_C1) * (ug * ug)))
        hu = 0.5 * ug
        gb_s[ro, cs] = hu + hu * t

    units = [(c, n) for c in range(r // rc) for n in range(nblk)]
    ug = in_proj(*units[0])
    for k, (c, n) in enumerate(units):
        xc, pre = conv_and_gate_dots(c, n)
        ug_next = in_proj(*units[k + 1]) if k + 1 < len(units) else None
        gate_tail(c, n, xc, pre, ug)
        ug = ug_next
    xbuf[0:tail, :] = xbuf[r:r + tail, :]

    def step(t, hprev):
        r0 = pl.multiple_of(t * nb, nb)
        hnew = a_s[pl.ds(r0, nb), :] * hprev + b_s[pl.ds(r0, nb), :]
        b_s[pl.ds(r0, nb), :] = hnew
        return hnew

    hc[...] = lax.fori_loop(0, ts, step, hc[...], unroll=8)

    y = b_s[...] * gb_s[...]
    out = jnp.dot(y.astype(BF16), wout_ref[...], preferred_element_type=F32)
    o_ref[...] = jnp.swapaxes(x + out.reshape(ts, nb, d) * g1[None], 0, 1)


def _rglru_call(x, mod0, g, w_in, conv_w, conv_b, wr, br, wi, bi, lam, w_out, *, ts=128, rc=256):
    nb, s, d = x.shape
    assert nb == SUBLANES, "the recurrence keeps the batch on the sublane axis"
    r = ts * nb
    dr = lam.shape[-1]
    tail = (conv_w.shape[0] - 1) * nb
    row = lambda v: v.reshape(1, -1)
    args = (x, mod0, row(g), w_in.astype(BF16), conv_w, row(conv_b), wr.astype(BF16), row(br),
            wi.astype(BF16), row(bi), row(lam), w_out.astype(BF16))
    x_spec = pl.BlockSpec((nb, ts, d), lambda i: (0, i, 0))
    return pl.pallas_call(
        functools.partial(_rglru_kernel, rc=rc),
        grid=(s // ts,),
        in_specs=[x_spec] + [_const_spec(a.shape) for a in args[1:]],
        out_specs=x_spec,
        out_shape=jax.ShapeDtypeStruct((nb, s, d), F32),
        scratch_shapes=[pltpu.VMEM((r, d), BF16), pltpu.VMEM((r, dr), F32),
                        pltpu.VMEM((r + tail, dr), F32), pltpu.VMEM((r, dr), F32),
                        pltpu.VMEM((r, dr), F32), pltpu.VMEM((nb, dr), F32)],
        compiler_params=pltpu.CompilerParams(
            dimension_semantics=("arbitrary",), vmem_limit_bytes=VMEM_LIMIT),
        name="rglru_mixer",
    )(*args)


def _ffn_kernel(*refs, has_proj, fc):
    if has_proj:
        x_ref, mod_ref, g_ref, w1_ref, w2_ref, o_in_ref, wo_ref, out_ref = refs
    else:
        x_ref, mod_ref, g_ref, w1_ref, w2_ref, out_ref = refs
    d = x_ref.shape[1]
    dff = w1_ref.shape[1]
    x = x_ref[...]
    if has_proj:
        g1 = mod_ref[:, 2 * d:3 * d]
        o = jnp.concatenate([o_in_ref[p] for p in range(o_in_ref.shape[0])], axis=-1)
        x = x + g1 * jnp.dot(o, wo_ref[...], preferred_element_type=F32)
    sh2 = mod_ref[:, 3 * d:4 * d]
    sc2 = mod_ref[:, 4 * d:5 * d]
    g2 = mod_ref[:, 5 * d:6 * d]
    hb = (_rmsnorm(x, g_ref[...]) * (1.0 + sc2) + sh2).astype(BF16)
    acc = jnp.zeros(x.shape, F32)
    for c in range(dff // fc):
        hid = jnp.dot(hb, w1_ref[:, c * fc:(c + 1) * fc], preferred_element_type=F32)
        hid = jnp.square(jnp.maximum(hid, 0.0))
        acc = acc + jnp.dot(hid.astype(BF16), w2_ref[c * fc:(c + 1) * fc, :],
                            preferred_element_type=F32)
    out_ref[...] = x + g2 * acc


def _ffn_call(x, mod_l, g, w1, w2, o_pairs=None, w_o=None, *, rf=512, fc=1024):
    b, s, d = x.shape
    has_proj = o_pairs is not None
    args = [x, mod_l, g.reshape(1, -1), w1.astype(BF16), w2.astype(BF16)]
    in_specs = [pl.BlockSpec((None, rf, d), lambda bi, i: (bi, i, 0)),
                pl.BlockSpec((None, 1, mod_l.shape[-1]), lambda bi, i: (bi, 0, 0)),
                _const_spec((1, d)), _const_spec(w1.shape), _const_spec(w2.shape)]
    if has_proj:
        npair = o_pairs.shape[1]
        args += [o_pairs, w_o.astype(BF16)]
        in_specs += [pl.BlockSpec((None, npair, rf, LANES), lambda bi, i: (bi, 0, i, 0)),
                     _const_spec(w_o.shape)]
    return pl.pallas_call(
        functools.partial(_ffn_kernel, has_proj=has_proj, fc=fc),
        grid=(b, s // rf),
        in_specs=in_specs,
        out_specs=pl.BlockSpec((None, rf, d), lambda bi, i: (bi, i, 0)),
        out_shape=jax.ShapeDtypeStruct((b, s, d), F32),
        compiler_params=pltpu.CompilerParams(
            dimension_semantics=("arbitrary", "arbitrary"), vmem_limit_bytes=VMEM_LIMIT),
        name="ffn_proj" if has_proj else "ffn",
    )(*args)


def _rot_half(x, lane):
    return jnp.where((lane & 63) < 32, pltpu.roll(x, 96, 1), pltpu.roll(x, 32, 1))


def _dsa_in_kernel(x_ref, mod_ref, g_ref, w_ref, qg_ref, kg_ref, e_ref, cos_ref, sin_ref,
                   qt_ref, qit_ref, kk_ref, vt_ref, wit_ref, *, nq, nqi, hd, idx_heads):
    d = x_ref.shape[1]
    tq = x_ref.shape[0]
    x = x_ref[...]
    sh1 = mod_ref[:, 0:d]
    sc1 = mod_ref[:, d:2 * d]
    h = _rmsnorm(x, g_ref[...]) * (1.0 + sc1) + sh1
    u = jnp.dot(h.astype(BF16), w_ref[...], preferred_element_type=F32)
    lane = lax.broadcasted_iota(I32, (tq, LANES), 1)
    cos = cos_ref[...]
    sin = sin_ref[...]
    qg = qg_ref[...]
    e = e_ref[...]
    qscale = hd ** -0.5

    for p in range(nq):
        t = u[:, p * LANES:(p + 1) * LANES]
        t2 = t * t
        hi = t2.astype(BF16)
        lo = (t2 - hi.astype(F32)).astype(BF16)
        ss = (jnp.dot(hi, e, preferred_element_type=F32) + jnp.dot(lo, e, preferred_element_type=F32))
        tn = t * lax.rsqrt(ss * (1.0 / hd) + RMS_EPS) * qg
        qt_ref[p] = ((tn * cos + _rot_half(tn, lane) * sin) * (qscale * LOG2E)).T.astype(BF16)

    for p in range(nqi):
        t = u[:, (nq + p) * LANES:(nq + p + 1) * LANES]
        qit_ref[p] = ((t * cos + _rot_half(t, lane) * sin) * qscale).T.astype(BF16)

    kk = u[:, (nq + nqi) * LANES:(nq + nqi + 1) * LANES]
    left = lane < hd
    ssk = jnp.sum(jnp.where(left, kk * kk, 0.0), axis=-1, keepdims=True)
    fac = jnp.where(left, lax.rsqrt(ssk * (1.0 / hd) + RMS_EPS) * kg_ref[...], 1.0)
    kn = kk * fac
    kr = kn * cos + _rot_half(kn, lane) * sin
    k_l = jnp.where(left, kr, 0.0)
    ki_r = jnp.where(left, 0.0, kr)
    kk_ref[0] = k_l.astype(BF16)
    kk_ref[1] = pltpu.roll(k_l, hd, 1).astype(BF16)
    kk_ref[2] = pltpu.roll(ki_r, hd, 1).astype(BF16)
    kk_ref[3] = ki_r.astype(BF16)

    vw = u[:, (nq + nqi + 1) * LANES:(nq + nqi + 2) * LANES]
    vt_ref[...] = jnp.where(left, vw, 1.0).T.astype(BF16)
    wit_ref[...] = (vw * (idx_heads ** -0.5)).T[hd:hd + idx_heads, :]


def _dsa_in_call(x, mod_l, g, w_in, q_g, k_g, *, n_heads, hd, idx_heads, tq=512):
    b, s, d = x.shape
    assert 2 * hd == LANES, "two heads per lane tile"
    nq = n_heads * hd // LANES
    nqi = idx_heads * hd // LANES
    o_k = n_heads * hd
    o_v = o_k + hd
    o_qi = o_v + hd
    o_ki = o_qi + idx_heads * hd
    o_wi = o_ki + hd
    pad = LANES - hd - idx_heads
    w = jnp.concatenate([w_in[:, :o_k], w_in[:, o_qi:o_ki], w_in[:, o_k:o_v], w_in[:, o_ki:o_wi],
                         w_in[:, o_v:o_qi], w_in[:, o_wi:], jnp.zeros((d, pad), w_in.dtype)],
                        axis=1).astype(BF16)
    ncol = w.shape[1]
    inv = ROPE_THETA ** (-jnp.arange(0, hd, 2, dtype=F32) / hd)
    ang = jnp.arange(s, dtype=F32)[:, None] * inv[None, :]
    cos_t = jnp.tile(jnp.cos(ang), (1, 4))
    sin_h = jnp.sin(ang)
    sin_t = jnp.tile(jnp.concatenate([-sin_h, sin_h], axis=1), (1, 2))
    head_of_lane = jnp.arange(LANES) // hd
    e = (head_of_lane[:, None] == head_of_lane[None, :]).astype(BF16)
    qg = jnp.tile(q_g, 2).reshape(1, LANES)
    kg = jnp.concatenate([k_g, jnp.ones((hd,), k_g.dtype)]).reshape(1, LANES)

    return pl.pallas_call(
        functools.partial(_dsa_in_kernel, nq=nq, nqi=nqi, hd=hd, idx_heads=idx_heads),
        grid=(b, s // tq),
        in_specs=[pl.BlockSpec((None, tq, d), lambda bi, i: (bi, i, 0)),
                  pl.BlockSpec((None, 1, mod_l.shape[-1]), lambda bi, i: (bi, 0, 0)),
                  _const_spec((1, d)), _const_spec((d, ncol)), _const_spec((1, LANES)),
                  _const_spec((1, LANES)), _const_spec((LANES, LANES)),
                  pl.BlockSpec((tq, LANES), lambda bi, i: (i, 0)),
                  pl.BlockSpec((tq, LANES), lambda bi, i: (i, 0))],
        out_specs=[pl.BlockSpec((None, nq, LANES, tq), lambda bi, i: (bi, 0, 0, i)),
                   pl.BlockSpec((None, nqi, LANES, tq), lambda bi, i: (bi, 0, 0, i)),
                   pl.BlockSpec((None, 4, tq, LANES), lambda bi, i: (bi, 0, i, 0)),
                   pl.BlockSpec((None, LANES, tq), lambda bi, i: (bi, 0, i)),
                   pl.BlockSpec((None, idx_heads, tq), lambda bi, i: (bi, 0, i))],
        out_shape=[jax.ShapeDtypeStruct((b, nq, LANES, s), BF16),
                   jax.ShapeDtypeStruct((b, nqi, LANES, s), BF16),
                   jax.ShapeDtypeStruct((b, 4, s, LANES), BF16),
                   jax.ShapeDtypeStruct((b, LANES, s), BF16),
                   jax.ShapeDtypeStruct((b, idx_heads, s), F32)],
        compiler_params=pltpu.CompilerParams(
            dimension_semantics=("arbitrary", "arbitrary"), vmem_limit_bytes=VMEM_LIMIT),
        name="dsa_in",
    )(x, mod_l, g.reshape(1, -1), w, qg, kg, e, cos_t, sin_t)


def _dsa_attn_kernel(bnd_ref, qit0_ref, wit0_ref, qitn_ref, witn_ref, qt_ref, kk_ref, vt_ref, o_ref,
                     sct_s, mask_s, st_s, mm_s, m_s, acc_s, *, kc, ks, topk, hd, idx_heads, group,
                     tie_from, max_groups, n_blocks, head_passes, sl):
    npair, _, qb = qt_ref.shape
    nh = 2 * npair
    nsub = SUBLANES
    i = pl.program_id(1)
    has_sel = i < n_blocks
    has_att = i >= 1
    nk = (i * qb) // kc + 1
    nkp = ((i - 1) * qb) // kc + 1
    slot = lax.rem(i, 2)
    mask_w = mask_s.at[slot]
    mask_r = mask_s.at[1 - slot]
    sct_c = sct_s.at[slot]
    sct_n = sct_s.at[1 - slot]
    kf = float(topk)
    fast = bnd_ref[0] <= EXP_SAFE_BOUND
    merged = jnp.logical_and(jnp.logical_and(has_sel, has_att), fast)

    def slab(src, c, t):
        return src[pl.ds(pl.multiple_of(c * kc + t * sl, sl), sl), :]

    def finish(acc, op, red):
        n = sl // nsub
        acc = acc.reshape(n, nsub, qb)
        r = acc[0]
        for t in range(1, n):
            r = op(r, acc[t])
        return jnp.broadcast_to(red(r, axis=0, keepdims=True), (nsub, qb))

    def reduce_keys(src, fn, op, red, init):
        def body(ci, acc):
            for t in range(kc // sl):
                acc = op(acc, fn(slab(src, ci, t), ci, t))
            return acc
        return finish(lax.fori_loop(0, nk, body, jnp.full((sl, qb), init, F32)), op, red)

    def rows(x):
        return jnp.concatenate([x] * (sl // nsub), axis=0)

    def score_chunk(qit_ref, wit_ref, blk, dst, ci, mm):
        qpos = blk * qb + lax.broadcasted_iota(I32, (ks, qb), 1)
        krow = lax.broadcasted_iota(I32, (ks, qb), 0)
        mx, mn = mm
        for sub in range(kc // ks):
            k0 = pl.multiple_of(ci * kc + sub * ks, ks)
            kis = (kk_ref[2, pl.ds(k0, ks), :], kk_ref[3, pl.ds(k0, ks), :])
            acc = jnp.zeros((ks, qb), F32)
            for hh in range(idx_heads):
                lg = jnp.dot(kis[hh % 2], qit_ref[hh // 2], preferred_element_type=F32)
                acc = acc + wit_ref[hh:hh + 1, :] * jnp.maximum(lg, 0.0)
            causal = (k0 + krow) <= qpos
            lo_v = jnp.where(causal, acc, -jnp.inf)
            hi_v = jnp.where(causal, acc, jnp.inf)
            dst[pl.ds(k0, ks), :] = lo_v
            for t in range(ks // sl):
                mx = jnp.maximum(mx, lo_v[t * sl:(t + 1) * sl])
                mn = jnp.minimum(mn, hi_v[t * sl:(t + 1) * sl])
        return mx, mn

    mm0 = (jnp.full((sl, qb), -jnp.inf, F32), jnp.full((sl, qb), jnp.inf, F32))

    def save_range(mm):
        mm_s[0] = finish(mm[0], jnp.maximum, jnp.max)
        mm_s[1] = finish(mm[1], jnp.minimum, jnp.min)

    @pl.when(i == 0)
    def _():
        save_range(lax.fori_loop(
            0, nk, functools.partial(score_chunk, qit0_ref, wit0_ref, 0, sct_c), mm0))

    @pl.when(has_sel)
    def _():
        mx = mm_s[0]
        st_s[0] = mm_s[1]
        st_s[1] = mx + (jnp.abs(mx) * 1e-6 + 1e-30)
        st_s[2] = (i * qb + lax.broadcasted_iota(I32, (nsub, qb), 1) + 1).astype(F32)
        st_s[3] = jnp.zeros((nsub, qb), F32)

    def count_step(st):
        c, lo, hi, clo, chi, acc = st
        mid = 0.5 * lo + 0.5 * hi
        midr = rows(mid)
        for t in range(kc // sl):
            acc = acc + jnp.where(slab(sct_c, c, t) >= midr, 1.0, 0.0)
        last = c == nk - 1
        tot = finish(acc, jnp.add, jnp.sum)
        ge = tot >= kf
        lo = jnp.where(last, jnp.where(ge, mid, lo), lo)
        clo = jnp.where(last, jnp.where(ge, tot, clo), clo)
        hi = jnp.where(last, jnp.where(ge, hi, mid), hi)
        chi = jnp.where(last, jnp.where(ge, chi, tot), chi)
        return (jnp.where(last, 0, c + 1), lo, hi, clo, chi, jnp.where(last, 0.0, acc))

    def scores(ci, h):
        kk = kk_ref[h % 2, pl.ds(pl.multiple_of(ci * kc, kc), kc), :]
        return jnp.dot(kk, qt_ref[h // 2], preferred_element_type=F32)

    def pv_step(ci, st, use_max, passes):
        c0 = pl.multiple_of(ci * kc, kc)
        vt = vt_ref[:, pl.ds(c0, kc)]
        msk = mask_r[pl.ds(c0, kc), :]
        ahead = [scores(ci, h) for h in range(min(QK_LOOKAHEAD, nh))]
        for h in range(nh):
            s_cur = ahead.pop(0)
            if h + QK_LOOKAHEAD < nh:
                ahead.append(scores(ci, h + QK_LOOKAHEAD))
            if use_max:
                pr = jnp.where(msk > 0, jnp.exp2(s_cur - m_s[h][0:1, :]), 0.0).astype(BF16)
            else:
                pr = jnp.exp2(s_cur).astype(BF16) * msk
            acc_s[h] += jnp.dot(vt, pr, preferred_element_type=F32)
            for _ in range(passes[h] if passes else 0):
                st = count_step(st)
        return st

    @pl.when(has_att)
    def _():
        acc_s[...] = jnp.zeros(acc_s.shape, F32)

    @pl.when(merged)
    def _():
        st = (jnp.int32(0), st_s[0], st_s[1], st_s[2], st_s[3], jnp.zeros((sl, qb), F32))
        st = lax.fori_loop(0, nkp, functools.partial(pv_step, use_max=False, passes=head_passes), st)
        st_s[0], st_s[1], st_s[2], st_s[3] = st[1], st[2], st[3], st[4]

    @pl.when(jnp.logical_and(has_att, jnp.logical_and(fast, jnp.logical_not(has_sel))))
    def _():
        lax.fori_loop(0, nkp, functools.partial(pv_step, use_max=False, passes=None), 0)

    @pl.when(jnp.logical_and(has_att, jnp.logical_not(fast)))
    def _():
        def max_step(ci, _):
            @pl.when(ci == 0)
            def _():
                m_s[...] = jnp.full(m_s.shape, NEG_BIG, F32)
            msk = mask_r[pl.ds(pl.multiple_of(ci * kc, kc), kc), :]
            for h in range(nh):
                mx = jnp.max(jnp.where(msk > 0, scores(ci, h), NEG_BIG), axis=0, keepdims=True)
                m_s[h] = jnp.maximum(m_s[h], jnp.broadcast_to(mx, (nsub, qb)))
            return 0

        lax.fori_loop(0, nkp, max_step, 0)
        lax.fori_loop(0, nkp, functools.partial(pv_step, use_max=True, passes=None), 0)

    @pl.when(has_att)
    def _():
        for p in range(npair):
            a0, a1 = acc_s[2 * p], acc_s[2 * p + 1]
            ot = jnp.concatenate([a0[0:hd] / a0[hd:2 * hd], a1[0:hd] / a1[hd:2 * hd]], axis=0)
            o_ref[p] = ot.T.astype(BF16)

    @pl.when(has_sel)
    def _():
        def count_ge(thr):
            thr = rows(thr)
            return reduce_keys(sct_c, lambda blk, c, t: jnp.where(blk >= thr, 1.0, 0.0),
                               jnp.add, jnp.sum, 0.0)

        def bisect(_, st):
            lo, hi, clo, chi = st
            mid = 0.5 * lo + 0.5 * hi
            c = count_ge(mid)
            ge = c >= kf
            return (jnp.where(ge, mid, lo), jnp.where(ge, hi, mid),
                    jnp.where(ge, c, clo), jnp.where(ge, chi, c))

        def open_rows(clo, tied):
            return jnp.max(jnp.where(clo > kf, 1.0 - tied, 0.0)) > 0.0

        g0 = jnp.where(merged, tie_from // group, 0)

        def search_group(st):
            g, _, lo, hi, clo, chi, tied = st
            lo, hi, clo, chi = lax.fori_loop(0, group, bisect, (lo, hi, clo, chi))

            def tie_check(tied):
                lor, hir = rows(lo), rows(hi)
                a = reduce_keys(sct_c, lambda blk, c, t: jnp.where(blk >= lor, blk, jnp.inf),
                                jnp.minimum, jnp.min, jnp.inf)
                b = reduce_keys(sct_c, lambda blk, c, t: jnp.where(blk < hir, blk, -jnp.inf),
                                jnp.maximum, jnp.max, -jnp.inf)
                return jnp.where(jnp.logical_and(clo > kf, a == b), 1.0, tied)

            tied = lax.cond(jnp.logical_and((g + 1) * group >= tie_from, open_rows(clo, tied)),
                            tie_check, lambda t: t, tied)
            return (g + 1, open_rows(clo, tied).astype(I32), lo, hi, clo, chi, tied)

        zero = jnp.zeros((nsub, qb), F32)
        _, _, lo, hi, clo, chi, _ = lax.while_loop(
            lambda st: jnp.logical_and(st[1] > 0, st[0] < max_groups), search_group,
            (g0, open_rows(st_s[2], zero).astype(I32), st_s[0], st_s[1], st_s[2], st_s[3], zero))

        def tile_k(x):
            return jnp.concatenate([x] * (ks // nsub), axis=0)

        lok, hik, free = tile_k(lo), tile_k(hi), tile_k(kf - chi)
        tri = jnp.where(lax.broadcasted_iota(I32, (ks, ks), 1) <= lax.broadcasted_iota(I32, (ks, ks), 0),
                        1.0, 0.0).astype(BF16)

        def mask_chunk(ci, carry):
            subs = []
            for t in range(kc // ks):
                k0 = pl.multiple_of(ci * kc + t * ks, ks)
                blk = sct_c[pl.ds(k0, ks), :]
                top = blk >= hik
                tie = jnp.where(top, 0.0, jnp.where(blk >= lok, 1.0, 0.0))
                subs.append((k0, top, tie, jnp.dot(tri, tie.astype(BF16), preferred_element_type=F32)))
            for k0, top, tie, rank in subs:
                keep = jnp.where(rank + tile_k(carry) <= free, tie, 0.0)
                mask_w[pl.ds(k0, ks), :] = jnp.where(top, 1.0, keep).astype(BF16)
                carry = carry + jnp.broadcast_to(rank[ks - 1:ks, :], (nsub, qb))
            return carry

        zero8 = jnp.zeros((nsub, qb), F32)
        has_next = i + 1 < n_blocks

        @pl.when(has_next)
        def _():
            nkn = ((i + 1) * qb) // kc + 1
            nxt = functools.partial(score_chunk, qitn_ref, witn_ref, i + 1, sct_n)

            def both(ci, st):
                return (mask_chunk(ci, st[0]), nxt(ci, st[1]))

            _, mm = lax.fori_loop(0, nk, both, (zero8, mm0))
            save_range(lax.fori_loop(nk, nkn, nxt, mm))

        @pl.when(jnp.logical_not(has_next))
        def _():
            lax.fori_loop(0, nk, mask_chunk, zero8)


def _dsa_attn_call(bnd, qt, qit, kk, vt, wit, *, hd, idx_heads, topk, qb=256, kc=512, ks=128,
                   sl=32):
    b, npair, _, s = qt.shape
    nqi = qit.shape[1]
    assert kc % qb == 0 and s % kc == 0 and kc % ks == 0 and ks % sl == 0 and sl % SUBLANES == 0
    nb = s // qb
    head_passes = tuple(2 if h % 2 == 0 else 1 for h in range(2 * npair))
    assert 2 * hd == LANES
    prev = lambda i: jnp.maximum(i - 1, 0)
    cur = lambda i: jnp.minimum(i, nb - 1)
    nxt = lambda i: jnp.minimum(i + 1, nb - 1)
    return pl.pallas_call(
        functools.partial(_dsa_attn_kernel, kc=kc, ks=ks, topk=topk, hd=hd, idx_heads=idx_heads,
                          group=4, tie_from=16, max_groups=96, n_blocks=nb,
                          head_passes=head_passes, sl=sl),
        grid=(b, nb + 1),
        in_specs=[pl.BlockSpec(memory_space=pltpu.SMEM),
                  pl.BlockSpec((None, nqi, LANES, qb), lambda bi, i: (bi, 0, 0, cur(i))),
                  pl.BlockSpec((None, idx_heads, qb), lambda bi, i: (bi, 0, cur(i))),
                  pl.BlockSpec((None, nqi, LANES, qb), lambda bi, i: (bi, 0, 0, nxt(i))),
                  pl.BlockSpec((None, idx_heads, qb), lambda bi, i: (bi, 0, nxt(i))),
                  pl.BlockSpec((None, npair, LANES, qb), lambda bi, i: (bi, 0, 0, prev(i))),
                  pl.BlockSpec((None, 4, s, LANES), lambda bi, i: (bi, 0, 0, 0)),
                  pl.BlockSpec((None, LANES, s), lambda bi, i: (bi, 0, 0))],
        out_specs=pl.BlockSpec((None, npair, qb, LANES), lambda bi, i: (bi, 0, prev(i), 0)),
        out_shape=jax.ShapeDtypeStruct((b, npair, s, LANES), BF16),
        scratch_shapes=[pltpu.VMEM((2, s, qb), F32),
                        pltpu.VMEM((2, s, qb), BF16), pltpu.VMEM((4, SUBLANES, qb), F32),
                        pltpu.VMEM((2, SUBLANES, qb), F32),
                        pltpu.VMEM((2 * npair, SUBLANES, qb), F32),
                        pltpu.VMEM((2 * npair, LANES, qb), F32)],
        compiler_params=pltpu.CompilerParams(
            dimension_semantics=("arbitrary", "arbitrary"), vmem_limit_bytes=VMEM_LIMIT),
        name="dsa_attn",
    )(bnd, qit, wit, qit, wit, qt, kk, vt)


def kernel(x, c, norm_mix_g, norm_ffn_g, ada_w, ada_b, a_w_in, a_conv_w, a_conv_b, a_gate_r_w,
           a_gate_r_b, a_gate_i_w, a_gate_i_b, a_lambda, a_w_out, b_w_in, b_q_norm_g, b_k_norm_g,
           b_w_out, ffn_w1, ffn_w2):
    b, s, d = x.shape
    hd = b_q_norm_g.shape[-1]
    n_heads = b_w_out.shape[1] // hd
    idx_heads = (b_w_in.shape[-1] - n_heads * hd - 3 * hd) // (hd + 1)
    topk = min(TOPK_MAX, s // 4)
    depth = ada_w.shape[0]

    mod = _mod_call(c, ada_w, ada_b)
    mod_rows = mod.reshape(depth, b, 1, mod.shape[-1])

    for i in range(depth):
        j = i // 2
        if i % 2 == 0:
            x = _rglru_call(x, mod[i], norm_mix_g[i], a_w_in[j], a_conv_w[j], a_conv_b[j],
                            a_gate_r_w[j], a_gate_r_b[j], a_gate_i_w[j], a_gate_i_b[j],
                            a_lambda[j], a_w_out[j])
            x = _ffn_call(x, mod_rows[i], norm_ffn_g[i], ffn_w1[i], ffn_w2[i])
        else:
            qt, qit, kk, vt, wit = _dsa_in_call(x, mod_rows[i], norm_mix_g[i], b_w_in[j],
                                                 b_q_norm_g[j], b_k_norm_g[j], n_heads=n_heads,
                                                 hd=hd, idx_heads=idx_heads)
            bnd = (1.02 * hd * hd ** -0.5) * jnp.max(jnp.abs(b_q_norm_g[j])) * jnp.max(jnp.abs(b_k_norm_g[j]))
            o = _dsa_attn_call(bnd.reshape(1), qt, qit, kk, vt, wit, hd=hd, idx_heads=idx_heads,
                               topk=topk)
            x = _ffn_call(x, mod_rows[i], norm_ffn_g[i], ffn_w1[i], ffn_w2[i], o_pairs=o,
                          w_o=b_w_out[j])
    return x
```

```python
import functools

import jax
import jax.numpy as jnp
from jax import lax
from jax.experimental import pallas as pl
from jax.experimental.pallas import tpu as pltpu

F32 = jnp.float32
BF16 = jnp.bfloat16
I32 = jnp.int32

RMS_EPS = 1e-6
ROPE_THETA = 10000.0
LRU_C = 8.0
TOPK_MAX = 256
N_MOD = 6

LANES = 128
SUBLANES = 8
VMEM_LIMIT = 56 * 1024 * 1024

NEG_BIG = -1e30
EXP_SAFE_BOUND = 70.0
LOG2E = 1.4426950408889634
QK_LOOKAHEAD = 2
GELU_C0 = 0.7978845608028654
GELU_C1 = 0.044715


def _const_spec(shape):
    nd = len(shape)
    return pl.BlockSpec(shape, lambda *_: (0,) * nd, pipeline_mode=pl.Buffered(1))


def _rmsnorm(x, g):
    return x * lax.rsqrt(jnp.mean(x * x, axis=-1, keepdims=True) + RMS_EPS) * g


def _mod_kernel(c_ref, w_ref, b_ref, o_ref):
    c = c_ref[...]
    cond = c * jax.nn.sigmoid(c)
    o_ref[0] = jnp.dot(cond.astype(BF16), w_ref[0].astype(BF16),
                       preferred_element_type=F32) + b_ref[0]


def _mod_call(c, ada_w, ada_b):
    depth, d, n = ada_w.shape
    b = c.shape[0]
    tn = 1536
    return pl.pallas_call(
        _mod_kernel,
        grid=(depth, n // tn),
        in_specs=[pl.BlockSpec((b, d), lambda l, j: (0, 0)),
                  pl.BlockSpec((1, d, tn), lambda l, j: (l, 0, j)),
                  pl.BlockSpec((1, 1, tn), lambda l, j: (l, 0, j))],
        out_specs=pl.BlockSpec((1, b, tn), lambda l, j: (l, 0, j)),
        out_shape=jax.ShapeDtypeStruct((depth, b, n), F32),
        compiler_params=pltpu.CompilerParams(
            dimension_semantics=("arbitrary", "arbitrary"), vmem_limit_bytes=VMEM_LIMIT),
        name="adaln_mod",
    )(c, ada_w, ada_b.reshape(depth, 1, n))


def _rglru_kernel(x_ref, mod_ref, g_ref, win_ref, cw_ref, cb_ref, wr_ref, br_ref, wi_ref, bi_ref,
                  lam_ref, wout_ref, o_ref, h_s, gb_s, xbuf, a_s, b_s, hc, *, rc):
    nb, ts, d = x_ref.shape
    r = ts * nb
    dr = lam_ref.shape[1]
    nblk, blk, _ = wr_ref.shape
    cwid = cw_ref.shape[0]
    tail = (cwid - 1) * nb

    @pl.when(pl.program_id(0) == 0)
    def _():
        xbuf[0:tail, :] = jnp.zeros((tail, dr), F32)
        hc[...] = jnp.zeros((nb, dr), F32)

    x = jnp.swapaxes(x_ref[...], 0, 1)
    sh1 = mod_ref[:, 0:d]
    sc1 = mod_ref[:, d:2 * d]
    g1 = mod_ref[:, 2 * d:3 * d]
    h_s[...] = (_rmsnorm(x, g_ref[...]) * (1.0 + sc1)[None] + sh1[None]).reshape(r, d).astype(BF16)

    sp_h = (0.5 * LRU_C) * jax.nn.softplus(-lam_ref[...])
    br_h = 0.5 * br_ref[...]
    bi_h = 0.5 * bi_ref[...]

    def in_proj(c, n):
        rows = slice(c * rc, (c + 1) * rc)
        hb = h_s[rows, :]
        ux = jnp.dot(hb, win_ref[:, n * blk:(n + 1) * blk], preferred_element_type=F32)
        ug = jnp.dot(hb, win_ref[:, dr + n * blk:dr + (n + 1) * blk], preferred_element_type=F32)
        xbuf[tail + c * rc:tail + (c + 1) * rc, n * blk:(n + 1) * blk] = ux
        return ug

    def conv_and_gate_dots(c, n):
        r0 = c * rc
        cs = slice(n * blk, (n + 1) * blk)
        xc = cb_ref[:, cs] + cw_ref[0:1, cs] * xbuf[r0:r0 + rc, cs]
        for k in range(1, cwid):
            xc = xc + cw_ref[k:k + 1, cs] * xbuf[r0 + k * nb:r0 + k * nb + rc, cs]
        xn = xc.astype(BF16)
        return xc, (jnp.dot(xn, wr_ref[n], preferred_element_type=F32),
                    jnp.dot(xn, wi_ref[n], preferred_element_type=F32))

    def gate_tail(c, n, xc, pre, ug):
        ro = slice(c * rc, (c + 1) * rc)
        cs = slice(n * blk, (n + 1) * blk)
        p = sp_h[:, cs] + sp_h[:, cs] * jnp.tanh(0.5 * pre[0] + br_h[:, cs])
        ig = 0.5 + 0.5 * jnp.tanh(0.5 * pre[1] + bi_h[:, cs])
        a = jnp.exp2(p * (-LOG2E))
        a_s[ro, cs] = a
        b_s[ro, cs] = jnp.sqrt(jnp.tanh(p) * (a * a + 1.0)) * (ig * xc)
        t = jnp.tanh(ug * (GELU_C0 + (GELU_C0 * GELU_C1) * (ug * ug)))
        hu = 0.5 * ug
        gb_s[ro, cs] = hu + hu * t

    units = [(c, n) for c in range(r // rc) for n in range(nblk)]
    ug = in_proj(*units[0])
    for k, (c, n) in enumerate(units):
        xc, pre = conv_and_gate_dots(c, n)
        ug_next = in_proj(*units[k + 1]) if k + 1 < len(units) else None
        gate_tail(c, n, xc, pre, ug)
        ug = ug_next
    xbuf[0:tail, :] = xbuf[r:r + tail, :]

    def step(t, hprev):
        r0 = pl.multiple_of(t * nb, nb)
        hnew = a_s[pl.ds(r0, nb), :] * hprev + b_s[pl.ds(r0, nb), :]
        b_s[pl.ds(r0, nb), :] = hnew
        return hnew

    hc[...] = lax.fori_loop(0, ts, step, hc[...], unroll=8)

    y = b_s[...] * gb_s[...]
    out = jnp.dot(y.astype(BF16), wout_ref[...], preferred_element_type=F32)
    o_ref[...] = jnp.swapaxes(x + out.reshape(ts, nb, d) * g1[None], 0, 1)


def _rglru_call(x, mod0, g, w_in, conv_w, conv_b, wr, br, wi, bi, lam, w_out, *, ts=128, rc=256):
    nb, s, d = x.shape
    assert nb == SUBLANES, "the recurrence keeps the batch on the sublane axis"
    r = ts * nb
    dr = lam.shape[-1]
    tail = (conv_w.shape[0] - 1) * nb
    row = lambda v: v.reshape(1, -1)
    args = (x, mod0, row(g), w_in.astype(BF16), conv_w, row(conv_b), wr.astype(BF16), row(br),
            wi.astype(BF16), row(bi), row(lam), w_out.astype(BF16))
    x_spec = pl.BlockSpec((nb, ts, d), lambda i: (0, i, 0))
    return pl.pallas_call(
        functools.partial(_rglru_kernel, rc=rc),
        grid=(s // ts,),
        in_specs=[x_spec] + [_const_spec(a.shape) for a in args[1:]],
        out_specs=x_spec,
        out_shape=jax.ShapeDtypeStruct((nb, s, d), F32),
        scratch_shapes=[pltpu.VMEM((r, d), BF16), pltpu.VMEM((r, dr), F32),
                        pltpu.VMEM((r + tail, dr), F32), pltpu.VMEM((r, dr), F32),
                        pltpu.VMEM((r, dr), F32), pltpu.VMEM((nb, dr), F32)],
        compiler_params=pltpu.CompilerParams(
            dimension_semantics=("arbitrary",), vmem_limit_bytes=VMEM_LIMIT),
        name="rglru_mixer",
    )(*args)


def _ffn_kernel(*refs, has_proj, fc):
    if has_proj:
        x_ref, mod_ref, g_ref, w1_ref, w2_ref, o_in_ref, wo_ref, out_ref = refs
    else:
        x_ref, mod_ref, g_ref, w1_ref, w2_ref, out_ref = refs
    d = x_ref.shape[1]
    dff = w1_ref.shape[1]
    x = x_ref[...]
    if has_proj:
        g1 = mod_ref[:, 2 * d:3 * d]
        o = jnp.concatenate([o_in_ref[p] for p in range(o_in_ref.shape[0])], axis=-1)
        x = x + g1 * jnp.dot(o, wo_ref[...], preferred_element_type=F32)
    sh2 = mod_ref[:, 3 * d:4 * d]
    sc2 = mod_ref[:, 4 * d:5 * d]
    g2 = mod_ref[:, 5 * d:6 * d]
    hb = (_rmsnorm(x, g_ref[...]) * (1.0 + sc2) + sh2).astype(BF16)
    acc = jnp.zeros(x.shape, F32)
    for c in range(dff // fc):
        hid = jnp.dot(hb, w1_ref[:, c * fc:(c + 1) * fc], preferred_element_type=F32)
        hid = jnp.square(jnp.maximum(hid, 0.0))
        acc = acc + jnp.dot(hid.astype(BF16), w2_ref[c * fc:(c + 1) * fc, :],
                            preferred_element_type=F32)
    out_ref[...] = x + g2 * acc


def _ffn_call(x, mod_l, g, w1, w2, o_pairs=None, w_o=None, *, rf=512, fc=1024):
    b, s, d = x.shape
    has_proj = o_pairs is not None
    args = [x, mod_l, g.reshape(1, -1), w1.astype(BF16), w2.astype(BF16)]
    in_specs = [pl.BlockSpec((None, rf, d), lambda bi, i: (bi, i, 0)),
                pl.BlockSpec((None, 1, mod_l.shape[-1]), lambda bi, i: (bi, 0, 0)),
                _const_spec((1, d)), _const_spec(w1.shape), _const_spec(w2.shape)]
    if has_proj:
        npair = o_pairs.shape[1]
        args += [o_pairs, w_o.astype(BF16)]
        in_specs += [pl.BlockSpec((None, npair, rf, LANES), lambda bi, i: (bi, 0, i, 0)),
                     _const_spec(w_o.shape)]
    return pl.pallas_call(
        functools.partial(_ffn_kernel, has_proj=has_proj, fc=fc),
        grid=(b, s // rf),
        in_specs=in_specs,
        out_specs=pl.BlockSpec((None, rf, d), lambda bi, i: (bi, i, 0)),
        out_shape=jax.ShapeDtypeStruct((b, s, d), F32),
        compiler_params=pltpu.CompilerParams(
            dimension_semantics=("arbitrary", "arbitrary"), vmem_limit_bytes=VMEM_LIMIT),
        name="ffn_proj" if has_proj else "ffn",
    )(*args)


def _rot_half(x, lane):
    return jnp.where((lane & 63) < 32, pltpu.roll(x, 96, 1), pltpu.roll(x, 32, 1))


def _dsa_in_kernel(x_ref, mod_ref, g_ref, w_ref, qg_ref, kg_ref, e_ref, cos_ref, sin_ref,
                   qt_ref, qit_ref, kk_ref, vt_ref, wit_ref, *, nq, nqi, hd, idx_heads):
    d = x_ref.shape[1]
    tq = x_ref.shape[0]
    x = x_ref[...]
    sh1 = mod_ref[:, 0:d]
    sc1 = mod_ref[:, d:2 * d]
    hb = (_rmsnorm(x, g_ref[...]) * (1.0 + sc1) + sh1).astype(BF16)
    lane = lax.broadcasted_iota(I32, (tq, LANES), 1)
    left = lane < hd
    cos = cos_ref[...]
    sin = sin_ref[...]
    qg = qg_ref[...]
    e = e_ref[...]
    qscale = hd ** -0.5

    def q_tile(p, t):
        t2 = t * t
        hi = t2.astype(BF16)
        lo = (t2 - hi.astype(F32)).astype(BF16)
        ss = (jnp.dot(hi, e, preferred_element_type=F32) + jnp.dot(lo, e, preferred_element_type=F32))
        tn = t * lax.rsqrt(ss * (1.0 / hd) + RMS_EPS) * qg
        qt_ref[p] = ((tn * cos + _rot_half(tn, lane) * sin) * (qscale * LOG2E)).T.astype(BF16)

    def qi_tile(p, t):
        qit_ref[p] = ((t * cos + _rot_half(t, lane) * sin) * qscale).T.astype(BF16)

    def kk_tile(kk):
        ssk = jnp.sum(jnp.where(left, kk * kk, 0.0), axis=-1, keepdims=True)
        fac = jnp.where(left, lax.rsqrt(ssk * (1.0 / hd) + RMS_EPS) * kg_ref[...], 1.0)
        kn = kk * fac
        kr = kn * cos + _rot_half(kn, lane) * sin
        k_l = jnp.where(left, kr, 0.0)
        ki_r = jnp.where(left, 0.0, kr)
        kk_ref[0] = k_l.astype(BF16)
        kk_ref[1] = pltpu.roll(k_l, hd, 1).astype(BF16)
        kk_ref[2] = pltpu.roll(ki_r, hd, 1).astype(BF16)
        kk_ref[3] = ki_r.astype(BF16)

    def vw_tile(vw):
        vt_ref[...] = jnp.where(left, vw, 1.0).T.astype(BF16)
        wit_ref[...] = (vw * (idx_heads ** -0.5)).T[hd:hd + idx_heads, :]

    def tile(p, t):
        if p < nq:
            q_tile(p, t)
        elif p < nq + nqi:
            qi_tile(p - nq, t)
        elif p == nq + nqi:
            kk_tile(t)
        else:
            vw_tile(t)

    gw = 2 * LANES
    ngroups = w_ref.shape[1] // gw

    def proj(gi):
        return jnp.dot(hb, w_ref[:, gi * gw:(gi + 1) * gw], preferred_element_type=F32)

    u_next = proj(0)
    for gi in range(ngroups):
        u = u_next
        if gi + 1 < ngroups:
            u_next = proj(gi + 1)
        for half in range(gw // LANES):
            tile(gi * (gw // LANES) + half, u[:, half * LANES:(half + 1) * LANES])


def _dsa_in_call(x, mod_l, g, w_in, q_g, k_g, *, n_heads, hd, idx_heads, tq=512):
    b, s, d = x.shape
    assert 2 * hd == LANES, "two heads per lane tile"
    nq = n_heads * hd // LANES
    nqi = idx_heads * hd // LANES
    o_k = n_heads * hd
    o_v = o_k + hd
    o_qi = o_v + hd
    o_ki = o_qi + idx_heads * hd
    o_wi = o_ki + hd
    pad = LANES - hd - idx_heads
    w = jnp.concatenate([w_in[:, :o_k], w_in[:, o_qi:o_ki], w_in[:, o_k:o_v], w_in[:, o_ki:o_wi],
                         w_in[:, o_v:o_qi], w_in[:, o_wi:], jnp.zeros((d, pad), w_in.dtype)],
                        axis=1).astype(BF16)
    ncol = w.shape[1]
    inv = ROPE_THETA ** (-jnp.arange(0, hd, 2, dtype=F32) / hd)
    ang = jnp.arange(s, dtype=F32)[:, None] * inv[None, :]
    cos_t = jnp.tile(jnp.cos(ang), (1, 4))
    sin_h = jnp.sin(ang)
    sin_t = jnp.tile(jnp.concatenate([-sin_h, sin_h], axis=1), (1, 2))
    head_of_lane = jnp.arange(LANES) // hd
    e = (head_of_lane[:, None] == head_of_lane[None, :]).astype(BF16)
    qg = jnp.tile(q_g, 2).reshape(1, LANES)
    kg = jnp.concatenate([k_g, jnp.ones((hd,), k_g.dtype)]).reshape(1, LANES)

    return pl.pallas_call(
        functools.partial(_dsa_in_kernel, nq=nq, nqi=nqi, hd=hd, idx_heads=idx_heads),
        grid=(b, s // tq),
        in_specs=[pl.BlockSpec((None, tq, d), lambda bi, i: (bi, i, 0)),
                  pl.BlockSpec((None, 1, mod_l.shape[-1]), lambda bi, i: (bi, 0, 0)),
                  _const_spec((1, d)), _const_spec((d, ncol)), _const_spec((1, LANES)),
                  _const_spec((1, LANES)), _const_spec((LANES, LANES)),
                  pl.BlockSpec((tq, LANES), lambda bi, i: (i, 0)),
                  pl.BlockSpec((tq, LANES), lambda bi, i: (i, 0))],
        out_specs=[pl.BlockSpec((None, nq, LANES, tq), lambda bi, i: (bi, 0, 0, i)),
                   pl.BlockSpec((None, nqi, LANES, tq), lambda bi, i: (bi, 0, 0, i)),
                   pl.BlockSpec((None, 4, tq, LANES), lambda bi, i: (bi, 0, i, 0)),
                   pl.BlockSpec((None, LANES, tq), lambda bi, i: (bi, 0, i)),
                   pl.BlockSpec((None, idx_heads, tq), lambda bi, i: (bi, 0, i))],
        out_shape=[jax.ShapeDtypeStruct((b, nq, LANES, s), BF16),
                   jax.ShapeDtypeStruct((b, nqi, LANES, s), BF16),
                   jax.ShapeDtypeStruct((b, 4, s, LANES), BF16),
                   jax.ShapeDtypeStruct((b, LANES, s), BF16),
                   jax.ShapeDtypeStruct((b, idx_heads, s), F32)],
        compiler_params=pltpu.CompilerParams(
            dimension_semantics=("arbitrary", "arbitrary"), vmem_limit_bytes=VMEM_LIMIT),
        name="dsa_in",
    )(x, mod_l, g.reshape(1, -1), w, qg, kg, e, cos_t, sin_t)


def _dsa_attn_kernel(bnd_ref, qit0_ref, wit0_ref, qitn_ref, witn_ref, qt_ref, kk_ref, vt_ref, o_ref,
                     sct_s, mask_s, st_s, mm_s, m_s, acc_s, *, kc, ks, topk, hd, idx_heads, group,
                     tie_from, max_groups, n_blocks, head_passes, sl):
    npair, _, qb = qt_ref.shape
    nh = 2 * npair
    nsub = SUBLANES
    i = pl.program_id(1)
    has_sel = i < n_blocks
    has_att = i >= 1
    nk = (i * qb) // kc + 1
    nkp = ((i - 1) * qb) // kc + 1
    slot = lax.rem(i, 2)
    mask_w = mask_s.at[slot]
    mask_r = mask_s.at[1 - slot]
    sct_c = sct_s.at[slot]
    sct_n = sct_s.at[1 - slot]
    kf = float(topk)
    fast = bnd_ref[0] <= EXP_SAFE_BOUND
    merged = jnp.logical_and(jnp.logical_and(has_sel, has_att), fast)

    def slab(src, c, t):
        return src[pl.ds(pl.multiple_of(c * kc + t * sl, sl), sl), :]

    def finish(acc, op, red):
        n = sl // nsub
        acc = acc.reshape(n, nsub, qb)
        r = acc[0]
        for t in range(1, n):
            r = op(r, acc[t])
        return jnp.broadcast_to(red(r, axis=0, keepdims=True), (nsub, qb))

    def reduce_keys(src, fn, op, red, init):
        def body(ci, acc):
            for t in range(kc // sl):
                acc = op(acc, fn(slab(src, ci, t), ci, t))
            return acc
        return finish(lax.fori_loop(0, nk, body, jnp.full((sl, qb), init, F32)), op, red)

    def rows(x):
        return jnp.concatenate([x] * (sl // nsub), axis=0)

    def score_chunk(qit_ref, wit_ref, blk, dst, ci, mm):
        qpos = blk * qb + lax.broadcasted_iota(I32, (ks, qb), 1)
        krow = lax.broadcasted_iota(I32, (ks, qb), 0)
        mx, mn = mm
        for sub in range(kc // ks):
            k0 = pl.multiple_of(ci * kc + sub * ks, ks)
            kis = (kk_ref[2, pl.ds(k0, ks), :], kk_ref[3, pl.ds(k0, ks), :])
            acc = jnp.zeros((ks, qb), F32)
            for hh in range(idx_heads):
                lg = jnp.dot(kis[hh % 2], qit_ref[hh // 2], preferred_element_type=F32)
                acc = acc + wit_ref[hh:hh + 1, :] * jnp.maximum(lg, 0.0)
            causal = (k0 + krow) <= qpos
            lo_v = jnp.where(causal, acc, -jnp.inf)
            hi_v = jnp.where(causal, acc, jnp.inf)
            dst[pl.ds(k0, ks), :] = lo_v
            for t in range(ks // sl):
                mx = jnp.maximum(mx, lo_v[t * sl:(t + 1) * sl])
                mn = jnp.minimum(mn, hi_v[t * sl:(t + 1) * sl])
        return mx, mn

    mm0 = (jnp.full((sl, qb), -jnp.inf, F32), jnp.full((sl, qb), jnp.inf, F32))

    def save_range(mm):
        mm_s[0] = finish(mm[0], jnp.maximum, jnp.max)
        mm_s[1] = finish(mm[1], jnp.minimum, jnp.min)

    @pl.when(i == 0)
    def _():
        save_range(lax.fori_loop(
            0, nk, functools.partial(score_chunk, qit0_ref, wit0_ref, 0, sct_c), mm0))

    @pl.when(has_sel)
    def _():
        mx = mm_s[0]
        st_s[0] = mm_s[1]
        st_s[1] = mx + (jnp.abs(mx) * 1e-6 + 1e-30)
        st_s[2] = (i * qb + lax.broadcasted_iota(I32, (nsub, qb), 1) + 1).astype(F32)
        st_s[3] = jnp.zeros((nsub, qb), F32)

    def count_step(st):
        c, lo, hi, clo, chi, acc = st
        mid = 0.5 * lo + 0.5 * hi
        midr = rows(mid)
        for t in range(kc // sl):
            acc = acc + jnp.where(slab(sct_c, c, t) >= midr, 1.0, 0.0)
        last = c == nk - 1
        tot = finish(acc, jnp.add, jnp.sum)
        ge = tot >= kf
        lo = jnp.where(last, jnp.where(ge, mid, lo), lo)
        clo = jnp.where(last, jnp.where(ge, tot, clo), clo)
        hi = jnp.where(last, jnp.where(ge, hi, mid), hi)
        chi = jnp.where(last, jnp.where(ge, chi, tot), chi)
        return (jnp.where(last, 0, c + 1), lo, hi, clo, chi, jnp.where(last, 0.0, acc))

    def scores(ci, h):
        kk = kk_ref[h % 2, pl.ds(pl.multiple_of(ci * kc, kc), kc), :]
        return jnp.dot(kk, qt_ref[h // 2], preferred_element_type=F32)

    def pv_step(ci, st, use_max, passes):
        c0 = pl.multiple_of(ci * kc, kc)
        vt = vt_ref[:, pl.ds(c0, kc)]
        msk = mask_r[pl.ds(c0, kc), :]
        ahead = [scores(ci, h) for h in range(min(QK_LOOKAHEAD, nh))]
        for h in range(nh):
            s_cur = ahead.pop(0)
            if h + QK_LOOKAHEAD < nh:
                ahead.append(scores(ci, h + QK_LOOKAHEAD))
            if use_max:
                pr = jnp.where(msk > 0, jnp.exp2(s_cur - m_s[h][0:1, :]), 0.0).astype(BF16)
            else:
                pr = jnp.exp2(s_cur).astype(BF16) * msk
            acc_s[h] += jnp.dot(vt, pr, preferred_element_type=F32)
            for _ in range(passes[h] if passes else 0):
                st = count_step(st)
        return st

    @pl.when(has_att)
    def _():
        acc_s[...] = jnp.zeros(acc_s.shape, F32)

    @pl.when(merged)
    def _():
        st = (jnp.int32(0), st_s[0], st_s[1], st_s[2], st_s[3], jnp.zeros((sl, qb), F32))
        st = lax.fori_loop(0, nkp, functools.partial(pv_step, use_max=False, passes=head_passes), st)
        st_s[0], st_s[1], st_s[2], st_s[3] = st[1], st[2], st[3], st[4]

    @pl.when(jnp.logical_and(has_att, jnp.logical_and(fast, jnp.logical_not(has_sel))))
    def _():
        lax.fori_loop(0, nkp, functools.partial(pv_step, use_max=False, passes=None), 0)

    @pl.when(jnp.logical_and(has_att, jnp.logical_not(fast)))
    def _():
        def max_step(ci, _):
            @pl.when(ci == 0)
            def _():
                m_s[...] = jnp.full(m_s.shape, NEG_BIG, F32)
            msk = mask_r[pl.ds(pl.multiple_of(ci * kc, kc), kc), :]
            for h in range(nh):
                mx = jnp.max(jnp.where(msk > 0, scores(ci, h), NEG_BIG), axis=0, keepdims=True)
                m_s[h] = jnp.maximum(m_s[h], jnp.broadcast_to(mx, (nsub, qb)))
            return 0

        lax.fori_loop(0, nkp, max_step, 0)
        lax.fori_loop(0, nkp, functools.partial(pv_step, use_max=True, passes=None), 0)

    @pl.when(has_att)
    def _():
        for p in range(npair):
            a0, a1 = acc_s[2 * p], acc_s[2 * p + 1]
            ot = jnp.concatenate([a0[0:hd] / a0[hd:2 * hd], a1[0:hd] / a1[hd:2 * hd]], axis=0)
            o_ref[p] = ot.T.astype(BF16)

    @pl.when(has_sel)
    def _():
        def count_ge(thr):
            thr = rows(thr)
            return reduce_keys(sct_c, lambda blk, c, t: jnp.where(blk >= thr, 1.0, 0.0),
                               jnp.add, jnp.sum, 0.0)

        def bisect(_, st):
            lo, hi, clo, chi = st
            mid = 0.5 * lo + 0.5 * hi
            c = count_ge(mid)
            ge = c >= kf
            return (jnp.where(ge, mid, lo), jnp.where(ge, hi, mid),
                    jnp.where(ge, c, clo), jnp.where(ge, chi, c))

        def open_rows(clo, tied):
            return jnp.max(jnp.where(clo > kf, 1.0 - tied, 0.0)) > 0.0

        g0 = jnp.where(merged, tie_from // group, 0)

        def search_group(st):
            g, _, lo, hi, clo, chi, tied = st
            lo, hi, clo, chi = lax.fori_loop(0, group, bisect, (lo, hi, clo, chi))

            def tie_check(tied):
                lor, hir = rows(lo), rows(hi)
                a = reduce_keys(sct_c, lambda blk, c, t: jnp.where(blk >= lor, blk, jnp.inf),
                                jnp.minimum, jnp.min, jnp.inf)
                b = reduce_keys(sct_c, lambda blk, c, t: jnp.where(blk < hir, blk, -jnp.inf),
                                jnp.maximum, jnp.max, -jnp.inf)
                return jnp.where(jnp.logical_and(clo > kf, a == b), 1.0, tied)

            tied = lax.cond(jnp.logical_and((g + 1) * group >= tie_from, open_rows(clo, tied)),
                            tie_check, lambda t: t, tied)
            return (g + 1, open_rows(clo, tied).astype(I32), lo, hi, clo, chi, tied)

        zero = jnp.zeros((nsub, qb), F32)
        _, _, lo, hi, clo, chi, _ = lax.while_loop(
            lambda st: jnp.logical_and(st[1] > 0, st[0] < max_groups), search_group,
            (g0, open_rows(st_s[2], zero).astype(I32), st_s[0], st_s[1], st_s[2], st_s[3], zero))

        def tile_k(x):
            return jnp.concatenate([x] * (ks // nsub), axis=0)

        lok, hik, free = tile_k(lo), tile_k(hi), tile_k(kf - chi)
        tri = jnp.where(lax.broadcasted_iota(I32, (ks, ks), 1) <= lax.broadcasted_iota(I32, (ks, ks), 0),
                        1.0, 0.0).astype(BF16)

        def mask_chunk(ci, carry):
            subs = []
            for t in range(kc // ks):
                k0 = pl.multiple_of(ci * kc + t * ks, ks)
                blk = sct_c[pl.ds(k0, ks), :]
                top = blk >= hik
                tie = jnp.where(top, 0.0, jnp.where(blk >= lok, 1.0, 0.0))
                subs.append((k0, top, tie, jnp.dot(tri, tie.astype(BF16), preferred_element_type=F32)))
            for k0, top, tie, rank in subs:
                keep = jnp.where(rank + tile_k(carry) <= free, tie, 0.0)
                mask_w[pl.ds(k0, ks), :] = jnp.where(top, 1.0, keep).astype(BF16)
                carry = carry + jnp.broadcast_to(rank[ks - 1:ks, :], (nsub, qb))
            return carry

        zero8 = jnp.zeros((nsub, qb), F32)
        has_next = i + 1 < n_blocks

        @pl.when(has_next)
        def _():
            nkn = ((i + 1) * qb) // kc + 1
            nxt = functools.partial(score_chunk, qitn_ref, witn_ref, i + 1, sct_n)

            def both(ci, st):
                return (mask_chunk(ci, st[0]), nxt(ci, st[1]))

            _, mm = lax.fori_loop(0, nk, both, (zero8, mm0))
            save_range(lax.fori_loop(nk, nkn, nxt, mm))

        @pl.when(jnp.logical_not(has_next))
        def _():
            lax.fori_loop(0, nk, mask_chunk, zero8)


def _dsa_attn_call(bnd, qt, qit, kk, vt, wit, *, hd, idx_heads, topk, qb=256, kc=512, ks=128,
                   sl=32):
    b, npair, _, s = qt.shape
    nqi = qit.shape[1]
    assert kc % qb == 0 and s % kc == 0 and kc % ks == 0 and ks % sl == 0 and sl % SUBLANES == 0
    nb = s // qb
    head_passes = tuple(2 if h % 2 == 0 else 1 for h in range(2 * npair))
    assert 2 * hd == LANES
    prev = lambda i: jnp.maximum(i - 1, 0)
    cur = lambda i: jnp.minimum(i, nb - 1)
    nxt = lambda i: jnp.minimum(i + 1, nb - 1)
    return pl.pallas_call(
        functools.partial(_dsa_attn_kernel, kc=kc, ks=ks, topk=topk, hd=hd, idx_heads=idx_heads,
                          group=4, tie_from=16, max_groups=96, n_blocks=nb,
                          head_passes=head_passes, sl=sl),
        grid=(b, nb + 1),
        in_specs=[pl.BlockSpec(memory_space=pltpu.SMEM),
                  pl.BlockSpec((None, nqi, LANES, qb), lambda bi, i: (bi, 0, 0, cur(i))),
                  pl.BlockSpec((None, idx_heads, qb), lambda bi, i: (bi, 0, cur(i))),
                  pl.BlockSpec((None, nqi, LANES, qb), lambda bi, i: (bi, 0, 0, nxt(i))),
                  pl.BlockSpec((None, idx_heads, qb), lambda bi, i: (bi, 0, nxt(i))),
                  pl.BlockSpec((None, npair, LANES, qb), lambda bi, i: (bi, 0, 0, prev(i))),
                  pl.BlockSpec((None, 4, s, LANES), lambda bi, i: (bi, 0, 0, 0)),
                  pl.BlockSpec((None, LANES, s), lambda bi, i: (bi, 0, 0))],
        out_specs=pl.BlockSpec((None, npair, qb, LANES), lambda bi, i: (bi, 0, prev(i), 0)),
        out_shape=jax.ShapeDtypeStruct((b, npair, s, LANES), BF16),
        scratch_shapes=[pltpu.VMEM((2, s, qb), F32),
                        pltpu.VMEM((2, s, qb), BF16), pltpu.VMEM((4, SUBLANES, qb), F32),
                        pltpu.VMEM((2, SUBLANES, qb), F32),
                        pltpu.VMEM((2 * npair, SUBLANES, qb), F32),
                        pltpu.VMEM((2 * npair, LANES, qb), F32)],
        compiler_params=pltpu.CompilerParams(
            dimension_semantics=("arbitrary", "arbitrary"), vmem_limit_bytes=VMEM_LIMIT),
        name="dsa_attn",
    )(bnd, qit, wit, qit, wit, qt, kk, vt)


def kernel(x, c, norm_mix_g, norm_ffn_g, ada_w, ada_b, a_w_in, a_conv_w, a_conv_b, a_gate_r_w,
           a_gate_r_b, a_gate_i_w, a_gate_i_b, a_lambda, a_w_out, b_w_in, b_q_norm_g, b_k_norm_g,
           b_w_out, ffn_w1, ffn_w2):
    b, s, d = x.shape
    hd = b_q_norm_g.shape[-1]
    n_heads = b_w_out.shape[1] // hd
    idx_heads = (b_w_in.shape[-1] - n_heads * hd - 3 * hd) // (hd + 1)
    topk = min(TOPK_MAX, s // 4)
    depth = ada_w.shape[0]

    mod = _mod_call(c, ada_w, ada_b)
    mod_rows = mod.reshape(depth, b, 1, mod.shape[-1])

    for i in range(depth):
        j = i // 2
        if i % 2 == 0:
            x = _rglru_call(x, mod[i], norm_mix_g[i], a_w_in[j], a_conv_w[j], a_conv_b[j],
                            a_gate_r_w[j], a_gate_r_b[j], a_gate_i_w[j], a_gate_i_b[j],
                            a_lambda[j], a_w_out[j])
            x = _ffn_call(x, mod_rows[i], norm_ffn_g[i], ffn_w1[i], ffn_w2[i])
        else:
            qt, qit, kk, vt, wit = _dsa_in_call(x, mod_rows[i], norm_mix_g[i], b_w_in[j],
                                                 b_q_norm_g[j], b_k_norm_g[j], n_heads=n_heads,
                                                 hd=hd, idx_heads=idx_heads)
            bnd = (1.02 * hd * hd ** -0.5) * jnp.max(jnp.abs(b_q_norm_g[j])) * jnp.max(jnp.abs(b_k_norm_g[j]))
            o = _dsa_attn_call(bnd.reshape(1), qt, qit, kk, vt, wit, hd=hd, idx_heads=idx_heads,
                               topk=topk)
            x = _ffn_call(x, mod_rows[i], norm_ffn_g[i], ffn_w1[i], ffn_w2[i], o_pairs=o,
                          w_o=b_w_out[j])
    return x
```

```python
import functools

import jax
import jax.numpy as jnp
from jax import lax
from jax.experimental import pallas as pl
from jax.experimental.pallas import tpu as pltpu

F32 = jnp.float32
BF16 = jnp.bfloat16
I32 = jnp.int32

RMS_EPS = 1e-6
ROPE_THETA = 10000.0
LRU_C = 8.0
TOPK_MAX = 256
N_MOD = 6

LANES = 128
SUBLANES = 8
VMEM_LIMIT = 56 * 1024 * 1024

NEG_BIG = -1e30
EXP_SAFE_BOUND = 70.0
LOG2E = 1.4426950408889634
QK_LOOKAHEAD = 2
GELU_C0 = 0.7978845608028654
GELU_C1 = 0.044715


def _const_spec(shape):
    nd = len(shape)
    return pl.BlockSpec(shape, lambda *_: (0,) * nd, pipeline_mode=pl.Buffered(1))


def _rmsnorm(x, g):
    return x * lax.rsqrt(jnp.mean(x * x, axis=-1, keepdims=True) + RMS_EPS) * g


def _mod_kernel(c_ref, w_ref, b_ref, o_ref):
    c = c_ref[...]
    cond = c * jax.nn.sigmoid(c)
    o_ref[0] = jnp.dot(cond.astype(BF16), w_ref[0].astype(BF16),
                       preferred_element_type=F32) + b_ref[0]


def _mod_call(c, ada_w, ada_b):
    depth, d, n = ada_w.shape
    b = c.shape[0]
    tn = 1536
    return pl.pallas_call(
        _mod_kernel,
        grid=(depth, n // tn),
        in_specs=[pl.BlockSpec((b, d), lambda l, j: (0, 0)),
                  pl.BlockSpec((1, d, tn), lambda l, j: (l, 0, j)),
                  pl.BlockSpec((1, 1, tn), lambda l, j: (l, 0, j))],
        out_specs=pl.BlockSpec((1, b, tn), lambda l, j: (l, 0, j)),
        out_shape=jax.ShapeDtypeStruct((depth, b, n), F32),
        compiler_params=pltpu.CompilerParams(
            dimension_semantics=("arbitrary", "arbitrary"), vmem_limit_bytes=VMEM_LIMIT),
        name="adaln_mod",
    )(c, ada_w, ada_b.reshape(depth, 1, n))


def _rglru_kernel(x_ref, mod_ref, g_ref, win_ref, cw_ref, cb_ref, wr_ref, br_ref, wi_ref, bi_ref,
                  lam_ref, wout_ref, o_ref, h_s, gb_s, xbuf, a_s, b_s, hc, *, rc):
    nb, ts, d = x_ref.shape
    r = ts * nb
    dr = lam_ref.shape[1]
    nblk, blk, _ = wr_ref.shape
    cwid = cw_ref.shape[0]
    tail = (cwid - 1) * nb

    @pl.when(pl.program_id(0) == 0)
    def _():
        xbuf[0:tail, :] = jnp.zeros((tail, dr), F32)
        hc[...] = jnp.zeros((nb, dr), F32)

    x = jnp.swapaxes(x_ref[...], 0, 1)
    sh1 = mod_ref[:, 0:d]
    sc1 = mod_ref[:, d:2 * d]
    g1 = mod_ref[:, 2 * d:3 * d]
    h_s[...] = (_rmsnorm(x, g_ref[...]) * (1.0 + sc1)[None] + sh1[None]).reshape(r, d).astype(BF16)

    sp_h = (0.5 * LRU_C) * jax.nn.softplus(-lam_ref[...])
    br_h = 0.5 * br_ref[...]
    bi_h = 0.5 * bi_ref[...]

    def in_proj(c, n):
        rows = slice(c * rc, (c + 1) * rc)
        hb = h_s[rows, :]
        ux = jnp.dot(hb, win_ref[:, n * blk:(n + 1) * blk], preferred_element_type=F32)
        ug = jnp.dot(hb, win_ref[:, dr + n * blk:dr + (n + 1) * blk], preferred_element_type=F32)
        xbuf[tail + c * rc:tail + (c + 1) * rc, n * blk:(n + 1) * blk] = ux
        return ug

    def conv_and_gate_dots(c, n):
        r0 = c * rc
        cs = slice(n * blk, (n + 1) * blk)
        xc = cb_ref[:, cs] + cw_ref[0:1, cs] * xbuf[r0:r0 + rc, cs]
        for k in range(1, cwid):
            xc = xc + cw_ref[k:k + 1, cs] * xbuf[r0 + k * nb:r0 + k * nb + rc, cs]
        xn = xc.astype(BF16)
        return xc, (jnp.dot(xn, wr_ref[n], preferred_element_type=F32),
                    jnp.dot(xn, wi_ref[n], preferred_element_type=F32))

    def gate_tail(c, n, xc, pre, ug):
        ro = slice(c * rc, (c + 1) * rc)
        cs = slice(n * blk, (n + 1) * blk)
        p = sp_h[:, cs] + sp_h[:, cs] * jnp.tanh(0.5 * pre[0] + br_h[:, cs])
        ig = 0.5 + 0.5 * jnp.tanh(0.5 * pre[1] + bi_h[:, cs])
        a = jnp.exp2(p * (-LOG2E))
        a_s[ro, cs] = a
        b_s[ro, cs] = jnp.sqrt(jnp.tanh(p) * (a * a + 1.0)) * (ig * xc)
        t = jnp.tanh(ug * (GELU_C0 + (GELU_C0 * GELU_C1) * (ug * ug)))
        hu = 0.5 * ug
        gb_s[ro, cs] = hu + hu * t

    units = [(c, n) for c in range(r // rc) for n in range(nblk)]
    ug = in_proj(*units[0])
    for k, (c, n) in enumerate(units):
        xc, pre = conv_and_gate_dots(c, n)
        ug_next = in_proj(*units[k + 1]) if k + 1 < len(units) else None
        gate_tail(c, n, xc, pre, ug)
        ug = ug_next
    xbuf[0:tail, :] = xbuf[r:r + tail, :]

    def step(t, hprev):
        r0 = pl.multiple_of(t * nb, nb)
        hnew = a_s[pl.ds(r0, nb), :] * hprev + b_s[pl.ds(r0, nb), :]
        b_s[pl.ds(r0, nb), :] = hnew
        return hnew

    hc[...] = lax.fori_loop(0, ts, step, hc[...], unroll=8)

    y = b_s[...] * gb_s[...]
    out = jnp.dot(y.astype(BF16), wout_ref[...], preferred_element_type=F32)
    o_ref[...] = jnp.swapaxes(x + out.reshape(ts, nb, d) * g1[None], 0, 1)


def _rglru_call(x, mod0, g, w_in, conv_w, conv_b, wr, br, wi, bi, lam, w_out, *, ts=128, rc=256):
    nb, s, d = x.shape
    assert nb == SUBLANES, "the recurrence keeps the batch on the sublane axis"
    r = ts * nb
    dr = lam.shape[-1]
    tail = (conv_w.shape[0] - 1) * nb
    row = lambda v: v.reshape(1, -1)
    args = (x, mod0, row(g), w_in.astype(BF16), conv_w, row(conv_b), wr.astype(BF16), row(br),
            wi.astype(BF16), row(bi), row(lam), w_out.astype(BF16))
    x_spec = pl.BlockSpec((nb, ts, d), lambda i: (0, i, 0))
    return pl.pallas_call(
        functools.partial(_rglru_kernel, rc=rc),
        grid=(s // ts,),
        in_specs=[x_spec] + [_const_spec(a.shape) for a in args[1:]],
        out_specs=x_spec,
        out_shape=jax.ShapeDtypeStruct((nb, s, d), F32),
        scratch_shapes=[pltpu.VMEM((r, d), BF16), pltpu.VMEM((r, dr), F32),
                        pltpu.VMEM((r + tail, dr), F32), pltpu.VMEM((r, dr), F32),
                        pltpu.VMEM((r, dr), F32), pltpu.VMEM((nb, dr), F32)],
        compiler_params=pltpu.CompilerParams(
            dimension_semantics=("arbitrary",), vmem_limit_bytes=VMEM_LIMIT),
        name="rglru_mixer",
    )(*args)


def _ffn_kernel(*refs, has_proj, fc):
    if has_proj:
        x_ref, mod_ref, g_ref, w1_ref, w2_ref, o_in_ref, wo_ref, out_ref = refs
    else:
        x_ref, mod_ref, g_ref, w1_ref, w2_ref, out_ref = refs
    d = x_ref.shape[1]
    dff = w1_ref.shape[1]
    x = x_ref[...]
    if has_proj:
        g1 = mod_ref[:, 2 * d:3 * d]
        o = jnp.concatenate([o_in_ref[p] for p in range(o_in_ref.shape[0])], axis=-1)
        x = x + g1 * jnp.dot(o, wo_ref[...], preferred_element_type=F32)
    sh2 = mod_ref[:, 3 * d:4 * d]
    sc2 = mod_ref[:, 4 * d:5 * d]
    g2 = mod_ref[:, 5 * d:6 * d]
    hb = (_rmsnorm(x, g_ref[...]) * (1.0 + sc2) + sh2).astype(BF16)
    acc = jnp.zeros(x.shape, F32)
    for c in range(dff // fc):
        hid = jnp.dot(hb, w1_ref[:, c * fc:(c + 1) * fc], preferred_element_type=F32)
        hid = jnp.square(jnp.maximum(hid, 0.0))
        acc = acc + jnp.dot(hid.astype(BF16), w2_ref[c * fc:(c + 1) * fc, :],
                            preferred_element_type=F32)
    out_ref[...] = x + g2 * acc


def _ffn_call(x, mod_l, g, w1, w2, o_pairs=None, w_o=None, *, rf=512, fc=1024):
    b, s, d = x.shape
    has_proj = o_pairs is not None
    args = [x, mod_l, g.reshape(1, -1), w1.astype(BF16), w2.astype(BF16)]
    in_specs = [pl.BlockSpec((None, rf, d), lambda bi, i: (bi, i, 0)),
                pl.BlockSpec((None, 1, mod_l.shape[-1]), lambda bi, i: (bi, 0, 0)),
                _const_spec((1, d)), _const_spec(w1.shape), _const_spec(w2.shape)]
    if has_proj:
        npair = o_pairs.shape[1]
        args += [o_pairs, w_o.astype(BF16)]
        in_specs += [pl.BlockSpec((None, npair, rf, LANES), lambda bi, i: (bi, 0, i, 0)),
                     _const_spec(w_o.shape)]
    return pl.pallas_call(
        functools.partial(_ffn_kernel, has_proj=has_proj, fc=fc),
        grid=(b, s // rf),
        in_specs=in_specs,
        out_specs=pl.BlockSpec((None, rf, d), lambda bi, i: (bi, i, 0)),
        out_shape=jax.ShapeDtypeStruct((b, s, d), F32),
        compiler_params=pltpu.CompilerParams(
            dimension_semantics=("arbitrary", "arbitrary"), vmem_limit_bytes=VMEM_LIMIT),
        name="ffn_proj" if has_proj else "ffn",
    )(*args)


def _rot_half(x, lane):
    return jnp.where((lane & 63) < 32, pltpu.roll(x, 96, 1), pltpu.roll(x, 32, 1))


def _dsa_in_kernel(x_ref, mod_ref, g_ref, w_ref, qg_ref, kg_ref, e_ref, cos_ref, sin_ref,
                   qt_ref, qit_ref, kk_ref, vt_ref, wit_ref, *, nq, nqi, hd, idx_heads):
    d = x_ref.shape[1]
    tq = x_ref.shape[0]
    x = x_ref[...]
    sh1 = mod_ref[:, 0:d]
    sc1 = mod_ref[:, d:2 * d]
    hb = (_rmsnorm(x, g_ref[...]) * (1.0 + sc1) + sh1).astype(BF16)
    lane = lax.broadcasted_iota(I32, (tq, LANES), 1)
    left = lane < hd
    cos = cos_ref[...]
    sin = sin_ref[...]
    qg = qg_ref[...]
    e = e_ref[...]
    qscale = hd ** -0.5

    def q_tile(p, t):
        t2 = t * t
        hi = t2.astype(BF16)
        lo = (t2 - hi.astype(F32)).astype(BF16)
        ss = (jnp.dot(hi, e, preferred_element_type=F32) + jnp.dot(lo, e, preferred_element_type=F32))
        tn = t * lax.rsqrt(ss * (1.0 / hd) + RMS_EPS) * qg
        qt_ref[p] = ((tn * cos + _rot_half(tn, lane) * sin) * (qscale * LOG2E)).T.astype(BF16)

    def qi_tile(p, t):
        qit_ref[p] = ((t * cos + _rot_half(t, lane) * sin) * qscale).T.astype(BF16)

    def kk_tile(kk):
        ssk = jnp.sum(jnp.where(left, kk * kk, 0.0), axis=-1, keepdims=True)
        fac = jnp.where(left, lax.rsqrt(ssk * (1.0 / hd) + RMS_EPS) * kg_ref[...], 1.0)
        kn = kk * fac
        kr = kn * cos + _rot_half(kn, lane) * sin
        k_l = jnp.where(left, kr, 0.0)
        ki_r = jnp.where(left, 0.0, kr)
        kk_ref[0] = k_l.astype(BF16)
        kk_ref[1] = pltpu.roll(k_l, hd, 1).astype(BF16)
        kk_ref[2] = pltpu.roll(ki_r, hd, 1).astype(BF16)
        kk_ref[3] = ki_r.astype(BF16)

    def vw_tile(vw):
        vt_ref[...] = jnp.where(left, vw, 1.0).T.astype(BF16)
        wit_ref[...] = (vw * (idx_heads ** -0.5)).T[hd:hd + idx_heads, :]

    def tile(p, t):
        if p < nq:
            q_tile(p, t)
        elif p < nq + nqi:
            qi_tile(p - nq, t)
        elif p == nq + nqi:
            kk_tile(t)
        else:
            vw_tile(t)

    gw = 2 * LANES
    ngroups = w_ref.shape[1] // gw

    def proj(gi):
        return jnp.dot(hb, w_ref[:, gi * gw:(gi + 1) * gw], preferred_element_type=F32)

    u_next = proj(0)
    for gi in range(ngroups):
        u = u_next
        if gi + 1 < ngroups:
            u_next = proj(gi + 1)
        for half in range(gw // LANES):
            tile(gi * (gw // LANES) + half, u[:, half * LANES:(half + 1) * LANES])


def _dsa_in_call(x, mod_l, g, w_in, q_g, k_g, *, n_heads, hd, idx_heads, tq=512):
    b, s, d = x.shape
    assert 2 * hd == LANES, "two heads per lane tile"
    nq = n_heads * hd // LANES
    nqi = idx_heads * hd // LANES
    o_k = n_heads * hd
    o_v = o_k + hd
    o_qi = o_v + hd
    o_ki = o_qi + idx_heads * hd
    o_wi = o_ki + hd
    pad = LANES - hd - idx_heads
    w = jnp.concatenate([w_in[:, :o_k], w_in[:, o_qi:o_ki], w_in[:, o_k:o_v], w_in[:, o_ki:o_wi],
                         w_in[:, o_v:o_qi], w_in[:, o_wi:], jnp.zeros((d, pad), w_in.dtype)],
                        axis=1).astype(BF16)
    ncol = w.shape[1]
    inv = ROPE_THETA ** (-jnp.arange(0, hd, 2, dtype=F32) / hd)
    ang = jnp.arange(s, dtype=F32)[:, None] * inv[None, :]
    cos_t = jnp.tile(jnp.cos(ang), (1, 4))
    sin_h = jnp.sin(ang)
    sin_t = jnp.tile(jnp.concatenate([-sin_h, sin_h], axis=1), (1, 2))
    head_of_lane = jnp.arange(LANES) // hd
    e = (head_of_lane[:, None] == head_of_lane[None, :]).astype(BF16)
    qg = jnp.tile(q_g, 2).reshape(1, LANES)
    kg = jnp.concatenate([k_g, jnp.ones((hd,), k_g.dtype)]).reshape(1, LANES)

    return pl.pallas_call(
        functools.partial(_dsa_in_kernel, nq=nq, nqi=nqi, hd=hd, idx_heads=idx_heads),
        grid=(b, s // tq),
        in_specs=[pl.BlockSpec((None, tq, d), lambda bi, i: (bi, i, 0)),
                  pl.BlockSpec((None, 1, mod_l.shape[-1]), lambda bi, i: (bi, 0, 0)),
                  _const_spec((1, d)), _const_spec((d, ncol)), _const_spec((1, LANES)),
                  _const_spec((1, LANES)), _const_spec((LANES, LANES)),
                  pl.BlockSpec((tq, LANES), lambda bi, i: (i, 0)),
                  pl.BlockSpec((tq, LANES), lambda bi, i: (i, 0))],
        out_specs=[pl.BlockSpec((None, nq, LANES, tq), lambda bi, i: (bi, 0, 0, i)),
                   pl.BlockSpec((None, nqi, LANES, tq), lambda bi, i: (bi, 0, 0, i)),
                   pl.BlockSpec((None, 4, tq, LANES), lambda bi, i: (bi, 0, i, 0)),
                   pl.BlockSpec((None, LANES, tq), lambda bi, i: (bi, 0, i)),
                   pl.BlockSpec((None, idx_heads, tq), lambda bi, i: (bi, 0, i))],
        out_shape=[jax.ShapeDtypeStruct((b, nq, LANES, s), BF16),
                   jax.ShapeDtypeStruct((b, nqi, LANES, s), BF16),
                   jax.ShapeDtypeStruct((b, 4, s, LANES), BF16),
                   jax.ShapeDtypeStruct((b, LANES, s), BF16),
                   jax.ShapeDtypeStruct((b, idx_heads, s), F32)],
        compiler_params=pltpu.CompilerParams(
            dimension_semantics=("arbitrary", "arbitrary"), vmem_limit_bytes=VMEM_LIMIT),
        name="dsa_in",
    )(x, mod_l, g.reshape(1, -1), w, qg, kg, e, cos_t, sin_t)


def _dsa_attn_kernel(bnd_ref, qit0_ref, wit0_ref, qitn_ref, witn_ref, qt_ref, kk_ref, vt_ref, o_ref,
                     sct_s, mask_s, st_s, mm_s, m_s, acc_s, *, kc, ks, topk, hd, idx_heads, group,
                     tie_from, max_groups, n_blocks, head_passes, sl):
    npair, _, qb = qt_ref.shape
    nh = 2 * npair
    nsub = SUBLANES
    i = pl.program_id(1)
    has_sel = i < n_blocks
    has_att = i >= 1
    nk = (i * qb) // kc + 1
    nkp = ((i - 1) * qb) // kc + 1
    slot = lax.rem(i, 2)
    mask_w = mask_s.at[slot]
    mask_r = mask_s.at[1 - slot]
    sct_c = sct_s.at[slot]
    sct_n = sct_s.at[1 - slot]
    kf = float(topk)
    fast = bnd_ref[0] <= EXP_SAFE_BOUND
    merged = jnp.logical_and(jnp.logical_and(has_sel, has_att), fast)

    def slab(src, c, t):
        return src[pl.ds(pl.multiple_of(c * kc + t * sl, sl), sl), :]

    def finish(acc, op, red):
        n = sl // nsub
        acc = acc.reshape(n, nsub, qb)
        r = acc[0]
        for t in range(1, n):
            r = op(r, acc[t])
        return jnp.broadcast_to(red(r, axis=0, keepdims=True), (nsub, qb))

    def reduce_keys(src, fn, op, red, init):
        def body(ci, acc):
            for t in range(kc // sl):
                acc = op(acc, fn(slab(src, ci, t), ci, t))
            return acc
        return finish(lax.fori_loop(0, nk, body, jnp.full((sl, qb), init, F32)), op, red)

    def rows(x):
        return jnp.concatenate([x] * (sl // nsub), axis=0)

    def score_chunk(qit_ref, wit_ref, blk, dst, ci, mm):
        qpos = blk * qb + lax.broadcasted_iota(I32, (ks, qb), 1)
        krow = lax.broadcasted_iota(I32, (ks, qb), 0)
        mx, mn = mm
        for sub in range(kc // ks):
            k0 = pl.multiple_of(ci * kc + sub * ks, ks)
            kis = (kk_ref[2, pl.ds(k0, ks), :], kk_ref[3, pl.ds(k0, ks), :])
            acc = jnp.zeros((ks, qb), F32)
            for hh in range(idx_heads):
                lg = jnp.dot(kis[hh % 2], qit_ref[hh // 2], preferred_element_type=F32)
                acc = acc + wit_ref[hh:hh + 1, :] * jnp.maximum(lg, 0.0)
            causal = (k0 + krow) <= qpos
            lo_v = jnp.where(causal, acc, -jnp.inf)
            hi_v = jnp.where(causal, acc, jnp.inf)
            dst[pl.ds(k0, ks), :] = lo_v
            for t in range(ks // sl):
                mx = jnp.maximum(mx, lo_v[t * sl:(t + 1) * sl])
                mn = jnp.minimum(mn, hi_v[t * sl:(t + 1) * sl])
        return mx, mn

    mm0 = (jnp.full((sl, qb), -jnp.inf, F32), jnp.full((sl, qb), jnp.inf, F32))

    def save_range(mm):
        mm_s[0] = finish(mm[0], jnp.maximum, jnp.max)
        mm_s[1] = finish(mm[1], jnp.minimum, jnp.min)

    @pl.when(i == 0)
    def _():
        save_range(lax.fori_loop(
            0, nk, functools.partial(score_chunk, qit0_ref, wit0_ref, 0, sct_c), mm0))

    @pl.when(has_sel)
    def _():
        mx = mm_s[0]
        st_s[0] = mm_s[1]
        st_s[1] = mx + (jnp.abs(mx) * 1e-6 + 1e-30)
        st_s[2] = (i * qb + lax.broadcasted_iota(I32, (nsub, qb), 1) + 1).astype(F32)
        st_s[3] = jnp.zeros((nsub, qb), F32)

    def count_step(st):
        c, lo, hi, clo, chi, acc = st
        mid = 0.5 * lo + 0.5 * hi
        midr = rows(mid)
        for t in range(kc // sl):
            acc = acc + jnp.where(slab(sct_c, c, t) >= midr, 1.0, 0.0)
        last = c == nk - 1
        tot = finish(acc, jnp.add, jnp.sum)
        ge = tot >= kf
        lo = jnp.where(last, jnp.where(ge, mid, lo), lo)
        clo = jnp.where(last, jnp.where(ge, tot, clo), clo)
        hi = jnp.where(last, jnp.where(ge, hi, mid), hi)
        chi = jnp.where(last, jnp.where(ge, chi, tot), chi)
        return (jnp.where(last, 0, c + 1), lo, hi, clo, chi, jnp.where(last, 0.0, acc))

    def scores(ci, h):
        kk = kk_ref[h % 2, pl.ds(pl.multiple_of(ci * kc, kc), kc), :]
        return jnp.dot(kk, qt_ref[h // 2], preferred_element_type=F32)

    def pv_step(ci, st, use_max, passes):
        c0 = pl.multiple_of(ci * kc, kc)
        vt = vt_ref[:, pl.ds(c0, kc)]
        msk = mask_r[pl.ds(c0, kc), :]
        ahead = [scores(ci, h) for h in range(min(QK_LOOKAHEAD, nh))]
        for h in range(nh):
            s_cur = ahead.pop(0)
            if h + QK_LOOKAHEAD < nh:
                ahead.append(scores(ci, h + QK_LOOKAHEAD))
            if use_max:
                pr = jnp.where(msk > 0, jnp.exp2(s_cur - m_s[h][0:1, :]), 0.0).astype(BF16)
            else:
                pr = jnp.exp2(s_cur).astype(BF16) * msk
            acc_s[h] += jnp.dot(vt, pr, preferred_element_type=F32)
            for _ in range(passes[h] if passes else 0):
                st = count_step(st)
        return st

    @pl.when(has_att)
    def _():
        acc_s[...] = jnp.zeros(acc_s.shape, F32)

    @pl.when(merged)
    def _():
        st = (jnp.int32(0), st_s[0], st_s[1], st_s[2], st_s[3], jnp.zeros((sl, qb), F32))
        st = lax.fori_loop(0, nkp, functools.partial(pv_step, use_max=False, passes=head_passes), st)
        st_s[0], st_s[1], st_s[2], st_s[3] = st[1], st[2], st[3], st[4]

    @pl.when(jnp.logical_and(has_att, jnp.logical_and(fast, jnp.logical_not(has_sel))))
    def _():
        lax.fori_loop(0, nkp, functools.partial(pv_step, use_max=False, passes=None), 0)

    @pl.when(jnp.logical_and(has_att, jnp.logical_not(fast)))
    def _():
        def max_step(ci, _):
            @pl.when(ci == 0)
            def _():
                m_s[...] = jnp.full(m_s.shape, NEG_BIG, F32)
            msk = mask_r[pl.ds(pl.multiple_of(ci * kc, kc), kc), :]
            for h in range(nh):
                mx = jnp.max(jnp.where(msk > 0, scores(ci, h), NEG_BIG), axis=0, keepdims=True)
                m_s[h] = jnp.maximum(m_s[h], jnp.broadcast_to(mx, (nsub, qb)))
            return 0

        lax.fori_loop(0, nkp, max_step, 0)
        lax.fori_loop(0, nkp, functools.partial(pv_step, use_max=True, passes=None), 0)

    @pl.when(has_att)
    def _():
        for p in range(npair):
            a0, a1 = acc_s[2 * p], acc_s[2 * p + 1]
            ot = jnp.concatenate([a0[0:hd] / a0[hd:2 * hd], a1[0:hd] / a1[hd:2 * hd]], axis=0)
            o_ref[p] = ot.T.astype(BF16)

    @pl.when(has_sel)
    def _():
        def count_ge(thr):
            thr = rows(thr)
            return reduce_keys(sct_c, lambda blk, c, t: jnp.where(blk >= thr, 1.0, 0.0),
                               jnp.add, jnp.sum, 0.0)

        def bisect(_, st):
            lo, hi, clo, chi = st
            mid = 0.5 * lo + 0.5 * hi
            c = count_ge(mid)
            ge = c >= kf
            return (jnp.where(ge, mid, lo), jnp.where(ge, hi, mid),
                    jnp.where(ge, c, clo), jnp.where(ge, chi, c))

        def open_rows(clo, tied):
            return jnp.max(jnp.where(clo > kf, 1.0 - tied, 0.0)) > 0.0

        g0 = jnp.where(merged, tie_from // group, 0)

        def search_group(st):
            g, _, lo, hi, clo, chi, tied = st
            lo, hi, clo, chi = lax.fori_loop(0, group, bisect, (lo, hi, clo, chi))

            def tie_check(tied):
                lor, hir = rows(lo), rows(hi)
                a = reduce_keys(sct_c, lambda blk, c, t: jnp.where(blk >= lor, blk, jnp.inf),
                                jnp.minimum, jnp.min, jnp.inf)
                b = reduce_keys(sct_c, lambda blk, c, t: jnp.where(blk < hir, blk, -jnp.inf),
                                jnp.maximum, jnp.max, -jnp.inf)
                return jnp.where(jnp.logical_and(clo > kf, a == b), 1.0, tied)

            tied = lax.cond(jnp.logical_and((g + 1) * group >= tie_from, open_rows(clo, tied)),
                            tie_check, lambda t: t, tied)
            return (g + 1, open_rows(clo, tied).astype(I32), lo, hi, clo, chi, tied)

        zero = jnp.zeros((nsub, qb), F32)
        _, _, lo, hi, clo, chi, _ = lax.while_loop(
            lambda st: jnp.logical_and(st[1] > 0, st[0] < max_groups), search_group,
            (g0, open_rows(st_s[2], zero).astype(I32), st_s[0], st_s[1], st_s[2], st_s[3], zero))

        def tile_k(x):
            return jnp.concatenate([x] * (ks // nsub), axis=0)

        lok, hik, free = tile_k(lo), tile_k(hi), tile_k(kf - chi)
        tri = jnp.where(lax.broadcasted_iota(I32, (ks, ks), 1) <= lax.broadcasted_iota(I32, (ks, ks), 0),
                        1.0, 0.0).astype(BF16)

        def mask_chunk(ci, carry):
            subs = []
            for t in range(kc // ks):
                k0 = pl.multiple_of(ci * kc + t * ks, ks)
                blk = sct_c[pl.ds(k0, ks), :]
                top = blk >= hik
                tie = jnp.where(top, 0.0, jnp.where(blk >= lok, 1.0, 0.0))
                subs.append((k0, top, tie, jnp.dot(tri, tie.astype(BF16), preferred_element_type=F32)))
            for k0, top, tie, rank in subs:
                keep = jnp.where(rank + tile_k(carry) <= free, tie, 0.0)
                mask_w[pl.ds(k0, ks), :] = jnp.where(top, 1.0, keep).astype(BF16)
                carry = carry + jnp.broadcast_to(rank[ks - 1:ks, :], (nsub, qb))
            return carry

        zero8 = jnp.zeros((nsub, qb), F32)
        has_next = i + 1 < n_blocks

        @pl.when(has_next)
        def _():
            nkn = ((i + 1) * qb) // kc + 1
            nxt = functools.partial(score_chunk, qitn_ref, witn_ref, i + 1, sct_n)

            def both(ci, st):
                return (mask_chunk(ci, st[0]), nxt(ci, st[1]))

            _, mm = lax.fori_loop(0, nk, both, (zero8, mm0))
            save_range(lax.fori_loop(nk, nkn, nxt, mm))

        @pl.when(jnp.logical_not(has_next))
        def _():
            lax.fori_loop(0, nk, mask_chunk, zero8)


def _dsa_attn_call(bnd, qt, qit, kk, vt, wit, *, hd, idx_heads, topk, qb=256, kc=512, ks=128,
                   sl=32):
    b, npair, _, s = qt.shape
    nqi = qit.shape[1]
    assert kc % qb == 0 and s % kc == 0 and kc % ks == 0 and ks % sl == 0 and sl % SUBLANES == 0
    nb = s // qb
    head_passes = tuple(2 if h % 4 == 0 else 1 for h in range(2 * npair))
    assert 2 * hd == LANES
    prev = lambda i: jnp.maximum(i - 1, 0)
    cur = lambda i: jnp.minimum(i, nb - 1)
    nxt = lambda i: jnp.minimum(i + 1, nb - 1)
    return pl.pallas_call(
        functools.partial(_dsa_attn_kernel, kc=kc, ks=ks, topk=topk, hd=hd, idx_heads=idx_heads,
                          group=4, tie_from=16, max_groups=96, n_blocks=nb,
                          head_passes=head_passes, sl=sl),
        grid=(b, nb + 1),
        in_specs=[pl.BlockSpec(memory_space=pltpu.SMEM),
                  pl.BlockSpec((None, nqi, LANES, qb), lambda bi, i: (bi, 0, 0, cur(i))),
                  pl.BlockSpec((None, idx_heads, qb), lambda bi, i: (bi, 0, cur(i))),
                  pl.BlockSpec((None, nqi, LANES, qb), lambda bi, i: (bi, 0, 0, nxt(i))),
                  pl.BlockSpec((None, idx_heads, qb), lambda bi, i: (bi, 0, nxt(i))),
                  pl.BlockSpec((None, npair, LANES, qb), lambda bi, i: (bi, 0, 0, prev(i))),
                  pl.BlockSpec((None, 4, s, LANES), lambda bi, i: (bi, 0, 0, 0)),
                  pl.BlockSpec((None, LANES, s), lambda bi, i: (bi, 0, 0))],
        out_specs=pl.BlockSpec((None, npair, qb, LANES), lambda bi, i: (bi, 0, prev(i), 0)),
        out_shape=jax.ShapeDtypeStruct((b, npair, s, LANES), BF16),
        scratch_shapes=[pltpu.VMEM((2, s, qb), F32),
                        pltpu.VMEM((2, s, qb), BF16), pltpu.VMEM((4, SUBLANES, qb), F32),
                        pltpu.VMEM((2, SUBLANES, qb), F32),
                        pltpu.VMEM((2 * npair, SUBLANES, qb), F32),
                        pltpu.VMEM((2 * npair, LANES, qb), F32)],
        compiler_params=pltpu.CompilerParams(
            dimension_semantics=("arbitrary", "arbitrary"), vmem_limit_bytes=VMEM_LIMIT),
        name="dsa_attn",
    )(bnd, qit, wit, qit, wit, qt, kk, vt)


def kernel(x, c, norm_mix_g, norm_ffn_g, ada_w, ada_b, a_w_in, a_conv_w, a_conv_b, a_gate_r_w,
           a_gate_r_b, a_gate_i_w, a_gate_i_b, a_lambda, a_w_out, b_w_in, b_q_norm_g, b_k_norm_g,
           b_w_out, ffn_w1, ffn_w2):
    b, s, d = x.shape
    hd = b_q_norm_g.shape[-1]
    n_heads = b_w_out.shape[1] // hd
    idx_heads = (b_w_in.shape[-1] - n_heads * hd - 3 * hd) // (hd + 1)
    topk = min(TOPK_MAX, s // 4)
    depth = ada_w.shape[0]

    mod = _mod_call(c, ada_w, ada_b)
    mod_rows = mod.reshape(depth, b, 1, mod.shape[-1])

    for i in range(depth):
        j = i // 2
        if i % 2 == 0:
            x = _rglru_call(x, mod[i], norm_mix_g[i], a_w_in[j], a_conv_w[j], a_conv_b[j],
                            a_gate_r_w[j], a_gate_r_b[j], a_gate_i_w[j], a_gate_i_b[j],
                            a_lambda[j], a_w_out[j])
            x = _ffn_call(x, mod_rows[i], norm_ffn_g[i], ffn_w1[i], ffn_w2[i])
        else:
            qt, qit, kk, vt, wit = _dsa_in_call(x, mod_rows[i], norm_mix_g[i], b_w_in[j],
                                                 b_q_norm_g[j], b_k_norm_g[j], n_heads=n_heads,
                                                 hd=hd, idx_heads=idx_heads)
            bnd = (1.02 * hd * hd ** -0.5) * jnp.max(jnp.abs(b_q_norm_g[j])) * jnp.max(jnp.abs(b_k_norm_g[j]))
            o = _dsa_attn_call(bnd.reshape(1), qt, qit, kk, vt, wit, hd=hd, idx_heads=idx_heads,
                               topk=topk)
            x = _ffn_call(x, mod_rows[i], norm_ffn_g[i], ffn_w1[i], ffn_w2[i], o_pairs=o,
                          w_o=b_w_out[j])
    return x
```

```python
import functools

import jax
import jax.numpy as jnp
from jax import lax
from jax.experimental import pallas as pl
from jax.experimental.pallas import tpu as pltpu

F32 = jnp.float32
BF16 = jnp.bfloat16
I32 = jnp.int32

RMS_EPS = 1e-6
ROPE_THETA = 10000.0
LRU_C = 8.0
TOPK_MAX = 256
N_MOD = 6

LANES = 128
SUBLANES = 8
VMEM_LIMIT = 56 * 1024 * 1024

NEG_BIG = -1e30
EXP_SAFE_BOUND = 70.0
LOG2E = 1.4426950408889634
QK_LOOKAHEAD = 2
GELU_C0 = 0.7978845608028654
GELU_C1 = 0.044715


def _const_spec(shape):
    nd = len(shape)
    return pl.BlockSpec(shape, lambda *_: (0,) * nd, pipeline_mode=pl.Buffered(1))


def _rmsnorm(x, g):
    return x * lax.rsqrt(jnp.mean(x * x, axis=-1, keepdims=True) + RMS_EPS) * g


def _mod_kernel(c_ref, w_ref, b_ref, o_ref):
    c = c_ref[...]
    cond = c * jax.nn.sigmoid(c)
    o_ref[0] = jnp.dot(cond.astype(BF16), w_ref[0].astype(BF16),
                       preferred_element_type=F32) + b_ref[0]


def _mod_call(c, ada_w, ada_b):
    depth, d, n = ada_w.shape
    b = c.shape[0]
    tn = 1536
    return pl.pallas_call(
        _mod_kernel,
        grid=(depth, n // tn),
        in_specs=[pl.BlockSpec((b, d), lambda l, j: (0, 0)),
                  pl.BlockSpec((1, d, tn), lambda l, j: (l, 0, j)),
                  pl.BlockSpec((1, 1, tn), lambda l, j: (l, 0, j))],
        out_specs=pl.BlockSpec((1, b, tn), lambda l, j: (l, 0, j)),
        out_shape=jax.ShapeDtypeStruct((depth, b, n), F32),
        compiler_params=pltpu.CompilerParams(
            dimension_semantics=("arbitrary", "arbitrary"), vmem_limit_bytes=VMEM_LIMIT),
        name="adaln_mod",
    )(c, ada_w, ada_b.reshape(depth, 1, n))


def _rglru_kernel(x_ref, mod_ref, g_ref, win_ref, cw_ref, cb_ref, wr_ref, br_ref, wi_ref, bi_ref,
                  lam_ref, wout_ref, o_ref, h_s, gb_s, xbuf, a_s, b_s, hc, *, rc):
    nb, ts, d = x_ref.shape
    r = ts * nb
    dr = lam_ref.shape[1]
    nblk, blk, _ = wr_ref.shape
    cwid = cw_ref.shape[0]
    tail = (cwid - 1) * nb

    @pl.when(pl.program_id(0) == 0)
    def _():
        xbuf[0:tail, :] = jnp.zeros((tail, dr), F32)
        hc[...] = jnp.zeros((nb, dr), F32)

    x = jnp.swapaxes(x_ref[...], 0, 1)
    sh1 = mod_ref[:, 0:d]
    sc1 = mod_ref[:, d:2 * d]
    g1 = mod_ref[:, 2 * d:3 * d]
    h_s[...] = (_rmsnorm(x, g_ref[...]) * (1.0 + sc1)[None] + sh1[None]).reshape(r, d).astype(BF16)

    sp_h = (0.5 * LRU_C) * jax.nn.softplus(-lam_ref[...])
    br_h = 0.5 * br_ref[...]
    bi_h = 0.5 * bi_ref[...]

    def in_proj(c, n):
        rows = slice(c * rc, (c + 1) * rc)
        hb = h_s[rows, :]
        ux = jnp.dot(hb, win_ref[:, n * blk:(n + 1) * blk], preferred_element_type=F32)
        ug = jnp.dot(hb, win_ref[:, dr + n * blk:dr + (n + 1) * blk], preferred_element_type=F32)
        xbuf[tail + c * rc:tail + (c + 1) * rc, n * blk:(n + 1) * blk] = ux
        return ug

    def conv_and_gate_dots(c, n):
        r0 = c * rc
        cs = slice(n * blk, (n + 1) * blk)
        xc = cb_ref[:, cs] + cw_ref[0:1, cs] * xbuf[r0:r0 + rc, cs]
        for k in range(1, cwid):
            xc = xc + cw_ref[k:k + 1, cs] * xbuf[r0 + k * nb:r0 + k * nb + rc, cs]
        xn = xc.astype(BF16)
        return xc, (jnp.dot(xn, wr_ref[n], preferred_element_type=F32),
                    jnp.dot(xn, wi_ref[n], preferred_element_type=F32))

    def gate_tail(c, n, xc, pre, ug):
        ro = slice(c * rc, (c + 1) * rc)
        cs = slice(n * blk, (n + 1) * blk)
        p = sp_h[:, cs] + sp_h[:, cs] * jnp.tanh(0.5 * pre[0] + br_h[:, cs])
        ig = 0.5 + 0.5 * jnp.tanh(0.5 * pre[1] + bi_h[:, cs])
        a = jnp.exp2(p * (-LOG2E))
        a_s[ro, cs] = a
        b_s[ro, cs] = jnp.sqrt(jnp.tanh(p) * (a * a + 1.0)) * (ig * xc)
        t = jnp.tanh(ug * (GELU_C0 + (GELU_C0 * GELU_C1) * (ug * ug)))
        hu = 0.5 * ug
        gb_s[ro, cs] = hu + hu * t

    units = [(c, n) for c in range(r // rc) for n in range(nblk)]
    ug = in_proj(*units[0])
    for k, (c, n) in enumerate(units):
        xc, pre = conv_and_gate_dots(c, n)
        ug_next = in_proj(*units[k + 1]) if k + 1 < len(units) else None
        gate_tail(c, n, xc, pre, ug)
        ug = ug_next
    xbuf[0:tail, :] = xbuf[r:r + tail, :]

    def step(t, hprev):
        r0 = pl.multiple_of(t * nb, nb)
        hnew = a_s[pl.ds(r0, nb), :] * hprev + b_s[pl.ds(r0, nb), :]
        b_s[pl.ds(r0, nb), :] = hnew
        return hnew

    hc[...] = lax.fori_loop(0, ts, step, hc[...], unroll=8)

    y = b_s[...] * gb_s[...]
    out = jnp.dot(y.astype(BF16), wout_ref[...], preferred_element_type=F32)
    o_ref[...] = jnp.swapaxes(x + out.reshape(ts, nb, d) * g1[None], 0, 1)


def _rglru_call(x, mod0, g, w_in, conv_w, conv_b, wr, br, wi, bi, lam, w_out, *, ts=128, rc=256):
    nb, s, d = x.shape
    assert nb == SUBLANES, "the recurrence keeps the batch on the sublane axis"
    r = ts * nb
    dr = lam.shape[-1]
    tail = (conv_w.shape[0] - 1) * nb
    row = lambda v: v.reshape(1, -1)
    args = (x, mod0, row(g), w_in.astype(BF16), conv_w, row(conv_b), wr.astype(BF16), row(br),
            wi.astype(BF16), row(bi), row(lam), w_out.astype(BF16))
    x_spec = pl.BlockSpec((nb, ts, d), lambda i: (0, i, 0))
    return pl.pallas_call(
        functools.partial(_rglru_kernel, rc=rc),
        grid=(s // ts,),
        in_specs=[x_spec] + [_const_spec(a.shape) for a in args[1:]],
        out_specs=x_spec,
        out_shape=jax.ShapeDtypeStruct((nb, s, d), F32),
        scratch_shapes=[pltpu.VMEM((r, d), BF16), pltpu.VMEM((r, dr), F32),
                        pltpu.VMEM((r + tail, dr), F32), pltpu.VMEM((r, dr), F32),
                        pltpu.VMEM((r, dr), F32), pltpu.VMEM((nb, dr), F32)],
        compiler_params=pltpu.CompilerParams(
            dimension_semantics=("arbitrary",), vmem_limit_bytes=VMEM_LIMIT),
        name="rglru_mixer",
    )(*args)


def _ffn_kernel(*refs, has_proj, fc):
    if has_proj:
        x_ref, mod_ref, g_ref, w1_ref, w2_ref, o_in_ref, wo_ref, out_ref = refs
    else:
        x_ref, mod_ref, g_ref, w1_ref, w2_ref, out_ref = refs
    d = x_ref.shape[1]
    dff = w1_ref.shape[1]
    x = x_ref[...]
    if has_proj:
        g1 = mod_ref[:, 2 * d:3 * d]
        o = jnp.concatenate([o_in_ref[p] for p in range(o_in_ref.shape[0])], axis=-1)
        x = x + g1 * jnp.dot(o, wo_ref[...], preferred_element_type=F32)
    sh2 = mod_ref[:, 3 * d:4 * d]
    sc2 = mod_ref[:, 4 * d:5 * d]
    g2 = mod_ref[:, 5 * d:6 * d]
    hb = (_rmsnorm(x, g_ref[...]) * (1.0 + sc2) + sh2).astype(BF16)
    acc = jnp.zeros(x.shape, F32)
    for c in range(dff // fc):
        hid = jnp.dot(hb, w1_ref[:, c * fc:(c + 1) * fc], preferred_element_type=F32)
        hid = jnp.square(jnp.maximum(hid, 0.0))
        acc = acc + jnp.dot(hid.astype(BF16), w2_ref[c * fc:(c + 1) * fc, :],
                            preferred_element_type=F32)
    out_ref[...] = x + g2 * acc


def _ffn_call(x, mod_l, g, w1, w2, o_pairs=None, w_o=None, *, rf=512, fc=1024):
    b, s, d = x.shape
    has_proj = o_pairs is not None
    args = [x, mod_l, g.reshape(1, -1), w1.astype(BF16), w2.astype(BF16)]
    in_specs = [pl.BlockSpec((None, rf, d), lambda bi, i: (bi, i, 0)),
                pl.BlockSpec((None, 1, mod_l.shape[-1]), lambda bi, i: (bi, 0, 0)),
                _const_spec((1, d)), _const_spec(w1.shape), _const_spec(w2.shape)]
    if has_proj:
        npair = o_pairs.shape[1]
        args += [o_pairs, w_o.astype(BF16)]
        in_specs += [pl.BlockSpec((None, npair, rf, LANES), lambda bi, i: (bi, 0, i, 0)),
                     _const_spec(w_o.shape)]
    return pl.pallas_call(
        functools.partial(_ffn_kernel, has_proj=has_proj, fc=fc),
        grid=(b, s // rf),
        in_specs=in_specs,
        out_specs=pl.BlockSpec((None, rf, d), lambda bi, i: (bi, i, 0)),
        out_shape=jax.ShapeDtypeStruct((b, s, d), F32),
        compiler_params=pltpu.CompilerParams(
            dimension_semantics=("arbitrary", "arbitrary"), vmem_limit_bytes=VMEM_LIMIT),
        name="ffn_proj" if has_proj else "ffn",
    )(*args)


def _rot_half(x, lane):
    return jnp.where((lane & 63) < 32, pltpu.roll(x, 96, 1), pltpu.roll(x, 32, 1))


def _dsa_in_kernel(x_ref, mod_ref, g_ref, w_ref, qg_ref, kg_ref, e_ref, cos_ref, sin_ref,
                   qt_ref, qit_ref, kk_ref, vt_ref, wit_ref, *, nq, nqi, hd, idx_heads):
    d = x_ref.shape[1]
    tq = x_ref.shape[0]
    x = x_ref[...]
    sh1 = mod_ref[:, 0:d]
    sc1 = mod_ref[:, d:2 * d]
    hb = (_rmsnorm(x, g_ref[...]) * (1.0 + sc1) + sh1).astype(BF16)
    lane = lax.broadcasted_iota(I32, (tq, LANES), 1)
    left = lane < hd
    cos = cos_ref[...]
    sin = sin_ref[...]
    qg = qg_ref[...]
    e = e_ref[...]
    qscale = hd ** -0.5

    def q_tile(p, t):
        t2 = t * t
        hi = t2.astype(BF16)
        lo = (t2 - hi.astype(F32)).astype(BF16)
        ss = (jnp.dot(hi, e, preferred_element_type=F32) + jnp.dot(lo, e, preferred_element_type=F32))
        tn = t * lax.rsqrt(ss * (1.0 / hd) + RMS_EPS) * qg
        qt_ref[p] = ((tn * cos + _rot_half(tn, lane) * sin) * (qscale * LOG2E)).T.astype(BF16)

    def qi_tile(p, t):
        qit_ref[p] = ((t * cos + _rot_half(t, lane) * sin) * qscale).T.astype(BF16)

    def kk_tile(kk):
        ssk = jnp.sum(jnp.where(left, kk * kk, 0.0), axis=-1, keepdims=True)
        fac = jnp.where(left, lax.rsqrt(ssk * (1.0 / hd) + RMS_EPS) * kg_ref[...], 1.0)
        kn = kk * fac
        kr = kn * cos + _rot_half(kn, lane) * sin
        k_l = jnp.where(left, kr, 0.0)
        ki_r = jnp.where(left, 0.0, kr)
        kk_ref[0] = k_l.astype(BF16)
        kk_ref[1] = pltpu.roll(k_l, hd, 1).astype(BF16)
        kk_ref[2] = pltpu.roll(ki_r, hd, 1).astype(BF16)
        kk_ref[3] = ki_r.astype(BF16)

    def vw_tile(vw):
        vt_ref[...] = jnp.where(left, vw, 1.0).T.astype(BF16)
        wit_ref[...] = (vw * (idx_heads ** -0.5)).T[hd:hd + idx_heads, :]

    def tile(p, t):
        if p < nq:
            q_tile(p, t)
        elif p < nq + nqi:
            qi_tile(p - nq, t)
        elif p == nq + nqi:
            kk_tile(t)
        else:
            vw_tile(t)

    gw = 2 * LANES
    ngroups = w_ref.shape[1] // gw

    def proj(gi):
        return jnp.dot(hb, w_ref[:, gi * gw:(gi + 1) * gw], preferred_element_type=F32)

    u_next = proj(0)
    for gi in range(ngroups):
        u = u_next
        if gi + 1 < ngroups:
            u_next = proj(gi + 1)
        for half in range(gw // LANES):
            tile(gi * (gw // LANES) + half, u[:, half * LANES:(half + 1) * LANES])


def _dsa_in_call(x, mod_l, g, w_in, q_g, k_g, *, n_heads, hd, idx_heads, tq=512):
    b, s, d = x.shape
    assert 2 * hd == LANES, "two heads per lane tile"
    nq = n_heads * hd // LANES
    nqi = idx_heads * hd // LANES
    o_k = n_heads * hd
    o_v = o_k + hd
    o_qi = o_v + hd
    o_ki = o_qi + idx_heads * hd
    o_wi = o_ki + hd
    pad = LANES - hd - idx_heads
    w = jnp.concatenate([w_in[:, :o_k], w_in[:, o_qi:o_ki], w_in[:, o_k:o_v], w_in[:, o_ki:o_wi],
                         w_in[:, o_v:o_qi], w_in[:, o_wi:], jnp.zeros((d, pad), w_in.dtype)],
                        axis=1).astype(BF16)
    ncol = w.shape[1]
    inv = ROPE_THETA ** (-jnp.arange(0, hd, 2, dtype=F32) / hd)
    ang = jnp.arange(s, dtype=F32)[:, None] * inv[None, :]
    cos_t = jnp.tile(jnp.cos(ang), (1, 4))
    sin_h = jnp.sin(ang)
    sin_t = jnp.tile(jnp.concatenate([-sin_h, sin_h], axis=1), (1, 2))
    head_of_lane = jnp.arange(LANES) // hd
    e = (head_of_lane[:, None] == head_of_lane[None, :]).astype(BF16)
    qg = jnp.tile(q_g, 2).reshape(1, LANES)
    kg = jnp.concatenate([k_g, jnp.ones((hd,), k_g.dtype)]).reshape(1, LANES)

    return pl.pallas_call(
        functools.partial(_dsa_in_kernel, nq=nq, nqi=nqi, hd=hd, idx_heads=idx_heads),
        grid=(b, s // tq),
        in_specs=[pl.BlockSpec((None, tq, d), lambda bi, i: (bi, i, 0)),
                  pl.BlockSpec((None, 1, mod_l.shape[-1]), lambda bi, i: (bi, 0, 0)),
                  _const_spec((1, d)), _const_spec((d, ncol)), _const_spec((1, LANES)),
                  _const_spec((1, LANES)), _const_spec((LANES, LANES)),
                  pl.BlockSpec((tq, LANES), lambda bi, i: (i, 0)),
                  pl.BlockSpec((tq, LANES), lambda bi, i: (i, 0))],
        out_specs=[pl.BlockSpec((None, nq, LANES, tq), lambda bi, i: (bi, 0, 0, i)),
                   pl.BlockSpec((None, nqi, LANES, tq), lambda bi, i: (bi, 0, 0, i)),
                   pl.BlockSpec((None, 4, tq, LANES), lambda bi, i: (bi, 0, i, 0)),
                   pl.BlockSpec((None, LANES, tq), lambda bi, i: (bi, 0, i)),
                   pl.BlockSpec((None, idx_heads, tq), lambda bi, i: (bi, 0, i))],
        out_shape=[jax.ShapeDtypeStruct((b, nq, LANES, s), BF16),
                   jax.ShapeDtypeStruct((b, nqi, LANES, s), BF16),
                   jax.ShapeDtypeStruct((b, 4, s, LANES), BF16),
                   jax.ShapeDtypeStruct((b, LANES, s), BF16),
                   jax.ShapeDtypeStruct((b, idx_heads, s), F32)],
        compiler_params=pltpu.CompilerParams(
            dimension_semantics=("arbitrary", "arbitrary"), vmem_limit_bytes=VMEM_LIMIT),
        name="dsa_in",
    )(x, mod_l, g.reshape(1, -1), w, qg, kg, e, cos_t, sin_t)


def _dsa_attn_kernel(bnd_ref, qit0_ref, wit0_ref, qitn_ref, witn_ref, qt_ref, kk_ref, vt_ref, o_ref,
                     sct_s, mask_s, st_s, mm_s, m_s, acc_s, *, kc, ks, topk, hd, idx_heads, group,
                     tie_from, max_groups, n_blocks, head_passes, sl):
    npair, _, qb = qt_ref.shape
    nh = 2 * npair
    nsub = SUBLANES
    i = pl.program_id(1)
    has_sel = i < n_blocks
    has_att = i >= 1
    nk = (i * qb) // kc + 1
    nkp = ((i - 1) * qb) // kc + 1
    slot = lax.rem(i, 2)
    mask_w = mask_s.at[slot]
    mask_r = mask_s.at[1 - slot]
    sct_c = sct_s.at[slot]
    sct_n = sct_s.at[1 - slot]
    kf = float(topk)
    fast = bnd_ref[0] <= EXP_SAFE_BOUND
    merged = jnp.logical_and(jnp.logical_and(has_sel, has_att), fast)

    def slab(src, c, t):
        return src[pl.ds(pl.multiple_of(c * kc + t * sl, sl), sl), :]

    def finish(acc, op, red):
        n = sl // nsub
        acc = acc.reshape(n, nsub, qb)
        r = acc[0]
        for t in range(1, n):
            r = op(r, acc[t])
        return jnp.broadcast_to(red(r, axis=0, keepdims=True), (nsub, qb))

    def reduce_keys(src, fn, op, red, init):
        def body(ci, acc):
            for t in range(kc // sl):
                acc = op(acc, fn(slab(src, ci, t), ci, t))
            return acc
        return finish(lax.fori_loop(0, nk, body, jnp.full((sl, qb), init, F32)), op, red)

    def rows(x):
        return jnp.concatenate([x] * (sl // nsub), axis=0)

    def score_chunk(qit_ref, wit_ref, blk, dst, ci, mm):
        qpos = blk * qb + lax.broadcasted_iota(I32, (ks, qb), 1)
        krow = lax.broadcasted_iota(I32, (ks, qb), 0)
        mx, mn = mm
        for sub in range(kc // ks):
            k0 = pl.multiple_of(ci * kc + sub * ks, ks)
            kis = (kk_ref[2, pl.ds(k0, ks), :], kk_ref[3, pl.ds(k0, ks), :])
            acc = jnp.zeros((ks, qb), F32)
            for hh in range(idx_heads):
                lg = jnp.dot(kis[hh % 2], qit_ref[hh // 2], preferred_element_type=F32)
                acc = acc + wit_ref[hh:hh + 1, :] * jnp.maximum(lg, 0.0)
            causal = (k0 + krow) <= qpos
            lo_v = jnp.where(causal, acc, -jnp.inf)
            hi_v = jnp.where(causal, acc, jnp.inf)
            dst[pl.ds(k0, ks), :] = lo_v
            for t in range(ks // sl):
                mx = jnp.maximum(mx, lo_v[t * sl:(t + 1) * sl])
                mn = jnp.minimum(mn, hi_v[t * sl:(t + 1) * sl])
        return mx, mn

    mm0 = (jnp.full((sl, qb), -jnp.inf, F32), jnp.full((sl, qb), jnp.inf, F32))

    def save_range(mm):
        mm_s[0] = finish(mm[0], jnp.maximum, jnp.max)
        mm_s[1] = finish(mm[1], jnp.minimum, jnp.min)

    @pl.when(i == 0)
    def _():
        save_range(lax.fori_loop(
            0, nk, functools.partial(score_chunk, qit0_ref, wit0_ref, 0, sct_c), mm0))

    @pl.when(has_sel)
    def _():
        mx = mm_s[0]
        st_s[0] = mm_s[1]
        st_s[1] = mx + (jnp.abs(mx) * 1e-6 + 1e-30)
        st_s[2] = (i * qb + lax.broadcasted_iota(I32, (nsub, qb), 1) + 1).astype(F32)
        st_s[3] = jnp.zeros((nsub, qb), F32)

    def count_step(st):
        c, lo, hi, clo, chi, acc = st
        mid = 0.5 * lo + 0.5 * hi
        midr = rows(mid)
        for t in range(kc // sl):
            acc = acc + jnp.where(slab(sct_c, c, t) >= midr, 1.0, 0.0)
        last = c == nk - 1
        tot = finish(acc, jnp.add, jnp.sum)
        ge = tot >= kf
        lo = jnp.where(last, jnp.where(ge, mid, lo), lo)
        clo = jnp.where(last, jnp.where(ge, tot, clo), clo)
        hi = jnp.where(last, jnp.where(ge, hi, mid), hi)
        chi = jnp.where(last, jnp.where(ge, chi, tot), chi)
        return (jnp.where(last, 0, c + 1), lo, hi, clo, chi, jnp.where(last, 0.0, acc))

    def scores(ci, h):
        kk = kk_ref[h % 2, pl.ds(pl.multiple_of(ci * kc, kc), kc), :]
        return jnp.dot(kk, qt_ref[h // 2], preferred_element_type=F32)

    def pv_step(ci, st, use_max, passes):
        c0 = pl.multiple_of(ci * kc, kc)
        vt = vt_ref[:, pl.ds(c0, kc)]
        msk = mask_r[pl.ds(c0, kc), :]
        ahead = [scores(ci, h) for h in range(min(QK_LOOKAHEAD, nh))]
        for h in range(nh):
            s_cur = ahead.pop(0)
            if h + QK_LOOKAHEAD < nh:
                ahead.append(scores(ci, h + QK_LOOKAHEAD))
            if use_max:
                pr = jnp.where(msk > 0, jnp.exp2(s_cur - m_s[h][0:1, :]), 0.0).astype(BF16)
            else:
                pr = jnp.exp2(s_cur).astype(BF16) * msk
            acc_s[h] += jnp.dot(vt, pr, preferred_element_type=F32)
            for _ in range(passes[h] if passes else 0):
                st = count_step(st)
        return st

    @pl.when(has_att)
    def _():
        acc_s[...] = jnp.zeros(acc_s.shape, F32)

    @pl.when(merged)
    def _():
        st = (jnp.int32(0), st_s[0], st_s[1], st_s[2], st_s[3], jnp.zeros((sl, qb), F32))
        st = lax.fori_loop(0, nkp, functools.partial(pv_step, use_max=False, passes=head_passes), st)
        st_s[0], st_s[1], st_s[2], st_s[3] = st[1], st[2], st[3], st[4]

    @pl.when(jnp.logical_and(has_att, jnp.logical_and(fast, jnp.logical_not(has_sel))))
    def _():
        lax.fori_loop(0, nkp, functools.partial(pv_step, use_max=False, passes=None), 0)

    @pl.when(jnp.logical_and(has_att, jnp.logical_not(fast)))
    def _():
        def max_step(ci, _):
            @pl.when(ci == 0)
            def _():
                m_s[...] = jnp.full(m_s.shape, NEG_BIG, F32)
            msk = mask_r[pl.ds(pl.multiple_of(ci * kc, kc), kc), :]
            for h in range(nh):
                mx = jnp.max(jnp.where(msk > 0, scores(ci, h), NEG_BIG), axis=0, keepdims=True)
                m_s[h] = jnp.maximum(m_s[h], jnp.broadcast_to(mx, (nsub, qb)))
            return 0

        lax.fori_loop(0, nkp, max_step, 0)
        lax.fori_loop(0, nkp, functools.partial(pv_step, use_max=True, passes=None), 0)

    @pl.when(has_att)
    def _():
        for p in range(npair):
            a0, a1 = acc_s[2 * p], acc_s[2 * p + 1]
            ot = jnp.concatenate([a0[0:hd] / a0[hd:2 * hd], a1[0:hd] / a1[hd:2 * hd]], axis=0)
            o_ref[p] = ot.T.astype(BF16)

    @pl.when(has_sel)
    def _():
        def count_ge(thr):
            thr = rows(thr)
            return reduce_keys(sct_c, lambda blk, c, t: jnp.where(blk >= thr, 1.0, 0.0),
                               jnp.add, jnp.sum, 0.0)

        def bisect(_, st):
            lo, hi, clo, chi = st
            mid = 0.5 * lo + 0.5 * hi
            c = count_ge(mid)
            ge = c >= kf
            return (jnp.where(ge, mid, lo), jnp.where(ge, hi, mid),
                    jnp.where(ge, c, clo), jnp.where(ge, chi, c))

        def open_rows(clo, tied):
            return jnp.max(jnp.where(clo > kf, 1.0 - tied, 0.0)) > 0.0

        g0 = jnp.where(merged, tie_from // group, 0)

        def search_group(st):
            g, _, lo, hi, clo, chi, tied = st
            lo, hi, clo, chi = lax.fori_loop(0, group, bisect, (lo, hi, clo, chi))

            def tie_check(tied):
                lor, hir = rows(lo), rows(hi)
                a = reduce_keys(sct_c, lambda blk, c, t: jnp.where(blk >= lor, blk, jnp.inf),
                                jnp.minimum, jnp.min, jnp.inf)
                b = reduce_keys(sct_c, lambda blk, c, t: jnp.where(blk < hir, blk, -jnp.inf),
                                jnp.maximum, jnp.max, -jnp.inf)
                return jnp.where(jnp.logical_and(clo > kf, a == b), 1.0, tied)

            tied = lax.cond(jnp.logical_and((g + 1) * group >= tie_from, open_rows(clo, tied)),
                            tie_check, lambda t: t, tied)
            return (g + 1, open_rows(clo, tied).astype(I32), lo, hi, clo, chi, tied)

        zero = jnp.zeros((nsub, qb), F32)
        _, _, lo, hi, clo, chi, _ = lax.while_loop(
            lambda st: jnp.logical_and(st[1] > 0, st[0] < max_groups), search_group,
            (g0, open_rows(st_s[2], zero).astype(I32), st_s[0], st_s[1], st_s[2], st_s[3], zero))

        def tile_k(x):
            return jnp.concatenate([x] * (ks // nsub), axis=0)

        lok, hik, free = tile_k(lo), tile_k(hi), tile_k(kf - chi)
        tri = jnp.where(lax.broadcasted_iota(I32, (ks, ks), 1) <= lax.broadcasted_iota(I32, (ks, ks), 0),
                        1.0, 0.0).astype(BF16)

        def mask_chunk(ci, carry):
            subs = []
            for t in range(kc // ks):
                k0 = pl.multiple_of(ci * kc + t * ks, ks)
                blk = sct_c[pl.ds(k0, ks), :]
                top = blk >= hik
                tie = jnp.where(top, 0.0, jnp.where(blk >= lok, 1.0, 0.0))
                subs.append((k0, top, tie, jnp.dot(tri, tie.astype(BF16), preferred_element_type=F32)))
            for k0, top, tie, rank in subs:
                keep = jnp.where(rank + tile_k(carry) <= free, tie, 0.0)
                mask_w[pl.ds(k0, ks), :] = jnp.where(top, 1.0, keep).astype(BF16)
                carry = carry + jnp.broadcast_to(rank[ks - 1:ks, :], (nsub, qb))
            return carry

        zero8 = jnp.zeros((nsub, qb), F32)
        has_next = i + 1 < n_blocks

        @pl.when(has_next)
        def _():
            nkn = ((i + 1) * qb) // kc + 1
            nxt = functools.partial(score_chunk, qitn_ref, witn_ref, i + 1, sct_n)

            def both(ci, st):
                return (mask_chunk(ci, st[0]), nxt(ci, st[1]))

            _, mm = lax.fori_loop(0, nk, both, (zero8, mm0))
            save_range(lax.fori_loop(nk, nkn, nxt, mm))

        @pl.when(jnp.logical_not(has_next))
        def _():
            lax.fori_loop(0, nk, mask_chunk, zero8)


def _dsa_attn_call(bnd, qt, qit, kk, vt, wit, *, hd, idx_heads, topk, qb=256, kc=512, ks=128,
                   sl=32):
    b, npair, _, s = qt.shape
    nqi = qit.shape[1]
    assert kc % qb == 0 and s % kc == 0 and kc % ks == 0 and ks % sl == 0 and sl % SUBLANES == 0
    nb = s // qb
    head_passes = (1,) * (2 * npair)
    assert 2 * hd == LANES
    prev = lambda i: jnp.maximum(i - 1, 0)
    cur = lambda i: jnp.minimum(i, nb - 1)
    nxt = lambda i: jnp.minimum(i + 1, nb - 1)
    return pl.pallas_call(
        functools.partial(_dsa_attn_kernel, kc=kc, ks=ks, topk=topk, hd=hd, idx_heads=idx_heads,
                          group=4, tie_from=16, max_groups=96, n_blocks=nb,
                          head_passes=head_passes, sl=sl),
        grid=(b, nb + 1),
        in_specs=[pl.BlockSpec(memory_space=pltpu.SMEM),
                  pl.BlockSpec((None, nqi, LANES, qb), lambda bi, i: (bi, 0, 0, cur(i))),
                  pl.BlockSpec((None, idx_heads, qb), lambda bi, i: (bi, 0, cur(i))),
                  pl.BlockSpec((None, nqi, LANES, qb), lambda bi, i: (bi, 0, 0, nxt(i))),
                  pl.BlockSpec((None, idx_heads, qb), lambda bi, i: (bi, 0, nxt(i))),
                  pl.BlockSpec((None, npair, LANES, qb), lambda bi, i: (bi, 0, 0, prev(i))),
                  pl.BlockSpec((None, 4, s, LANES), lambda bi, i: (bi, 0, 0, 0)),
                  pl.BlockSpec((None, LANES, s), lambda bi, i: (bi, 0, 0))],
        out_specs=pl.BlockSpec((None, npair, qb, LANES), lambda bi, i: (bi, 0, prev(i), 0)),
        out_shape=jax.ShapeDtypeStruct((b, npair, s, LANES), BF16),
        scratch_shapes=[pltpu.VMEM((2, s, qb), F32),
                        pltpu.VMEM((2, s, qb), BF16), pltpu.VMEM((4, SUBLANES, qb), F32),
                        pltpu.VMEM((2, SUBLANES, qb), F32),
                        pltpu.VMEM((2 * npair, SUBLANES, qb), F32),
                        pltpu.VMEM((2 * npair, LANES, qb), F32)],
        compiler_params=pltpu.CompilerParams(
            dimension_semantics=("arbitrary", "arbitrary"), vmem_limit_bytes=VMEM_LIMIT),
        name="dsa_attn",
    )(bnd, qit, wit, qit, wit, qt, kk, vt)


def kernel(x, c, norm_mix_g, norm_ffn_g, ada_w, ada_b, a_w_in, a_conv_w, a_conv_b, a_gate_r_w,
           a_gate_r_b, a_gate_i_w, a_gate_i_b, a_lambda, a_w_out, b_w_in, b_q_norm_g, b_k_norm_g,
           b_w_out, ffn_w1, ffn_w2):
    b, s, d = x.shape
    hd = b_q_norm_g.shape[-1]
    n_heads = b_w_out.shape[1] // hd
    idx_heads = (b_w_in.shape[-1] - n_heads * hd - 3 * hd) // (hd + 1)
    topk = min(TOPK_MAX, s // 4)
    depth = ada_w.shape[0]

    mod = _mod_call(c, ada_w, ada_b)
    mod_rows = mod.reshape(depth, b, 1, mod.shape[-1])

    for i in range(depth):
        j = i // 2
        if i % 2 == 0:
            x = _rglru_call(x, mod[i], norm_mix_g[i], a_w_in[j], a_conv_w[j], a_conv_b[j],
                            a_gate_r_w[j], a_gate_r_b[j], a_gate_i_w[j], a_gate_i_b[j],
                            a_lambda[j], a_w_out[j])
            x = _ffn_call(x, mod_rows[i], norm_ffn_g[i], ffn_w1[i], ffn_w2[i])
        else:
            qt, qit, kk, vt, wit = _dsa_in_call(x, mod_rows[i], norm_mix_g[i], b_w_in[j],
                                                 b_q_norm_g[j], b_k_norm_g[j], n_heads=n_heads,
                                                 hd=hd, idx_heads=idx_heads)
            bnd = (1.02 * hd * hd ** -0.5) * jnp.max(jnp.abs(b_q_norm_g[j])) * jnp.max(jnp.abs(b_k_norm_g[j]))
            o = _dsa_attn_call(bnd.reshape(1), qt, qit, kk, vt, wit, hd=hd, idx_heads=idx_heads,
                               topk=topk)
            x = _ffn_call(x, mod_rows[i], norm_ffn_g[i], ffn_w1[i], ffn_w2[i], o_pairs=o,
                          w_o=b_w_out[j])
    return x
```

```python
import functools

import jax
import jax.numpy as jnp
from jax import lax
from jax.experimental import pallas as pl
from jax.experimental.pallas import tpu as pltpu

F32 = jnp.float32
BF16 = jnp.bfloat16
I32 = jnp.int32

RMS_EPS = 1e-6
ROPE_THETA = 10000.0
LRU_C = 8.0
TOPK_MAX = 256
N_MOD = 6

LANES = 128
SUBLANES = 8
VMEM_LIMIT = 56 * 1024 * 1024

NEG_BIG = -1e30
EXP_SAFE_BOUND = 70.0
LOG2E = 1.4426950408889634
QK_LOOKAHEAD = 2
GELU_C0 = 0.7978845608028654
GELU_C1 = 0.044715


def _const_spec(shape):
    nd = len(shape)
    return pl.BlockSpec(shape, lambda *_: (0,) * nd, pipeline_mode=pl.Buffered(1))


def _rmsnorm(x, g):
    return x * lax.rsqrt(jnp.mean(x * x, axis=-1, keepdims=True) + RMS_EPS) * g


def _mod_kernel(c_ref, w_ref, b_ref, o_ref):
    c = c_ref[...]
    cond = c * jax.nn.sigmoid(c)
    o_ref[0] = jnp.dot(cond.astype(BF16), w_ref[0].astype(BF16),
                       preferred_element_type=F32) + b_ref[0]


def _mod_call(c, ada_w, ada_b):
    depth, d, n = ada_w.shape
    b = c.shape[0]
    tn = 1536
    return pl.pallas_call(
        _mod_kernel,
        grid=(depth, n // tn),
        in_specs=[pl.BlockSpec((b, d), lambda l, j: (0, 0)),
                  pl.BlockSpec((1, d, tn), lambda l, j: (l, 0, j)),
                  pl.BlockSpec((1, 1, tn), lambda l, j: (l, 0, j))],
        out_specs=pl.BlockSpec((1, b, tn), lambda l, j: (l, 0, j)),
        out_shape=jax.ShapeDtypeStruct((depth, b, n), F32),
        compiler_params=pltpu.CompilerParams(
            dimension_semantics=("arbitrary", "arbitrary"), vmem_limit_bytes=VMEM_LIMIT),
        name="adaln_mod",
    )(c, ada_w, ada_b.reshape(depth, 1, n))


def _rglru_kernel(x_ref, mod_ref, g_ref, win_ref, cw_ref, cb_ref, wr_ref, br_ref, wi_ref, bi_ref,
                  lam_ref, wout_ref, o_ref, h_s, gb_s, xbuf, a_s, b_s, hc, *, rc):
    nb, ts, d = x_ref.shape
    r = ts * nb
    dr = lam_ref.shape[1]
    nblk, blk, _ = wr_ref.shape
    cwid = cw_ref.shape[0]
    tail = (cwid - 1) * nb

    @pl.when(pl.program_id(0) == 0)
    def _():
        xbuf[0:tail, :] = jnp.zeros((tail, dr), F32)
        hc[...] = jnp.zeros((nb, dr), F32)

    x = jnp.swapaxes(x_ref[...], 0, 1)
    sh1 = mod_ref[:, 0:d]
    sc1 = mod_ref[:, d:2 * d]
    g1 = mod_ref[:, 2 * d:3 * d]
    h_s[...] = (_rmsnorm(x, g_ref[...]) * (1.0 + sc1)[None] + sh1[None]).reshape(r, d).astype(BF16)

    sp_h = (0.5 * LRU_C) * jax.nn.softplus(-lam_ref[...])
    br_h = 0.5 * br_ref[...]
    bi_h = 0.5 * bi_ref[...]

    def in_proj(c, n):
        rows = slice(c * rc, (c + 1) * rc)
        hb = h_s[rows, :]
        ux = jnp.dot(hb, win_ref[:, n * blk:(n + 1) * blk], preferred_element_type=F32)
        ug = jnp.dot(hb, win_ref[:, dr + n * blk:dr + (n + 1) * blk], preferred_element_type=F32)
        xbuf[tail + c * rc:tail + (c + 1) * rc, n * blk:(n + 1) * blk] = ux
        return ug

    def conv_and_gate_dots(c, n):
        r0 = c * rc
        cs = slice(n * blk, (n + 1) * blk)
        xc = cb_ref[:, cs] + cw_ref[0:1, cs] * xbuf[r0:r0 + rc, cs]
        for k in range(1, cwid):
            xc = xc + cw_ref[k:k + 1, cs] * xbuf[r0 + k * nb:r0 + k * nb + rc, cs]
        xn = xc.astype(BF16)
        return xc, (jnp.dot(xn, wr_ref[n], preferred_element_type=F32),
                    jnp.dot(xn, wi_ref[n], preferred_element_type=F32))

    def gate_tail(c, n, xc, pre, ug):
        ro = slice(c * rc, (c + 1) * rc)
        cs = slice(n * blk, (n + 1) * blk)
        p = sp_h[:, cs] + sp_h[:, cs] * jnp.tanh(0.5 * pre[0] + br_h[:, cs])
        ig = 0.5 + 0.5 * jnp.tanh(0.5 * pre[1] + bi_h[:, cs])
        a = jnp.exp2(p * (-LOG2E))
        a_s[ro, cs] = a
        z = jnp.tanh(p) * (a * a + 1.0)
        b_s[ro, cs] = jnp.where(z > 0.0, z * lax.rsqrt(z), 0.0) * (ig * xc)
        t = jnp.tanh(ug * (GELU_C0 + (GELU_C0 * GELU_C1) * (ug * ug)))
        hu = 0.5 * ug
        gb_s[ro, cs] = hu + hu * t

    units = [(c, n) for c in range(r // rc) for n in range(nblk)]
    ug = in_proj(*units[0])
    for k, (c, n) in enumerate(units):
        ug_next = in_proj(*units[k + 1]) if k + 1 < len(units) else None
        xc, pre = conv_and_gate_dots(c, n)
        gate_tail(c, n, xc, pre, ug)
        ug = ug_next
    xbuf[0:tail, :] = xbuf[r:r + tail, :]

    def step(t, hprev):
        r0 = pl.multiple_of(t * nb, nb)
        hnew = a_s[pl.ds(r0, nb), :] * hprev + b_s[pl.ds(r0, nb), :]
        b_s[pl.ds(r0, nb), :] = hnew
        return hnew

    hc[...] = lax.fori_loop(0, ts, step, hc[...], unroll=8)

    y = b_s[...] * gb_s[...]
    out = jnp.dot(y.astype(BF16), wout_ref[...], preferred_element_type=F32)
    o_ref[...] = jnp.swapaxes(x + out.reshape(ts, nb, d) * g1[None], 0, 1)


def _rglru_call(x, mod0, g, w_in, conv_w, conv_b, wr, br, wi, bi, lam, w_out, *, ts=128, rc=256):
    nb, s, d = x.shape
    assert nb == SUBLANES, "the recurrence keeps the batch on the sublane axis"
    r = ts * nb
    dr = lam.shape[-1]
    tail = (conv_w.shape[0] - 1) * nb
    row = lambda v: v.reshape(1, -1)
    args = (x, mod0, row(g), w_in.astype(BF16), conv_w, row(conv_b), wr.astype(BF16), row(br),
            wi.astype(BF16), row(bi), row(lam), w_out.astype(BF16))
    x_spec = pl.BlockSpec((nb, ts, d), lambda i: (0, i, 0))
    return pl.pallas_call(
        functools.partial(_rglru_kernel, rc=rc),
        grid=(s // ts,),
        in_specs=[x_spec] + [_const_spec(a.shape) for a in args[1:]],
        out_specs=x_spec,
        out_shape=jax.ShapeDtypeStruct((nb, s, d), F32),
        scratch_shapes=[pltpu.VMEM((r, d), BF16), pltpu.VMEM((r, dr), F32),
                        pltpu.VMEM((r + tail, dr), F32), pltpu.VMEM((r, dr), F32),
                        pltpu.VMEM((r, dr), F32), pltpu.VMEM((nb, dr), F32)],
        compiler_params=pltpu.CompilerParams(
            dimension_semantics=("arbitrary",), vmem_limit_bytes=VMEM_LIMIT),
        name="rglru_mixer",
    )(*args)


def _ffn_kernel(*refs, has_proj, fc):
    if has_proj:
        x_ref, mod_ref, g_ref, w1_ref, w2_ref, o_in_ref, wo_ref, out_ref = refs
    else:
        x_ref, mod_ref, g_ref, w1_ref, w2_ref, out_ref = refs
    d = x_ref.shape[1]
    dff = w1_ref.shape[1]
    x = x_ref[...]
    if has_proj:
        g1 = mod_ref[:, 2 * d:3 * d]
        o = jnp.concatenate([o_in_ref[p] for p in range(o_in_ref.shape[0])], axis=-1)
        x = x + g1 * jnp.dot(o, wo_ref[...], preferred_element_type=F32)
    sh2 = mod_ref[:, 3 * d:4 * d]
    sc2 = mod_ref[:, 4 * d:5 * d]
    g2 = mod_ref[:, 5 * d:6 * d]
    hb = (_rmsnorm(x, g_ref[...]) * (1.0 + sc2) + sh2).astype(BF16)
    acc = jnp.zeros(x.shape, F32)
    for c in range(dff // fc):
        hid = jnp.dot(hb, w1_ref[:, c * fc:(c + 1) * fc], preferred_element_type=F32)
        hid = jnp.square(jnp.maximum(hid, 0.0))
        acc = acc + jnp.dot(hid.astype(BF16), w2_ref[c * fc:(c + 1) * fc, :],
                            preferred_element_type=F32)
    out_ref[...] = x + g2 * acc


def _ffn_call(x, mod_l, g, w1, w2, o_pairs=None, w_o=None, *, rf=512, fc=1024):
    b, s, d = x.shape
    has_proj = o_pairs is not None
    args = [x, mod_l, g.reshape(1, -1), w1.astype(BF16), w2.astype(BF16)]
    in_specs = [pl.BlockSpec((None, rf, d), lambda bi, i: (bi, i, 0)),
                pl.BlockSpec((None, 1, mod_l.shape[-1]), lambda bi, i: (bi, 0, 0)),
                _const_spec((1, d)), _const_spec(w1.shape), _const_spec(w2.shape)]
    if has_proj:
        npair = o_pairs.shape[1]
        args += [o_pairs, w_o.astype(BF16)]
        in_specs += [pl.BlockSpec((None, npair, rf, LANES), lambda bi, i: (bi, 0, i, 0)),
                     _const_spec(w_o.shape)]
    return pl.pallas_call(
        functools.partial(_ffn_kernel, has_proj=has_proj, fc=fc),
        grid=(b, s // rf),
        in_specs=in_specs,
        out_specs=pl.BlockSpec((None, rf, d), lambda bi, i: (bi, i, 0)),
        out_shape=jax.ShapeDtypeStruct((b, s, d), F32),
        compiler_params=pltpu.CompilerParams(
            dimension_semantics=("arbitrary", "arbitrary"), vmem_limit_bytes=VMEM_LIMIT),
        name="ffn_proj" if has_proj else "ffn",
    )(*args)


def _rot_half(x, lane):
    return jnp.where((lane & 63) < 32, pltpu.roll(x, 96, 1), pltpu.roll(x, 32, 1))


def _dsa_in_kernel(x_ref, mod_ref, g_ref, w_ref, qg_ref, kg_ref, e_ref, cos_ref, sin_ref,
                   qt_ref, qit_ref, kk_ref, vt_ref, wit_ref, *, nq, nqi, hd, idx_heads):
    d = x_ref.shape[1]
    tq = x_ref.shape[0]
    x = x_ref[...]
    sh1 = mod_ref[:, 0:d]
    sc1 = mod_ref[:, d:2 * d]
    hb = (_rmsnorm(x, g_ref[...]) * (1.0 + sc1) + sh1).astype(BF16)
    lane = lax.broadcasted_iota(I32, (tq, LANES), 1)
    left = lane < hd
    cos = cos_ref[...]
    sin = sin_ref[...]
    qg = qg_ref[...]
    e = e_ref[...]
    qscale = hd ** -0.5

    def q_tile(p, t):
        t2 = t * t
        hi = t2.astype(BF16)
        lo = (t2 - hi.astype(F32)).astype(BF16)
        ss = (jnp.dot(hi, e, preferred_element_type=F32) + jnp.dot(lo, e, preferred_element_type=F32))
        tn = t * lax.rsqrt(ss * (1.0 / hd) + RMS_EPS) * qg
        qt_ref[p] = ((tn * cos + _rot_half(tn, lane) * sin) * (qscale * LOG2E)).T.astype(BF16)

    def qi_tile(p, t):
        qit_ref[p] = ((t * cos + _rot_half(t, lane) * sin) * qscale).T.astype(BF16)

    def kk_tile(kk):
        ssk = jnp.sum(jnp.where(left, kk * kk, 0.0), axis=-1, keepdims=True)
        fac = jnp.where(left, lax.rsqrt(ssk * (1.0 / hd) + RMS_EPS) * kg_ref[...], 1.0)
        kn = kk * fac
        kr = kn * cos + _rot_half(kn, lane) * sin
        k_l = jnp.where(left, kr, 0.0)
        ki_r = jnp.where(left, 0.0, kr)
        kk_ref[0] = k_l.astype(BF16)
        kk_ref[1] = pltpu.roll(k_l, hd, 1).astype(BF16)
        kk_ref[2] = pltpu.roll(ki_r, hd, 1).astype(BF16)
        kk_ref[3] = ki_r.astype(BF16)

    def vw_tile(vw):
        vt_ref[...] = jnp.where(left, vw, 1.0).T.astype(BF16)
        wit_ref[...] = (vw * (idx_heads ** -0.5)).T[hd:hd + idx_heads, :]

    def tile(p, t):
        if p < nq:
            q_tile(p, t)
        elif p < nq + nqi:
            qi_tile(p - nq, t)
        elif p == nq + nqi:
            kk_tile(t)
        else:
            vw_tile(t)

    gw = 2 * LANES
    ngroups = w_ref.shape[1] // gw

    def proj(gi):
        return jnp.dot(hb, w_ref[:, gi * gw:(gi + 1) * gw], preferred_element_type=F32)

    u_next = proj(0)
    for gi in range(ngroups):
        u = u_next
        if gi + 1 < ngroups:
            u_next = proj(gi + 1)
        for half in range(gw // LANES):
            tile(gi * (gw // LANES) + half, u[:, half * LANES:(half + 1) * LANES])


def _dsa_in_call(x, mod_l, g, w_in, q_g, k_g, *, n_heads, hd, idx_heads, tq=512):
    b, s, d = x.shape
    assert 2 * hd == LANES, "two heads per lane tile"
    nq = n_heads * hd // LANES
    nqi = idx_heads * hd // LANES
    o_k = n_heads * hd
    o_v = o_k + hd
    o_qi = o_v + hd
    o_ki = o_qi + idx_heads * hd
    o_wi = o_ki + hd
    pad = LANES - hd - idx_heads
    w = jnp.concatenate([w_in[:, :o_k], w_in[:, o_qi:o_ki], w_in[:, o_k:o_v], w_in[:, o_ki:o_wi],
                         w_in[:, o_v:o_qi], w_in[:, o_wi:], jnp.zeros((d, pad), w_in.dtype)],
                        axis=1).astype(BF16)
    ncol = w.shape[1]
    inv = ROPE_THETA ** (-jnp.arange(0, hd, 2, dtype=F32) / hd)
    ang = jnp.arange(s, dtype=F32)[:, None] * inv[None, :]
    cos_t = jnp.tile(jnp.cos(ang), (1, 4))
    sin_h = jnp.sin(ang)
    sin_t = jnp.tile(jnp.concatenate([-sin_h, sin_h], axis=1), (1, 2))
    head_of_lane = jnp.arange(LANES) // hd
    e = (head_of_lane[:, None] == head_of_lane[None, :]).astype(BF16)
    qg = jnp.tile(q_g, 2).reshape(1, LANES)
    kg = jnp.concatenate([k_g, jnp.ones((hd,), k_g.dtype)]).reshape(1, LANES)

    return pl.pallas_call(
        functools.partial(_dsa_in_kernel, nq=nq, nqi=nqi, hd=hd, idx_heads=idx_heads),
        grid=(b, s // tq),
        in_specs=[pl.BlockSpec((None, tq, d), lambda bi, i: (bi, i, 0)),
                  pl.BlockSpec((None, 1, mod_l.shape[-1]), lambda bi, i: (bi, 0, 0)),
                  _const_spec((1, d)), _const_spec((d, ncol)), _const_spec((1, LANES)),
                  _const_spec((1, LANES)), _const_spec((LANES, LANES)),
                  pl.BlockSpec((tq, LANES), lambda bi, i: (i, 0)),
                  pl.BlockSpec((tq, LANES), lambda bi, i: (i, 0))],
        out_specs=[pl.BlockSpec((None, nq, LANES, tq), lambda bi, i: (bi, 0, 0, i)),
                   pl.BlockSpec((None, nqi, LANES, tq), lambda bi, i: (bi, 0, 0, i)),
                   pl.BlockSpec((None, 4, tq, LANES), lambda bi, i: (bi, 0, i, 0)),
                   pl.BlockSpec((None, LANES, tq), lambda bi, i: (bi, 0, i)),
                   pl.BlockSpec((None, idx_heads, tq), lambda bi, i: (bi, 0, i))],
        out_shape=[jax.ShapeDtypeStruct((b, nq, LANES, s), BF16),
                   jax.ShapeDtypeStruct((b, nqi, LANES, s), BF16),
                   jax.ShapeDtypeStruct((b, 4, s, LANES), BF16),
                   jax.ShapeDtypeStruct((b, LANES, s), BF16),
                   jax.ShapeDtypeStruct((b, idx_heads, s), F32)],
        compiler_params=pltpu.CompilerParams(
            dimension_semantics=("arbitrary", "arbitrary"), vmem_limit_bytes=VMEM_LIMIT),
        name="dsa_in",
    )(x, mod_l, g.reshape(1, -1), w, qg, kg, e, cos_t, sin_t)


def _dsa_attn_kernel(bnd_ref, qit0_ref, wit0_ref, qitn_ref, witn_ref, qt_ref, kk_ref, vt_ref, o_ref,
                     sct_s, mask_s, st_s, mm_s, m_s, acc_s, *, kc, ks, topk, hd, idx_heads, group,
                     tie_from, max_groups, n_blocks, head_passes, sl):
    npair, _, qb = qt_ref.shape
    nh = 2 * npair
    nsub = SUBLANES
    i = pl.program_id(1)
    has_sel = i < n_blocks
    has_att = i >= 1
    nk = (i * qb) // kc + 1
    nkp = ((i - 1) * qb) // kc + 1
    slot = lax.rem(i, 2)
    mask_w = mask_s.at[slot]
    mask_r = mask_s.at[1 - slot]
    sct_c = sct_s.at[slot]
    sct_n = sct_s.at[1 - slot]
    kf = float(topk)
    fast = bnd_ref[0] <= EXP_SAFE_BOUND
    merged = jnp.logical_and(jnp.logical_and(has_sel, has_att), fast)

    def slab(src, c, t):
        return src[pl.ds(pl.multiple_of(c * kc + t * sl, sl), sl), :]

    def finish(acc, op, red):
        n = sl // nsub
        acc = acc.reshape(n, nsub, qb)
        r = acc[0]
        for t in range(1, n):
            r = op(r, acc[t])
        return jnp.broadcast_to(red(r, axis=0, keepdims=True), (nsub, qb))

    def reduce_keys(src, fn, op, red, init):
        def body(ci, acc):
            for t in range(kc // sl):
                acc = op(acc, fn(slab(src, ci, t), ci, t))
            return acc
        return finish(lax.fori_loop(0, nk, body, jnp.full((sl, qb), init, F32)), op, red)

    def rows(x):
        return jnp.concatenate([x] * (sl // nsub), axis=0)

    def score_chunk(qit_ref, wit_ref, blk, dst, ci, mm):
        qpos = blk * qb + lax.broadcasted_iota(I32, (ks, qb), 1)
        krow = lax.broadcasted_iota(I32, (ks, qb), 0)
        mx, mn = mm
        for sub in range(kc // ks):
            k0 = pl.multiple_of(ci * kc + sub * ks, ks)
            kis = (kk_ref[2, pl.ds(k0, ks), :], kk_ref[3, pl.ds(k0, ks), :])
            acc = jnp.zeros((ks, qb), F32)
            for hh in range(idx_heads):
                lg = jnp.dot(kis[hh % 2], qit_ref[hh // 2], preferred_element_type=F32)
                acc = acc + wit_ref[hh:hh + 1, :] * jnp.maximum(lg, 0.0)
            causal = (k0 + krow) <= qpos
            lo_v = jnp.where(causal, acc, -jnp.inf)
            hi_v = jnp.where(causal, acc, jnp.inf)
            dst[pl.ds(k0, ks), :] = lo_v
            for t in range(ks // sl):
                mx = jnp.maximum(mx, lo_v[t * sl:(t + 1) * sl])
                mn = jnp.minimum(mn, hi_v[t * sl:(t + 1) * sl])
        return mx, mn

    mm0 = (jnp.full((sl, qb), -jnp.inf, F32), jnp.full((sl, qb), jnp.inf, F32))

    def save_range(mm):
        mm_s[0] = finish(mm[0], jnp.maximum, jnp.max)
        mm_s[1] = finish(mm[1], jnp.minimum, jnp.min)

    @pl.when(i == 0)
    def _():
        save_range(lax.fori_loop(
            0, nk, functools.partial(score_chunk, qit0_ref, wit0_ref, 0, sct_c), mm0))

    @pl.when(has_sel)
    def _():
        mx = mm_s[0]
        st_s[0] = mm_s[1]
        st_s[1] = mx + (jnp.abs(mx) * 1e-6 + 1e-30)
        st_s[2] = (i * qb + lax.broadcasted_iota(I32, (nsub, qb), 1) + 1).astype(F32)
        st_s[3] = jnp.zeros((nsub, qb), F32)

    def count_step(st):
        c, lo, hi, clo, chi, acc = st
        mid = 0.5 * lo + 0.5 * hi
        midr = rows(mid)
        for t in range(kc // sl):
            acc = acc + jnp.where(slab(sct_c, c, t) >= midr, 1.0, 0.0)
        last = c == nk - 1
        tot = finish(acc, jnp.add, jnp.sum)
        ge = tot >= kf
        lo = jnp.where(last, jnp.where(ge, mid, lo), lo)
        clo = jnp.where(last, jnp.where(ge, tot, clo), clo)
        hi = jnp.where(last, jnp.where(ge, hi, mid), hi)
        chi = jnp.where(last, jnp.where(ge, chi, tot), chi)
        return (jnp.where(last, 0, c + 1), lo, hi, clo, chi, jnp.where(last, 0.0, acc))

    def scores(ci, h):
        kk = kk_ref[h % 2, pl.ds(pl.multiple_of(ci * kc, kc), kc), :]
        return jnp.dot(kk, qt_ref[h // 2], preferred_element_type=F32)

    def pv_step(ci, st, use_max, passes):
        c0 = pl.multiple_of(ci * kc, kc)
        vt = vt_ref[:, pl.ds(c0, kc)]
        msk = mask_r[pl.ds(c0, kc), :]
        ahead = [scores(ci, h) for h in range(min(QK_LOOKAHEAD, nh))]
        for h in range(nh):
            s_cur = ahead.pop(0)
            if h + QK_LOOKAHEAD < nh:
                ahead.append(scores(ci, h + QK_LOOKAHEAD))
            if use_max:
                pr = jnp.where(msk > 0, jnp.exp2(s_cur - m_s[h][0:1, :]), 0.0).astype(BF16)
            else:
                pr = jnp.exp2(s_cur).astype(BF16) * msk
            acc_s[h] += jnp.dot(vt, pr, preferred_element_type=F32)
            for _ in range(passes[h] if passes else 0):
                st = count_step(st)
        return st

    @pl.when(has_att)
    def _():
        acc_s[...] = jnp.zeros(acc_s.shape, F32)

    @pl.when(merged)
    def _():
        st = (jnp.int32(0), st_s[0], st_s[1], st_s[2], st_s[3], jnp.zeros((sl, qb), F32))
        st = lax.fori_loop(0, nkp, functools.partial(pv_step, use_max=False, passes=head_passes), st)
        st_s[0], st_s[1], st_s[2], st_s[3] = st[1], st[2], st[3], st[4]

    @pl.when(jnp.logical_and(has_att, jnp.logical_and(fast, jnp.logical_not(has_sel))))
    def _():
        lax.fori_loop(0, nkp, functools.partial(pv_step, use_max=False, passes=None), 0)

    @pl.when(jnp.logical_and(has_att, jnp.logical_not(fast)))
    def _():
        def max_step(ci, _):
            @pl.when(ci == 0)
            def _():
                m_s[...] = jnp.full(m_s.shape, NEG_BIG, F32)
            msk = mask_r[pl.ds(pl.multiple_of(ci * kc, kc), kc), :]
            for h in range(nh):
                mx = jnp.max(jnp.where(msk > 0, scores(ci, h), NEG_BIG), axis=0, keepdims=True)
                m_s[h] = jnp.maximum(m_s[h], jnp.broadcast_to(mx, (nsub, qb)))
            return 0

        lax.fori_loop(0, nkp, max_step, 0)
        lax.fori_loop(0, nkp, functools.partial(pv_step, use_max=True, passes=None), 0)

    @pl.when(has_att)
    def _():
        for p in range(npair):
            a0, a1 = acc_s[2 * p], acc_s[2 * p + 1]
            ot = jnp.concatenate([a0[0:hd] / a0[hd:2 * hd], a1[0:hd] / a1[hd:2 * hd]], axis=0)
            o_ref[p] = ot.T.astype(BF16)

    @pl.when(has_sel)
    def _():
        def count_ge(thr):
            thr = rows(thr)
            return reduce_keys(sct_c, lambda blk, c, t: jnp.where(blk >= thr, 1.0, 0.0),
                               jnp.add, jnp.sum, 0.0)

        def bisect(_, st):
            lo, hi, clo, chi = st
            mid = 0.5 * lo + 0.5 * hi
            c = count_ge(mid)
            ge = c >= kf
            return (jnp.where(ge, mid, lo), jnp.where(ge, hi, mid),
                    jnp.where(ge, c, clo), jnp.where(ge, chi, c))

        def open_rows(clo, tied):
            return jnp.max(jnp.where(clo > kf, 1.0 - tied, 0.0)) > 0.0

        g0 = jnp.where(merged, tie_from // group, 0)

        def search_group(st):
            g, _, lo, hi, clo, chi, tied = st
            lo, hi, clo, chi = lax.fori_loop(0, group, bisect, (lo, hi, clo, chi))

            def tie_check(tied):
                lor, hir = rows(lo), rows(hi)
                a = reduce_keys(sct_c, lambda blk, c, t: jnp.where(blk >= lor, blk, jnp.inf),
                                jnp.minimum, jnp.min, jnp.inf)
                b = reduce_keys(sct_c, lambda blk, c, t: jnp.where(blk < hir, blk, -jnp.inf),
                                jnp.maximum, jnp.max, -jnp.inf)
                return jnp.where(jnp.logical_and(clo > kf, a == b), 1.0, tied)

            tied = lax.cond(jnp.logical_and((g + 1) * group >= tie_from, open_rows(clo, tied)),
                            tie_check, lambda t: t, tied)
            return (g + 1, open_rows(clo, tied).astype(I32), lo, hi, clo, chi, tied)

        zero = jnp.zeros((nsub, qb), F32)
        _, _, lo, hi, clo, chi, _ = lax.while_loop(
            lambda st: jnp.logical_and(st[1] > 0, st[0] < max_groups), search_group,
            (g0, open_rows(st_s[2], zero).astype(I32), st_s[0], st_s[1], st_s[2], st_s[3], zero))

        def tile_k(x):
            return jnp.concatenate([x] * (ks // nsub), axis=0)

        lok, hik, free = tile_k(lo), tile_k(hi), tile_k(kf - chi)
        tri = jnp.where(lax.broadcasted_iota(I32, (ks, ks), 1) <= lax.broadcasted_iota(I32, (ks, ks), 0),
                        1.0, 0.0).astype(BF16)

        def mask_chunk(ci, carry):
            subs = []
            for t in range(kc // ks):
                k0 = pl.multiple_of(ci * kc + t * ks, ks)
                blk = sct_c[pl.ds(k0, ks), :]
                top = blk >= hik
                tie = jnp.where(top, 0.0, jnp.where(blk >= lok, 1.0, 0.0))
                subs.append((k0, top, tie, jnp.dot(tri, tie.astype(BF16), preferred_element_type=F32)))
            for k0, top, tie, rank in subs:
                keep = jnp.where(rank + tile_k(carry) <= free, tie, 0.0)
                mask_w[pl.ds(k0, ks), :] = jnp.where(top, 1.0, keep).astype(BF16)
                carry = carry + jnp.broadcast_to(rank[ks - 1:ks, :], (nsub, qb))
            return carry

        zero8 = jnp.zeros((nsub, qb), F32)
        has_next = i + 1 < n_blocks

        @pl.when(has_next)
        def _():
            nkn = ((i + 1) * qb) // kc + 1
            nxt = functools.partial(score_chunk, qitn_ref, witn_ref, i + 1, sct_n)

            def both(ci, st):
                return (mask_chunk(ci, st[0]), nxt(ci, st[1]))

            _, mm = lax.fori_loop(0, nk, both, (zero8, mm0))
            save_range(lax.fori_loop(nk, nkn, nxt, mm))

        @pl.when(jnp.logical_not(has_next))
        def _():
            lax.fori_loop(0, nk, mask_chunk, zero8)


def _dsa_attn_call(bnd, qt, qit, kk, vt, wit, *, hd, idx_heads, topk, qb=256, kc=512, ks=128,
                   sl=32):
    b, npair, _, s = qt.shape
    nqi = qit.shape[1]
    assert kc % qb == 0 and s % kc == 0 and kc % ks == 0 and ks % sl == 0 and sl % SUBLANES == 0
    nb = s // qb
    head_passes = tuple(2 if h % 4 == 0 else 1 for h in range(2 * npair))
    assert 2 * hd == LANES
    prev = lambda i: jnp.maximum(i - 1, 0)
    cur = lambda i: jnp.minimum(i, nb - 1)
    nxt = lambda i: jnp.minimum(i + 1, nb - 1)
    return pl.pallas_call(
        functools.partial(_dsa_attn_kernel, kc=kc, ks=ks, topk=topk, hd=hd, idx_heads=idx_heads,
                          group=4, tie_from=16, max_groups=96, n_blocks=nb,
                          head_passes=head_passes, sl=sl),
        grid=(b, nb + 1),
        in_specs=[pl.BlockSpec(memory_space=pltpu.SMEM),
                  pl.BlockSpec((None, nqi, LANES, qb), lambda bi, i: (bi, 0, 0, cur(i))),
                  pl.BlockSpec((None, idx_heads, qb), lambda bi, i: (bi, 0, cur(i))),
                  pl.BlockSpec((None, nqi, LANES, qb), lambda bi, i: (bi, 0, 0, nxt(i))),
                  pl.BlockSpec((None, idx_heads, qb), lambda bi, i: (bi, 0, nxt(i))),
                  pl.BlockSpec((None, npair, LANES, qb), lambda bi, i: (bi, 0, 0, prev(i))),
                  pl.BlockSpec((None, 4, s, LANES), lambda bi, i: (bi, 0, 0, 0)),
                  pl.BlockSpec((None, LANES, s), lambda bi, i: (bi, 0, 0))],
        out_specs=pl.BlockSpec((None, npair, qb, LANES), lambda bi, i: (bi, 0, prev(i), 0)),
        out_shape=jax.ShapeDtypeStruct((b, npair, s, LANES), BF16),
        scratch_shapes=[pltpu.VMEM((2, s, qb), F32),
                        pltpu.VMEM((2, s, qb), BF16), pltpu.VMEM((4, SUBLANES, qb), F32),
                        pltpu.VMEM((2, SUBLANES, qb), F32),
                        pltpu.VMEM((2 * npair, SUBLANES, qb), F32),
                        pltpu.VMEM((2 * npair, LANES, qb), F32)],
        compiler_params=pltpu.CompilerParams(
            dimension_semantics=("arbitrary", "arbitrary"), vmem_limit_bytes=VMEM_LIMIT),
        name="dsa_attn",
    )(bnd, qit, wit, qit, wit, qt, kk, vt)


def kernel(x, c, norm_mix_g, norm_ffn_g, ada_w, ada_b, a_w_in, a_conv_w, a_conv_b, a_gate_r_w,
           a_gate_r_b, a_gate_i_w, a_gate_i_b, a_lambda, a_w_out, b_w_in, b_q_norm_g, b_k_norm_g,
           b_w_out, ffn_w1, ffn_w2):
    b, s, d = x.shape
    hd = b_q_norm_g.shape[-1]
    n_heads = b_w_out.shape[1] // hd
    idx_heads = (b_w_in.shape[-1] - n_heads * hd - 3 * hd) // (hd + 1)
    topk = min(TOPK_MAX, s // 4)
    depth = ada_w.shape[0]

    mod = _mod_call(c, ada_w, ada_b)
    mod_rows = mod.reshape(depth, b, 1, mod.shape[-1])

    for i in range(depth):
        j = i // 2
        if i % 2 == 0:
            x = _rglru_call(x, mod[i], norm_mix_g[i], a_w_in[j], a_conv_w[j], a_conv_b[j],
                            a_gate_r_w[j], a_gate_r_b[j], a_gate_i_w[j], a_gate_i_b[j],
                            a_lambda[j], a_w_out[j])
            x = _ffn_call(x, mod_rows[i], norm_ffn_g[i], ffn_w1[i], ffn_w2[i])
        else:
            qt, qit, kk, vt, wit = _dsa_in_call(x, mod_rows[i], norm_mix_g[i], b_w_in[j],
                                                 b_q_norm_g[j], b_k_norm_g[j], n_heads=n_heads,
                                                 hd=hd, idx_heads=idx_heads)
            bnd = (1.02 * hd * hd ** -0.5) * jnp.max(jnp.abs(b_q_norm_g[j])) * jnp.max(jnp.abs(b_k_norm_g[j]))
            o = _dsa_attn_call(bnd.reshape(1), qt, qit, kk, vt, wit, hd=hd, idx_heads=idx_heads,
                               topk=topk)
            x = _ffn_call(x, mod_rows[i], norm_ffn_g[i], ffn_w1[i], ffn_w2[i], o_pairs=o,
                          w_o=b_w_out[j])
    return x
```

```python
import functools

import jax
import jax.numpy as jnp
from jax import lax
from jax.experimental import pallas as pl
from jax.experimental.pallas import tpu as pltpu

F32 = jnp.float32
BF16 = jnp.bfloat16
I32 = jnp.int32

RMS_EPS = 1e-6
ROPE_THETA = 10000.0
LRU_C = 8.0
TOPK_MAX = 256
N_MOD = 6

LANES = 128
SUBLANES = 8
VMEM_LIMIT = 56 * 1024 * 1024

NEG_BIG = -1e30
EXP_SAFE_BOUND = 70.0
LOG2E = 1.4426950408889634
QK_LOOKAHEAD = 2
GELU_C0 = 0.7978845608028654
GELU_C1 = 0.044715


def _const_spec(shape):
    nd = len(shape)
    return pl.BlockSpec(shape, lambda *_: (0,) * nd, pipeline_mode=pl.Buffered(1))


def _rmsnorm(x, g):
    return x * lax.rsqrt(jnp.mean(x * x, axis=-1, keepdims=True) + RMS_EPS) * g


def _mod_kernel(c_ref, w_ref, b_ref, o_ref):
    c = c_ref[...]
    cond = c * jax.nn.sigmoid(c)
    o_ref[0] = jnp.dot(cond.astype(BF16), w_ref[0].astype(BF16),
                       preferred_element_type=F32) + b_ref[0]


def _mod_call(c, ada_w, ada_b):
    depth, d, n = ada_w.shape
    b = c.shape[0]
    tn = 1536
    return pl.pallas_call(
        _mod_kernel,
        grid=(depth, n // tn),
        in_specs=[pl.BlockSpec((b, d), lambda l, j: (0, 0)),
                  pl.BlockSpec((1, d, tn), lambda l, j: (l, 0, j)),
                  pl.BlockSpec((1, 1, tn), lambda l, j: (l, 0, j))],
        out_specs=pl.BlockSpec((1, b, tn), lambda l, j: (l, 0, j)),
        out_shape=jax.ShapeDtypeStruct((depth, b, n), F32),
        compiler_params=pltpu.CompilerParams(
            dimension_semantics=("arbitrary", "arbitrary"), vmem_limit_bytes=VMEM_LIMIT),
        name="adaln_mod",
    )(c, ada_w, ada_b.reshape(depth, 1, n))


def _rglru_kernel(x_ref, mod_ref, g_ref, win_ref, cw_ref, cb_ref, wr_ref, br_ref, wi_ref, bi_ref,
                  lam_ref, wout_ref, o_ref, h_s, gb_s, xbuf, a_s, b_s, hc, *, rc):
    nb, ts, d = x_ref.shape
    r = ts * nb
    dr = lam_ref.shape[1]
    nblk, blk, _ = wr_ref.shape
    cwid = cw_ref.shape[0]
    tail = (cwid - 1) * nb

    @pl.when(pl.program_id(0) == 0)
    def _():
        xbuf[0:tail, :] = jnp.zeros((tail, dr), F32)
        hc[...] = jnp.zeros((nb, dr), F32)

    x = jnp.swapaxes(x_ref[...], 0, 1)
    sh1 = mod_ref[:, 0:d]
    sc1 = mod_ref[:, d:2 * d]
    g1 = mod_ref[:, 2 * d:3 * d]
    h_s[...] = (_rmsnorm(x, g_ref[...]) * (1.0 + sc1)[None] + sh1[None]).reshape(r, d).astype(BF16)

    sp_h = (0.5 * LRU_C) * jax.nn.softplus(-lam_ref[...])
    br_h = 0.5 * br_ref[...]
    bi_h = 0.5 * bi_ref[...]

    def in_proj(c, n):
        rows = slice(c * rc, (c + 1) * rc)
        hb = h_s[rows, :]
        ux = jnp.dot(hb, win_ref[:, n * blk:(n + 1) * blk], preferred_element_type=F32)
        ug = jnp.dot(hb, win_ref[:, dr + n * blk:dr + (n + 1) * blk], preferred_element_type=F32)
        xbuf[tail + c * rc:tail + (c + 1) * rc, n * blk:(n + 1) * blk] = ux
        return ug

    def conv_and_gate_dots(c, n):
        r0 = c * rc
        cs = slice(n * blk, (n + 1) * blk)
        xc = cb_ref[:, cs] + cw_ref[0:1, cs] * xbuf[r0:r0 + rc, cs]
        for k in range(1, cwid):
            xc = xc + cw_ref[k:k + 1, cs] * xbuf[r0 + k * nb:r0 + k * nb + rc, cs]
        xn = xc.astype(BF16)
        return xc, (jnp.dot(xn, wr_ref[n], preferred_element_type=F32),
                    jnp.dot(xn, wi_ref[n], preferred_element_type=F32))

    def gate_tail(c, n, xc, pre, ug):
        ro = slice(c * rc, (c + 1) * rc)
        cs = slice(n * blk, (n + 1) * blk)
        p = sp_h[:, cs] + sp_h[:, cs] * jnp.tanh(0.5 * pre[0] + br_h[:, cs])
        ig = 0.5 + 0.5 * jnp.tanh(0.5 * pre[1] + bi_h[:, cs])
        a = jnp.exp2(p * (-LOG2E))
        a_s[ro, cs] = a
        z = jnp.tanh(p) * (a * a + 1.0)
        b_s[ro, cs] = jnp.where(z > 0.0, z * lax.rsqrt(z), 0.0) * (ig * xc)
        t = jnp.tanh(ug * (GELU_C0 + (GELU_C0 * GELU_C1) * (ug * ug)))
        hu = 0.5 * ug
        gb_s[ro, cs] = hu + hu * t

    units = [(c, n) for c in range(r // rc) for n in range(nblk)]
    ug = in_proj(*units[0])
    for k, (c, n) in enumerate(units):
        ug_next = in_proj(*units[k + 1]) if k + 1 < len(units) else None
        xc, pre = conv_and_gate_dots(c, n)
        gate_tail(c, n, xc, pre, ug)
        ug = ug_next
    xbuf[0:tail, :] = xbuf[r:r + tail, :]

    def step(t, hprev):
        r0 = pl.multiple_of(t * nb, nb)
        hnew = a_s[pl.ds(r0, nb), :] * hprev + b_s[pl.ds(r0, nb), :]
        b_s[pl.ds(r0, nb), :] = hnew
        return hnew

    hc[...] = lax.fori_loop(0, ts, step, hc[...], unroll=8)

    y = b_s[...] * gb_s[...]
    out = jnp.dot(y.astype(BF16), wout_ref[...], preferred_element_type=F32)
    o_ref[...] = jnp.swapaxes(x + out.reshape(ts, nb, d) * g1[None], 0, 1)


def _rglru_call(x, mod0, g, w_in, conv_w, conv_b, wr, br, wi, bi, lam, w_out, *, ts=128, rc=256):
    nb, s, d = x.shape
    assert nb == SUBLANES, "the recurrence keeps the batch on the sublane axis"
    r = ts * nb
    dr = lam.shape[-1]
    tail = (conv_w.shape[0] - 1) * nb
    row = lambda v: v.reshape(1, -1)
    args = (x, mod0, row(g), w_in.astype(BF16), conv_w, row(conv_b), wr.astype(BF16), row(br),
            wi.astype(BF16), row(bi), row(lam), w_out.astype(BF16))
    x_spec = pl.BlockSpec((nb, ts, d), lambda i: (0, i, 0))
    return pl.pallas_call(
        functools.partial(_rglru_kernel, rc=rc),
        grid=(s // ts,),
        in_specs=[x_spec] + [_const_spec(a.shape) for a in args[1:]],
        out_specs=x_spec,
        out_shape=jax.ShapeDtypeStruct((nb, s, d), F32),
        scratch_shapes=[pltpu.VMEM((r, d), BF16), pltpu.VMEM((r, dr), F32),
                        pltpu.VMEM((r + tail, dr), F32), pltpu.VMEM((r, dr), F32),
                        pltpu.VMEM((r, dr), F32), pltpu.VMEM((nb, dr), F32)],
        compiler_params=pltpu.CompilerParams(
            dimension_semantics=("arbitrary",), vmem_limit_bytes=VMEM_LIMIT),
        name="rglru_mixer",
    )(*args)


def _ffn_kernel(*refs, has_proj, fc):
    if has_proj:
        x_ref, mod_ref, g_ref, w1_ref, w2_ref, o_in_ref, wo_ref, out_ref = refs
    else:
        x_ref, mod_ref, g_ref, w1_ref, w2_ref, out_ref = refs
    d = x_ref.shape[1]
    dff = w1_ref.shape[1]
    x = x_ref[...]
    if has_proj:
        g1 = mod_ref[:, 2 * d:3 * d]
        o = jnp.concatenate([o_in_ref[p] for p in range(o_in_ref.shape[0])], axis=-1)
        x = x + g1 * jnp.dot(o, wo_ref[...], preferred_element_type=F32)
    sh2 = mod_ref[:, 3 * d:4 * d]
    sc2 = mod_ref[:, 4 * d:5 * d]
    g2 = mod_ref[:, 5 * d:6 * d]
    hb = (_rmsnorm(x, g_ref[...]) * (1.0 + sc2) + sh2).astype(BF16)
    acc = jnp.zeros(x.shape, F32)
    for c in range(dff // fc):
        hid = jnp.dot(hb, w1_ref[:, c * fc:(c + 1) * fc], preferred_element_type=F32)
        hid = jnp.square(jnp.maximum(hid, 0.0))
        acc = acc + jnp.dot(hid.astype(BF16), w2_ref[c * fc:(c + 1) * fc, :],
                            preferred_element_type=F32)
    out_ref[...] = x + g2 * acc


def _ffn_call(x, mod_l, g, w1, w2, o_pairs=None, w_o=None, *, rf=512, fc=1024):
    b, s, d = x.shape
    has_proj = o_pairs is not None
    args = [x, mod_l, g.reshape(1, -1), w1.astype(BF16), w2.astype(BF16)]
    in_specs = [pl.BlockSpec((None, rf, d), lambda bi, i: (bi, i, 0)),
                pl.BlockSpec((None, 1, mod_l.shape[-1]), lambda bi, i: (bi, 0, 0)),
                _const_spec((1, d)), _const_spec(w1.shape), _const_spec(w2.shape)]
    if has_proj:
        npair = o_pairs.shape[1]
        args += [o_pairs, w_o.astype(BF16)]
        in_specs += [pl.BlockSpec((None, npair, rf, LANES), lambda bi, i: (bi, 0, i, 0)),
                     _const_spec(w_o.shape)]
    return pl.pallas_call(
        functools.partial(_ffn_kernel, has_proj=has_proj, fc=fc),
        grid=(b, s // rf),
        in_specs=in_specs,
        out_specs=pl.BlockSpec((None, rf, d), lambda bi, i: (bi, i, 0)),
        out_shape=jax.ShapeDtypeStruct((b, s, d), F32),
        compiler_params=pltpu.CompilerParams(
            dimension_semantics=("arbitrary", "arbitrary"), vmem_limit_bytes=VMEM_LIMIT),
        name="ffn_proj" if has_proj else "ffn",
    )(*args)


def _rot_half(x, lane):
    return jnp.where((lane & 63) < 32, pltpu.roll(x, 96, 1), pltpu.roll(x, 32, 1))


def _dsa_in_kernel(x_ref, mod_ref, g_ref, w_ref, qg_ref, kg_ref, e_ref, cos_ref, sin_ref,
                   qt_ref, qit_ref, kk_ref, vt_ref, wit_ref, *, nq, nqi, hd, idx_heads):
    d = x_ref.shape[1]
    tq = x_ref.shape[0]
    x = x_ref[...]
    sh1 = mod_ref[:, 0:d]
    sc1 = mod_ref[:, d:2 * d]
    hb = (_rmsnorm(x, g_ref[...]) * (1.0 + sc1) + sh1).astype(BF16)
    lane = lax.broadcasted_iota(I32, (tq, LANES), 1)
    left = lane < hd
    cos = cos_ref[...]
    sin = sin_ref[...]
    qg = qg_ref[...]
    e = e_ref[...]
    qscale = hd ** -0.5

    def q_tile(p, t):
        t2 = t * t
        hi = t2.astype(BF16)
        lo = (t2 - hi.astype(F32)).astype(BF16)
        ss = (jnp.dot(hi, e, preferred_element_type=F32) + jnp.dot(lo, e, preferred_element_type=F32))
        tn = t * lax.rsqrt(ss * (1.0 / hd) + RMS_EPS) * qg
        qt_ref[p] = ((tn * cos + _rot_half(tn, lane) * sin) * (qscale * LOG2E)).T.astype(BF16)

    def qi_tile(p, t):
        qit_ref[p] = ((t * cos + _rot_half(t, lane) * sin) * qscale).T.astype(BF16)

    def kk_tile(kk):
        ssk = jnp.sum(jnp.where(left, kk * kk, 0.0), axis=-1, keepdims=True)
        fac = jnp.where(left, lax.rsqrt(ssk * (1.0 / hd) + RMS_EPS) * kg_ref[...], 1.0)
        kn = kk * fac
        kr = kn * cos + _rot_half(kn, lane) * sin
        k_l = jnp.where(left, kr, 0.0)
        ki_r = jnp.where(left, 0.0, kr)
        kk_ref[0] = k_l.astype(BF16)
        kk_ref[1] = pltpu.roll(k_l, hd, 1).astype(BF16)
        kk_ref[2] = pltpu.roll(ki_r, hd, 1).astype(BF16)
        kk_ref[3] = ki_r.astype(BF16)

    def vw_tile(vw):
        vt_ref[...] = jnp.where(left, vw, 1.0).T.astype(BF16)
        wit_ref[...] = (vw * (idx_heads ** -0.5)).T[hd:hd + idx_heads, :]

    def tile(p, t):
        if p < nq:
            q_tile(p, t)
        elif p < nq + nqi:
            qi_tile(p - nq, t)
        elif p == nq + nqi:
            kk_tile(t)
        else:
            vw_tile(t)

    gw = 2 * LANES
    ngroups = w_ref.shape[1] // gw

    def proj(gi):
        return jnp.dot(hb, w_ref[:, gi * gw:(gi + 1) * gw], preferred_element_type=F32)

    u_next = proj(0)
    for gi in range(ngroups):
        u = u_next
        if gi + 1 < ngroups:
            u_next = proj(gi + 1)
        for half in range(gw // LANES):
            tile(gi * (gw // LANES) + half, u[:, half * LANES:(half + 1) * LANES])


def _dsa_in_call(x, mod_l, g, w_in, q_g, k_g, *, n_heads, hd, idx_heads, tq=512):
    b, s, d = x.shape
    assert 2 * hd == LANES, "two heads per lane tile"
    nq = n_heads * hd // LANES
    nqi = idx_heads * hd // LANES
    o_k = n_heads * hd
    o_v = o_k + hd
    o_qi = o_v + hd
    o_ki = o_qi + idx_heads * hd
    o_wi = o_ki + hd
    pad = LANES - hd - idx_heads
    w = jnp.concatenate([w_in[:, :o_k], w_in[:, o_qi:o_ki], w_in[:, o_k:o_v], w_in[:, o_ki:o_wi],
                         w_in[:, o_v:o_qi], w_in[:, o_wi:], jnp.zeros((d, pad), w_in.dtype)],
                        axis=1).astype(BF16)
    ncol = w.shape[1]
    inv = ROPE_THETA ** (-jnp.arange(0, hd, 2, dtype=F32) / hd)
    ang = jnp.arange(s, dtype=F32)[:, None] * inv[None, :]
    cos_t = jnp.tile(jnp.cos(ang), (1, 4))
    sin_h = jnp.sin(ang)
    sin_t = jnp.tile(jnp.concatenate([-sin_h, sin_h], axis=1), (1, 2))
    head_of_lane = jnp.arange(LANES) // hd
    e = (head_of_lane[:, None] == head_of_lane[None, :]).astype(BF16)
    qg = jnp.tile(q_g, 2).reshape(1, LANES)
    kg = jnp.concatenate([k_g, jnp.ones((hd,), k_g.dtype)]).reshape(1, LANES)

    return pl.pallas_call(
        functools.partial(_dsa_in_kernel, nq=nq, nqi=nqi, hd=hd, idx_heads=idx_heads),
        grid=(b, s // tq),
        in_specs=[pl.BlockSpec((None, tq, d), lambda bi, i: (bi, i, 0)),
                  pl.BlockSpec((None, 1, mod_l.shape[-1]), lambda bi, i: (bi, 0, 0)),
                  _const_spec((1, d)), _const_spec((d, ncol)), _const_spec((1, LANES)),
                  _const_spec((1, LANES)), _const_spec((LANES, LANES)),
                  pl.BlockSpec((tq, LANES), lambda bi, i: (i, 0)),
                  pl.BlockSpec((tq, LANES), lambda bi, i: (i, 0))],
        out_specs=[pl.BlockSpec((None, nq, LANES, tq), lambda bi, i: (bi, 0, 0, i)),
                   pl.BlockSpec((None, nqi, LANES, tq), lambda bi, i: (bi, 0, 0, i)),
                   pl.BlockSpec((None, 4, tq, LANES), lambda bi, i: (bi, 0, i, 0)),
                   pl.BlockSpec((None, LANES, tq), lambda bi, i: (bi, 0, i)),
                   pl.BlockSpec((None, idx_heads, tq), lambda bi, i: (bi, 0, i))],
        out_shape=[jax.ShapeDtypeStruct((b, nq, LANES, s), BF16),
                   jax.ShapeDtypeStruct((b, nqi, LANES, s), BF16),
                   jax.ShapeDtypeStruct((b, 4, s, LANES), BF16),
                   jax.ShapeDtypeStruct((b, LANES, s), BF16),
                   jax.ShapeDtypeStruct((b, idx_heads, s), F32)],
        compiler_params=pltpu.CompilerParams(
            dimension_semantics=("arbitrary", "arbitrary"), vmem_limit_bytes=VMEM_LIMIT),
        name="dsa_in",
    )(x, mod_l, g.reshape(1, -1), w, qg, kg, e, cos_t, sin_t)


def _dsa_attn_kernel(bnd_ref, qit0_ref, wit0_ref, qitn_ref, witn_ref, qt_ref, kk_ref, vt_ref, o_ref,
                     sct_s, mask_s, st_s, mm_s, m_s, acc_s, *, kc, ks, topk, hd, idx_heads, group,
                     tie_from, max_groups, n_blocks, head_passes, sl):
    npair, _, qb = qt_ref.shape
    nh = 2 * npair
    nsub = SUBLANES
    i = pl.program_id(1)
    has_sel = i < n_blocks
    has_att = i >= 1
    nk = (i * qb) // kc + 1
    nkp = ((i - 1) * qb) // kc + 1
    slot = lax.rem(i, 2)
    mask_w = mask_s.at[slot]
    mask_r = mask_s.at[1 - slot]
    sct_c = sct_s.at[slot]
    sct_n = sct_s.at[1 - slot]
    kf = float(topk)
    fast = bnd_ref[0] <= EXP_SAFE_BOUND
    merged = jnp.logical_and(jnp.logical_and(has_sel, has_att), fast)

    def slab(src, c, t):
        return src[pl.ds(pl.multiple_of(c * kc + t * sl, sl), sl), :]

    def finish(acc, op, red):
        n = sl // nsub
        acc = acc.reshape(n, nsub, qb)
        r = acc[0]
        for t in range(1, n):
            r = op(r, acc[t])
        return jnp.broadcast_to(red(r, axis=0, keepdims=True), (nsub, qb))

    def reduce_keys(src, fn, op, red, init):
        def body(ci, acc):
            for t in range(kc // sl):
                acc = op(acc, fn(slab(src, ci, t), ci, t))
            return acc
        return finish(lax.fori_loop(0, nk, body, jnp.full((sl, qb), init, F32)), op, red)

    def rows(x):
        return jnp.concatenate([x] * (sl // nsub), axis=0)

    def score_chunk(qit_ref, wit_ref, blk, dst, ci, mm):
        qpos = blk * qb + lax.broadcasted_iota(I32, (ks, qb), 1)
        krow = lax.broadcasted_iota(I32, (ks, qb), 0)
        mx, mn = mm
        for sub in range(kc // ks):
            k0 = pl.multiple_of(ci * kc + sub * ks, ks)
            kis = (kk_ref[2, pl.ds(k0, ks), :], kk_ref[3, pl.ds(k0, ks), :])
            acc = jnp.zeros((ks, qb), F32)
            for hh in range(idx_heads):
                lg = jnp.dot(kis[hh % 2], qit_ref[hh // 2], preferred_element_type=F32)
                acc = acc + wit_ref[hh:hh + 1, :] * jnp.maximum(lg, 0.0)
            causal = (k0 + krow) <= qpos
            lo_v = jnp.where(causal, acc, -jnp.inf)
            hi_v = jnp.where(causal, acc, jnp.inf)
            dst[pl.ds(k0, ks), :] = lo_v
            for t in range(ks // sl):
                mx = jnp.maximum(mx, lo_v[t * sl:(t + 1) * sl])
                mn = jnp.minimum(mn, hi_v[t * sl:(t + 1) * sl])
        return mx, mn

    mm0 = (jnp.full((sl, qb), -jnp.inf, F32), jnp.full((sl, qb), jnp.inf, F32))

    def save_range(mm):
        mm_s[0] = finish(mm[0], jnp.maximum, jnp.max)
        mm_s[1] = finish(mm[1], jnp.minimum, jnp.min)

    @pl.when(i == 0)
    def _():
        save_range(lax.fori_loop(
            0, nk, functools.partial(score_chunk, qit0_ref, wit0_ref, 0, sct_c), mm0))

    @pl.when(has_sel)
    def _():
        mx = mm_s[0]
        st_s[0] = mm_s[1]
        st_s[1] = mx + (jnp.abs(mx) * 1e-6 + 1e-30)
        st_s[2] = (i * qb + lax.broadcasted_iota(I32, (nsub, qb), 1) + 1).astype(F32)
        st_s[3] = jnp.zeros((nsub, qb), F32)

    def count_step(st):
        c, lo, hi, clo, chi, acc = st
        mid = 0.5 * lo + 0.5 * hi
        midr = rows(mid)
        for t in range(kc // sl):
            acc = acc + jnp.where(slab(sct_c, c, t) >= midr, 1.0, 0.0)
        last = c == nk - 1
        tot = finish(acc, jnp.add, jnp.sum)
        ge = tot >= kf
        lo = jnp.where(last, jnp.where(ge, mid, lo), lo)
        clo = jnp.where(last, jnp.where(ge, tot, clo), clo)
        hi = jnp.where(last, jnp.where(ge, hi, mid), hi)
        chi = jnp.where(last, jnp.where(ge, chi, tot), chi)
        return (jnp.where(last, 0, c + 1), lo, hi, clo, chi, jnp.where(last, 0.0, acc))

    def scores(ci, h):
        kk = kk_ref[h % 2, pl.ds(pl.multiple_of(ci * kc, kc), kc), :]
        return jnp.dot(kk, qt_ref[h // 2], preferred_element_type=F32)

    def pv_step(ci, st, use_max, passes):
        c0 = pl.multiple_of(ci * kc, kc)
        vt = vt_ref[:, pl.ds(c0, kc)]
        msk = mask_r[pl.ds(c0, kc), :]
        ahead = [scores(ci, h) for h in range(min(QK_LOOKAHEAD, nh))]
        for h in range(nh):
            s_cur = ahead.pop(0)
            if h + QK_LOOKAHEAD < nh:
                ahead.append(scores(ci, h + QK_LOOKAHEAD))
            if use_max:
                pr = jnp.where(msk > 0, jnp.exp2(s_cur - m_s[h][0:1, :]), 0.0).astype(BF16)
            else:
                pr = jnp.exp2(s_cur).astype(BF16) * msk
            acc_s[h] += jnp.dot(vt, pr, preferred_element_type=F32)
            for _ in range(passes[h] if passes else 0):
                st = count_step(st)
        return st

    @pl.when(has_att)
    def _():
        acc_s[...] = jnp.zeros(acc_s.shape, F32)

    @pl.when(merged)
    def _():
        st = (jnp.int32(0), st_s[0], st_s[1], st_s[2], st_s[3], jnp.zeros((sl, qb), F32))
        st = lax.fori_loop(0, nkp, functools.partial(pv_step, use_max=False, passes=head_passes), st)
        st_s[0], st_s[1], st_s[2], st_s[3] = st[1], st[2], st[3], st[4]

    @pl.when(jnp.logical_and(has_att, jnp.logical_and(fast, jnp.logical_not(has_sel))))
    def _():
        lax.fori_loop(0, nkp, functools.partial(pv_step, use_max=False, passes=None), 0)

    @pl.when(jnp.logical_and(has_att, jnp.logical_not(fast)))
    def _():
        def max_step(ci, _):
            @pl.when(ci == 0)
            def _():
                m_s[...] = jnp.full(m_s.shape, NEG_BIG, F32)
            msk = mask_r[pl.ds(pl.multiple_of(ci * kc, kc), kc), :]
            for h in range(nh):
                mx = jnp.max(jnp.where(msk > 0, scores(ci, h), NEG_BIG), axis=0, keepdims=True)
                m_s[h] = jnp.maximum(m_s[h], jnp.broadcast_to(mx, (nsub, qb)))
            return 0

        lax.fori_loop(0, nkp, max_step, 0)
        lax.fori_loop(0, nkp, functools.partial(pv_step, use_max=True, passes=None), 0)

    @pl.when(has_att)
    def _():
        for p in range(npair):
            a0, a1 = acc_s[2 * p], acc_s[2 * p + 1]
            ot = jnp.concatenate([a0[0:hd] / a0[hd:2 * hd], a1[0:hd] / a1[hd:2 * hd]], axis=0)
            o_ref[p] = ot.T.astype(BF16)

    @pl.when(has_sel)
    def _():
        def count_ge(thr):
            thr = rows(thr)
            return reduce_keys(sct_c, lambda blk, c, t: jnp.where(blk >= thr, 1.0, 0.0),
                               jnp.add, jnp.sum, 0.0)

        def bisect(_, st):
            lo, hi, clo, chi = st
            mid = 0.5 * lo + 0.5 * hi
            c = count_ge(mid)
            ge = c >= kf
            return (jnp.where(ge, mid, lo), jnp.where(ge, hi, mid),
                    jnp.where(ge, c, clo), jnp.where(ge, chi, c))

        def open_rows(clo, tied):
            return jnp.max(jnp.where(clo > kf, 1.0 - tied, 0.0)) > 0.0

        g0 = jnp.where(merged, tie_from // group, 0)

        def search_group(st):
            g, _, lo, hi, clo, chi, tied = st
            lo, hi, clo, chi = lax.fori_loop(0, group, bisect, (lo, hi, clo, chi))

            def tie_check(tied):
                lor, hir = rows(lo), rows(hi)
                a = reduce_keys(sct_c, lambda blk, c, t: jnp.where(blk >= lor, blk, jnp.inf),
                                jnp.minimum, jnp.min, jnp.inf)
                b = reduce_keys(sct_c, lambda blk, c, t: jnp.where(blk < hir, blk, -jnp.inf),
                                jnp.maximum, jnp.max, -jnp.inf)
                return jnp.where(jnp.logical_and(clo > kf, a == b), 1.0, tied)

            tied = lax.cond(jnp.logical_and((g + 1) * group >= tie_from, open_rows(clo, tied)),
                            tie_check, lambda t: t, tied)
            return (g + 1, open_rows(clo, tied).astype(I32), lo, hi, clo, chi, tied)

        zero = jnp.zeros((nsub, qb), F32)
        _, _, lo, hi, clo, chi, _ = lax.while_loop(
            lambda st: jnp.logical_and(st[1] > 0, st[0] < max_groups), search_group,
            (g0, open_rows(st_s[2], zero).astype(I32), st_s[0], st_s[1], st_s[2], st_s[3], zero))

        def tile_k(x):
            return jnp.concatenate([x] * (ks // nsub), axis=0)

        lok, hik, free = tile_k(lo), tile_k(hi), tile_k(kf - chi)
        tri = jnp.where(lax.broadcasted_iota(I32, (ks, ks), 1) <= lax.broadcasted_iota(I32, (ks, ks), 0),
                        1.0, 0.0).astype(BF16)

        def mask_chunk(ci, carry):
            subs = []
            for t in range(kc // ks):
                k0 = pl.multiple_of(ci * kc + t * ks, ks)
                blk = sct_c[pl.ds(k0, ks), :]
                top = blk >= hik
                tie = jnp.where(top, 0.0, jnp.where(blk >= lok, 1.0, 0.0))
                subs.append((k0, top, tie, jnp.dot(tri, tie.astype(BF16), preferred_element_type=F32)))
            for k0, top, tie, rank in subs:
                keep = jnp.where(rank + tile_k(carry) <= free, tie, 0.0)
                mask_w[pl.ds(k0, ks), :] = jnp.where(top, 1.0, keep).astype(BF16)
                carry = carry + jnp.broadcast_to(rank[ks - 1:ks, :], (nsub, qb))
            return carry

        zero8 = jnp.zeros((nsub, qb), F32)
        has_next = i + 1 < n_blocks

        @pl.when(has_next)
        def _():
            nkn = ((i + 1) * qb) // kc + 1
            nxt = functools.partial(score_chunk, qitn_ref, witn_ref, i + 1, sct_n)

            def both(ci, st):
                return (mask_chunk(ci, st[0]), nxt(ci, st[1]))

            _, mm = lax.fori_loop(0, nk, both, (zero8, mm0))
            save_range(lax.fori_loop(nk, nkn, nxt, mm))

        @pl.when(jnp.logical_not(has_next))
        def _():
            lax.fori_loop(0, nk, mask_chunk, zero8)


def _dsa_attn_call(bnd, qt, qit, kk, vt, wit, *, hd, idx_heads, topk, qb=256, kc=512, ks=128,
                   sl=32):
    b, npair, _, s = qt.shape
    nqi = qit.shape[1]
    assert kc % qb == 0 and s % kc == 0 and kc % ks == 0 and ks % sl == 0 and sl % SUBLANES == 0
    nb = s // qb
    head_passes = tuple(2 if h % 4 == 0 else 1 for h in range(2 * npair))
    assert 2 * hd == LANES
    prev = lambda i: jnp.maximum(i - 1, 0)
    cur = lambda i: jnp.minimum(i, nb - 1)
    nxt = lambda i: jnp.minimum(i + 1, nb - 1)
    return pl.pallas_call(
        functools.partial(_dsa_attn_kernel, kc=kc, ks=ks, topk=topk, hd=hd, idx_heads=idx_heads,
                          group=2, tie_from=16, max_groups=192, n_blocks=nb,
                          head_passes=head_passes, sl=sl),
        grid=(b, nb + 1),
        in_specs=[pl.BlockSpec(memory_space=pltpu.SMEM),
                  pl.BlockSpec((None, nqi, LANES, qb), lambda bi, i: (bi, 0, 0, cur(i))),
                  pl.BlockSpec((None, idx_heads, qb), lambda bi, i: (bi, 0, cur(i))),
                  pl.BlockSpec((None, nqi, LANES, qb), lambda bi, i: (bi, 0, 0, nxt(i))),
                  pl.BlockSpec((None, idx_heads, qb), lambda bi, i: (bi, 0, nxt(i))),
                  pl.BlockSpec((None, npair, LANES, qb), lambda bi, i: (bi, 0, 0, prev(i))),
                  pl.BlockSpec((None, 4, s, LANES), lambda bi, i: (bi, 0, 0, 0)),
                  pl.BlockSpec((None, LANES, s), lambda bi, i: (bi, 0, 0))],
        out_specs=pl.BlockSpec((None, npair, qb, LANES), lambda bi, i: (bi, 0, prev(i), 0)),
        out_shape=jax.ShapeDtypeStruct((b, npair, s, LANES), BF16),
        scratch_shapes=[pltpu.VMEM((2, s, qb), F32),
                        pltpu.VMEM((2, s, qb), BF16), pltpu.VMEM((4, SUBLANES, qb), F32),
                        pltpu.VMEM((2, SUBLANES, qb), F32),
                        pltpu.VMEM((2 * npair, SUBLANES, qb), F32),
                        pltpu.VMEM((2 * npair, LANES, qb), F32)],
        compiler_params=pltpu.CompilerParams(
            dimension_semantics=("arbitrary", "arbitrary"), vmem_limit_bytes=VMEM_LIMIT),
        name="dsa_attn",
    )(bnd, qit, wit, qit, wit, qt, kk, vt)


def kernel(x, c, norm_mix_g, norm_ffn_g, ada_w, ada_b, a_w_in, a_conv_w, a_conv_b, a_gate_r_w,
           a_gate_r_b, a_gate_i_w, a_gate_i_b, a_lambda, a_w_out, b_w_in, b_q_norm_g, b_k_norm_g,
           b_w_out, ffn_w1, ffn_w2):
    b, s, d = x.shape
    hd = b_q_norm_g.shape[-1]
    n_heads = b_w_out.shape[1] // hd
    idx_heads = (b_w_in.shape[-1] - n_heads * hd - 3 * hd) // (hd + 1)
    topk = min(TOPK_MAX, s // 4)
    depth = ada_w.shape[0]

    mod = _mod_call(c, ada_w, ada_b)
    mod_rows = mod.reshape(depth, b, 1, mod.shape[-1])

    for i in range(depth):
        j = i // 2
        if i % 2 == 0:
            x = _rglru_call(x, mod[i], norm_mix_g[i], a_w_in[j], a_conv_w[j], a_conv_b[j],
                            a_gate_r_w[j], a_gate_r_b[j], a_gate_i_w[j], a_gate_i_b[j],
                            a_lambda[j], a_w_out[j])
            x = _ffn_call(x, mod_rows[i], norm_ffn_g[i], ffn_w1[i], ffn_w2[i])
        else:
            qt, qit, kk, vt, wit = _dsa_in_call(x, mod_rows[i], norm_mix_g[i], b_w_in[j],
                                                 b_q_norm_g[j], b_k_norm_g[j], n_heads=n_heads,
                                                 hd=hd, idx_heads=idx_heads)
            bnd = (1.02 * hd * hd ** -0.5) * jnp.max(jnp.abs(b_q_norm_g[j])) * jnp.max(jnp.abs(b_k_norm_g[j]))
            o = _dsa_attn_call(bnd.reshape(1), qt, qit, kk, vt, wit, hd=hd, idx_heads=idx_heads,
                               topk=topk)
            x = _ffn_call(x, mod_rows[i], norm_ffn_g[i], ffn_w1[i], ffn_w2[i], o_pairs=o,
                          w_o=b_w_out[j])
    return x
```

```python
import functools

import jax
import jax.numpy as jnp
from jax import lax
from jax.experimental import pallas as pl
from jax.experimental.pallas import tpu as pltpu

F32 = jnp.float32
BF16 = jnp.bfloat16
I32 = jnp.int32

RMS_EPS = 1e-6
ROPE_THETA = 10000.0
LRU_C = 8.0
TOPK_MAX = 256

LANES = 128
SUBLANES = 8
VMEM_LIMIT = 56 * 1024 * 1024

NEG_BIG = -1e30
EXP_SAFE_BOUND = 70.0
LOG2E = 1.4426950408889634
QK_LOOKAHEAD = 2
GELU_C0 = 0.7978845608028654
GELU_C1 = 0.044715


def _const_spec(shape):
    nd = len(shape)
    return pl.BlockSpec(shape, lambda *_: (0,) * nd, pipeline_mode=pl.Buffered(1))


def _rmsnorm(x, g):
    return x * lax.rsqrt(jnp.mean(x * x, axis=-1, keepdims=True) + RMS_EPS) * g


def _mod_kernel(c_ref, w_ref, b_ref, o_ref):
    c = c_ref[...]
    cond = c * jax.nn.sigmoid(c)
    o_ref[0] = jnp.dot(cond.astype(BF16), w_ref[0].astype(BF16),
                       preferred_element_type=F32) + b_ref[0]


def _mod_call(c, ada_w, ada_b, *, tn=1536):
    depth, d, n = ada_w.shape
    b = c.shape[0]
    return pl.pallas_call(
        _mod_kernel,
        grid=(depth, n // tn),
        in_specs=[pl.BlockSpec((b, d), lambda l, j: (0, 0)),
                  pl.BlockSpec((1, d, tn), lambda l, j: (l, 0, j)),
                  pl.BlockSpec((1, 1, tn), lambda l, j: (l, 0, j))],
        out_specs=pl.BlockSpec((1, b, tn), lambda l, j: (l, 0, j)),
        out_shape=jax.ShapeDtypeStruct((depth, b, n), F32),
        compiler_params=pltpu.CompilerParams(
            dimension_semantics=("arbitrary", "arbitrary"), vmem_limit_bytes=VMEM_LIMIT),
        name="adaln_mod",
    )(c, ada_w, ada_b.reshape(depth, 1, n))


def _rglru_kernel(x_ref, mod_ref, g_ref, win_ref, cw_ref, cb_ref, wr_ref, br_ref, wi_ref, bi_ref,
                  lam_ref, wout_ref, o_ref, h_s, gb_s, xbuf, a_s, b_s, out_s, hc, *, rc):
    nb, ts, d = x_ref.shape
    r = ts * nb
    dr = lam_ref.shape[1]
    nblk, blk, _ = wr_ref.shape
    cwid = cw_ref.shape[0]
    tail = (cwid - 1) * nb

    @pl.when(pl.program_id(0) == 0)
    def _():
        xbuf[0:tail, :] = jnp.zeros((tail, dr), F32)
        hc[...] = jnp.zeros((nb, dr), F32)

    x = jnp.swapaxes(x_ref[...], 0, 1)
    sh1 = mod_ref[:, 0:d]
    sc1 = mod_ref[:, d:2 * d]
    g1 = mod_ref[:, 2 * d:3 * d]
    h_s[...] = (_rmsnorm(x, g_ref[...]) * (1.0 + sc1)[None] + sh1[None]).reshape(r, d).astype(BF16)

    sp_h = (0.5 * LRU_C) * jax.nn.softplus(-lam_ref[...])
    br_h = 0.5 * br_ref[...]
    bi_h = 0.5 * bi_ref[...]

    def in_proj(c, n):
        rows = slice(c * rc, (c + 1) * rc)
        hb = h_s[rows, :]
        ux = jnp.dot(hb, win_ref[:, n * blk:(n + 1) * blk], preferred_element_type=F32)
        ug = jnp.dot(hb, win_ref[:, dr + n * blk:dr + (n + 1) * blk], preferred_element_type=F32)
        xbuf[tail + c * rc:tail + (c + 1) * rc, n * blk:(n + 1) * blk] = ux
        return ug

    def conv_and_gate_dots(c, n):
        r0 = c * rc
        cs = slice(n * blk, (n + 1) * blk)
        xc = cb_ref[:, cs] + cw_ref[0:1, cs] * xbuf[r0:r0 + rc, cs]
        for k in range(1, cwid):
            xc = xc + cw_ref[k:k + 1, cs] * xbuf[r0 + k * nb:r0 + k * nb + rc, cs]
        xn = xc.astype(BF16)
        return xc, (jnp.dot(xn, wr_ref[n], preferred_element_type=F32),
                    jnp.dot(xn, wi_ref[n], preferred_element_type=F32))

    def gate_tail(c, n, xc, pre, ug):
        ro = slice(c * rc, (c + 1) * rc)
        cs = slice(n * blk, (n + 1) * blk)
        p = sp_h[:, cs] + sp_h[:, cs] * jnp.tanh(0.5 * pre[0] + br_h[:, cs])
        ig = 0.5 + 0.5 * jnp.tanh(0.5 * pre[1] + bi_h[:, cs])
        a = jnp.exp2(p * (-LOG2E))
        a_s[ro, cs] = a
        z = jnp.tanh(p) * (a * a + 1.0)
        b_s[ro, cs] = jnp.where(z > 0.0, z * lax.rsqrt(z), 0.0) * (ig * xc)
        t = jnp.tanh(ug * (GELU_C0 + (GELU_C0 * GELU_C1) * (ug * ug)))
        hu = 0.5 * ug
        gb_s[ro, cs] = hu + hu * t

    def scan_and_project(c, hprev):
        for t in range(c * rc // nb, (c + 1) * rc // nb):
            hprev = a_s[t * nb:(t + 1) * nb, :] * hprev + b_s[t * nb:(t + 1) * nb, :]
            b_s[t * nb:(t + 1) * nb, :] = hprev
        ro = slice(c * rc, (c + 1) * rc)
        y = b_s[ro, :] * gb_s[ro, :]
        out_s[ro, :] = jnp.dot(y.astype(BF16), wout_ref[...], preferred_element_type=F32)
        return hprev

    units = [(c, n) for c in range(r // rc) for n in range(nblk)]
    ug = in_proj(*units[0])
    hcur = hc[...]
    for k, (c, n) in enumerate(units):
        ug_next = in_proj(*units[k + 1]) if k + 1 < len(units) else None
        xc, pre = conv_and_gate_dots(c, n)
        gate_tail(c, n, xc, pre, ug)
        ug = ug_next
        if n == nblk - 1:
            hcur = scan_and_project(c, hcur)
    hc[...] = hcur
    xbuf[0:tail, :] = xbuf[r:r + tail, :]
    o_ref[...] = jnp.swapaxes(x + out_s[...].reshape(ts, nb, d) * g1[None], 0, 1)


def _rglru_call(x, mod0, g, w_in, conv_w, conv_b, wr, br, wi, bi, lam, w_out, *, ts=128, rc=256):
    nb, s, d = x.shape
    assert nb == SUBLANES, "the recurrence keeps the batch on the sublane axis"
    r = ts * nb
    dr = lam.shape[-1]
    tail = (conv_w.shape[0] - 1) * nb
    row = lambda v: v.reshape(1, -1)
    args = (x, mod0, row(g), w_in.astype(BF16), conv_w, row(conv_b), wr.astype(BF16), row(br),
            wi.astype(BF16), row(bi), row(lam), w_out.astype(BF16))
    x_spec = pl.BlockSpec((nb, ts, d), lambda i: (0, i, 0))
    return pl.pallas_call(
        functools.partial(_rglru_kernel, rc=rc),
        grid=(s // ts,),
        in_specs=[x_spec] + [_const_spec(a.shape) for a in args[1:]],
        out_specs=x_spec,
        out_shape=jax.ShapeDtypeStruct((nb, s, d), F32),
        scratch_shapes=[pltpu.VMEM((r, d), BF16), pltpu.VMEM((r, dr), F32),
                        pltpu.VMEM((r + tail, dr), F32), pltpu.VMEM((r, dr), F32),
                        pltpu.VMEM((r, dr), F32), pltpu.VMEM((r, d), F32),
                        pltpu.VMEM((nb, dr), F32)],
        compiler_params=pltpu.CompilerParams(
            dimension_semantics=("arbitrary",), vmem_limit_bytes=VMEM_LIMIT),
        name="rglru_mixer",
    )(*args)


def _ffn_kernel(*refs, has_proj, fc):
    if has_proj:
        x_ref, mod_ref, g_ref, w1_ref, w2_ref, o_in_ref, wo_ref, out_ref = refs
    else:
        x_ref, mod_ref, g_ref, w1_ref, w2_ref, out_ref = refs
    d = x_ref.shape[1]
    dff = w1_ref.shape[1]
    x = x_ref[...]
    if has_proj:
        g1 = mod_ref[:, 2 * d:3 * d]
        o = jnp.concatenate([o_in_ref[p] for p in range(o_in_ref.shape[0])], axis=-1)
        x = x + g1 * jnp.dot(o, wo_ref[...], preferred_element_type=F32)
    sh2 = mod_ref[:, 3 * d:4 * d]
    sc2 = mod_ref[:, 4 * d:5 * d]
    g2 = mod_ref[:, 5 * d:6 * d]
    hb = (_rmsnorm(x, g_ref[...]) * (1.0 + sc2) + sh2).astype(BF16)
    acc = jnp.zeros(x.shape, F32)
    for c in range(dff // fc):
        hid = jnp.dot(hb, w1_ref[:, c * fc:(c + 1) * fc], preferred_element_type=F32)
        hid = jnp.square(jnp.maximum(hid, 0.0))
        acc = acc + jnp.dot(hid.astype(BF16), w2_ref[c * fc:(c + 1) * fc, :],
                            preferred_element_type=F32)
    out_ref[...] = x + g2 * acc


def _ffn_call(x, mod_l, g, w1, w2, o_pairs=None, w_o=None, *, rf=512, fc=1024):
    b, s, d = x.shape
    has_proj = o_pairs is not None
    args = [x, mod_l, g.reshape(1, -1), w1.astype(BF16), w2.astype(BF16)]
    in_specs = [pl.BlockSpec((None, rf, d), lambda bi, i: (bi, i, 0)),
                pl.BlockSpec((None, 1, mod_l.shape[-1]), lambda bi, i: (bi, 0, 0)),
                _const_spec((1, d)), _const_spec(w1.shape), _const_spec(w2.shape)]
    if has_proj:
        npair = o_pairs.shape[1]
        args += [o_pairs, w_o.astype(BF16)]
        in_specs += [pl.BlockSpec((None, npair, rf, LANES), lambda bi, i: (bi, 0, i, 0)),
                     _const_spec(w_o.shape)]
    return pl.pallas_call(
        functools.partial(_ffn_kernel, has_proj=has_proj, fc=fc),
        grid=(b, s // rf),
        in_specs=in_specs,
        out_specs=pl.BlockSpec((None, rf, d), lambda bi, i: (bi, i, 0)),
        out_shape=jax.ShapeDtypeStruct((b, s, d), F32),
        compiler_params=pltpu.CompilerParams(
            dimension_semantics=("arbitrary", "arbitrary"), vmem_limit_bytes=VMEM_LIMIT),
        name="ffn_proj" if has_proj else "ffn",
    )(*args)


def _rot_half(x):
    return pltpu.roll(x, LANES // 2, 1)


def _dsa_in_kernel(x_ref, mod_ref, g_ref, w_ref, qg_ref, kg_ref, e_ref, cos_ref, sin_ref,
                   qt_ref, qit_ref, kk_ref, vt_ref, wit_ref, *, nq, nqi, hd, idx_heads):
    d = x_ref.shape[1]
    tq = x_ref.shape[0]
    x = x_ref[...]
    sh1 = mod_ref[:, 0:d]
    sc1 = mod_ref[:, d:2 * d]
    hb = (_rmsnorm(x, g_ref[...]) * (1.0 + sc1) + sh1).astype(BF16)
    lane = lax.broadcasted_iota(I32, (tq, LANES), 1)
    left = lane < hd
    head_a = (lane & (hd // 2)) == 0
    cos = cos_ref[...]
    sin = sin_ref[...]
    qg = qg_ref[...]
    e = e_ref[...]
    qscale = hd ** -0.5

    def q_tile(p, t):
        t2 = t * t
        hi = t2.astype(BF16)
        lo = (t2 - hi.astype(F32)).astype(BF16)
        ss = (jnp.dot(hi, e, preferred_element_type=F32) + jnp.dot(lo, e, preferred_element_type=F32))
        tn = t * lax.rsqrt(ss * (1.0 / hd) + RMS_EPS) * qg
        qt_ref[p] = ((tn * cos + _rot_half(tn) * sin) * (qscale * LOG2E)).T.astype(BF16)

    def qi_tile(p, t):
        qit_ref[p] = ((t * cos + _rot_half(t) * sin) * qscale).T.astype(BF16)

    def kk_tile(kk):
        ssk = jnp.sum(jnp.where(head_a, kk * kk, 0.0), axis=-1, keepdims=True)
        fac = jnp.where(head_a, lax.rsqrt(ssk * (1.0 / hd) + RMS_EPS) * kg_ref[...], 1.0)
        kn = kk * fac
        kr = kn * cos + _rot_half(kn) * sin
        k_a = jnp.where(head_a, kr, 0.0)
        ki_b = jnp.where(head_a, 0.0, kr)
        kk_ref[0] = k_a.astype(BF16)
        kk_ref[1] = pltpu.roll(k_a, hd // 2, 1).astype(BF16)
        kk_ref[2] = pltpu.roll(ki_b, LANES - hd // 2, 1).astype(BF16)
        kk_ref[3] = ki_b.astype(BF16)

    def vw_tile(vw):
        vt_ref[...] = jnp.where(left, vw, 1.0).T.astype(BF16)
        wit_ref[...] = (vw * (idx_heads ** -0.5)).T[hd:hd + idx_heads, :]

    def tile(p, t):
        if p < nq:
            q_tile(p, t)
        elif p < nq + nqi:
            qi_tile(p - nq, t)
        elif p == nq + nqi:
            kk_tile(t)
        else:
            vw_tile(t)

    gw = 2 * LANES
    ngroups = w_ref.shape[1] // gw

    def proj(gi):
        return jnp.dot(hb, w_ref[:, gi * gw:(gi + 1) * gw], preferred_element_type=F32)

    u_next = proj(0)
    for gi in range(ngroups):
        u = u_next
        if gi + 1 < ngroups:
            u_next = proj(gi + 1)
        for half in range(gw // LANES):
            tile(gi * (gw // LANES) + half, u[:, half * LANES:(half + 1) * LANES])


def _dsa_in_call(x, mod_l, g, w_in, q_g, k_g, *, n_heads, hd, idx_heads, tq=512):
    b, s, d = x.shape
    assert 2 * hd == LANES, "two heads per lane tile"
    nq = n_heads * hd // LANES
    nqi = idx_heads * hd // LANES
    o_k = n_heads * hd
    o_v = o_k + hd
    o_qi = o_v + hd
    o_ki = o_qi + idx_heads * hd
    o_wi = o_ki + hd
    pad = LANES - hd - idx_heads
    w = jnp.concatenate([w_in[:, :o_k], w_in[:, o_qi:o_ki], w_in[:, o_k:o_v], w_in[:, o_ki:o_wi],
                         w_in[:, o_v:o_qi], w_in[:, o_wi:], jnp.zeros((d, pad), w_in.dtype)],
                        axis=1).astype(BF16)
    half = hd // 2
    lane_src = jnp.concatenate([jnp.arange(0, half), jnp.arange(hd, hd + half),
                                jnp.arange(half, hd), jnp.arange(hd + half, LANES)])
    ncol = w.shape[1]
    nrot = nq + nqi + 1
    w = w.reshape(d, ncol // LANES, LANES)
    w = jnp.concatenate([w[:, :nrot, lane_src], w[:, nrot:, :]], axis=1).reshape(d, ncol)
    inv = ROPE_THETA ** (-jnp.arange(0, hd, 2, dtype=F32) / hd)
    ang = jnp.arange(s, dtype=F32)[:, None] * inv[None, :]
    cos_t = jnp.tile(jnp.cos(ang), (1, 4))
    sin_h = jnp.sin(ang)
    sin_t = jnp.concatenate([-sin_h, -sin_h, sin_h, sin_h], axis=1)
    head_of_lane = (jnp.arange(LANES) // half) % 2
    e = (head_of_lane[:, None] == head_of_lane[None, :]).astype(BF16)
    qg = jnp.tile(q_g, 2)[lane_src].reshape(1, LANES)
    kg = jnp.concatenate([k_g, jnp.ones((hd,), k_g.dtype)])[lane_src].reshape(1, LANES)

    return pl.pallas_call(
        functools.partial(_dsa_in_kernel, nq=nq, nqi=nqi, hd=hd, idx_heads=idx_heads),
        grid=(b, s // tq),
        in_specs=[pl.BlockSpec((None, tq, d), lambda bi, i: (bi, i, 0)),
                  pl.BlockSpec((None, 1, mod_l.shape[-1]), lambda bi, i: (bi, 0, 0)),
                  _const_spec((1, d)), _const_spec((d, ncol)), _const_spec((1, LANES)),
                  _const_spec((1, LANES)), _const_spec((LANES, LANES)),
                  pl.BlockSpec((tq, LANES), lambda bi, i: (i, 0)),
                  pl.BlockSpec((tq, LANES), lambda bi, i: (i, 0))],
        out_specs=[pl.BlockSpec((None, nq, LANES, tq), lambda bi, i: (bi, 0, 0, i)),
                   pl.BlockSpec((None, nqi, LANES, tq), lambda bi, i: (bi, 0, 0, i)),
                   pl.BlockSpec((None, 4, tq, LANES), lambda bi, i: (bi, 0, i, 0)),
                   pl.BlockSpec((None, LANES, tq), lambda bi, i: (bi, 0, i)),
                   pl.BlockSpec((None, idx_heads, tq), lambda bi, i: (bi, 0, i))],
        out_shape=[jax.ShapeDtypeStruct((b, nq, LANES, s), BF16),
                   jax.ShapeDtypeStruct((b, nqi, LANES, s), BF16),
                   jax.ShapeDtypeStruct((b, 4, s, LANES), BF16),
                   jax.ShapeDtypeStruct((b, LANES, s), BF16),
                   jax.ShapeDtypeStruct((b, idx_heads, s), F32)],
        compiler_params=pltpu.CompilerParams(
            dimension_semantics=("arbitrary", "arbitrary"), vmem_limit_bytes=VMEM_LIMIT),
        name="dsa_in",
    )(x, mod_l, g.reshape(1, -1), w, qg, kg, e, cos_t, sin_t)


def _dsa_attn_kernel(bnd_ref, qit0_ref, wit0_ref, qitn_ref, witn_ref, qt_ref, kk_ref, vt_ref, o_ref,
                     sct_s, mask_s, st_s, mm_s, m_s, acc_s, *, kc, ks, topk, hd, idx_heads, group,
                     tie_from, max_groups, n_blocks, head_passes, sl):
    npair, _, qb = qt_ref.shape
    nh = 2 * npair
    nsub = SUBLANES
    i = pl.program_id(1)
    has_sel = i < n_blocks
    has_att = i >= 1
    nk = (i * qb) // kc + 1
    nkp = ((i - 1) * qb) // kc + 1
    slot = lax.rem(i, 2)
    mask_w = mask_s.at[slot]
    mask_r = mask_s.at[1 - slot]
    sct_c = sct_s.at[slot]
    sct_n = sct_s.at[1 - slot]
    kf = float(topk)
    fast = bnd_ref[0] <= EXP_SAFE_BOUND
    merged = jnp.logical_and(jnp.logical_and(has_sel, has_att), fast)

    def slab(src, c, t):
        return src[pl.ds(pl.multiple_of(c * kc + t * sl, sl), sl), :]

    def finish(acc, op, red):
        n = sl // nsub
        acc = acc.reshape(n, nsub, qb)
        r = acc[0]
        for t in range(1, n):
            r = op(r, acc[t])
        return jnp.broadcast_to(red(r, axis=0, keepdims=True), (nsub, qb))

    def reduce_keys(src, fn, op, red, init):
        def body(ci, acc):
            for t in range(kc // sl):
                acc = op(acc, fn(slab(src, ci, t), ci, t))
            return acc
        return finish(lax.fori_loop(0, nk, body, jnp.full((sl, qb), init, F32)), op, red)

    def rows(x):
        return jnp.concatenate([x] * (sl // nsub), axis=0)

    def score_chunk(qit_ref, wit_ref, blk, dst, ci, mm):
        qpos = blk * qb + lax.broadcasted_iota(I32, (ks, qb), 1)
        krow = lax.broadcasted_iota(I32, (ks, qb), 0)
        mx, mn = mm
        for sub in range(kc // ks):
            k0 = pl.multiple_of(ci * kc + sub * ks, ks)
            kis = (kk_ref[2, pl.ds(k0, ks), :], kk_ref[3, pl.ds(k0, ks), :])
            acc = jnp.zeros((ks, qb), F32)
            for hh in range(idx_heads):
                lg = jnp.dot(kis[hh % 2], qit_ref[hh // 2], preferred_element_type=F32)
                acc = acc + wit_ref[hh:hh + 1, :] * jnp.maximum(lg, 0.0)
            causal = (k0 + krow) <= qpos
            lo_v = jnp.where(causal, acc, -jnp.inf)
            hi_v = jnp.where(causal, acc, jnp.inf)
            dst[pl.ds(k0, ks), :] = lo_v
            for t in range(ks // sl):
                mx = jnp.maximum(mx, lo_v[t * sl:(t + 1) * sl])
                mn = jnp.minimum(mn, hi_v[t * sl:(t + 1) * sl])
        return mx, mn

    mm0 = (jnp.full((sl, qb), -jnp.inf, F32), jnp.full((sl, qb), jnp.inf, F32))

    def save_range(mm):
        mm_s[0] = finish(mm[0], jnp.maximum, jnp.max)
        mm_s[1] = finish(mm[1], jnp.minimum, jnp.min)

    @pl.when(i == 0)
    def _():
        save_range(lax.fori_loop(
            0, nk, functools.partial(score_chunk, qit0_ref, wit0_ref, 0, sct_c), mm0))

    @pl.when(has_sel)
    def _():
        mx = mm_s[0]
        st_s[0] = mm_s[1]
        st_s[1] = mx + (jnp.abs(mx) * 1e-6 + 1e-30)
        st_s[2] = (i * qb + lax.broadcasted_iota(I32, (nsub, qb), 1) + 1).astype(F32)
        st_s[3] = jnp.zeros((nsub, qb), F32)

    def count_step(st):
        c, lo, hi, clo, chi, acc = st
        mid = 0.5 * lo + 0.5 * hi
        midr = rows(mid)
        for t in range(kc // sl):
            acc = acc + jnp.where(slab(sct_c, c, t) >= midr, 1.0, 0.0)
        last = c == nk - 1
        tot = finish(acc, jnp.add, jnp.sum)
        ge = tot >= kf
        lo = jnp.where(last, jnp.where(ge, mid, lo), lo)
        clo = jnp.where(last, jnp.where(ge, tot, clo), clo)
        hi = jnp.where(last, jnp.where(ge, hi, mid), hi)
        chi = jnp.where(last, jnp.where(ge, chi, tot), chi)
        return (jnp.where(last, 0, c + 1), lo, hi, clo, chi, jnp.where(last, 0.0, acc))

    def scores(ci, h):
        kk = kk_ref[h % 2, pl.ds(pl.multiple_of(ci * kc, kc), kc), :]
        return jnp.dot(kk, qt_ref[h // 2], preferred_element_type=F32)

    def pv_step(ci, st, use_max, passes):
        c0 = pl.multiple_of(ci * kc, kc)
        vt = vt_ref[:, pl.ds(c0, kc)]
        msk = mask_r[pl.ds(c0, kc), :]
        ahead = [scores(ci, h) for h in range(min(QK_LOOKAHEAD, nh))]
        for h in range(nh):
            s_cur = ahead.pop(0)
            if h + QK_LOOKAHEAD < nh:
                ahead.append(scores(ci, h + QK_LOOKAHEAD))
            if use_max:
                pr = jnp.where(msk > 0, jnp.exp2(s_cur - m_s[h][0:1, :]), 0.0).astype(BF16)
            else:
                pr = jnp.exp2(s_cur).astype(BF16) * msk
            acc_s[h] += jnp.dot(vt, pr, preferred_element_type=F32)
            for _ in range(passes[h] if passes else 0):
                st = count_step(st)
        return st

    @pl.when(has_att)
    def _():
        acc_s[...] = jnp.zeros(acc_s.shape, F32)

    @pl.when(merged)
    def _():
        st = (jnp.int32(0), st_s[0], st_s[1], st_s[2], st_s[3], jnp.zeros((sl, qb), F32))
        st = lax.fori_loop(0, nkp, functools.partial(pv_step, use_max=False, passes=head_passes), st)
        st_s[0], st_s[1], st_s[2], st_s[3] = st[1], st[2], st[3], st[4]

    @pl.when(jnp.logical_and(has_att, jnp.logical_and(fast, jnp.logical_not(has_sel))))
    def _():
        lax.fori_loop(0, nkp, functools.partial(pv_step, use_max=False, passes=None), 0)

    @pl.when(jnp.logical_and(has_att, jnp.logical_not(fast)))
    def _():
        def max_step(ci, _):
            @pl.when(ci == 0)
            def _():
                m_s[...] = jnp.full(m_s.shape, NEG_BIG, F32)
            msk = mask_r[pl.ds(pl.multiple_of(ci * kc, kc), kc), :]
            for h in range(nh):
                mx = jnp.max(jnp.where(msk > 0, scores(ci, h), NEG_BIG), axis=0, keepdims=True)
                m_s[h] = jnp.maximum(m_s[h], jnp.broadcast_to(mx, (nsub, qb)))
            return 0

        lax.fori_loop(0, nkp, max_step, 0)
        lax.fori_loop(0, nkp, functools.partial(pv_step, use_max=True, passes=None), 0)

    @pl.when(has_att)
    def _():
        for p in range(npair):
            a0, a1 = acc_s[2 * p], acc_s[2 * p + 1]
            ot = jnp.concatenate([a0[0:hd] / a0[hd:2 * hd], a1[0:hd] / a1[hd:2 * hd]], axis=0)
            o_ref[p] = ot.T.astype(BF16)

    @pl.when(has_sel)
    def _():
        def count_ge(thr):
            thr = rows(thr)
            return reduce_keys(sct_c, lambda blk, c, t: jnp.where(blk >= thr, 1.0, 0.0),
                               jnp.add, jnp.sum, 0.0)

        def bisect(_, st):
            lo, hi, clo, chi = st
            mid = 0.5 * lo + 0.5 * hi
            c = count_ge(mid)
            ge = c >= kf
            return (jnp.where(ge, mid, lo), jnp.where(ge, hi, mid),
                    jnp.where(ge, c, clo), jnp.where(ge, chi, c))

        def open_rows(clo, tied):
            return jnp.max(jnp.where(clo > kf, 1.0 - tied, 0.0)) > 0.0

        g0 = jnp.where(merged, tie_from // group, 0)

        def search_group(st):
            g, _, lo, hi, clo, chi, tied = st
            lo, hi, clo, chi = lax.fori_loop(0, group, bisect, (lo, hi, clo, chi))

            def tie_check(tied):
                lor, hir = rows(lo), rows(hi)
                a = reduce_keys(sct_c, lambda blk, c, t: jnp.where(blk >= lor, blk, jnp.inf),
                                jnp.minimum, jnp.min, jnp.inf)
                b = reduce_keys(sct_c, lambda blk, c, t: jnp.where(blk < hir, blk, -jnp.inf),
                                jnp.maximum, jnp.max, -jnp.inf)
                return jnp.where(jnp.logical_and(clo > kf, a == b), 1.0, tied)

            tied = lax.cond(jnp.logical_and((g + 1) * group >= tie_from, open_rows(clo, tied)),
                            tie_check, lambda t: t, tied)
            return (g + 1, open_rows(clo, tied).astype(I32), lo, hi, clo, chi, tied)

        zero = jnp.zeros((nsub, qb), F32)
        _, _, lo, hi, clo, chi, _ = lax.while_loop(
            lambda st: jnp.logical_and(st[1] > 0, st[0] < max_groups), search_group,
            (g0, open_rows(st_s[2], zero).astype(I32), st_s[0], st_s[1], st_s[2], st_s[3], zero))

        def tile_k(x):
            return jnp.concatenate([x] * (ks // nsub), axis=0)

        lok, hik, free = tile_k(lo), tile_k(hi), tile_k(kf - chi)
        tri = jnp.where(lax.broadcasted_iota(I32, (ks, ks), 1) <= lax.broadcasted_iota(I32, (ks, ks), 0),
                        1.0, 0.0).astype(BF16)

        def mask_chunk(ci, carry):
            subs = []
            for t in range(kc // ks):
                k0 = pl.multiple_of(ci * kc + t * ks, ks)
                blk = sct_c[pl.ds(k0, ks), :]
                top = blk >= hik
                tie = jnp.where(top, 0.0, jnp.where(blk >= lok, 1.0, 0.0))
                subs.append((k0, top, tie, jnp.dot(tri, tie.astype(BF16), preferred_element_type=F32)))
            for k0, top, tie, rank in subs:
                keep = jnp.where(rank + tile_k(carry) <= free, tie, 0.0)
                mask_w[pl.ds(k0, ks), :] = jnp.where(top, 1.0, keep).astype(BF16)
                carry = carry + jnp.broadcast_to(rank[ks - 1:ks, :], (nsub, qb))
            return carry

        zero8 = jnp.zeros((nsub, qb), F32)
        has_next = i + 1 < n_blocks

        @pl.when(has_next)
        def _():
            nkn = ((i + 1) * qb) // kc + 1
            nxt = functools.partial(score_chunk, qitn_ref, witn_ref, i + 1, sct_n)

            def both(ci, st):
                return (mask_chunk(ci, st[0]), nxt(ci, st[1]))

            _, mm = lax.fori_loop(0, nk, both, (zero8, mm0))
            save_range(lax.fori_loop(nk, nkn, nxt, mm))

        @pl.when(jnp.logical_not(has_next))
        def _():
            lax.fori_loop(0, nk, mask_chunk, zero8)


def _dsa_attn_call(bnd, qt, qit, kk, vt, wit, *, hd, idx_heads, topk, qb=256, kc=512, ks=128,
                   sl=32):
    b, npair, _, s = qt.shape
    nqi = qit.shape[1]
    assert kc % qb == 0 and s % kc == 0 and kc % ks == 0 and ks % sl == 0 and sl % SUBLANES == 0
    nb = s // qb
    head_passes = tuple(2 if h % 4 == 0 else 1 for h in range(2 * npair))
    assert 2 * hd == LANES
    prev = lambda i: jnp.maximum(i - 1, 0)
    cur = lambda i: jnp.minimum(i, nb - 1)
    nxt = lambda i: jnp.minimum(i + 1, nb - 1)
    return pl.pallas_call(
        functools.partial(_dsa_attn_kernel, kc=kc, ks=ks, topk=topk, hd=hd, idx_heads=idx_heads,
                          group=4, tie_from=16, max_groups=96, n_blocks=nb,
                          head_passes=head_passes, sl=sl),
        grid=(b, nb + 1),
        in_specs=[pl.BlockSpec(memory_space=pltpu.SMEM),
                  pl.BlockSpec((None, nqi, LANES, qb), lambda bi, i: (bi, 0, 0, cur(i))),
                  pl.BlockSpec((None, idx_heads, qb), lambda bi, i: (bi, 0, cur(i))),
                  pl.BlockSpec((None, nqi, LANES, qb), lambda bi, i: (bi, 0, 0, nxt(i))),
                  pl.BlockSpec((None, idx_heads, qb), lambda bi, i: (bi, 0, nxt(i))),
                  pl.BlockSpec((None, npair, LANES, qb), lambda bi, i: (bi, 0, 0, prev(i))),
                  pl.BlockSpec((None, 4, s, LANES), lambda bi, i: (bi, 0, 0, 0)),
                  pl.BlockSpec((None, LANES, s), lambda bi, i: (bi, 0, 0))],
        out_specs=pl.BlockSpec((None, npair, qb, LANES), lambda bi, i: (bi, 0, prev(i), 0)),
        out_shape=jax.ShapeDtypeStruct((b, npair, s, LANES), BF16),
        scratch_shapes=[pltpu.VMEM((2, s, qb), F32),
                        pltpu.VMEM((2, s, qb), BF16), pltpu.VMEM((4, SUBLANES, qb), F32),
                        pltpu.VMEM((2, SUBLANES, qb), F32),
                        pltpu.VMEM((2 * npair, SUBLANES, qb), F32),
                        pltpu.VMEM((2 * npair, LANES, qb), F32)],
        compiler_params=pltpu.CompilerParams(
            dimension_semantics=("arbitrary", "arbitrary"), vmem_limit_bytes=VMEM_LIMIT),
        name="dsa_attn",
    )(bnd, qit, wit, qit, wit, qt, kk, vt)


def kernel(x, c, norm_mix_g, norm_ffn_g, ada_w, ada_b, a_w_in, a_conv_w, a_conv_b, a_gate_r_w,
           a_gate_r_b, a_gate_i_w, a_gate_i_b, a_lambda, a_w_out, b_w_in, b_q_norm_g, b_k_norm_g,
           b_w_out, ffn_w1, ffn_w2):
    b, s, d = x.shape
    hd = b_q_norm_g.shape[-1]
    n_heads = b_w_out.shape[1] // hd
    idx_heads = (b_w_in.shape[-1] - n_heads * hd - 3 * hd) // (hd + 1)
    topk = min(TOPK_MAX, s // 4)
    depth = ada_w.shape[0]

    mod = _mod_call(c, ada_w, ada_b)
    mod_rows = mod.reshape(depth, b, 1, mod.shape[-1])

    for i in range(depth):
        j = i // 2
        if i % 2 == 0:
            x = _rglru_call(x, mod[i], norm_mix_g[i], a_w_in[j], a_conv_w[j], a_conv_b[j],
                            a_gate_r_w[j], a_gate_r_b[j], a_gate_i_w[j], a_gate_i_b[j],
                            a_lambda[j], a_w_out[j])
            x = _ffn_call(x, mod_rows[i], norm_ffn_g[i], ffn_w1[i], ffn_w2[i])
        else:
            qt, qit, kk, vt, wit = _dsa_in_call(x, mod_rows[i], norm_mix_g[i], b_w_in[j],
                                                 b_q_norm_g[j], b_k_norm_g[j], n_heads=n_heads,
                                                 hd=hd, idx_heads=idx_heads)
            bnd = (1.02 * hd * hd ** -0.5) * jnp.max(jnp.abs(b_q_norm_g[j])) * jnp.max(jnp.abs(b_k_norm_g[j]))
            o = _dsa_attn_call(bnd.reshape(1), qt, qit, kk, vt, wit, hd=hd, idx_heads=idx_heads,
                               topk=topk)
            x = _ffn_call(x, mod_rows[i], norm_ffn_g[i], ffn_w1[i], ffn_w2[i], o_pairs=o,
                          w_o=b_w_out[j])
    return x
```

```python
import functools

import jax
import jax.numpy as jnp
from jax import lax
from jax.experimental import pallas as pl
from jax.experimental.pallas import tpu as pltpu

F32 = jnp.float32
BF16 = jnp.bfloat16
I32 = jnp.int32

RMS_EPS = 1e-6
ROPE_THETA = 10000.0
LRU_C = 8.0
TOPK_MAX = 256

LANES = 128
SUBLANES = 8
VMEM_LIMIT = 56 * 1024 * 1024

NEG_BIG = -1e30
EXP_SAFE_BOUND = 70.0
LOG2E = 1.4426950408889634
QK_LOOKAHEAD = 2
GELU_C0 = 0.7978845608028654
GELU_C1 = 0.044715


def _const_spec(shape):
    nd = len(shape)
    return pl.BlockSpec(shape, lambda *_: (0,) * nd, pipeline_mode=pl.Buffered(1))


def _rmsnorm(x, g):
    return x * lax.rsqrt(jnp.mean(x * x, axis=-1, keepdims=True) + RMS_EPS) * g


def _mod_kernel(c_ref, w_ref, b_ref, o_ref):
    c = c_ref[...]
    cond = c * jax.nn.sigmoid(c)
    o_ref[0] = jnp.dot(cond.astype(BF16), w_ref[0].astype(BF16),
                       preferred_element_type=F32) + b_ref[0]


def _mod_call(c, ada_w, ada_b, *, tn=1536):
    depth, d, n = ada_w.shape
    b = c.shape[0]
    return pl.pallas_call(
        _mod_kernel,
        grid=(depth, n // tn),
        in_specs=[pl.BlockSpec((b, d), lambda l, j: (0, 0)),
                  pl.BlockSpec((1, d, tn), lambda l, j: (l, 0, j)),
                  pl.BlockSpec((1, 1, tn), lambda l, j: (l, 0, j))],
        out_specs=pl.BlockSpec((1, b, tn), lambda l, j: (l, 0, j)),
        out_shape=jax.ShapeDtypeStruct((depth, b, n), F32),
        compiler_params=pltpu.CompilerParams(
            dimension_semantics=("arbitrary", "arbitrary"), vmem_limit_bytes=VMEM_LIMIT),
        name="adaln_mod",
    )(c, ada_w, ada_b.reshape(depth, 1, n))


def _rglru_kernel(x_ref, mod_ref, g_ref, win_ref, cw_ref, cb_ref, wr_ref, br_ref, wi_ref, bi_ref,
                  lam_ref, wout_ref, o_ref, h_s, gb_s, xbuf, a_s, b_s, hc, *, rc):
    nb, ts, d = x_ref.shape
    r = ts * nb
    dr = lam_ref.shape[1]
    nblk, blk, _ = wr_ref.shape
    cwid = cw_ref.shape[0]
    tail = (cwid - 1) * nb

    @pl.when(pl.program_id(0) == 0)
    def _():
        xbuf[0:tail, :] = jnp.zeros((tail, dr), F32)
        hc[...] = jnp.zeros((nb, dr), F32)

    x = jnp.swapaxes(x_ref[...], 0, 1)
    sh1 = mod_ref[:, 0:d]
    sc1 = mod_ref[:, d:2 * d]
    g1 = mod_ref[:, 2 * d:3 * d]
    h_s[...] = (_rmsnorm(x, g_ref[...]) * (1.0 + sc1)[None] + sh1[None]).reshape(r, d).astype(BF16)

    sp_h = (0.5 * LRU_C) * jax.nn.softplus(-lam_ref[...])
    br_h = 0.5 * br_ref[...]
    bi_h = 0.5 * bi_ref[...]

    def in_proj(c, n):
        rows = slice(c * rc, (c + 1) * rc)
        hb = h_s[rows, :]
        ux = jnp.dot(hb, win_ref[:, n * blk:(n + 1) * blk], preferred_element_type=F32)
        ug = jnp.dot(hb, win_ref[:, dr + n * blk:dr + (n + 1) * blk], preferred_element_type=F32)
        xbuf[tail + c * rc:tail + (c + 1) * rc, n * blk:(n + 1) * blk] = ux
        return ug

    def conv_and_gate_dots(c, n):
        r0 = c * rc
        cs = slice(n * blk, (n + 1) * blk)
        xc = cb_ref[:, cs] + cw_ref[0:1, cs] * xbuf[r0:r0 + rc, cs]
        for k in range(1, cwid):
            xc = xc + cw_ref[k:k + 1, cs] * xbuf[r0 + k * nb:r0 + k * nb + rc, cs]
        xn = xc.astype(BF16)
        return xc, (jnp.dot(xn, wr_ref[n], preferred_element_type=F32),
                    jnp.dot(xn, wi_ref[n], preferred_element_type=F32))

    def gate_tail(c, n, xc, pre, ug):
        ro = slice(c * rc, (c + 1) * rc)
        cs = slice(n * blk, (n + 1) * blk)
        p = sp_h[:, cs] + sp_h[:, cs] * jnp.tanh(0.5 * pre[0] + br_h[:, cs])
        ig = 0.5 + 0.5 * jnp.tanh(0.5 * pre[1] + bi_h[:, cs])
        a = jnp.exp2(p * (-LOG2E))
        a_s[ro, cs] = a
        z = jnp.tanh(p) * (a * a + 1.0)
        b_s[ro, cs] = jnp.where(z > 0.0, z * lax.rsqrt(z), 0.0) * (ig * xc)
        t = jnp.tanh(ug * (GELU_C0 + (GELU_C0 * GELU_C1) * (ug * ug)))
        hu = 0.5 * ug
        gb_s[ro, cs] = hu + hu * t

    units = [(c, n) for c in range(r // rc) for n in range(nblk)]
    ug = in_proj(*units[0])
    for k, (c, n) in enumerate(units):
        ug_next = in_proj(*units[k + 1]) if k + 1 < len(units) else None
        xc, pre = conv_and_gate_dots(c, n)
        gate_tail(c, n, xc, pre, ug)
        ug = ug_next
    xbuf[0:tail, :] = xbuf[r:r + tail, :]

    def step(t, hprev):
        r0 = pl.multiple_of(t * nb, nb)
        hnew = a_s[pl.ds(r0, nb), :] * hprev + b_s[pl.ds(r0, nb), :]
        b_s[pl.ds(r0, nb), :] = hnew
        return hnew

    hc[...] = lax.fori_loop(0, ts, step, hc[...], unroll=8)

    y = b_s[...] * gb_s[...]
    out = jnp.dot(y.astype(BF16), wout_ref[...], preferred_element_type=F32)
    o_ref[...] = jnp.swapaxes(x + out.reshape(ts, nb, d) * g1[None], 0, 1)


def _rglru_call(x, mod0, g, w_in, conv_w, conv_b, wr, br, wi, bi, lam, w_out, *, ts=128, rc=256):
    nb, s, d = x.shape
    assert nb == SUBLANES, "the recurrence keeps the batch on the sublane axis"
    r = ts * nb
    dr = lam.shape[-1]
    tail = (conv_w.shape[0] - 1) * nb
    row = lambda v: v.reshape(1, -1)
    args = (x, mod0, row(g), w_in.astype(BF16), conv_w, row(conv_b), wr.astype(BF16), row(br),
            wi.astype(BF16), row(bi), row(lam), w_out.astype(BF16))
    x_spec = pl.BlockSpec((nb, ts, d), lambda i: (0, i, 0))
    return pl.pallas_call(
        functools.partial(_rglru_kernel, rc=rc),
        grid=(s // ts,),
        in_specs=[x_spec] + [_const_spec(a.shape) for a in args[1:]],
        out_specs=x_spec,
        out_shape=jax.ShapeDtypeStruct((nb, s, d), F32),
        scratch_shapes=[pltpu.VMEM((r, d), BF16), pltpu.VMEM((r, dr), F32),
                        pltpu.VMEM((r + tail, dr), F32), pltpu.VMEM((r, dr), F32),
                        pltpu.VMEM((r, dr), F32), pltpu.VMEM((nb, dr), F32)],
        compiler_params=pltpu.CompilerParams(
            dimension_semantics=("arbitrary",), vmem_limit_bytes=VMEM_LIMIT),
        name="rglru_mixer",
    )(*args)


def _ffn_kernel(*refs, has_proj, fc):
    if has_proj:
        x_ref, mod_ref, g_ref, w1_ref, w2_ref, o_in_ref, wo_ref, out_ref = refs
    else:
        x_ref, mod_ref, g_ref, w1_ref, w2_ref, out_ref = refs
    d = x_ref.shape[1]
    dff = w1_ref.shape[1]
    x = x_ref[...]
    if has_proj:
        g1 = mod_ref[:, 2 * d:3 * d]
        o = jnp.concatenate([o_in_ref[p] for p in range(o_in_ref.shape[0])], axis=-1)
        x = x + g1 * jnp.dot(o, wo_ref[...], preferred_element_type=F32)
    sh2 = mod_ref[:, 3 * d:4 * d]
    sc2 = mod_ref[:, 4 * d:5 * d]
    g2 = mod_ref[:, 5 * d:6 * d]
    hb = (_rmsnorm(x, g_ref[...]) * (1.0 + sc2) + sh2).astype(BF16)
    acc = jnp.zeros(x.shape, F32)
    for c in range(dff // fc):
        hid = jnp.dot(hb, w1_ref[:, c * fc:(c + 1) * fc], preferred_element_type=F32)
        hid = jnp.square(jnp.maximum(hid, 0.0))
        acc = acc + jnp.dot(hid.astype(BF16), w2_ref[c * fc:(c + 1) * fc, :],
                            preferred_element_type=F32)
    out_ref[...] = x + g2 * acc


def _ffn_call(x, mod_l, g, w1, w2, o_pairs=None, w_o=None, *, rf=512, fc=1024):
    b, s, d = x.shape
    has_proj = o_pairs is not None
    args = [x, mod_l, g.reshape(1, -1), w1.astype(BF16), w2.astype(BF16)]
    in_specs = [pl.BlockSpec((None, rf, d), lambda bi, i: (bi, i, 0)),
                pl.BlockSpec((None, 1, mod_l.shape[-1]), lambda bi, i: (bi, 0, 0)),
                _const_spec((1, d)), _const_spec(w1.shape), _const_spec(w2.shape)]
    if has_proj:
        npair = o_pairs.shape[1]
        args += [o_pairs, w_o.astype(BF16)]
        in_specs += [pl.BlockSpec((None, npair, rf, LANES), lambda bi, i: (bi, 0, i, 0)),
                     _const_spec(w_o.shape)]
    return pl.pallas_call(
        functools.partial(_ffn_kernel, has_proj=has_proj, fc=fc),
        grid=(b, s // rf),
        in_specs=in_specs,
        out_specs=pl.BlockSpec((None, rf, d), lambda bi, i: (bi, i, 0)),
        out_shape=jax.ShapeDtypeStruct((b, s, d), F32),
        compiler_params=pltpu.CompilerParams(
            dimension_semantics=("arbitrary", "arbitrary"), vmem_limit_bytes=VMEM_LIMIT),
        name="ffn_proj" if has_proj else "ffn",
    )(*args)


def _rot_half(x):
    return pltpu.roll(x, LANES // 2, 1)


def _dsa_in_kernel(x_ref, mod_ref, g_ref, w_ref, qg_ref, kg_ref, e_ref, cos_ref, sin_ref,
                   qt_ref, qit_ref, kk_ref, vt_ref, wit_ref, *, nq, nqi, hd, idx_heads):
    d = x_ref.shape[1]
    tq = x_ref.shape[0]
    x = x_ref[...]
    sh1 = mod_ref[:, 0:d]
    sc1 = mod_ref[:, d:2 * d]
    hb = (_rmsnorm(x, g_ref[...]) * (1.0 + sc1) + sh1).astype(BF16)
    lane = lax.broadcasted_iota(I32, (tq, LANES), 1)
    left = lane < hd
    head_a = (lane & (hd // 2)) == 0
    cos = cos_ref[...]
    sin = sin_ref[...]
    qg = qg_ref[...]
    e = e_ref[...]
    qscale = hd ** -0.5

    def q_tile(p, t):
        t2 = t * t
        hi = t2.astype(BF16)
        lo = (t2 - hi.astype(F32)).astype(BF16)
        ss = (jnp.dot(hi, e, preferred_element_type=F32) + jnp.dot(lo, e, preferred_element_type=F32))
        tn = t * lax.rsqrt(ss * (1.0 / hd) + RMS_EPS) * qg
        qt_ref[p] = ((tn * cos + _rot_half(tn) * sin) * (qscale * LOG2E)).T.astype(BF16)

    def qi_tile(p, t):
        qit_ref[p] = ((t * cos + _rot_half(t) * sin) * qscale).T.astype(BF16)

    def kk_tile(kk):
        ssk = jnp.sum(jnp.where(head_a, kk * kk, 0.0), axis=-1, keepdims=True)
        fac = jnp.where(head_a, lax.rsqrt(ssk * (1.0 / hd) + RMS_EPS) * kg_ref[...], 1.0)
        kn = kk * fac
        kr = kn * cos + _rot_half(kn) * sin
        k_a = jnp.where(head_a, kr, 0.0)
        ki_b = jnp.where(head_a, 0.0, kr)
        kk_ref[0] = k_a.astype(BF16)
        kk_ref[1] = pltpu.roll(k_a, hd // 2, 1).astype(BF16)
        kk_ref[2] = pltpu.roll(ki_b, LANES - hd // 2, 1).astype(BF16)
        kk_ref[3] = ki_b.astype(BF16)

    def vw_tile(vw):
        vt_ref[...] = jnp.where(left, vw, 1.0).T.astype(BF16)
        wit_ref[...] = (vw * (idx_heads ** -0.5)).T[hd:hd + idx_heads, :]

    def tile(p, t):
        if p < nq:
            q_tile(p, t)
        elif p < nq + nqi:
            qi_tile(p - nq, t)
        elif p == nq + nqi:
            kk_tile(t)
        else:
            vw_tile(t)

    gw = 2 * LANES
    ngroups = w_ref.shape[1] // gw

    def proj(gi):
        return jnp.dot(hb, w_ref[:, gi * gw:(gi + 1) * gw], preferred_element_type=F32)

    u_next = proj(0)
    for gi in range(ngroups):
        u = u_next
        if gi + 1 < ngroups:
            u_next = proj(gi + 1)
        for half in range(gw // LANES):
            tile(gi * (gw // LANES) + half, u[:, half * LANES:(half + 1) * LANES])


def _dsa_in_call(x, mod_l, g, w_in, q_g, k_g, *, n_heads, hd, idx_heads, tq=512):
    b, s, d = x.shape
    assert 2 * hd == LANES, "two heads per lane tile"
    nq = n_heads * hd // LANES
    nqi = idx_heads * hd // LANES
    o_k = n_heads * hd
    o_v = o_k + hd
    o_qi = o_v + hd
    o_ki = o_qi + idx_heads * hd
    o_wi = o_ki + hd
    pad = LANES - hd - idx_heads
    w = jnp.concatenate([w_in[:, :o_k], w_in[:, o_qi:o_ki], w_in[:, o_k:o_v], w_in[:, o_ki:o_wi],
                         w_in[:, o_v:o_qi], w_in[:, o_wi:], jnp.zeros((d, pad), w_in.dtype)],
                        axis=1).astype(BF16)
    half = hd // 2
    lane_src = jnp.concatenate([jnp.arange(0, half), jnp.arange(hd, hd + half),
                                jnp.arange(half, hd), jnp.arange(hd + half, LANES)])
    ncol = w.shape[1]
    nrot = nq + nqi + 1
    w = w.reshape(d, ncol // LANES, LANES)
    w = jnp.concatenate([w[:, :nrot, lane_src], w[:, nrot:, :]], axis=1).reshape(d, ncol)
    inv = ROPE_THETA ** (-jnp.arange(0, hd, 2, dtype=F32) / hd)
    ang = jnp.arange(s, dtype=F32)[:, None] * inv[None, :]
    cos_t = jnp.tile(jnp.cos(ang), (1, 4))
    sin_h = jnp.sin(ang)
    sin_t = jnp.concatenate([-sin_h, -sin_h, sin_h, sin_h], axis=1)
    head_of_lane = (jnp.arange(LANES) // half) % 2
    e = (head_of_lane[:, None] == head_of_lane[None, :]).astype(BF16)
    qg = jnp.tile(q_g, 2)[lane_src].reshape(1, LANES)
    kg = jnp.concatenate([k_g, jnp.ones((hd,), k_g.dtype)])[lane_src].reshape(1, LANES)

    return pl.pallas_call(
        functools.partial(_dsa_in_kernel, nq=nq, nqi=nqi, hd=hd, idx_heads=idx_heads),
        grid=(b, s // tq),
        in_specs=[pl.BlockSpec((None, tq, d), lambda bi, i: (bi, i, 0)),
                  pl.BlockSpec((None, 1, mod_l.shape[-1]), lambda bi, i: (bi, 0, 0)),
                  _const_spec((1, d)), _const_spec((d, ncol)), _const_spec((1, LANES)),
                  _const_spec((1, LANES)), _const_spec((LANES, LANES)),
                  pl.BlockSpec((tq, LANES), lambda bi, i: (i, 0)),
                  pl.BlockSpec((tq, LANES), lambda bi, i: (i, 0))],
        out_specs=[pl.BlockSpec((None, nq, LANES, tq), lambda bi, i: (bi, 0, 0, i)),
                   pl.BlockSpec((None, nqi, LANES, tq), lambda bi, i: (bi, 0, 0, i)),
                   pl.BlockSpec((None, 4, tq, LANES), lambda bi, i: (bi, 0, i, 0)),
                   pl.BlockSpec((None, LANES, tq), lambda bi, i: (bi, 0, i)),
                   pl.BlockSpec((None, idx_heads, tq), lambda bi, i: (bi, 0, i))],
        out_shape=[jax.ShapeDtypeStruct((b, nq, LANES, s), BF16),
                   jax.ShapeDtypeStruct((b, nqi, LANES, s), BF16),
                   jax.ShapeDtypeStruct((b, 4, s, LANES), BF16),
                   jax.ShapeDtypeStruct((b, LANES, s), BF16),
                   jax.ShapeDtypeStruct((b, idx_heads, s), F32)],
        compiler_params=pltpu.CompilerParams(
            dimension_semantics=("arbitrary", "arbitrary"), vmem_limit_bytes=VMEM_LIMIT),
        name="dsa_in",
    )(x, mod_l, g.reshape(1, -1), w, qg, kg, e, cos_t, sin_t)


def _dsa_attn_kernel(bnd_ref, qit0_ref, wit0_ref, qitn_ref, witn_ref, qt_ref, kk_ref, vt_ref, o_ref,
                     sct_s, mask_s, st_s, mm_s, m_s, acc_s, *, kc, ks, topk, hd, idx_heads, group,
                     tie_from, max_groups, n_blocks, head_passes, sl):
    npair, _, qb = qt_ref.shape
    nh = 2 * npair
    nsub = SUBLANES
    i = pl.program_id(1)
    has_sel = i < n_blocks
    has_att = i >= 1
    nk = (i * qb) // kc + 1
    nkp = ((i - 1) * qb) // kc + 1
    slot = lax.rem(i, 2)
    mask_w = mask_s.at[slot]
    mask_r = mask_s.at[1 - slot]
    sct_c = sct_s.at[slot]
    sct_n = sct_s.at[1 - slot]
    kf = float(topk)
    fast = bnd_ref[0] <= EXP_SAFE_BOUND
    merged = jnp.logical_and(jnp.logical_and(has_sel, has_att), fast)

    def slab(src, c, t):
        return src[pl.ds(pl.multiple_of(c * kc + t * sl, sl), sl), :]

    def finish(acc, op, red):
        n = sl // nsub
        acc = acc.reshape(n, nsub, qb)
        r = acc[0]
        for t in range(1, n):
            r = op(r, acc[t])
        return jnp.broadcast_to(red(r, axis=0, keepdims=True), (nsub, qb))

    def reduce_keys(src, fn, op, red, init):
        def body(ci, acc):
            for t in range(kc // sl):
                acc = op(acc, fn(slab(src, ci, t), ci, t))
            return acc
        return finish(lax.fori_loop(0, nk, body, jnp.full((sl, qb), init, F32)), op, red)

    def rows(x):
        return jnp.concatenate([x] * (sl // nsub), axis=0)

    def score_chunk(qit_ref, wit_ref, blk, dst, ci, mm):
        qpos = blk * qb + lax.broadcasted_iota(I32, (ks, qb), 1)
        krow = lax.broadcasted_iota(I32, (ks, qb), 0)
        mx, mn = mm
        for sub in range(kc // ks):
            k0 = pl.multiple_of(ci * kc + sub * ks, ks)
            kis = (kk_ref[2, pl.ds(k0, ks), :], kk_ref[3, pl.ds(k0, ks), :])
            acc = jnp.zeros((ks, qb), F32)
            for hh in range(idx_heads):
                lg = jnp.dot(kis[hh % 2], qit_ref[hh // 2], preferred_element_type=F32)
                acc = acc + wit_ref[hh:hh + 1, :] * jnp.maximum(lg, 0.0)
            causal = (k0 + krow) <= qpos
            lo_v = jnp.where(causal, acc, -jnp.inf)
            hi_v = jnp.where(causal, acc, jnp.inf)
            dst[pl.ds(k0, ks), :] = lo_v
            for t in range(ks // sl):
                mx = jnp.maximum(mx, lo_v[t * sl:(t + 1) * sl])
                mn = jnp.minimum(mn, hi_v[t * sl:(t + 1) * sl])
        return mx, mn

    mm0 = (jnp.full((sl, qb), -jnp.inf, F32), jnp.full((sl, qb), jnp.inf, F32))

    def save_range(mm):
        mm_s[0] = finish(mm[0], jnp.maximum, jnp.max)
        mm_s[1] = finish(mm[1], jnp.minimum, jnp.min)

    @pl.when(i == 0)
    def _():
        save_range(lax.fori_loop(
            0, nk, functools.partial(score_chunk, qit0_ref, wit0_ref, 0, sct_c), mm0))

    @pl.when(has_sel)
    def _():
        mx = mm_s[0]
        st_s[0] = mm_s[1]
        st_s[1] = mx + (jnp.abs(mx) * 1e-6 + 1e-30)
        st_s[2] = (i * qb + lax.broadcasted_iota(I32, (nsub, qb), 1) + 1).astype(F32)
        st_s[3] = jnp.zeros((nsub, qb), F32)

    def count_step(st):
        c, lo, hi, clo, chi, acc = st
        mid = 0.5 * lo + 0.5 * hi
        midr = rows(mid)
        for t in range(kc // sl):
            acc = acc + jnp.where(slab(sct_c, c, t) >= midr, 1.0, 0.0)
        last = c == nk - 1
        tot = finish(acc, jnp.add, jnp.sum)
        ge = tot >= kf
        lo = jnp.where(last, jnp.where(ge, mid, lo), lo)
        clo = jnp.where(last, jnp.where(ge, tot, clo), clo)
        hi = jnp.where(last, jnp.where(ge, hi, mid), hi)
        chi = jnp.where(last, jnp.where(ge, chi, tot), chi)
        return (jnp.where(last, 0, c + 1), lo, hi, clo, chi, jnp.where(last, 0.0, acc))

    def scores(ci, h):
        kk = kk_ref[h % 2, pl.ds(pl.multiple_of(ci * kc, kc), kc), :]
        return jnp.dot(kk, qt_ref[h // 2], preferred_element_type=F32)

    def pv_step(ci, st, use_max, passes):
        c0 = pl.multiple_of(ci * kc, kc)
        vt = vt_ref[:, pl.ds(c0, kc)]
        msk = mask_r[pl.ds(c0, kc), :]
        ahead = [scores(ci, h) for h in range(min(QK_LOOKAHEAD, nh))]
        for h in range(nh):
            s_cur = ahead.pop(0)
            if h + QK_LOOKAHEAD < nh:
                ahead.append(scores(ci, h + QK_LOOKAHEAD))
            if use_max:
                pr = jnp.where(msk > 0, jnp.exp2(s_cur - m_s[h][0:1, :]), 0.0).astype(BF16)
            else:
                pr = jnp.exp2(s_cur).astype(BF16) * msk
            acc_s[h] += jnp.dot(vt, pr, preferred_element_type=F32)
            for _ in range(passes[h] if passes else 0):
                st = count_step(st)
        return st

    @pl.when(has_att)
    def _():
        acc_s[...] = jnp.zeros(acc_s.shape, F32)

    @pl.when(merged)
    def _():
        st = (jnp.int32(0), st_s[0], st_s[1], st_s[2], st_s[3], jnp.zeros((sl, qb), F32))
        st = lax.fori_loop(0, nkp, functools.partial(pv_step, use_max=False, passes=head_passes), st)
        st_s[0], st_s[1], st_s[2], st_s[3] = st[1], st[2], st[3], st[4]

    @pl.when(jnp.logical_and(has_att, jnp.logical_and(fast, jnp.logical_not(has_sel))))
    def _():
        lax.fori_loop(0, nkp, functools.partial(pv_step, use_max=False, passes=None), 0)

    @pl.when(jnp.logical_and(has_att, jnp.logical_not(fast)))
    def _():
        def max_step(ci, _):
            @pl.when(ci == 0)
            def _():
                m_s[...] = jnp.full(m_s.shape, NEG_BIG, F32)
            msk = mask_r[pl.ds(pl.multiple_of(ci * kc, kc), kc), :]
            for h in range(nh):
                mx = jnp.max(jnp.where(msk > 0, scores(ci, h), NEG_BIG), axis=0, keepdims=True)
                m_s[h] = jnp.maximum(m_s[h], jnp.broadcast_to(mx, (nsub, qb)))
            return 0

        lax.fori_loop(0, nkp, max_step, 0)
        lax.fori_loop(0, nkp, functools.partial(pv_step, use_max=True, passes=None), 0)

    @pl.when(has_att)
    def _():
        for p in range(npair):
            a0, a1 = acc_s[2 * p], acc_s[2 * p + 1]
            ot = jnp.concatenate([a0[0:hd] / a0[hd:2 * hd], a1[0:hd] / a1[hd:2 * hd]], axis=0)
            o_ref[p] = ot.T.astype(BF16)

    @pl.when(has_sel)
    def _():
        def count_ge(thr):
            thr = rows(thr)
            return reduce_keys(sct_c, lambda blk, c, t: jnp.where(blk >= thr, 1.0, 0.0),
                               jnp.add, jnp.sum, 0.0)

        def bisect(_, st):
            lo, hi, clo, chi = st
            mid = 0.5 * lo + 0.5 * hi
            c = count_ge(mid)
            ge = c >= kf
            return (jnp.where(ge, mid, lo), jnp.where(ge, hi, mid),
                    jnp.where(ge, c, clo), jnp.where(ge, chi, c))

        def open_rows(clo, tied):
            return jnp.max(jnp.where(clo > kf, 1.0 - tied, 0.0)) > 0.0

        g0 = jnp.where(merged, tie_from // group, 0)

        def search_group(st):
            g, _, lo, hi, clo, chi, tied = st
            lo, hi, clo, chi = lax.fori_loop(0, group, bisect, (lo, hi, clo, chi))

            def tie_check(tied):
                lor, hir = rows(lo), rows(hi)
                a = reduce_keys(sct_c, lambda blk, c, t: jnp.where(blk >= lor, blk, jnp.inf),
                                jnp.minimum, jnp.min, jnp.inf)
                b = reduce_keys(sct_c, lambda blk, c, t: jnp.where(blk < hir, blk, -jnp.inf),
                                jnp.maximum, jnp.max, -jnp.inf)
                return jnp.where(jnp.logical_and(clo > kf, a == b), 1.0, tied)

            tied = lax.cond(jnp.logical_and((g + 1) * group >= tie_from, open_rows(clo, tied)),
                            tie_check, lambda t: t, tied)
            return (g + 1, open_rows(clo, tied).astype(I32), lo, hi, clo, chi, tied)

        zero = jnp.zeros((nsub, qb), F32)
        _, _, lo, hi, clo, chi, _ = lax.while_loop(
            lambda st: jnp.logical_and(st[1] > 0, st[0] < max_groups), search_group,
            (g0, open_rows(st_s[2], zero).astype(I32), st_s[0], st_s[1], st_s[2], st_s[3], zero))

        def tile_k(x):
            return jnp.concatenate([x] * (ks // nsub), axis=0)

        lok, hik, free = tile_k(lo), tile_k(hi), tile_k(kf - chi)
        tri = jnp.where(lax.broadcasted_iota(I32, (ks, ks), 1) <= lax.broadcasted_iota(I32, (ks, ks), 0),
                        1.0, 0.0).astype(BF16)

        def mask_chunk(ci, carry):
            subs = []
            for t in range(kc // ks):
                k0 = pl.multiple_of(ci * kc + t * ks, ks)
                blk = sct_c[pl.ds(k0, ks), :]
                top = blk >= hik
                tie = jnp.where(top, 0.0, jnp.where(blk >= lok, 1.0, 0.0))
                subs.append((k0, top, tie, jnp.dot(tri, tie.astype(BF16), preferred_element_type=F32)))
            for k0, top, tie, rank in subs:
                keep = jnp.where(rank + tile_k(carry) <= free, tie, 0.0)
                mask_w[pl.ds(k0, ks), :] = jnp.where(top, 1.0, keep).astype(BF16)
                carry = carry + jnp.broadcast_to(rank[ks - 1:ks, :], (nsub, qb))
            return carry

        zero8 = jnp.zeros((nsub, qb), F32)
        has_next = i + 1 < n_blocks

        @pl.when(has_next)
        def _():
            nkn = ((i + 1) * qb) // kc + 1
            nxt = functools.partial(score_chunk, qitn_ref, witn_ref, i + 1, sct_n)

            def both(ci, st):
                return (mask_chunk(ci, st[0]), nxt(ci, st[1]))

            _, mm = lax.fori_loop(0, nk, both, (zero8, mm0))
            save_range(lax.fori_loop(nk, nkn, nxt, mm))

        @pl.when(jnp.logical_not(has_next))
        def _():
            lax.fori_loop(0, nk, mask_chunk, zero8)


def _dsa_attn_call(bnd, qt, qit, kk, vt, wit, *, hd, idx_heads, topk, qb=256, kc=512, ks=64,
                   sl=32):
    b, npair, _, s = qt.shape
    nqi = qit.shape[1]
    assert kc % qb == 0 and s % kc == 0 and kc % ks == 0 and ks % sl == 0 and sl % SUBLANES == 0
    nb = s // qb
    head_passes = tuple(2 if h % 4 == 0 else 1 for h in range(2 * npair))
    assert 2 * hd == LANES
    prev = lambda i: jnp.maximum(i - 1, 0)
    cur = lambda i: jnp.minimum(i, nb - 1)
    nxt = lambda i: jnp.minimum(i + 1, nb - 1)
    return pl.pallas_call(
        functools.partial(_dsa_attn_kernel, kc=kc, ks=ks, topk=topk, hd=hd, idx_heads=idx_heads,
                          group=4, tie_from=16, max_groups=96, n_blocks=nb,
                          head_passes=head_passes, sl=sl),
        grid=(b, nb + 1),
        in_specs=[pl.BlockSpec(memory_space=pltpu.SMEM),
                  pl.BlockSpec((None, nqi, LANES, qb), lambda bi, i: (bi, 0, 0, cur(i))),
                  pl.BlockSpec((None, idx_heads, qb), lambda bi, i: (bi, 0, cur(i))),
                  pl.BlockSpec((None, nqi, LANES, qb), lambda bi, i: (bi, 0, 0, nxt(i))),
                  pl.BlockSpec((None, idx_heads, qb), lambda bi, i: (bi, 0, nxt(i))),
                  pl.BlockSpec((None, npair, LANES, qb), lambda bi, i: (bi, 0, 0, prev(i))),
                  pl.BlockSpec((None, 4, s, LANES), lambda bi, i: (bi, 0, 0, 0)),
                  pl.BlockSpec((None, LANES, s), lambda bi, i: (bi, 0, 0))],
        out_specs=pl.BlockSpec((None, npair, qb, LANES), lambda bi, i: (bi, 0, prev(i), 0)),
        out_shape=jax.ShapeDtypeStruct((b, npair, s, LANES), BF16),
        scratch_shapes=[pltpu.VMEM((2, s, qb), F32),
                        pltpu.VMEM((2, s, qb), BF16), pltpu.VMEM((4, SUBLANES, qb), F32),
                        pltpu.VMEM((2, SUBLANES, qb), F32),
                        pltpu.VMEM((2 * npair, SUBLANES, qb), F32),
                        pltpu.VMEM((2 * npair, LANES, qb), F32)],
        compiler_params=pltpu.CompilerParams(
            dimension_semantics=("arbitrary", "arbitrary"), vmem_limit_bytes=VMEM_LIMIT),
        name="dsa_attn",
    )(bnd, qit, wit, qit, wit, qt, kk, vt)


def kernel(x, c, norm_mix_g, norm_ffn_g, ada_w, ada_b, a_w_in, a_conv_w, a_conv_b, a_gate_r_w,
           a_gate_r_b, a_gate_i_w, a_gate_i_b, a_lambda, a_w_out, b_w_in, b_q_norm_g, b_k_norm_g,
           b_w_out, ffn_w1, ffn_w2):
    b, s, d = x.shape
    hd = b_q_norm_g.shape[-1]
    n_heads = b_w_out.shape[1] // hd
    idx_heads = (b_w_in.shape[-1] - n_heads * hd - 3 * hd) // (hd + 1)
    topk = min(TOPK_MAX, s // 4)
    depth = ada_w.shape[0]

    mod = _mod_call(c, ada_w, ada_b)
    mod_rows = mod.reshape(depth, b, 1, mod.shape[-1])

    for i in range(depth):
        j = i // 2
        if i % 2 == 0:
            x = _rglru_call(x, mod[i], norm_mix_g[i], a_w_in[j], a_conv_w[j], a_conv_b[j],
                            a_gate_r_w[j], a_gate_r_b[j], a_gate_i_w[j], a_gate_i_b[j],
                            a_lambda[j], a_w_out[j])
            x = _ffn_call(x, mod_rows[i], norm_ffn_g[i], ffn_w1[i], ffn_w2[i])
        else:
            qt, qit, kk, vt, wit = _dsa_in_call(x, mod_rows[i], norm_mix_g[i], b_w_in[j],
                                                 b_q_norm_g[j], b_k_norm_g[j], n_heads=n_heads,
                                                 hd=hd, idx_heads=idx_heads)
            bnd = (1.02 * hd * hd ** -0.5) * jnp.max(jnp.abs(b_q_norm_g[j])) * jnp.max(jnp.abs(b_k_norm_g[j]))
            o = _dsa_attn_call(bnd.reshape(1), qt, qit, kk, vt, wit, hd=hd, idx_heads=idx_heads,
                               topk=topk)
            x = _ffn_call(x, mod_rows[i], norm_ffn_g[i], ffn_w1[i], ffn_w2[i], o_pairs=o,
                          w_o=b_w_out[j])
    return x
```

```python
import functools

import jax
import jax.numpy as jnp
from jax import lax
from jax.experimental import pallas as pl
from jax.experimental.pallas import tpu as pltpu

F32 = jnp.float32
BF16 = jnp.bfloat16
I32 = jnp.int32

RMS_EPS = 1e-6
ROPE_THETA = 10000.0
LRU_C = 8.0
TOPK_MAX = 256

LANES = 128
SUBLANES = 8
VMEM_LIMIT = 56 * 1024 * 1024

NEG_BIG = -1e30
EXP_SAFE_BOUND = 60.0
LOG2E = 1.4426950408889634
QK_LOOKAHEAD = 2
GELU_C0 = 0.7978845608028654
GELU_C1 = 0.044715


def _const_spec(shape):
    nd = len(shape)
    return pl.BlockSpec(shape, lambda *_: (0,) * nd, pipeline_mode=pl.Buffered(1))


def _rmsnorm(x, g):
    return x * lax.rsqrt(jnp.mean(x * x, axis=-1, keepdims=True) + RMS_EPS) * g


def _mod_kernel(c_ref, w_ref, b_ref, o_ref):
    c = c_ref[...]
    cond = c * jax.nn.sigmoid(c)
    o_ref[0] = jnp.dot(cond.astype(BF16), w_ref[0].astype(BF16),
                       preferred_element_type=F32) + b_ref[0]


def _mod_call(c, ada_w, ada_b, *, tn=1536):
    depth, d, n = ada_w.shape
    b = c.shape[0]
    return pl.pallas_call(
        _mod_kernel,
        grid=(depth, n // tn),
        in_specs=[pl.BlockSpec((b, d), lambda l, j: (0, 0)),
                  pl.BlockSpec((1, d, tn), lambda l, j: (l, 0, j)),
                  pl.BlockSpec((1, 1, tn), lambda l, j: (l, 0, j))],
        out_specs=pl.BlockSpec((1, b, tn), lambda l, j: (l, 0, j)),
        out_shape=jax.ShapeDtypeStruct((depth, b, n), F32),
        compiler_params=pltpu.CompilerParams(
            dimension_semantics=("arbitrary", "arbitrary"), vmem_limit_bytes=VMEM_LIMIT),
        name="adaln_mod",
    )(c, ada_w, ada_b.reshape(depth, 1, n))


def _rglru_kernel(x_ref, mod_ref, g_ref, win_ref, cw_ref, cb_ref, wr_ref, br_ref, wi_ref, bi_ref,
                  lam_ref, wout_ref, o_ref, h_s, gb_s, xbuf, a_s, b_s, hc, *, rc):
    nb, ts, d = x_ref.shape
    r = ts * nb
    dr = lam_ref.shape[1]
    nblk, blk, _ = wr_ref.shape
    cwid = cw_ref.shape[0]
    tail = (cwid - 1) * nb

    @pl.when(pl.program_id(0) == 0)
    def _():
        xbuf[0:tail, :] = jnp.zeros((tail, dr), F32)
        hc[...] = jnp.zeros((nb, dr), F32)

    x = jnp.swapaxes(x_ref[...], 0, 1)
    sh1 = mod_ref[:, 0:d]
    sc1 = mod_ref[:, d:2 * d]
    g1 = mod_ref[:, 2 * d:3 * d]
    h_s[...] = (_rmsnorm(x, g_ref[...]) * (1.0 + sc1)[None] + sh1[None]).reshape(r, d).astype(BF16)

    sp_h = (0.5 * LRU_C) * jax.nn.softplus(-lam_ref[...])
    br_h = 0.5 * br_ref[...]
    bi_h = 0.5 * bi_ref[...]

    def in_proj(c, n):
        rows = slice(c * rc, (c + 1) * rc)
        hb = h_s[rows, :]
        ux = jnp.dot(hb, win_ref[:, n * blk:(n + 1) * blk], preferred_element_type=F32)
        ug = jnp.dot(hb, win_ref[:, dr + n * blk:dr + (n + 1) * blk], preferred_element_type=F32)
        xbuf[tail + c * rc:tail + (c + 1) * rc, n * blk:(n + 1) * blk] = ux
        return ug

    def conv_and_gate_dots(c, n):
        r0 = c * rc
        cs = slice(n * blk, (n + 1) * blk)
        xc = cb_ref[:, cs] + cw_ref[0:1, cs] * xbuf[r0:r0 + rc, cs]
        for k in range(1, cwid):
            xc = xc + cw_ref[k:k + 1, cs] * xbuf[r0 + k * nb:r0 + k * nb + rc, cs]
        xn = xc.astype(BF16)
        return xc, (jnp.dot(xn, wr_ref[n], preferred_element_type=F32),
                    jnp.dot(xn, wi_ref[n], preferred_element_type=F32))

    def gate_tail(c, n, xc, pre, ug):
        ro = slice(c * rc, (c + 1) * rc)
        cs = slice(n * blk, (n + 1) * blk)
        p = sp_h[:, cs] + sp_h[:, cs] * jnp.tanh(0.5 * pre[0] + br_h[:, cs])
        ig = 0.5 + 0.5 * jnp.tanh(0.5 * pre[1] + bi_h[:, cs])
        a = jnp.exp2(p * (-LOG2E))
        a_s[ro, cs] = a
        z = jnp.tanh(p) * (a * a + 1.0)
        b_s[ro, cs] = jnp.where(z > 0.0, z * lax.rsqrt(z), 0.0) * (ig * xc)
        t = jnp.tanh(ug * (GELU_C0 + (GELU_C0 * GELU_C1) * (ug * ug)))
        hu = 0.5 * ug
        gb_s[ro, cs] = hu + hu * t

    units = [(c, n) for c in range(r // rc) for n in range(nblk)]
    ug = in_proj(*units[0])
    for k, (c, n) in enumerate(units):
        ug_next = in_proj(*units[k + 1]) if k + 1 < len(units) else None
        xc, pre = conv_and_gate_dots(c, n)
        gate_tail(c, n, xc, pre, ug)
        ug = ug_next
    xbuf[0:tail, :] = xbuf[r:r + tail, :]

    def step(t, hprev):
        r0 = pl.multiple_of(t * nb, nb)
        hnew = a_s[pl.ds(r0, nb), :] * hprev + b_s[pl.ds(r0, nb), :]
        b_s[pl.ds(r0, nb), :] = hnew
        return hnew

    hc[...] = lax.fori_loop(0, ts, step, hc[...], unroll=8)

    y = b_s[...] * gb_s[...]
    out = jnp.dot(y.astype(BF16), wout_ref[...], preferred_element_type=F32)
    o_ref[...] = jnp.swapaxes(x + out.reshape(ts, nb, d) * g1[None], 0, 1)


def _rglru_call(x, mod0, g, w_in, conv_w, conv_b, wr, br, wi, bi, lam, w_out, *, ts=128, rc=256):
    nb, s, d = x.shape
    assert nb == SUBLANES, "the recurrence keeps the batch on the sublane axis"
    r = ts * nb
    dr = lam.shape[-1]
    tail = (conv_w.shape[0] - 1) * nb
    row = lambda v: v.reshape(1, -1)
    args = (x, mod0, row(g), w_in.astype(BF16), conv_w, row(conv_b), wr.astype(BF16), row(br),
            wi.astype(BF16), row(bi), row(lam), w_out.astype(BF16))
    x_spec = pl.BlockSpec((nb, ts, d), lambda i: (0, i, 0))
    return pl.pallas_call(
        functools.partial(_rglru_kernel, rc=rc),
        grid=(s // ts,),
        in_specs=[x_spec] + [_const_spec(a.shape) for a in args[1:]],
        out_specs=x_spec,
        out_shape=jax.ShapeDtypeStruct((nb, s, d), F32),
        scratch_shapes=[pltpu.VMEM((r, d), BF16), pltpu.VMEM((r, dr), F32),
                        pltpu.VMEM((r + tail, dr), F32), pltpu.VMEM((r, dr), F32),
                        pltpu.VMEM((r, dr), F32), pltpu.VMEM((nb, dr), F32)],
        compiler_params=pltpu.CompilerParams(
            dimension_semantics=("arbitrary",), vmem_limit_bytes=VMEM_LIMIT),
        name="rglru_mixer",
    )(*args)


def _ffn_kernel(*refs, has_proj, fc):
    if has_proj:
        x_ref, mod_ref, g_ref, w1_ref, w2_ref, o_in_ref, wo_ref, out_ref = refs
    else:
        x_ref, mod_ref, g_ref, w1_ref, w2_ref, out_ref = refs
    d = x_ref.shape[1]
    dff = w1_ref.shape[1]
    x = x_ref[...]
    if has_proj:
        g1 = mod_ref[:, 2 * d:3 * d]
        o = jnp.concatenate([o_in_ref[p] for p in range(o_in_ref.shape[0])], axis=-1)
        x = x + g1 * jnp.dot(o, wo_ref[...], preferred_element_type=F32)
    sh2 = mod_ref[:, 3 * d:4 * d]
    sc2 = mod_ref[:, 4 * d:5 * d]
    g2 = mod_ref[:, 5 * d:6 * d]
    hb = (_rmsnorm(x, g_ref[...]) * (1.0 + sc2) + sh2).astype(BF16)
    acc = jnp.zeros(x.shape, F32)
    for c in range(dff // fc):
        hid = jnp.dot(hb, w1_ref[:, c * fc:(c + 1) * fc], preferred_element_type=F32)
        hid = jnp.square(jnp.maximum(hid, 0.0))
        acc = acc + jnp.dot(hid.astype(BF16), w2_ref[c * fc:(c + 1) * fc, :],
                            preferred_element_type=F32)
    out_ref[...] = x + g2 * acc


def _ffn_call(x, mod_l, g, w1, w2, o_pairs=None, w_o=None, *, rf=512, fc=1024):
    b, s, d = x.shape
    has_proj = o_pairs is not None
    args = [x, mod_l, g.reshape(1, -1), w1.astype(BF16), w2.astype(BF16)]
    in_specs = [pl.BlockSpec((None, rf, d), lambda bi, i: (bi, i, 0)),
                pl.BlockSpec((None, 1, mod_l.shape[-1]), lambda bi, i: (bi, 0, 0)),
                _const_spec((1, d)), _const_spec(w1.shape), _const_spec(w2.shape)]
    if has_proj:
        npair = o_pairs.shape[1]
        args += [o_pairs, w_o.astype(BF16)]
        in_specs += [pl.BlockSpec((None, npair, rf, LANES), lambda bi, i: (bi, 0, i, 0)),
                     _const_spec(w_o.shape)]
    return pl.pallas_call(
        functools.partial(_ffn_kernel, has_proj=has_proj, fc=fc),
        grid=(b, s // rf),
        in_specs=in_specs,
        out_specs=pl.BlockSpec((None, rf, d), lambda bi, i: (bi, i, 0)),
        out_shape=jax.ShapeDtypeStruct((b, s, d), F32),
        compiler_params=pltpu.CompilerParams(
            dimension_semantics=("arbitrary", "arbitrary"), vmem_limit_bytes=VMEM_LIMIT),
        name="ffn_proj" if has_proj else "ffn",
    )(*args)


def _rot_half(x):
    return pltpu.roll(x, LANES // 2, 1)


def _dsa_in_kernel(x_ref, mod_ref, g_ref, w_ref, qg_ref, kg_ref, e_ref, cos_ref, sin_ref,
                   qt_ref, qit_ref, kk_ref, vt_ref, wit_ref, *, nq, nqi, hd, idx_heads):
    d = x_ref.shape[1]
    tq = x_ref.shape[0]
    x = x_ref[...]
    sh1 = mod_ref[:, 0:d]
    sc1 = mod_ref[:, d:2 * d]
    hb = (_rmsnorm(x, g_ref[...]) * (1.0 + sc1) + sh1).astype(BF16)
    lane = lax.broadcasted_iota(I32, (tq, LANES), 1)
    left = lane < hd
    head_a = (lane & (hd // 2)) == 0
    cos = cos_ref[...]
    sin = sin_ref[...]
    qg = qg_ref[...]
    e = e_ref[...]
    qscale = hd ** -0.5

    def q_tile(p, t):
        t2 = t * t
        hi = t2.astype(BF16)
        lo = (t2 - hi.astype(F32)).astype(BF16)
        ss = (jnp.dot(hi, e, preferred_element_type=F32) + jnp.dot(lo, e, preferred_element_type=F32))
        tn = t * lax.rsqrt(ss * (1.0 / hd) + RMS_EPS) * qg
        qt_ref[p] = ((tn * cos + _rot_half(tn) * sin) * (qscale * LOG2E)).T.astype(BF16)

    def qi_tile(p, t):
        qit_ref[p] = ((t * cos + _rot_half(t) * sin) * qscale).T.astype(BF16)

    def kk_tile(kk):
        ssk = jnp.sum(jnp.where(head_a, kk * kk, 0.0), axis=-1, keepdims=True)
        fac = jnp.where(head_a, lax.rsqrt(ssk * (1.0 / hd) + RMS_EPS) * kg_ref[...], 1.0)
        kn = kk * fac
        kr = kn * cos + _rot_half(kn) * sin
        k_a = jnp.where(head_a, kr, 0.0)
        ki_b = jnp.where(head_a, 0.0, kr)
        kk_ref[0] = k_a.astype(BF16)
        kk_ref[1] = pltpu.roll(k_a, hd // 2, 1).astype(BF16)
        kk_ref[2] = pltpu.roll(ki_b, LANES - hd // 2, 1).astype(BF16)
        kk_ref[3] = ki_b.astype(BF16)

    def vw_tile(vw):
        vt_ref[...] = jnp.where(left, vw, 1.0).T.astype(BF16)
        wit_ref[...] = (vw * (idx_heads ** -0.5)).T[hd:hd + idx_heads, :]

    def tile(p, t):
        if p < nq:
            q_tile(p, t)
        elif p < nq + nqi:
            qi_tile(p - nq, t)
        elif p == nq + nqi:
            kk_tile(t)
        else:
            vw_tile(t)

    gw = 2 * LANES
    ngroups = w_ref.shape[1] // gw

    def proj(gi):
        return jnp.dot(hb, w_ref[:, gi * gw:(gi + 1) * gw], preferred_element_type=F32)

    u_next = proj(0)
    for gi in range(ngroups):
        u = u_next
        if gi + 1 < ngroups:
            u_next = proj(gi + 1)
        for half in range(gw // LANES):
            tile(gi * (gw // LANES) + half, u[:, half * LANES:(half + 1) * LANES])


def _dsa_in_call(x, mod_l, g, w_in, q_g, k_g, *, n_heads, hd, idx_heads, tq=512):
    b, s, d = x.shape
    assert 2 * hd == LANES, "two heads per lane tile"
    nq = n_heads * hd // LANES
    nqi = idx_heads * hd // LANES
    o_k = n_heads * hd
    o_v = o_k + hd
    o_qi = o_v + hd
    o_ki = o_qi + idx_heads * hd
    o_wi = o_ki + hd
    pad = LANES - hd - idx_heads
    w = jnp.concatenate([w_in[:, :o_k], w_in[:, o_qi:o_ki], w_in[:, o_k:o_v], w_in[:, o_ki:o_wi],
                         w_in[:, o_v:o_qi], w_in[:, o_wi:], jnp.zeros((d, pad), w_in.dtype)],
                        axis=1).astype(BF16)
    half = hd // 2
    lane_src = jnp.concatenate([jnp.arange(0, half), jnp.arange(hd, hd + half),
                                jnp.arange(half, hd), jnp.arange(hd + half, LANES)])
    ncol = w.shape[1]
    nrot = nq + nqi + 1
    w = w.reshape(d, ncol // LANES, LANES)
    w = jnp.concatenate([w[:, :nrot, lane_src], w[:, nrot:, :]], axis=1).reshape(d, ncol)
    inv = ROPE_THETA ** (-jnp.arange(0, hd, 2, dtype=F32) / hd)
    ang = jnp.arange(s, dtype=F32)[:, None] * inv[None, :]
    cos_t = jnp.tile(jnp.cos(ang), (1, 4))
    sin_h = jnp.sin(ang)
    sin_t = jnp.concatenate([-sin_h, -sin_h, sin_h, sin_h], axis=1)
    head_of_lane = (jnp.arange(LANES) // half) % 2
    e = (head_of_lane[:, None] == head_of_lane[None, :]).astype(BF16)
    qg = jnp.tile(q_g, 2)[lane_src].reshape(1, LANES)
    kg = jnp.concatenate([k_g, jnp.ones((hd,), k_g.dtype)])[lane_src].reshape(1, LANES)

    return pl.pallas_call(
        functools.partial(_dsa_in_kernel, nq=nq, nqi=nqi, hd=hd, idx_heads=idx_heads),
        grid=(b, s // tq),
        in_specs=[pl.BlockSpec((None, tq, d), lambda bi, i: (bi, i, 0)),
                  pl.BlockSpec((None, 1, mod_l.shape[-1]), lambda bi, i: (bi, 0, 0)),
                  _const_spec((1, d)), _const_spec((d, ncol)), _const_spec((1, LANES)),
                  _const_spec((1, LANES)), _const_spec((LANES, LANES)),
                  pl.BlockSpec((tq, LANES), lambda bi, i: (i, 0)),
                  pl.BlockSpec((tq, LANES), lambda bi, i: (i, 0))],
        out_specs=[pl.BlockSpec((None, nq, LANES, tq), lambda bi, i: (bi, 0, 0, i)),
                   pl.BlockSpec((None, nqi, LANES, tq), lambda bi, i: (bi, 0, 0, i)),
                   pl.BlockSpec((None, 4, tq, LANES), lambda bi, i: (bi, 0, i, 0)),
                   pl.BlockSpec((None, LANES, tq), lambda bi, i: (bi, 0, i)),
                   pl.BlockSpec((None, idx_heads, tq), lambda bi, i: (bi, 0, i))],
        out_shape=[jax.ShapeDtypeStruct((b, nq, LANES, s), BF16),
                   jax.ShapeDtypeStruct((b, nqi, LANES, s), BF16),
                   jax.ShapeDtypeStruct((b, 4, s, LANES), BF16),
                   jax.ShapeDtypeStruct((b, LANES, s), BF16),
                   jax.ShapeDtypeStruct((b, idx_heads, s), F32)],
        compiler_params=pltpu.CompilerParams(
            dimension_semantics=("arbitrary", "arbitrary"), vmem_limit_bytes=VMEM_LIMIT),
        name="dsa_in",
    )(x, mod_l, g.reshape(1, -1), w, qg, kg, e, cos_t, sin_t)


def _dsa_attn_kernel(bnd_ref, qit0_ref, wit0_ref, qitn_ref, witn_ref, qt_ref, kk_ref, vt_ref, o_ref,
                     sct_s, mask_s, st_s, mm_s, m_s, acc_s, *, kc, ks, topk, hd, idx_heads, group,
                     tie_from, max_groups, n_blocks, head_passes, sl):
    npair, _, qb = qt_ref.shape
    nh = 2 * npair
    nsub = SUBLANES
    i = pl.program_id(1)
    has_sel = i < n_blocks
    has_att = i >= 1
    nk = (i * qb) // kc + 1
    nkp = ((i - 1) * qb) // kc + 1
    slot = lax.rem(i, 2)
    mask_w = mask_s.at[slot]
    mask_r = mask_s.at[1 - slot]
    sct_c = sct_s.at[slot]
    sct_n = sct_s.at[1 - slot]
    kf = float(topk)
    fast = bnd_ref[0] <= EXP_SAFE_BOUND
    merged = jnp.logical_and(jnp.logical_and(has_sel, has_att), fast)

    def slab(src, c, t):
        return src[pl.ds(pl.multiple_of(c * kc + t * sl, sl), sl), :]

    def finish(acc, op, red):
        n = sl // nsub
        acc = acc.reshape(n, nsub, qb)
        r = acc[0]
        for t in range(1, n):
            r = op(r, acc[t])
        return jnp.broadcast_to(red(r, axis=0, keepdims=True), (nsub, qb))

    def reduce_keys(src, fn, op, red, init):
        def body(ci, acc):
            for t in range(kc // sl):
                acc = op(acc, fn(slab(src, ci, t), ci, t))
            return acc
        return finish(lax.fori_loop(0, nk, body, jnp.full((sl, qb), init, F32)), op, red)

    def rows(x):
        return jnp.concatenate([x] * (sl // nsub), axis=0)

    def score_chunk(qit_ref, wit_ref, blk, dst, ci, mm):
        qpos = blk * qb + lax.broadcasted_iota(I32, (ks, qb), 1)
        krow = lax.broadcasted_iota(I32, (ks, qb), 0)
        mx, mn = mm
        for sub in range(kc // ks):
            k0 = pl.multiple_of(ci * kc + sub * ks, ks)
            kis = (kk_ref[2, pl.ds(k0, ks), :], kk_ref[3, pl.ds(k0, ks), :])
            acc = jnp.zeros((ks, qb), F32)
            for hh in range(idx_heads):
                lg = jnp.dot(kis[hh % 2], qit_ref[hh // 2], preferred_element_type=F32)
                acc = acc + wit_ref[hh:hh + 1, :] * jnp.maximum(lg, 0.0)
            causal = (k0 + krow) <= qpos
            lo_v = jnp.where(causal, acc, -jnp.inf)
            hi_v = jnp.where(causal, acc, jnp.inf)
            dst[pl.ds(k0, ks), :] = lo_v
            for t in range(ks // sl):
                mx = jnp.maximum(mx, lo_v[t * sl:(t + 1) * sl])
                mn = jnp.minimum(mn, hi_v[t * sl:(t + 1) * sl])
        return mx, mn

    mm0 = (jnp.full((sl, qb), -jnp.inf, F32), jnp.full((sl, qb), jnp.inf, F32))

    def save_range(mm):
        mm_s[0] = finish(mm[0], jnp.maximum, jnp.max)
        mm_s[1] = finish(mm[1], jnp.minimum, jnp.min)

    @pl.when(i == 0)
    def _():
        save_range(lax.fori_loop(
            0, nk, functools.partial(score_chunk, qit0_ref, wit0_ref, 0, sct_c), mm0))

    @pl.when(has_sel)
    def _():
        mx = mm_s[0]
        st_s[0] = mm_s[1]
        st_s[1] = mx + (jnp.abs(mx) * 1e-6 + 1e-30)
        st_s[2] = (i * qb + lax.broadcasted_iota(I32, (nsub, qb), 1) + 1).astype(F32)
        st_s[3] = jnp.zeros((nsub, qb), F32)

    def count_step(st):
        c, lo, hi, clo, chi, acc = st
        mid = 0.5 * lo + 0.5 * hi
        midr = rows(mid)
        for t in range(kc // sl):
            acc = acc + jnp.where(slab(sct_c, c, t) >= midr, 1.0, 0.0)
        last = c == nk - 1
        tot = finish(acc, jnp.add, jnp.sum)
        ge = tot >= kf
        lo = jnp.where(last, jnp.where(ge, mid, lo), lo)
        clo = jnp.where(last, jnp.where(ge, tot, clo), clo)
        hi = jnp.where(last, jnp.where(ge, hi, mid), hi)
        chi = jnp.where(last, jnp.where(ge, chi, tot), chi)
        return (jnp.where(last, 0, c + 1), lo, hi, clo, chi, jnp.where(last, 0.0, acc))

    def scores(ci, h):
        kk = kk_ref[h % 2, pl.ds(pl.multiple_of(ci * kc, kc), kc), :]
        return jnp.dot(kk, qt_ref[h // 2], preferred_element_type=F32)

    def pv_step(ci, st, use_max, passes):
        c0 = pl.multiple_of(ci * kc, kc)
        vt = vt_ref[:, pl.ds(c0, kc)]
        msk = mask_r[pl.ds(c0, kc), :]
        ahead = [scores(ci, h) for h in range(min(QK_LOOKAHEAD, nh))]
        for h in range(nh):
            s_cur = ahead.pop(0)
            if h + QK_LOOKAHEAD < nh:
                ahead.append(scores(ci, h + QK_LOOKAHEAD))
            if use_max:
                pr = jnp.where(msk > 0, jnp.exp2(s_cur - m_s[h][0:1, :]), 0.0).astype(BF16)
            else:
                pr = jnp.exp2(s_cur).astype(BF16) * msk
            acc_s[h] += jnp.dot(vt, pr, preferred_element_type=F32)
            for _ in range(passes[h] if passes else 0):
                st = count_step(st)
        return st

    @pl.when(has_att)
    def _():
        acc_s[...] = jnp.zeros(acc_s.shape, F32)

    @pl.when(merged)
    def _():
        st = (jnp.int32(0), st_s[0], st_s[1], st_s[2], st_s[3], jnp.zeros((sl, qb), F32))
        st = lax.fori_loop(0, nkp, functools.partial(pv_step, use_max=False, passes=head_passes), st)
        st_s[0], st_s[1], st_s[2], st_s[3] = st[1], st[2], st[3], st[4]

    @pl.when(jnp.logical_and(has_att, jnp.logical_and(fast, jnp.logical_not(has_sel))))
    def _():
        lax.fori_loop(0, nkp, functools.partial(pv_step, use_max=False, passes=None), 0)

    @pl.when(jnp.logical_and(has_att, jnp.logical_not(fast)))
    def _():
        def max_step(ci, _):
            @pl.when(ci == 0)
            def _():
                m_s[...] = jnp.full(m_s.shape, NEG_BIG, F32)
            msk = mask_r[pl.ds(pl.multiple_of(ci * kc, kc), kc), :]
            for h in range(nh):
                mx = jnp.max(jnp.where(msk > 0, scores(ci, h), NEG_BIG), axis=0, keepdims=True)
                m_s[h] = jnp.maximum(m_s[h], jnp.broadcast_to(mx, (nsub, qb)))
            return 0

        lax.fori_loop(0, nkp, max_step, 0)
        lax.fori_loop(0, nkp, functools.partial(pv_step, use_max=True, passes=None), 0)

    @pl.when(has_att)
    def _():
        for p in range(npair):
            a0, a1 = acc_s[2 * p], acc_s[2 * p + 1]
            ot = jnp.concatenate([a0[0:hd] / a0[hd:2 * hd], a1[0:hd] / a1[hd:2 * hd]], axis=0)
            o_ref[p] = ot.T.astype(BF16)

    @pl.when(has_sel)
    def _():
        def count_ge(thr):
            thr = rows(thr)
            return reduce_keys(sct_c, lambda blk, c, t: jnp.where(blk >= thr, 1.0, 0.0),
                               jnp.add, jnp.sum, 0.0)

        def bisect(_, st):
            lo, hi, clo, chi = st
            mid = 0.5 * lo + 0.5 * hi
            c = count_ge(mid)
            ge = c >= kf
            return (jnp.where(ge, mid, lo), jnp.where(ge, hi, mid),
                    jnp.where(ge, c, clo), jnp.where(ge, chi, c))

        def open_rows(clo, tied):
            return jnp.max(jnp.where(clo > kf, 1.0 - tied, 0.0)) > 0.0

        g0 = jnp.where(merged, tie_from // group, 0)

        def search_group(st):
            g, _, lo, hi, clo, chi, tied = st
            lo, hi, clo, chi = lax.fori_loop(0, group, bisect, (lo, hi, clo, chi))

            def tie_check(tied):
                lor, hir = rows(lo), rows(hi)
                a = reduce_keys(sct_c, lambda blk, c, t: jnp.where(blk >= lor, blk, jnp.inf),
                                jnp.minimum, jnp.min, jnp.inf)
                b = reduce_keys(sct_c, lambda blk, c, t: jnp.where(blk < hir, blk, -jnp.inf),
                                jnp.maximum, jnp.max, -jnp.inf)
                return jnp.where(jnp.logical_and(clo > kf, a == b), 1.0, tied)

            tied = lax.cond(jnp.logical_and((g + 1) * group >= tie_from, open_rows(clo, tied)),
                            tie_check, lambda t: t, tied)
            return (g + 1, open_rows(clo, tied).astype(I32), lo, hi, clo, chi, tied)

        zero = jnp.zeros((nsub, qb), F32)
        _, _, lo, hi, clo, chi, _ = lax.while_loop(
            lambda st: jnp.logical_and(st[1] > 0, st[0] < max_groups), search_group,
            (g0, open_rows(st_s[2], zero).astype(I32), st_s[0], st_s[1], st_s[2], st_s[3], zero))

        def tile_k(x):
            return jnp.concatenate([x] * (ks // nsub), axis=0)

        lok, hik, free = tile_k(lo), tile_k(hi), tile_k(kf - chi)
        tri = jnp.where(lax.broadcasted_iota(I32, (ks, ks), 1) <= lax.broadcasted_iota(I32, (ks, ks), 0),
                        1.0, 0.0).astype(BF16)

        def mask_chunk(ci, carry):
            subs = []
            for t in range(kc // ks):
                k0 = pl.multiple_of(ci * kc + t * ks, ks)
                blk = sct_c[pl.ds(k0, ks), :]
                top = blk >= hik
                tie = jnp.where(top, 0.0, jnp.where(blk >= lok, 1.0, 0.0))
                subs.append((k0, top, tie, jnp.dot(tri, tie.astype(BF16), preferred_element_type=F32)))
            for k0, top, tie, rank in subs:
                keep = jnp.where(rank + tile_k(carry) <= free, tie, 0.0)
                mask_w[pl.ds(k0, ks), :] = jnp.where(top, 1.0, keep).astype(BF16)
                carry = carry + jnp.broadcast_to(rank[ks - 1:ks, :], (nsub, qb))
            return carry

        zero8 = jnp.zeros((nsub, qb), F32)
        has_next = i + 1 < n_blocks

        @pl.when(has_next)
        def _():
            nkn = ((i + 1) * qb) // kc + 1
            nxt = functools.partial(score_chunk, qitn_ref, witn_ref, i + 1, sct_n)

            def both(ci, st):
                return (mask_chunk(ci, st[0]), nxt(ci, st[1]))

            _, mm = lax.fori_loop(0, nk, both, (zero8, mm0))
            save_range(lax.fori_loop(nk, nkn, nxt, mm))

        @pl.when(jnp.logical_not(has_next))
        def _():
            lax.fori_loop(0, nk, mask_chunk, zero8)


def _dsa_attn_call(bnd, qt, qit, kk, vt, wit, *, hd, idx_heads, topk, qb=256, kc=512, ks=128,
                   sl=32):
    b, npair, _, s = qt.shape
    nqi = qit.shape[1]
    assert kc % qb == 0 and s % kc == 0 and kc % ks == 0 and ks % sl == 0 and sl % SUBLANES == 0
    nb = s // qb
    head_passes = tuple(2 if h % 4 == 0 else 1 for h in range(2 * npair))
    assert 2 * hd == LANES
    prev = lambda i: jnp.maximum(i - 1, 0)
    cur = lambda i: jnp.minimum(i, nb - 1)
    nxt = lambda i: jnp.minimum(i + 1, nb - 1)
    return pl.pallas_call(
        functools.partial(_dsa_attn_kernel, kc=kc, ks=ks, topk=topk, hd=hd, idx_heads=idx_heads,
                          group=4, tie_from=16, max_groups=96, n_blocks=nb,
                          head_passes=head_passes, sl=sl),
        grid=(b, nb + 1),
        in_specs=[pl.BlockSpec(memory_space=pltpu.SMEM),
                  pl.BlockSpec((None, nqi, LANES, qb), lambda bi, i: (bi, 0, 0, cur(i))),
                  pl.BlockSpec((None, idx_heads, qb), lambda bi, i: (bi, 0, cur(i))),
                  pl.BlockSpec((None, nqi, LANES, qb), lambda bi, i: (bi, 0, 0, nxt(i))),
                  pl.BlockSpec((None, idx_heads, qb), lambda bi, i: (bi, 0, nxt(i))),
                  pl.BlockSpec((None, npair, LANES, qb), lambda bi, i: (bi, 0, 0, prev(i))),
                  pl.BlockSpec((None, 4, s, LANES), lambda bi, i: (bi, 0, 0, 0)),
                  pl.BlockSpec((None, LANES, s), lambda bi, i: (bi, 0, 0))],
        out_specs=pl.BlockSpec((None, npair, qb, LANES), lambda bi, i: (bi, 0, prev(i), 0)),
        out_shape=jax.ShapeDtypeStruct((b, npair, s, LANES), BF16),
        scratch_shapes=[pltpu.VMEM((2, s, qb), F32),
                        pltpu.VMEM((2, s, qb), BF16), pltpu.VMEM((4, SUBLANES, qb), F32),
                        pltpu.VMEM((2, SUBLANES, qb), F32),
                        pltpu.VMEM((2 * npair, SUBLANES, qb), F32),
                        pltpu.VMEM((2 * npair, LANES, qb), F32)],
        compiler_params=pltpu.CompilerParams(
            dimension_semantics=("arbitrary", "arbitrary"), vmem_limit_bytes=VMEM_LIMIT),
        name="dsa_attn",
    )(bnd, qit, wit, qit, wit, qt, kk, vt)


def kernel(x, c, norm_mix_g, norm_ffn_g, ada_w, ada_b, a_w_in, a_conv_w, a_conv_b, a_gate_r_w,
           a_gate_r_b, a_gate_i_w, a_gate_i_b, a_lambda, a_w_out, b_w_in, b_q_norm_g, b_k_norm_g,
           b_w_out, ffn_w1, ffn_w2):
    b, s, d = x.shape
    hd = b_q_norm_g.shape[-1]
    n_heads = b_w_out.shape[1] // hd
    idx_heads = (b_w_in.shape[-1] - n_heads * hd - 3 * hd) // (hd + 1)
    topk = min(TOPK_MAX, s // 4)
    depth = ada_w.shape[0]

    mod = _mod_call(c, ada_w, ada_b)
    mod_rows = mod.reshape(depth, b, 1, mod.shape[-1])

    for i in range(depth):
        j = i // 2
        if i % 2 == 0:
            x = _rglru_call(x, mod[i], norm_mix_g[i], a_w_in[j], a_conv_w[j], a_conv_b[j],
                            a_gate_r_w[j], a_gate_r_b[j], a_gate_i_w[j], a_gate_i_b[j],
                            a_lambda[j], a_w_out[j])
            x = _ffn_call(x, mod_rows[i], norm_ffn_g[i], ffn_w1[i], ffn_w2[i])
        else:
            qt, qit, kk, vt, wit = _dsa_in_call(x, mod_rows[i], norm_mix_g[i], b_w_in[j],
                                                 b_q_norm_g[j], b_k_norm_g[j], n_heads=n_heads,
                                                 hd=hd, idx_heads=idx_heads)
            bnd = (1.02 * hd * hd ** -0.5) * jnp.max(jnp.abs(b_q_norm_g[j])) * jnp.max(jnp.abs(b_k_norm_g[j]))
            o = _dsa_attn_call(bnd.reshape(1), qt, qit, kk, vt, wit, hd=hd, idx_heads=idx_heads,
                               topk=topk)
            x = _ffn_call(x, mod_rows[i], norm_ffn_g[i], ffn_w1[i], ffn_w2[i], o_pairs=o,
                          w_o=b_w_out[j])
    return x
```

```python
import functools

import jax
import jax.numpy as jnp
from jax import lax
from jax.experimental import pallas as pl
from jax.experimental.pallas import tpu as pltpu

F32 = jnp.float32
BF16 = jnp.bfloat16
I32 = jnp.int32

RMS_EPS = 1e-6
ROPE_THETA = 10000.0
LRU_C = 8.0
TOPK_MAX = 256

LANES = 128
SUBLANES = 8
VMEM_LIMIT = 56 * 1024 * 1024

NEG_BIG = -1e30
EXP_SAFE_BOUND = 60.0
LOG2E = 1.4426950408889634
QK_LOOKAHEAD = 2
GELU_C0 = 0.7978845608028654
GELU_C1 = 0.044715


def _const_spec(shape):
    nd = len(shape)
    return pl.BlockSpec(shape, lambda *_: (0,) * nd, pipeline_mode=pl.Buffered(1))


def _rmsnorm(x, g):
    return x * lax.rsqrt(jnp.mean(x * x, axis=-1, keepdims=True) + RMS_EPS) * g


def _mod_kernel(c_ref, w_ref, b_ref, o_ref):
    c = c_ref[...]
    cond = c * jax.nn.sigmoid(c)
    o_ref[0] = jnp.dot(cond.astype(BF16), w_ref[0].astype(BF16),
                       preferred_element_type=F32) + b_ref[0]


def _mod_call(c, ada_w, ada_b, *, tn=1536):
    depth, d, n = ada_w.shape
    b = c.shape[0]
    return pl.pallas_call(
        _mod_kernel,
        grid=(depth, n // tn),
        in_specs=[pl.BlockSpec((b, d), lambda l, j: (0, 0)),
                  pl.BlockSpec((1, d, tn), lambda l, j: (l, 0, j)),
                  pl.BlockSpec((1, 1, tn), lambda l, j: (l, 0, j))],
        out_specs=pl.BlockSpec((1, b, tn), lambda l, j: (l, 0, j)),
        out_shape=jax.ShapeDtypeStruct((depth, b, n), F32),
        compiler_params=pltpu.CompilerParams(
            dimension_semantics=("arbitrary", "arbitrary"), vmem_limit_bytes=VMEM_LIMIT),
        name="adaln_mod",
    )(c, ada_w, ada_b.reshape(depth, 1, n))


def _rglru_kernel(x_ref, mod_ref, g_ref, win_ref, cw_ref, cb_ref, wr_ref, br_ref, wi_ref, bi_ref,
                  lam_ref, wout_ref, o_ref, h_s, gb_s, xbuf, a_s, b_s, hc, *, rc):
    nb, ts, d = x_ref.shape
    r = ts * nb
    dr = lam_ref.shape[1]
    nblk, blk, _ = wr_ref.shape
    cwid = cw_ref.shape[0]
    tail = (cwid - 1) * nb

    @pl.when(pl.program_id(0) == 0)
    def _():
        xbuf[0:tail, :] = jnp.zeros((tail, dr), F32)
        hc[...] = jnp.zeros((nb, dr), F32)

    x = jnp.swapaxes(x_ref[...], 0, 1)
    sh1 = mod_ref[:, 0:d]
    sc1 = mod_ref[:, d:2 * d]
    g1 = mod_ref[:, 2 * d:3 * d]
    h_s[...] = (_rmsnorm(x, g_ref[...]) * (1.0 + sc1)[None] + sh1[None]).reshape(r, d).astype(BF16)

    sp_h = (0.5 * LRU_C) * jax.nn.softplus(-lam_ref[...])
    br_h = 0.5 * br_ref[...]
    bi_h = 0.5 * bi_ref[...]

    def in_proj(c, n):
        rows = slice(c * rc, (c + 1) * rc)
        hb = h_s[rows, :]
        ux = jnp.dot(hb, win_ref[:, n * blk:(n + 1) * blk], preferred_element_type=F32)
        ug = jnp.dot(hb, win_ref[:, dr + n * blk:dr + (n + 1) * blk], preferred_element_type=F32)
        xbuf[tail + c * rc:tail + (c + 1) * rc, n * blk:(n + 1) * blk] = ux
        return ug

    def conv_and_gate_dots(c, n):
        r0 = c * rc
        cs = slice(n * blk, (n + 1) * blk)
        xc = cb_ref[:, cs] + cw_ref[0:1, cs] * xbuf[r0:r0 + rc, cs]
        for k in range(1, cwid):
            xc = xc + cw_ref[k:k + 1, cs] * xbuf[r0 + k * nb:r0 + k * nb + rc, cs]
        xn = xc.astype(BF16)
        return xc, (jnp.dot(xn, wr_ref[n], preferred_element_type=F32),
                    jnp.dot(xn, wi_ref[n], preferred_element_type=F32))

    def gate_tail(c, n, xc, pre, ug):
        ro = slice(c * rc, (c + 1) * rc)
        cs = slice(n * blk, (n + 1) * blk)
        p = sp_h[:, cs] + sp_h[:, cs] * jnp.tanh(0.5 * pre[0] + br_h[:, cs])
        ig = 0.5 + 0.5 * jnp.tanh(0.5 * pre[1] + bi_h[:, cs])
        a = jnp.exp2(p * (-LOG2E))
        a_s[ro, cs] = a
        z = jnp.tanh(p) * (a * a + 1.0)
        b_s[ro, cs] = jnp.where(z > 0.0, z * lax.rsqrt(z), 0.0) * (ig * xc)
        t = jnp.tanh(ug * (GELU_C0 + (GELU_C0 * GELU_C1) * (ug * ug)))
        hu = 0.5 * ug
        gb_s[ro, cs] = hu + hu * t

    units = [(c, n) for c in range(r // rc) for n in range(nblk)]
    ug = in_proj(*units[0])
    for k, (c, n) in enumerate(units):
        ug_next = in_proj(*units[k + 1]) if k + 1 < len(units) else None
        xc, pre = conv_and_gate_dots(c, n)
        gate_tail(c, n, xc, pre, ug)
        ug = ug_next
    xbuf[0:tail, :] = xbuf[r:r + tail, :]

    def step(t, hprev):
        r0 = pl.multiple_of(t * nb, nb)
        hnew = a_s[pl.ds(r0, nb), :] * hprev + b_s[pl.ds(r0, nb), :]
        b_s[pl.ds(r0, nb), :] = hnew
        return hnew

    hc[...] = lax.fori_loop(0, ts, step, hc[...], unroll=8)

    y = b_s[...] * gb_s[...]
    out = jnp.dot(y.astype(BF16), wout_ref[...], preferred_element_type=F32)
    o_ref[...] = jnp.swapaxes(x + out.reshape(ts, nb, d) * g1[None], 0, 1)


def _rglru_call(x, mod0, g, w_in, conv_w, conv_b, wr, br, wi, bi, lam, w_out, *, ts=128, rc=256):
    nb, s, d = x.shape
    assert nb == SUBLANES, "the recurrence keeps the batch on the sublane axis"
    r = ts * nb
    dr = lam.shape[-1]
    tail = (conv_w.shape[0] - 1) * nb
    row = lambda v: v.reshape(1, -1)
    args = (x, mod0, row(g), w_in.astype(BF16), conv_w, row(conv_b), wr.astype(BF16), row(br),
            wi.astype(BF16), row(bi), row(lam), w_out.astype(BF16))
    x_spec = pl.BlockSpec((nb, ts, d), lambda i: (0, i, 0))
    return pl.pallas_call(
        functools.partial(_rglru_kernel, rc=rc),
        grid=(s // ts,),
        in_specs=[x_spec] + [_const_spec(a.shape) for a in args[1:]],
        out_specs=x_spec,
        out_shape=jax.ShapeDtypeStruct((nb, s, d), F32),
        scratch_shapes=[pltpu.VMEM((r, d), BF16), pltpu.VMEM((r, dr), F32),
                        pltpu.VMEM((r + tail, dr), F32), pltpu.VMEM((r, dr), F32),
                        pltpu.VMEM((r, dr), F32), pltpu.VMEM((nb, dr), F32)],
        compiler_params=pltpu.CompilerParams(
            dimension_semantics=("arbitrary",), vmem_limit_bytes=VMEM_LIMIT),
        name="rglru_mixer",
    )(*args)


def _ffn_kernel(*refs, has_proj, fc):
    if has_proj:
        x_ref, mod_ref, g_ref, w1_ref, w2_ref, o_in_ref, wo_ref, out_ref = refs
    else:
        x_ref, mod_ref, g_ref, w1_ref, w2_ref, out_ref = refs
    d = x_ref.shape[1]
    dff = w1_ref.shape[1]
    x = x_ref[...]
    if has_proj:
        g1 = mod_ref[:, 2 * d:3 * d]
        o = jnp.concatenate([o_in_ref[p] for p in range(o_in_ref.shape[0])], axis=-1)
        x = x + g1 * jnp.dot(o, wo_ref[...], preferred_element_type=F32)
    sh2 = mod_ref[:, 3 * d:4 * d]
    sc2 = mod_ref[:, 4 * d:5 * d]
    g2 = mod_ref[:, 5 * d:6 * d]
    hb = (_rmsnorm(x, g_ref[...]) * (1.0 + sc2) + sh2).astype(BF16)
    acc = jnp.zeros(x.shape, F32)
    for c in range(dff // fc):
        hid = jnp.dot(hb, w1_ref[:, c * fc:(c + 1) * fc], preferred_element_type=F32)
        hid = jnp.square(jnp.maximum(hid, 0.0))
        acc = acc + jnp.dot(hid.astype(BF16), w2_ref[c * fc:(c + 1) * fc, :],
                            preferred_element_type=F32)
    out_ref[...] = x + g2 * acc


def _ffn_call(x, mod_l, g, w1, w2, o_pairs=None, w_o=None, *, rf=512, fc=1024):
    b, s, d = x.shape
    has_proj = o_pairs is not None
    args = [x, mod_l, g.reshape(1, -1), w1.astype(BF16), w2.astype(BF16)]
    in_specs = [pl.BlockSpec((None, rf, d), lambda bi, i: (bi, i, 0)),
                pl.BlockSpec((None, 1, mod_l.shape[-1]), lambda bi, i: (bi, 0, 0)),
                _const_spec((1, d)), _const_spec(w1.shape), _const_spec(w2.shape)]
    if has_proj:
        npair = o_pairs.shape[1]
        args += [o_pairs, w_o.astype(BF16)]
        in_specs += [pl.BlockSpec((None, npair, rf, LANES), lambda bi, i: (bi, 0, i, 0)),
                     _const_spec(w_o.shape)]
    return pl.pallas_call(
        functools.partial(_ffn_kernel, has_proj=has_proj, fc=fc),
        grid=(b, s // rf),
        in_specs=in_specs,
        out_specs=pl.BlockSpec((None, rf, d), lambda bi, i: (bi, i, 0)),
        out_shape=jax.ShapeDtypeStruct((b, s, d), F32),
        compiler_params=pltpu.CompilerParams(
            dimension_semantics=("arbitrary", "arbitrary"), vmem_limit_bytes=VMEM_LIMIT),
        name="ffn_proj" if has_proj else "ffn",
    )(*args)


def _rot_half(x):
    return pltpu.roll(x, LANES // 2, 1)


def _dsa_in_tiles(hb, mlp_dots, w_ref, qg_ref, kg_ref, e_ref, cos_ref, sin_ref,
                  qt_ref, qit_ref, kk_ref, vt_ref, wit_ref, *, nq, nqi, hd, idx_heads):
    tq = hb.shape[0]
    lane = lax.broadcasted_iota(I32, (tq, LANES), 1)
    left = lane < hd
    head_a = (lane & (hd // 2)) == 0
    cos = cos_ref[...]
    sin = sin_ref[...]
    qg = qg_ref[...]
    e = e_ref[...]
    qscale = hd ** -0.5

    def q_tile(p, t):
        t2 = t * t
        hi = t2.astype(BF16)
        lo = (t2 - hi.astype(F32)).astype(BF16)
        ss = (jnp.dot(hi, e, preferred_element_type=F32) + jnp.dot(lo, e, preferred_element_type=F32))
        tn = t * lax.rsqrt(ss * (1.0 / hd) + RMS_EPS) * qg
        qt_ref[p] = ((tn * cos + _rot_half(tn) * sin) * (qscale * LOG2E)).T.astype(BF16)

    def qi_tile(p, t):
        qit_ref[p] = ((t * cos + _rot_half(t) * sin) * qscale).T.astype(BF16)

    def kk_tile(kk):
        ssk = jnp.sum(jnp.where(head_a, kk * kk, 0.0), axis=-1, keepdims=True)
        fac = jnp.where(head_a, lax.rsqrt(ssk * (1.0 / hd) + RMS_EPS) * kg_ref[...], 1.0)
        kn = kk * fac
        kr = kn * cos + _rot_half(kn) * sin
        k_a = jnp.where(head_a, kr, 0.0)
        ki_b = jnp.where(head_a, 0.0, kr)
        kk_ref[0] = k_a.astype(BF16)
        kk_ref[1] = pltpu.roll(k_a, hd // 2, 1).astype(BF16)
        kk_ref[2] = pltpu.roll(ki_b, LANES - hd // 2, 1).astype(BF16)
        kk_ref[3] = ki_b.astype(BF16)

    def vw_tile(vw):
        vt_ref[...] = jnp.where(left, vw, 1.0).T.astype(BF16)
        wit_ref[...] = (vw * (idx_heads ** -0.5)).T[hd:hd + idx_heads, :]

    def tile(p, t):
        if p < nq:
            q_tile(p, t)
        elif p < nq + nqi:
            qi_tile(p - nq, t)
        elif p == nq + nqi:
            kk_tile(t)
        else:
            vw_tile(t)

    gw = 2 * LANES
    ngroups = w_ref.shape[1] // gw

    def proj(gi):
        return jnp.dot(hb, w_ref[:, gi * gw:(gi + 1) * gw], preferred_element_type=F32)

    u_next = proj(0)
    for gi in range(ngroups):
        u = u_next
        if gi + 1 < ngroups:
            u_next = proj(gi + 1)
        if mlp_dots:
            mlp_dots.pop(0)()
        for half in range(gw // LANES):
            tile(gi * (gw // LANES) + half, u[:, half * LANES:(half + 1) * LANES])


def _ffn_dsa_in_kernel(x_ref, modf_ref, gf_ref, w1_ref, w2_ref, mod_ref, g_ref, w_ref, qg_ref, kg_ref,
                       e_ref, cos_ref, sin_ref, xo_ref, qt_ref, qit_ref, kk_ref, vt_ref, wit_ref, x2_s,
                       *, fc, **dims):
    tq, d = x_ref.shape
    i = pl.program_id(1)
    slot = lax.rem(i, 2)

    def run(do_mlp, do_dsa):
        mlp_dots = []
        if do_mlp:
            x = x_ref[...]
            sh2 = modf_ref[:, 3 * d:4 * d]
            sc2 = modf_ref[:, 4 * d:5 * d]
            g2 = modf_ref[:, 5 * d:6 * d]
            hbf = (_rmsnorm(x, gf_ref[...]) * (1.0 + sc2) + sh2).astype(BF16)
            mlp = {"acc": jnp.zeros((tq, d), F32), "hid": None}

            def up(c):
                hid = jnp.dot(hbf, w1_ref[:, c * fc:(c + 1) * fc], preferred_element_type=F32)
                mlp["hid"] = jnp.square(jnp.maximum(hid, 0.0)).astype(BF16)

            def down(c):
                mlp["acc"] = mlp["acc"] + jnp.dot(mlp["hid"], w2_ref[c * fc:(c + 1) * fc, :],
                                                  preferred_element_type=F32)

            mlp_dots = [functools.partial(f, c)
                        for c in range(w1_ref.shape[1] // fc) for f in (up, down)]
        if do_dsa:
            sh1 = mod_ref[:, 0:d]
            sc1 = mod_ref[:, d:2 * d]
            hb = (_rmsnorm(x2_s[1 - slot], g_ref[...]) * (1.0 + sc1) + sh1).astype(BF16)
            _dsa_in_tiles(hb, mlp_dots, w_ref, qg_ref, kg_ref, e_ref, cos_ref, sin_ref,
                          qt_ref, qit_ref, kk_ref, vt_ref, wit_ref, **dims)
        while mlp_dots:
            mlp_dots.pop(0)()
        if do_mlp:
            x2 = x + g2 * mlp["acc"]
            xo_ref[...] = x2
            x2_s[slot] = x2

    last = pl.num_programs(1) - 1
    pl.when(i == 0)(functools.partial(run, True, False))
    pl.when(jnp.logical_and(i > 0, i < last))(functools.partial(run, True, True))
    pl.when(i == last)(functools.partial(run, False, True))


def _ffn_dsa_in_call(x, modf_l, gf, w1, w2, mod_l, g, w_in, q_g, k_g, *, n_heads, hd, idx_heads,
                     tq=512, fc=1024):
    b, s, d = x.shape
    assert 2 * hd == LANES, "two heads per lane tile"
    nq = n_heads * hd // LANES
    nqi = idx_heads * hd // LANES
    o_k = n_heads * hd
    o_v = o_k + hd
    o_qi = o_v + hd
    o_ki = o_qi + idx_heads * hd
    o_wi = o_ki + hd
    pad = LANES - hd - idx_heads
    w = jnp.concatenate([w_in[:, :o_k], w_in[:, o_qi:o_ki], w_in[:, o_k:o_v], w_in[:, o_ki:o_wi],
                         w_in[:, o_v:o_qi], w_in[:, o_wi:], jnp.zeros((d, pad), w_in.dtype)],
                        axis=1).astype(BF16)
    half = hd // 2
    lane_src = jnp.concatenate([jnp.arange(0, half), jnp.arange(hd, hd + half),
                                jnp.arange(half, hd), jnp.arange(hd + half, LANES)])
    ncol = w.shape[1]
    nrot = nq + nqi + 1
    w = w.reshape(d, ncol // LANES, LANES)
    w = jnp.concatenate([w[:, :nrot, lane_src], w[:, nrot:, :]], axis=1).reshape(d, ncol)
    inv = ROPE_THETA ** (-jnp.arange(0, hd, 2, dtype=F32) / hd)
    ang = jnp.arange(s, dtype=F32)[:, None] * inv[None, :]
    cos_t = jnp.tile(jnp.cos(ang), (1, 4))
    sin_h = jnp.sin(ang)
    sin_t = jnp.concatenate([-sin_h, -sin_h, sin_h, sin_h], axis=1)
    head_of_lane = (jnp.arange(LANES) // half) % 2
    e = (head_of_lane[:, None] == head_of_lane[None, :]).astype(BF16)
    qg = jnp.tile(q_g, 2)[lane_src].reshape(1, LANES)
    kg = jnp.concatenate([k_g, jnp.ones((hd,), k_g.dtype)])[lane_src].reshape(1, LANES)

    nt = s // tq
    cur = lambda i: jnp.minimum(i, nt - 1)
    prev = lambda i: jnp.maximum(i - 1, 0)
    mod_spec = pl.BlockSpec((None, 1, mod_l.shape[-1]), lambda bi, i: (bi, 0, 0))
    x_spec = pl.BlockSpec((None, tq, d), lambda bi, i: (bi, cur(i), 0))
    tab_spec = pl.BlockSpec((tq, LANES), lambda bi, i: (prev(i), 0))
    return pl.pallas_call(
        functools.partial(_ffn_dsa_in_kernel, nq=nq, nqi=nqi, hd=hd, idx_heads=idx_heads, fc=fc),
        grid=(b, nt + 1),
        in_specs=[x_spec, mod_spec, _const_spec((1, d)), _const_spec(w1.shape), _const_spec(w2.shape),
                  mod_spec, _const_spec((1, d)), _const_spec((d, ncol)), _const_spec((1, LANES)),
                  _const_spec((1, LANES)), _const_spec((LANES, LANES)), tab_spec, tab_spec],
        out_specs=[x_spec,
                   pl.BlockSpec((None, nq, LANES, tq), lambda bi, i: (bi, 0, 0, prev(i))),
                   pl.BlockSpec((None, nqi, LANES, tq), lambda bi, i: (bi, 0, 0, prev(i))),
                   pl.BlockSpec((None, 4, tq, LANES), lambda bi, i: (bi, 0, prev(i), 0)),
                   pl.BlockSpec((None, LANES, tq), lambda bi, i: (bi, 0, prev(i))),
                   pl.BlockSpec((None, idx_heads, tq), lambda bi, i: (bi, 0, prev(i)))],
        out_shape=[jax.ShapeDtypeStruct((b, s, d), F32),
                   jax.ShapeDtypeStruct((b, nq, LANES, s), BF16),
                   jax.ShapeDtypeStruct((b, nqi, LANES, s), BF16),
                   jax.ShapeDtypeStruct((b, 4, s, LANES), BF16),
                   jax.ShapeDtypeStruct((b, LANES, s), BF16),
                   jax.ShapeDtypeStruct((b, idx_heads, s), F32)],
        scratch_shapes=[pltpu.VMEM((2, tq, d), F32)],
        compiler_params=pltpu.CompilerParams(
            dimension_semantics=("arbitrary", "arbitrary"), vmem_limit_bytes=VMEM_LIMIT),
        name="ffn_dsa_in",
    )(x, modf_l, gf.reshape(1, -1), w1.astype(BF16), w2.astype(BF16), mod_l, g.reshape(1, -1), w, qg, kg,
      e, cos_t, sin_t)


def _dsa_attn_kernel(bnd_ref, qit0_ref, wit0_ref, qitn_ref, witn_ref, qt_ref, kk_ref, vt_ref, o_ref,
                     sct_s, mask_s, st_s, mm_s, m_s, acc_s, *, kc, ks, topk, hd, idx_heads, group,
                     tie_from, max_groups, n_blocks, head_passes, sl):
    npair, _, qb = qt_ref.shape
    nh = 2 * npair
    nsub = SUBLANES
    i = pl.program_id(1)
    has_sel = i < n_blocks
    has_att = i >= 1
    nk = (i * qb) // kc + 1
    nkp = ((i - 1) * qb) // kc + 1
    slot = lax.rem(i, 2)
    mask_w = mask_s.at[slot]
    mask_r = mask_s.at[1 - slot]
    sct_c = sct_s.at[slot]
    sct_n = sct_s.at[1 - slot]
    kf = float(topk)
    fast = bnd_ref[0] <= EXP_SAFE_BOUND
    merged = jnp.logical_and(jnp.logical_and(has_sel, has_att), fast)

    def slab(src, c, t):
        return src[pl.ds(pl.multiple_of(c * kc + t * sl, sl), sl), :]

    def finish(acc, op, red):
        n = sl // nsub
        acc = acc.reshape(n, nsub, qb)
        r = acc[0]
        for t in range(1, n):
            r = op(r, acc[t])
        return jnp.broadcast_to(red(r, axis=0, keepdims=True), (nsub, qb))

    def reduce_keys(src, fn, op, red, init):
        def body(ci, acc):
            for t in range(kc // sl):
                acc = op(acc, fn(slab(src, ci, t), ci, t))
            return acc
        return finish(lax.fori_loop(0, nk, body, jnp.full((sl, qb), init, F32)), op, red)

    def rows(x):
        return jnp.concatenate([x] * (sl // nsub), axis=0)

    def score_chunk(qit_ref, wit_ref, blk, dst, ci, mm):
        qpos = blk * qb + lax.broadcasted_iota(I32, (ks, qb), 1)
        krow = lax.broadcasted_iota(I32, (ks, qb), 0)
        mx, mn = mm
        for sub in range(kc // ks):
            k0 = pl.multiple_of(ci * kc + sub * ks, ks)
            kis = (kk_ref[2, pl.ds(k0, ks), :], kk_ref[3, pl.ds(k0, ks), :])
            acc = jnp.zeros((ks, qb), F32)
            for hh in range(idx_heads):
                lg = jnp.dot(kis[hh % 2], qit_ref[hh // 2], preferred_element_type=F32)
                acc = acc + wit_ref[hh:hh + 1, :] * jnp.maximum(lg, 0.0)
            causal = (k0 + krow) <= qpos
            lo_v = jnp.where(causal, acc, -jnp.inf)
            hi_v = jnp.where(causal, acc, jnp.inf)
            dst[pl.ds(k0, ks), :] = lo_v
            for t in range(ks // sl):
                mx = jnp.maximum(mx, lo_v[t * sl:(t + 1) * sl])
                mn = jnp.minimum(mn, hi_v[t * sl:(t + 1) * sl])
        return mx, mn

    mm0 = (jnp.full((sl, qb), -jnp.inf, F32), jnp.full((sl, qb), jnp.inf, F32))

    def save_range(mm):
        mm_s[0] = finish(mm[0], jnp.maximum, jnp.max)
        mm_s[1] = finish(mm[1], jnp.minimum, jnp.min)

    @pl.when(i == 0)
    def _():
        save_range(lax.fori_loop(
            0, nk, functools.partial(score_chunk, qit0_ref, wit0_ref, 0, sct_c), mm0))

    @pl.when(has_sel)
    def _():
        mx = mm_s[0]
        st_s[0] = mm_s[1]
        st_s[1] = mx + (jnp.abs(mx) * 1e-6 + 1e-30)
        st_s[2] = (i * qb + lax.broadcasted_iota(I32, (nsub, qb), 1) + 1).astype(F32)
        st_s[3] = jnp.zeros((nsub, qb), F32)

    def count_step(st):
        c, lo, hi, clo, chi, acc = st
        mid = 0.5 * lo + 0.5 * hi
        midr = rows(mid)
        for t in range(kc // sl):
            acc = acc + jnp.where(slab(sct_c, c, t) >= midr, 1.0, 0.0)
        last = c == nk - 1
        tot = finish(acc, jnp.add, jnp.sum)
        ge = tot >= kf
        lo = jnp.where(last, jnp.where(ge, mid, lo), lo)
        clo = jnp.where(last, jnp.where(ge, tot, clo), clo)
        hi = jnp.where(last, jnp.where(ge, hi, mid), hi)
        chi = jnp.where(last, jnp.where(ge, chi, tot), chi)
        return (jnp.where(last, 0, c + 1), lo, hi, clo, chi, jnp.where(last, 0.0, acc))

    def scores(ci, h):
        kk = kk_ref[h % 2, pl.ds(pl.multiple_of(ci * kc, kc), kc), :]
        return jnp.dot(kk, qt_ref[h // 2], preferred_element_type=F32)

    def pv_step(ci, st, use_max, passes):
        c0 = pl.multiple_of(ci * kc, kc)
        vt = vt_ref[:, pl.ds(c0, kc)]
        msk = mask_r[pl.ds(c0, kc), :]
        ahead = [scores(ci, h) for h in range(min(QK_LOOKAHEAD, nh))]
        for h in range(nh):
            s_cur = ahead.pop(0)
            if h + QK_LOOKAHEAD < nh:
                ahead.append(scores(ci, h + QK_LOOKAHEAD))
            if use_max:
                pr = jnp.where(msk > 0, jnp.exp2(s_cur - m_s[h][0:1, :]), 0.0).astype(BF16)
            else:
                pr = jnp.exp2(s_cur).astype(BF16) * msk
            acc_s[h] += jnp.dot(vt, pr, preferred_element_type=F32)
            for _ in range(passes[h] if passes else 0):
                st = count_step(st)
        return st

    @pl.when(has_att)
    def _():
        acc_s[...] = jnp.zeros(acc_s.shape, F32)

    @pl.when(merged)
    def _():
        st = (jnp.int32(0), st_s[0], st_s[1], st_s[2], st_s[3], jnp.zeros((sl, qb), F32))
        st = lax.fori_loop(0, nkp, functools.partial(pv_step, use_max=False, passes=head_passes), st)
        st_s[0], st_s[1], st_s[2], st_s[3] = st[1], st[2], st[3], st[4]

    @pl.when(jnp.logical_and(has_att, jnp.logical_and(fast, jnp.logical_not(has_sel))))
    def _():
        lax.fori_loop(0, nkp, functools.partial(pv_step, use_max=False, passes=None), 0)

    @pl.when(jnp.logical_and(has_att, jnp.logical_not(fast)))
    def _():
        def max_step(ci, _):
            @pl.when(ci == 0)
            def _():
                m_s[...] = jnp.full(m_s.shape, NEG_BIG, F32)
            msk = mask_r[pl.ds(pl.multiple_of(ci * kc, kc), kc), :]
            for h in range(nh):
                mx = jnp.max(jnp.where(msk > 0, scores(ci, h), NEG_BIG), axis=0, keepdims=True)
                m_s[h] = jnp.maximum(m_s[h], jnp.broadcast_to(mx, (nsub, qb)))
            return 0

        lax.fori_loop(0, nkp, max_step, 0)
        lax.fori_loop(0, nkp, functools.partial(pv_step, use_max=True, passes=None), 0)

    @pl.when(has_att)
    def _():
        for p in range(npair):
            a0, a1 = acc_s[2 * p], acc_s[2 * p + 1]
            ot = jnp.concatenate([a0[0:hd] / a0[hd:2 * hd], a1[0:hd] / a1[hd:2 * hd]], axis=0)
            o_ref[p] = ot.T.astype(BF16)

    @pl.when(has_sel)
    def _():
        def count_ge(thr):
            thr = rows(thr)
            return reduce_keys(sct_c, lambda blk, c, t: jnp.where(blk >= thr, 1.0, 0.0),
                               jnp.add, jnp.sum, 0.0)

        def bisect(_, st):
            lo, hi, clo, chi = st
            mid = 0.5 * lo + 0.5 * hi
            c = count_ge(mid)
            ge = c >= kf
            return (jnp.where(ge, mid, lo), jnp.where(ge, hi, mid),
                    jnp.where(ge, c, clo), jnp.where(ge, chi, c))

        def open_rows(clo, tied):
            return jnp.max(jnp.where(clo > kf, 1.0 - tied, 0.0)) > 0.0

        g0 = jnp.where(merged, tie_from // group, 0)

        def search_group(st):
            g, _, lo, hi, clo, chi, tied = st
            lo, hi, clo, chi = lax.fori_loop(0, group, bisect, (lo, hi, clo, chi))

            def tie_check(tied):
                lor, hir = rows(lo), rows(hi)
                a = reduce_keys(sct_c, lambda blk, c, t: jnp.where(blk >= lor, blk, jnp.inf),
                                jnp.minimum, jnp.min, jnp.inf)
                b = reduce_keys(sct_c, lambda blk, c, t: jnp.where(blk < hir, blk, -jnp.inf),
                                jnp.maximum, jnp.max, -jnp.inf)
                return jnp.where(jnp.logical_and(clo > kf, a == b), 1.0, tied)

            tied = lax.cond(jnp.logical_and((g + 1) * group >= tie_from, open_rows(clo, tied)),
                            tie_check, lambda t: t, tied)
            return (g + 1, open_rows(clo, tied).astype(I32), lo, hi, clo, chi, tied)

        zero = jnp.zeros((nsub, qb), F32)
        _, _, lo, hi, clo, chi, _ = lax.while_loop(
            lambda st: jnp.logical_and(st[1] > 0, st[0] < max_groups), search_group,
            (g0, open_rows(st_s[2], zero).astype(I32), st_s[0], st_s[1], st_s[2], st_s[3], zero))

        def tile_k(x):
            return jnp.concatenate([x] * (ks // nsub), axis=0)

        lok, hik, free = tile_k(lo), tile_k(hi), tile_k(kf - chi)
        tri = jnp.where(lax.broadcasted_iota(I32, (ks, ks), 1) <= lax.broadcasted_iota(I32, (ks, ks), 0),
                        1.0, 0.0).astype(BF16)

        def mask_chunk(ci, carry):
            subs = []
            for t in range(kc // ks):
                k0 = pl.multiple_of(ci * kc + t * ks, ks)
                blk = sct_c[pl.ds(k0, ks), :]
                top = blk >= hik
                tie = jnp.where(top, 0.0, jnp.where(blk >= lok, 1.0, 0.0))
                subs.append((k0, top, tie, jnp.dot(tri, tie.astype(BF16), preferred_element_type=F32)))
            for k0, top, tie, rank in subs:
                keep = jnp.where(rank + tile_k(carry) <= free, tie, 0.0)
                mask_w[pl.ds(k0, ks), :] = jnp.where(top, 1.0, keep).astype(BF16)
                carry = carry + jnp.broadcast_to(rank[ks - 1:ks, :], (nsub, qb))
            return carry

        zero8 = jnp.zeros((nsub, qb), F32)
        has_next = i + 1 < n_blocks

        @pl.when(has_next)
        def _():
            nkn = ((i + 1) * qb) // kc + 1
            nxt = functools.partial(score_chunk, qitn_ref, witn_ref, i + 1, sct_n)

            def both(ci, st):
                return (mask_chunk(ci, st[0]), nxt(ci, st[1]))

            _, mm = lax.fori_loop(0, nk, both, (zero8, mm0))
            save_range(lax.fori_loop(nk, nkn, nxt, mm))

        @pl.when(jnp.logical_not(has_next))
        def _():
            lax.fori_loop(0, nk, mask_chunk, zero8)


def _dsa_attn_call(bnd, qt, qit, kk, vt, wit, *, hd, idx_heads, topk, qb=256, kc=512, ks=128,
                   sl=32):
    b, npair, _, s = qt.shape
    nqi = qit.shape[1]
    assert kc % qb == 0 and s % kc == 0 and kc % ks == 0 and ks % sl == 0 and sl % SUBLANES == 0
    nb = s // qb
    head_passes = tuple(2 if h % 4 == 0 else 1 for h in range(2 * npair))
    assert 2 * hd == LANES
    prev = lambda i: jnp.maximum(i - 1, 0)
    cur = lambda i: jnp.minimum(i, nb - 1)
    nxt = lambda i: jnp.minimum(i + 1, nb - 1)
    return pl.pallas_call(
        functools.partial(_dsa_attn_kernel, kc=kc, ks=ks, topk=topk, hd=hd, idx_heads=idx_heads,
                          group=4, tie_from=16, max_groups=96, n_blocks=nb,
                          head_passes=head_passes, sl=sl),
        grid=(b, nb + 1),
        in_specs=[pl.BlockSpec(memory_space=pltpu.SMEM),
                  pl.BlockSpec((None, nqi, LANES, qb), lambda bi, i: (bi, 0, 0, cur(i))),
                  pl.BlockSpec((None, idx_heads, qb), lambda bi, i: (bi, 0, cur(i))),
                  pl.BlockSpec((None, nqi, LANES, qb), lambda bi, i: (bi, 0, 0, nxt(i))),
                  pl.BlockSpec((None, idx_heads, qb), lambda bi, i: (bi, 0, nxt(i))),
                  pl.BlockSpec((None, npair, LANES, qb), lambda bi, i: (bi, 0, 0, prev(i))),
                  pl.BlockSpec((None, 4, s, LANES), lambda bi, i: (bi, 0, 0, 0)),
                  pl.BlockSpec((None, LANES, s), lambda bi, i: (bi, 0, 0))],
        out_specs=pl.BlockSpec((None, npair, qb, LANES), lambda bi, i: (bi, 0, prev(i), 0)),
        out_shape=jax.ShapeDtypeStruct((b, npair, s, LANES), BF16),
        scratch_shapes=[pltpu.VMEM((2, s, qb), F32),
                        pltpu.VMEM((2, s, qb), BF16), pltpu.VMEM((4, SUBLANES, qb), F32),
                        pltpu.VMEM((2, SUBLANES, qb), F32),
                        pltpu.VMEM((2 * npair, SUBLANES, qb), F32),
                        pltpu.VMEM((2 * npair, LANES, qb), F32)],
        compiler_params=pltpu.CompilerParams(
            dimension_semantics=("arbitrary", "arbitrary"), vmem_limit_bytes=VMEM_LIMIT),
        name="dsa_attn",
    )(bnd, qit, wit, qit, wit, qt, kk, vt)


def kernel(x, c, norm_mix_g, norm_ffn_g, ada_w, ada_b, a_w_in, a_conv_w, a_conv_b, a_gate_r_w,
           a_gate_r_b, a_gate_i_w, a_gate_i_b, a_lambda, a_w_out, b_w_in, b_q_norm_g, b_k_norm_g,
           b_w_out, ffn_w1, ffn_w2):
    b, s, d = x.shape
    hd = b_q_norm_g.shape[-1]
    n_heads = b_w_out.shape[1] // hd
    idx_heads = (b_w_in.shape[-1] - n_heads * hd - 3 * hd) // (hd + 1)
    topk = min(TOPK_MAX, s // 4)
    depth = ada_w.shape[0]

    mod = _mod_call(c, ada_w, ada_b)
    mod_rows = mod.reshape(depth, b, 1, mod.shape[-1])

    for i in range(depth):
        j = i // 2
        if i % 2 == 0:
            x = _rglru_call(x, mod[i], norm_mix_g[i], a_w_in[j], a_conv_w[j], a_conv_b[j],
                            a_gate_r_w[j], a_gate_r_b[j], a_gate_i_w[j], a_gate_i_b[j],
                            a_lambda[j], a_w_out[j])
            if i + 1 == depth:
                x = _ffn_call(x, mod_rows[i], norm_ffn_g[i], ffn_w1[i], ffn_w2[i])
        else:
            x, qt, qit, kk, vt, wit = _ffn_dsa_in_call(
                x, mod_rows[i - 1], norm_ffn_g[i - 1], ffn_w1[i - 1], ffn_w2[i - 1], mod_rows[i],
                norm_mix_g[i], b_w_in[j], b_q_norm_g[j], b_k_norm_g[j], n_heads=n_heads, hd=hd,
                idx_heads=idx_heads)
            bnd = (1.02 * hd * hd ** -0.5) * jnp.max(jnp.abs(b_q_norm_g[j])) * jnp.max(jnp.abs(b_k_norm_g[j]))
            o = _dsa_attn_call(bnd.reshape(1), qt, qit, kk, vt, wit, hd=hd, idx_heads=idx_heads,
                               topk=topk)
            x = _ffn_call(x, mod_rows[i], norm_ffn_g[i], ffn_w1[i], ffn_w2[i], o_pairs=o,
                          w_o=b_w_out[j])
    return x
```
